```python
import math
import numpy as np
import jax
import jax.numpy as jnp
from jax import lax

D_MODEL = 1024
BATCH = 8
SEQ = 4096
DEPTH = 2

GRID_W = 64
CTX_LEN = 256
Q_BLOCK = 128
ROPE_BASE = 10000.0
EPS = 1e-6

NA_HEADS = 4
NA_DIM = 64
NA_KH = 8
NA_KW = 16
DF_HEADS = 4
DF_QK = 32
DF_V = 2 * DF_QK
GQ_HEADS = 4
GQ_KV_HEADS = 2
GQ_DIM = 64
ML_HEADS = 4
ML_NOPE = 64
ML_ROPE = 32
ML_V = 64
ML_Q_RANK = 256
ML_KV_RANK = 128
N_BRANCH = 4
BRANCH_W = 256
N_EXPERTS = 32
TOP_K = 4
D_FF = D_MODEL
SWIGLU_LIMIT = 7.0
SWIGLU_ALPHA = 1.702
DEEPNORM_ALPHA = (2 * DEPTH) ** 0.25
DEEPNORM_BETA = (8 * DEPTH) ** -0.25

IN_SIZES = (NA_HEADS * NA_DIM, NA_HEADS * NA_DIM, NA_HEADS * NA_DIM,
            DF_HEADS * 2 * DF_QK, DF_HEADS * 2 * DF_QK, DF_HEADS * DF_V,
            GQ_HEADS * GQ_DIM, GQ_KV_HEADS * GQ_DIM, GQ_KV_HEADS * GQ_DIM,
            ML_Q_RANK, ML_KV_RANK, ML_ROPE,
            N_BRANCH * D_MODEL)
IN_WIDTH = sum(IN_SIZES)
IN_OFFSETS = tuple(int(o) for o in np.cumsum(IN_SIZES)[:-1])

kernel_name = 'hybrid_gated_mixer_moe_trunk'


def layer_norm(x, g, b):
    xf = x.astype(jnp.float32)
    mu = jnp.mean(xf, axis=-1, keepdims=True)
    var = jnp.mean(jnp.square(xf - mu), axis=-1, keepdims=True)
    return ((xf - mu) * lax.rsqrt(var + EPS) * g + b).astype(x.dtype)


def rms_norm(x, g):
    xf = x.astype(jnp.float32)
    return (xf * lax.rsqrt(jnp.mean(jnp.square(xf), axis=-1, keepdims=True) + EPS) * g).astype(x.dtype)


def rope_1d(x, pos):
    n = x.shape[-1]
    inv_freq = ROPE_BASE ** (-jnp.arange(0, n, 2, dtype=jnp.float32) / n)
    ang = pos.astype(jnp.float32)[:, None] * inv_freq[None, :]
    cos, sin = jnp.cos(ang).astype(x.dtype), jnp.sin(ang).astype(x.dtype)
    x1, x2 = x[..., : n // 2], x[..., n // 2:]
    return jnp.concatenate([x1 * cos - x2 * sin, x2 * cos + x1 * sin], axis=-1)


def rope_2d(x, pos):
    rows, cols = pos
    h = x.shape[-1] // 2
    return jnp.concatenate([rope_1d(x[..., :h], rows), rope_1d(x[..., h:], cols)], axis=-1)


def to_heads(t, n):
    b, s, w = t.shape
    return t.reshape(b, s, n, w // n).transpose(0, 2, 1, 3)


def from_heads(t):
    b, h, s, d = t.shape
    return t.transpose(0, 2, 1, 3).reshape(b, s, h * d)


def softmax_f32(s, dtype):
    return jax.nn.softmax(s.astype(jnp.float32), axis=-1).astype(dtype)


def sdpa(q, k, v):
    s = jnp.einsum('bgrqd,bgkd->bgrqk', q, k) * (q.shape[-1] ** -0.5)
    return jnp.einsum('bgrqk,bgkd->bgrqd', softmax_f32(s, v.dtype), v)


def mha(q, k, v):
    return sdpa(q[:, :, None], k, v)[:, :, 0]


def group_queries(q):
    b, h, s, d = q.shape
    return q.reshape(b, GQ_KV_HEADS, h // GQ_KV_HEADS, s, d)


def ungroup(o):
    b, g, r, s, d = o.shape
    return o.reshape(b, g * r, s, d)


def diff_attend(q1, q2, k1, k2, v, lam):
    scale = q1.shape[-1] ** -0.5
    a1 = softmax_f32(jnp.einsum('bhqd,bhkd->bhqk', q1, k1) * scale, jnp.float32)
    a2 = softmax_f32(jnp.einsum('bhqd,bhkd->bhqk', q2, k2) * scale, jnp.float32)
    return jnp.einsum('bhqk,bhkd->bhqd', (a1 - lam * a2).astype(v.dtype), v)


def sweep_query_blocks(fn, qs):
    s = qs[0].shape[-2]
    nb = s // Q_BLOCK

    def to_blocks(q):
        return jnp.moveaxis(q.reshape(q.shape[:-2] + (nb, Q_BLOCK, q.shape[-1])), -3, 0)

    out = lax.map(lambda blk: fn(*blk), tuple(to_blocks(q) for q in qs))
    out = jnp.moveaxis(out, 0, -3)
    return out.reshape(out.shape[:-3] + (s, out.shape[-1]))


def neighbourhood_attend(q, k, v, k_ctx, v_ctx, rpb, rows):
    b, h, s, d = q.shape
    kh, kw = min(NA_KH, rows), NA_KW
    n_keys = kh * kw
    scale = d ** -0.5
    col = jnp.arange(GRID_W, dtype=jnp.int32)
    col_start = jnp.clip(col - kw // 2, 0, GRID_W - kw)
    row_start = jnp.clip(jnp.arange(rows, dtype=jnp.int32) - kh // 2, 0, rows - kh)
    di = jnp.repeat(jnp.arange(kh, dtype=jnp.int32), kw)
    dj = jnp.tile(jnp.arange(kw, dtype=jnp.int32), kh)
    key_col = col_start[:, None] + dj[None, :]
    col_bias_idx = key_col - col[:, None] + (NA_KW - 1)
    q_rows = q.reshape(b, h, rows, GRID_W, d).transpose(2, 0, 1, 3, 4)

    def one_row(args):
        qr, r, rs = args
        key_row = rs + di
        idx = key_row[None, :] * GRID_W + key_col
        kg = jnp.take(k, idx, axis=2)
        vg = jnp.take(v, idx, axis=2)
        row_bias_idx = (key_row - r + (NA_KH - 1))[None, :]
        bias = rpb[:, row_bias_idx, col_bias_idx]
        s_loc = jnp.einsum('bhqd,bhqkd->bhqk', qr, kg) * scale + bias[None]
        s_ctx = jnp.einsum('bhqd,bhkd->bhqk', qr, k_ctx) * scale
        p = softmax_f32(jnp.concatenate([s_loc, s_ctx], axis=-1), v.dtype)
        return (jnp.einsum('bhqk,bhqkd->bhqd', p[..., :n_keys], vg)
                + jnp.einsum('bhqk,bhkd->bhqd', p[..., n_keys:], v_ctx))

    out = lax.map(one_row, (q_rows, jnp.arange(rows, dtype=jnp.int32), row_start))
    return out.transpose(1, 2, 0, 3, 4).reshape(b, h, s, d)


def project_tokens(h, pos, w_in, gq_qnorm, gq_knorm, ml_qa_norm, ml_wq_b, ml_kva_norm, ml_wkv_b):
    rot = (lambda t: rope_2d(t, pos)) if pos is not None else (lambda t: t)
    (na_q, na_k, na_v, df_q, df_k, df_v, gq_q, gq_k, gq_v,
     ml_qa, ml_kva, ml_kr, gates) = jnp.split(h @ w_in, IN_OFFSETS, axis=-1)
    b, s, _ = h.shape
    df_q, df_k = to_heads(df_q, DF_HEADS), to_heads(df_k, DF_HEADS)
    ml_q = to_heads(rms_norm(ml_qa, ml_qa_norm) @ ml_wq_b, ML_HEADS)
    ml_kv = to_heads(rms_norm(ml_kva, ml_kva_norm) @ ml_wkv_b, ML_HEADS)
    k_rope = rot(ml_kr[:, None])
    return {
        'na_q': to_heads(na_q, NA_HEADS), 'na_k': to_heads(na_k, NA_HEADS), 'na_v': to_heads(na_v, NA_HEADS),
        'df_q1': rot(df_q[..., :DF_QK]), 'df_q2': rot(df_q[..., DF_QK:]),
        'df_k1': rot(df_k[..., :DF_QK]), 'df_k2': rot(df_k[..., DF_QK:]),
        'df_v': to_heads(df_v, DF_HEADS),
        'gq_q': rot(rms_norm(to_heads(gq_q, GQ_HEADS), gq_qnorm)),
        'gq_k': rot(rms_norm(to_heads(gq_k, GQ_KV_HEADS), gq_knorm)),
        'gq_v': to_heads(gq_v, GQ_KV_HEADS),
        'ml_q': jnp.concatenate([ml_q[..., :ML_NOPE], rot(ml_q[..., ML_NOPE:])], axis=-1),
        'ml_k': jnp.concatenate([ml_kv[..., :ML_NOPE],
                                 jnp.broadcast_to(k_rope, ml_kv.shape[:-1] + (ML_ROPE,))], axis=-1),
        'ml_v': ml_kv[..., ML_NOPE:],
        'gates': gates.reshape(b, s, N_BRANCH, D_MODEL),
    }


def merge_branches(outs, gates, w_branch, w_out):
    y = None
    for i, o in enumerate(outs):
        term = jax.nn.sigmoid(gates[:, :, i]) * (from_heads(o) @ w_branch[i])
        y = term if y is None else y + term
    return y @ w_out


def latent_mix(lat, cx, rows, lam, lam_init, rpb, subln, w_branch, w_out):
    cat = lambda a, b: jnp.concatenate([a, b], axis=2)
    o_na = neighbourhood_attend(lat['na_q'], lat['na_k'], lat['na_v'], cx['na_k'], cx['na_v'], rpb, rows)
    k1, k2, dv = cat(cx['df_k1'], lat['df_k1']), cat(cx['df_k2'], lat['df_k2']), cat(cx['df_v'], lat['df_v'])
    o_df = sweep_query_blocks(lambda a, b: diff_attend(a, b, k1, k2, dv, lam), (lat['df_q1'], lat['df_q2']))
    o_df = rms_norm(o_df, subln) * (1.0 - lam_init)
    gk, gv = cat(cx['gq_k'], lat['gq_k']), cat(cx['gq_v'], lat['gq_v'])
    o_gq = ungroup(sweep_query_blocks(lambda a: sdpa(a, gk, gv), (group_queries(lat['gq_q']),)))
    mk, mv = cat(cx['ml_k'], lat['ml_k']), cat(cx['ml_v'], lat['ml_v'])
    o_ml = sweep_query_blocks(lambda a: mha(a, mk, mv), (lat['ml_q'],))
    return merge_branches((o_na, o_df, o_gq, o_ml), lat['gates'], w_branch, w_out)


def context_mix(cx, lam, lam_init, subln, w_branch, w_out):
    o_na = mha(cx['na_q'], cx['na_k'], cx['na_v'])
    o_df = rms_norm(diff_attend(cx['df_q1'], cx['df_q2'], cx['df_k1'], cx['df_k2'], cx['df_v'], lam), subln) * (1.0 - lam_init)
    o_gq = ungroup(sdpa(group_queries(cx['gq_q']), cx['gq_k'], cx['gq_v']))
    o_ml = mha(cx['ml_q'], cx['ml_k'], cx['ml_v'])
    return merge_branches((o_na, o_df, o_gq, o_ml), cx['gates'], w_branch, w_out)


def moe(h, router_w, router_b, w1, b1, w2, b2):
    logits = (h @ router_w + router_b).astype(jnp.float32)
    top_val, top_idx = lax.top_k(logits, TOP_K)
    top_w = jax.nn.softmax(top_val, axis=-1)
    combine = jnp.einsum('nk,nke->ne', top_w, jax.nn.one_hot(top_idx, N_EXPERTS, dtype=jnp.float32)).astype(h.dtype)

    def expert_step(acc, xs):
        w1e, b1e, w2e, b2e, ge = xs
        gu = h @ w1e + b1e
        gate = jnp.minimum(gu[:, :D_FF], SWIGLU_LIMIT)
        up = jnp.clip(gu[:, D_FF:], -SWIGLU_LIMIT, SWIGLU_LIMIT)
        y = ((up + 1.0) * gate * jax.nn.sigmoid(SWIGLU_ALPHA * gate)) @ w2e + b2e
        return acc + ge[:, None] * y, None

    out, _ = lax.scan(expert_step, jnp.zeros_like(h), (w1, b1, w2, b2, combine.T))
    return out


def setup_inputs(seed: int = 0) -> dict:
    key = jax.random.key(seed)
    ks = iter(jax.random.split(key, 32))
    L, D, f32 = DEPTH, D_MODEL, jnp.float32

    def normal(shape, scale):
        return scale * jax.random.normal(next(ks), shape, f32)

    def gain(shape):
        return 1.0 + normal(shape, 0.1)

    return {
        'x': normal((BATCH, SEQ, D), 1.0),
        'c': normal((BATCH, D), 1.0),
        'ctx': normal((BATCH, CTX_LEN, D), 1.0),
        'c_ctx': normal((D,), 1.0),
        'w_ada': normal((L, D, 6 * D), 0.5 * D ** -0.5),
        'b_ada': normal((L, 6 * D), 0.02),
        'w_in': normal((L, D, IN_WIDTH), D ** -0.5),
        'na_rpb': normal((L, NA_HEADS, 2 * NA_KH - 1, 2 * NA_KW - 1), 0.1),
        'df_lam': normal((L, 4, DF_QK), 0.1),
        'df_subln': gain((L, DF_V)),
        'gq_qnorm': gain((L, GQ_DIM)),
        'gq_knorm': gain((L, GQ_DIM)),
        'ml_qa_norm': gain((L, ML_Q_RANK)),
        'ml_wq_b': normal((L, ML_Q_RANK, ML_HEADS * (ML_NOPE + ML_ROPE)), ML_Q_RANK ** -0.5),
        'ml_kva_norm': gain((L, ML_KV_RANK)),
        'ml_wkv_b': normal((L, ML_KV_RANK, ML_HEADS * (ML_NOPE + ML_V)), ML_KV_RANK ** -0.5),
        'w_branch': normal((L, N_BRANCH, BRANCH_W, D), DEEPNORM_BETA * BRANCH_W ** -0.5),
        'w_out': normal((L, D, D), DEEPNORM_BETA * D ** -0.5),
        'ln1_g': gain((L, D)),
        'ln1_b': normal((L, D), 0.02),
        'ln2_g': gain((L, D)),
        'ln2_b': normal((L, D), 0.02),
        'router_w': normal((L, D, N_EXPERTS), D ** -0.5),
        'router_b': normal((L, N_EXPERTS), 0.01),
        'exp_w1': normal((L, N_EXPERTS, D, 2 * D_FF), D ** -0.5),
        'exp_b1': normal((L, N_EXPERTS, 2 * D_FF), 0.02),
        'exp_w2': normal((L, N_EXPERTS, D_FF, D), DEEPNORM_BETA * D_FF ** -0.5),
        'exp_b2': normal((L, N_EXPERTS, D), 0.02),
    }


def reference(x, c, ctx, c_ctx, w_ada, b_ada, w_in, na_rpb, df_lam, df_subln, gq_qnorm, gq_knorm,
              ml_qa_norm, ml_wq_b, ml_kva_norm, ml_wkv_b, w_branch, w_out, ln1_g, ln1_b, ln2_g, ln2_b,
              router_w, router_b, exp_w1, exp_b1, exp_w2, exp_b2):
    B, S, D = x.shape
    rows = S // GRID_W
    t = jnp.arange(S, dtype=jnp.int32)
    pos = (t // GRID_W, t % GRID_W)
    xl, xc = x, ctx
    for l in range(DEPTH):
        last = l == DEPTH - 1
        mod_l = jax.nn.silu(c) @ w_ada[l] + b_ada[l]
        mod_c = jax.nn.silu(c_ctx) @ w_ada[l] + b_ada[l]
        sh1, sc1, g1, sh2, sc2, g2 = [m[:, None, :] for m in jnp.split(mod_l, 6, axis=-1)]
        csh1, csc1, cg1, csh2, csc2, cg2 = jnp.split(mod_c, 6, axis=-1)
        lam_init = 0.8 - 0.6 * math.exp(-0.3 * l)
        lp = df_lam[l].astype(jnp.float32)
        lam = jnp.exp(jnp.sum(lp[0] * lp[1])) - jnp.exp(jnp.sum(lp[2] * lp[3])) + lam_init
        proj_w = (w_in[l], gq_qnorm[l], gq_knorm[l], ml_qa_norm[l], ml_wq_b[l], ml_kva_norm[l], ml_wkv_b[l])
        lat = project_tokens(xl * (1.0 + sc1) + sh1, pos, *proj_w)
        cx = project_tokens(xc * (1.0 + csc1) + csh1, None, *proj_w)
        y_l = latent_mix(lat, cx, rows, lam, lam_init, na_rpb[l], df_subln[l], w_branch[l], w_out[l])
        xl = layer_norm(DEEPNORM_ALPHA * xl + g1 * y_l, ln1_g[l], ln1_b[l])
        h_l = (xl * (1.0 + sc2) + sh2).reshape(B * S, D)
        moe_w = (router_w[l], router_b[l], exp_w1[l], exp_b1[l], exp_w2[l], exp_b2[l])
        if last:
            y2 = moe(h_l, *moe_w)
        else:
            y_c = context_mix(cx, lam, lam_init, df_subln[l], w_branch[l], w_out[l])
            xc = layer_norm(DEEPNORM_ALPHA * xc + cg1 * y_c, ln1_g[l], ln1_b[l])
            h_c = (xc * (1.0 + csc2) + csh2).reshape(-1, D)
            y2 = moe(jnp.concatenate([h_l, h_c], axis=0), *moe_w)
            xc = layer_norm(DEEPNORM_ALPHA * xc + cg2 * y2[B * S:].reshape(xc.shape), ln2_g[l], ln2_b[l])
        xl = layer_norm(DEEPNORM_ALPHA * xl + g2 * y2[: B * S].reshape(B, S, D), ln2_g[l], ln2_b[l])
    return xl
```

```python
import functools
import math

import numpy as np
import jax
import jax.numpy as jnp
from jax import lax
from jax.experimental import pallas as pl
from jax.experimental.pallas import tpu as pltpu

F32, BF16, I32 = jnp.float32, jnp.bfloat16, jnp.int32

D_MODEL = 1024
DEPTH = 2
GRID_W = 64
ROPE_BASE = 10000.0
EPS = 1e-6
NA_HEADS, NA_DIM, NA_KH, NA_KW = 4, 64, 8, 16
DF_HEADS, DF_QK = 4, 32
DF_V = 2 * DF_QK
GQ_HEADS, GQ_KV_HEADS, GQ_DIM = 4, 2, 64
ML_HEADS, ML_NOPE, ML_ROPE, ML_V, ML_Q_RANK, ML_KV_RANK = 4, 64, 32, 64, 256, 128
N_BRANCH, BRANCH_W = 4, 256
N_EXPERTS, TOP_K = 32, 4
D_FF = D_MODEL
SWIGLU_LIMIT, SWIGLU_ALPHA = 7.0, 1.702
DEEPNORM_ALPHA = (2 * DEPTH) ** 0.25

IN_SIZES = (256, 256, 256, 256, 256, 256, 256, 128, 128, ML_Q_RANK, ML_KV_RANK, ML_ROPE, N_BRANCH * D_MODEL)
IN_OFFSETS = tuple(int(o) for o in np.cumsum((0,) + IN_SIZES)[:-1])
(O_NAQ, O_NAK, O_NAV, O_DFQ, O_DFK, O_DFV, O_GQQ, O_GQK, O_GQV, O_MLQA, O_MLKVA, O_MLKR, O_GATES) = IN_OFFSETS

LANES = 128
ROW_TILE = 256
MASK_VALUE = -1e30
VMEM_LIMIT = 56 * 1024 * 1024

_PROJ_GROUPS = (("na_q", 256), ("na_k", 256), ("na_v", 256),
                ("df_q", 256), ("df_q_sw", 256), ("df_k", 256), ("df_k_sw", 256), ("df_v", 256),
                ("gq_q", 256), ("gq_q_sw", 256), ("gq_k", 128), ("gq_k_sw", 128), ("gq_v", 128),
                ("ml_qa", 256), ("ml_kva", 128), ("ml_kr", 32), ("ml_kr_sw", 32))
_PROJ_OFF = {}
_o = 0
for _n, _r in _PROJ_GROUPS:
    _PROJ_OFF[_n] = (_o, _r)
    _o += _r
PROJ_ROWS = _o


def _cparams(sem):
    return pltpu.CompilerParams(dimension_semantics=sem, vmem_limit_bytes=VMEM_LIMIT)


def _swap_perm(n):
    h, q = n // 2, n // 4
    src = np.zeros(n, np.int64)
    sign = np.zeros(n, np.float32)
    for half in range(2):
        b = half * h
        for j in range(q):
            src[b + j], sign[b + j] = b + j + q, -1.0
            src[b + j + q], sign[b + j + q] = b + j, 1.0
    return src, sign


def _block_perm(width, n):
    src, sign = _swap_perm(n)
    cols = np.concatenate([blk * n + src for blk in range(width // n)])
    signs = np.tile(sign, width // n)
    return cols, signs


def _rope_tables(n, ctx_len, seq):
    h, q = n // 2, n // 4
    t = jnp.arange(seq, dtype=I32)
    rows, cols = (t // GRID_W).astype(F32), (t % GRID_W).astype(F32)
    inv_freq = ROPE_BASE ** (-jnp.arange(0, h, 2, dtype=F32) / h)
    ang_r = rows[:, None] * inv_freq[None, :]
    ang_c = cols[:, None] * inv_freq[None, :]
    ang = jnp.concatenate([ang_r, ang_r, ang_c, ang_c], axis=1)
    cos = jnp.concatenate([jnp.ones((ctx_len, n), F32), jnp.cos(ang)], axis=0).T
    sin = jnp.concatenate([jnp.zeros((ctx_len, n), F32), jnp.sin(ang)], axis=0).T
    return jnp.stack([cos, sin])


def _ada_kernel(c_ref, w_ref, b_ref, o_ref):
    cc = c_ref[...]
    h = (cc * jax.nn.sigmoid(cc)).astype(BF16)
    o_ref[0] = jnp.dot(h, w_ref[0].astype(BF16), preferred_element_type=F32) + b_ref[0]


def _ada(cc, w_ada, b_ada):
    L, D, N = w_ada.shape
    R = cc.shape[0]
    tn = 512
    return pl.pallas_call(
        _ada_kernel,
        out_shape=jax.ShapeDtypeStruct((L, R, N), F32),
        grid=(L, N // tn),
        in_specs=[pl.BlockSpec((R, D), lambda l, j: (0, 0)),
                  pl.BlockSpec((1, D, tn), lambda l, j: (l, 0, j)),
                  pl.BlockSpec((1, 1, tn), lambda l, j: (l, 0, j))],
        out_specs=pl.BlockSpec((1, R, tn), lambda l, j: (l, 0, j)),
        compiler_params=_cparams(("arbitrary", "arbitrary")),
        name="ada_mod",
    )(cc, w_ada, b_ada.reshape(L, 1, N))


def _proj_kernel(x_ref, sc_ref, sh_ref, w_ref, cs32_ref, cs64_ref, gqg_ref, qag_ref, kvag_ref, wq_ref, wkv_ref,
                 na_q, na_k, na_v, df_q, df_k, df_v, gq_q, gq_k, gq_v, ml_q, ml_k, ml_v):
    tm = x_ref.shape[1]
    h = (x_ref[0] * (1.0 + sc_ref[...]) + sh_ref[...]).astype(BF16)
    p = lax.dot_general(w_ref[...], h, (((1,), (1,)), ((), ())), preferred_element_type=F32)

    def grp(name):
        o, r = _PROJ_OFF[name]
        return p[o:o + r]

    def zeros(n):
        return jnp.zeros((n, tm), F32)

    def rope(a, a_sw, cs_ref):
        n = cs_ref.shape[1]
        r = a.shape[0]
        a3, s3 = a.reshape(r // n, n, tm), a_sw.reshape(r // n, n, tm)
        return (a3 * cs_ref[0][None] + s3 * cs_ref[1][None]).reshape(r, tm)

    def head_slots(a, width):
        pieces = []
        for i in range(a.shape[0] // width):
            pieces += [a[i * width:(i + 1) * width], zeros(LANES - width)]
        return jnp.concatenate(pieces, axis=0)

    na_q[0] = head_slots(grp("na_q") * (NA_DIM ** -0.5), NA_DIM).astype(BF16)
    na_k[0] = head_slots(grp("na_k"), NA_DIM).T.astype(BF16)
    na_v[0] = grp("na_v").astype(BF16)

    q = rope(grp("df_q"), grp("df_q_sw"), cs32_ref) * (DF_QK ** -0.5)
    pieces = []
    for i in range(DF_HEADS):
        q1, q2 = q[i * 64:i * 64 + 32], q[i * 64 + 32:i * 64 + 64]
        pieces += [q1, zeros(LANES - 32), zeros(32), q2, zeros(LANES - 64)]
    df_q[0] = jnp.concatenate(pieces, axis=0).astype(BF16)
    df_k[0] = head_slots(rope(grp("df_k"), grp("df_k_sw"), cs32_ref), 2 * DF_QK).T.astype(BF16)
    df_v[0] = grp("df_v").astype(BF16)

    cos64, sin64 = cs64_ref[0], cs64_ref[1]

    def norm_rope(a, a_sw, g, g_sw, scale):
        hh = a.shape[0] // GQ_DIM
        a3, s3 = a.reshape(hh, GQ_DIM, tm), a_sw.reshape(hh, GQ_DIM, tm)
        r = lax.rsqrt(jnp.mean(a3 * a3, axis=1, keepdims=True) + EPS) * scale
        return ((a3 * (g * cos64)[None] + s3 * (g_sw * sin64)[None]) * r).reshape(hh * GQ_DIM, tm)

    gq_q[0] = head_slots(norm_rope(grp("gq_q"), grp("gq_q_sw"), gqg_ref[0], gqg_ref[1], GQ_DIM ** -0.5),
                         GQ_DIM).astype(BF16)
    gq_k[0] = head_slots(norm_rope(grp("gq_k"), grp("gq_k_sw"), gqg_ref[2], gqg_ref[3], 1.0),
                         GQ_DIM).T.astype(BF16)
    gq_v[0] = grp("gq_v").astype(BF16)

    qa = grp("ml_qa")
    qan = (qa * lax.rsqrt(jnp.mean(qa * qa, axis=0, keepdims=True) + EPS) * qag_ref[...]).astype(BF16)
    qq = jnp.dot(wq_ref[...], qan, preferred_element_type=F32)
    kva = grp("ml_kva")
    kvan = (kva * lax.rsqrt(jnp.mean(kva * kva, axis=0, keepdims=True) + EPS) * kvag_ref[...]).astype(BF16)
    kv = jnp.dot(wkv_ref[...], kvan, preferred_element_type=F32)
    k_rope = rope(grp("ml_kr"), grp("ml_kr_sw"), cs32_ref)
    cos32, sin32 = cs32_ref[0], cs32_ref[1]
    ml_scale = (ML_NOPE + ML_ROPE) ** -0.5
    qp, kp, vp = [], [], []
    for i in range(ML_HEADS):
        b = i * LANES
        q_rope = qq[b + 64:b + 96] * cos32 + qq[b + 96:b + 128] * sin32
        qp += [qq[b:b + 64] * ml_scale, q_rope * ml_scale, zeros(32)]
        kp += [kv[b:b + 64], k_rope, zeros(32)]
        vp += [kv[b + 64:b + 128]]
    ml_q[0] = jnp.concatenate(qp, axis=0).astype(BF16)
    ml_k[0] = jnp.concatenate(kp, axis=0).T.astype(BF16)
    ml_v[0] = jnp.concatenate(vp, axis=0).astype(BF16)


def _project(xa, mod, w_t, cs32, cs64, gq_g, qa_g, kva_g, wq_t, wkv_t, ctx_len):
    B, T, D = xa.shape
    tm = ROW_TILE
    nt = T // tm
    ctx_tiles = ctx_len // tm

    def kind(t):
        return jnp.where(t >= ctx_tiles, 1, 0)

    def mod_spec(j):
        return pl.BlockSpec((None, None, None, 1, D), lambda b, t: (b, kind(t), j, 0, 0))

    def const(a):
        nd = a.ndim
        return pl.BlockSpec(a.shape, lambda b, t: (0,) * nd)

    def fm(rows):
        return jax.ShapeDtypeStruct((B, rows, T), BF16), pl.BlockSpec((1, rows, tm), lambda b, t: (b, 0, t))

    def tmj(cols):
        return jax.ShapeDtypeStruct((B, T, cols), BF16), pl.BlockSpec((1, tm, cols), lambda b, t: (b, t, 0))

    outs = [fm(512), tmj(512), fm(256),
            fm(1024), tmj(512), fm(256),
            fm(512), tmj(256), fm(128),
            fm(512), tmj(512), fm(256)]
    return pl.pallas_call(
        _proj_kernel,
        out_shape=[o[0] for o in outs],
        grid=(B, nt),
        in_specs=[pl.BlockSpec((1, tm, D), lambda b, t: (b, t, 0)),
                  mod_spec(1), mod_spec(0),
                  const(w_t),
                  pl.BlockSpec((2, 32, tm), lambda b, t: (0, 0, t)),
                  pl.BlockSpec((2, 64, tm), lambda b, t: (0, 0, t)),
                  const(gq_g), const(qa_g), const(kva_g), const(wq_t), const(wkv_t)],
        out_specs=[o[1] for o in outs],
        compiler_params=_cparams(("arbitrary", "arbitrary")),
        name="in_proj",
    )(xa, mod, mod, w_t, cs32, cs64, gq_g, qa_g, kva_g, wq_t, wkv_t)


def _softmax_pv(k, q_t, v_t):
    s = jnp.dot(k, q_t, preferred_element_type=F32)
    m = jnp.max(s, axis=0, keepdims=True)
    p = jnp.exp(s - m)
    l = jnp.sum(p, axis=0, keepdims=True)
    o = jnp.dot(v_t, p.astype(BF16), preferred_element_type=F32)
    return o, l


def _dense_attn_kernel(lam_ref, q_ref, k_ref, v_ref, post_ref, o_ref, ot_ref, *, heads, ctx_len, diff_norm):
    total = k_ref.shape[1]

    def run(n_keys):
        for i, (terms, v_idx) in enumerate(heads):
            v_t = v_ref[0, v_idx * 64:(v_idx + 1) * 64, 0:n_keys]
            acc = None
            for q_slot, k_slot, signed in terms:
                k = k_ref[0, 0:n_keys, k_slot * LANES:(k_slot + 1) * LANES]
                q_t = q_ref[0, q_slot * LANES:(q_slot + 1) * LANES, :]
                o, l = _softmax_pv(k, q_t, v_t)
                coef = (-lam_ref[0]) if signed else 1.0
                o = o * (coef / l)
                acc = o if acc is None else acc + o
            if diff_norm:
                acc = acc * lax.rsqrt(jnp.mean(acc * acc, axis=0, keepdims=True) + EPS) * post_ref[...]
            ot_ref[i * 64:(i + 1) * 64, :] = acc
        o_ref[0] = ot_ref[...].T.astype(BF16)

    qb = pl.program_id(1)

    @pl.when(qb * q_ref.shape[2] < ctx_len)
    def _():
        run(ctx_len)

    @pl.when(qb * q_ref.shape[2] >= ctx_len)
    def _():
        run(total)


def _dense_attn(lam, q_t, k, v_t, post, heads, ctx_len, diff_norm, name):
    B, qrows, T = q_t.shape
    tq = ROW_TILE
    kern = functools.partial(_dense_attn_kernel, heads=heads, ctx_len=ctx_len, diff_norm=diff_norm)
    return pl.pallas_call(
        kern,
        out_shape=jax.ShapeDtypeStruct((B, T, BRANCH_W), BF16),
        grid=(B, T // tq),
        in_specs=[pl.BlockSpec(memory_space=pltpu.SMEM),
                  pl.BlockSpec((1, qrows, tq), lambda b, t: (b, 0, t)),
                  pl.BlockSpec((1, T, k.shape[2]), lambda b, t: (b, 0, 0)),
                  pl.BlockSpec((1, v_t.shape[1], T), lambda b, t: (b, 0, 0)),
                  pl.BlockSpec(post.shape, lambda b, t: (0, 0))],
        out_specs=pl.BlockSpec((1, tq, BRANCH_W), lambda b, t: (b, t, 0)),
        scratch_shapes=[pltpu.VMEM((BRANCH_W, tq), F32)],
        compiler_params=_cparams(("arbitrary", "arbitrary")),
        name=name,
    )(lam, q_t, k, v_t, post)


def _na_plan(rows):
    kh, kw = min(NA_KH, rows), NA_KW
    q_rows = ROW_TILE // GRID_W
    win_rows = min(kh + q_rows, rows)
    nblk = rows // q_rows
    row_start = np.clip(np.arange(rows) - kh // 2, 0, rows - kh)
    col_start = np.clip(np.arange(GRID_W) - kw // 2, 0, GRID_W - kw)
    u0 = np.clip(np.arange(nblk) * q_rows - kh // 2, 0, rows - win_rows)
    wk, tq = win_rows * GRID_W, ROW_TILE
    kk, qq = np.arange(wk), np.arange(tq)
    kc, qc = kk % GRID_W, qq % GRID_W
    types, type_of, ri_l, ci_l, valid_l = {}, [], [], [], []
    for j in range(nblk):
        kr = u0[j] + kk // GRID_W
        qr = j * q_rows + qq // GRID_W
        rs = row_start[qr]
        valid = ((kr[:, None] >= rs[None, :]) & (kr[:, None] < rs[None, :] + kh)
                 & (kc[:, None] >= col_start[qc][None, :]) & (kc[:, None] < col_start[qc][None, :] + kw))
        assert valid.sum(axis=0).min() == kh * kw and valid.sum(axis=0).max() == kh * kw
        ri = np.clip(kr[:, None] - qr[None, :] + (NA_KH - 1), 0, 2 * NA_KH - 2)
        ci = np.clip(kc[:, None] - qc[None, :] + (NA_KW - 1), 0, 2 * NA_KW - 2)
        key = (valid.tobytes(), ri.tobytes())
        if key not in types:
            types[key] = len(ri_l)
            ri_l.append(ri), ci_l.append(ci), valid_l.append(valid)
        type_of.append(types[key])
    assert all(int(u) * GRID_W % LANES == 0 for u in u0)
    return (np.asarray(u0, np.int32), np.asarray(type_of, np.int32), np.stack(ri_l), np.stack(ci_l),
            np.stack(valid_l), wk)


def _na_kernel(u0_ref, ty_ref, q_ref, k_ref, v_ref, bias_ref, o_ref, ot_ref, *, ctx_len):
    del ty_ref
    wk = bias_ref.shape[2]
    j = pl.program_id(1)

    def finish():
        o_ref[0] = ot_ref[...].T.astype(BF16)

    @pl.when(j == 0)
    def _():
        for i in range(NA_HEADS):
            o, l = _softmax_pv(k_ref[0, 0:ctx_len, i * LANES:(i + 1) * LANES], q_ref[0, i * LANES:(i + 1) * LANES, :],
                               v_ref[0, i * 64:(i + 1) * 64, 0:ctx_len])
            ot_ref[i * 64:(i + 1) * 64, :] = o * (1.0 / l)
        finish()

    @pl.when(j > 0)
    def _():
        ws = pl.multiple_of(ctx_len + u0_ref[j - 1] * GRID_W, LANES)
        for i in range(NA_HEADS):
            q_t = q_ref[0, i * LANES:(i + 1) * LANES, :]
            s_c = jnp.dot(k_ref[0, 0:ctx_len, i * LANES:(i + 1) * LANES], q_t, preferred_element_type=F32)
            s_w = jnp.dot(k_ref[0, pl.ds(ws, wk), i * LANES:(i + 1) * LANES], q_t,
                          preferred_element_type=F32) + bias_ref[0, i]
            m = jnp.maximum(jnp.max(s_c, axis=0, keepdims=True), jnp.max(s_w, axis=0, keepdims=True))
            p_c, p_w = jnp.exp(s_c - m), jnp.exp(s_w - m)
            l = jnp.sum(p_c, axis=0, keepdims=True) + jnp.sum(p_w, axis=0, keepdims=True)
            o = (jnp.dot(v_ref[0, i * 64:(i + 1) * 64, 0:ctx_len], p_c.astype(BF16), preferred_element_type=F32)
                 + jnp.dot(v_ref[0, i * 64:(i + 1) * 64, pl.ds(ws, wk)], p_w.astype(BF16),
                           preferred_element_type=F32))
            ot_ref[i * 64:(i + 1) * 64, :] = o * (1.0 / l)
        finish()


def _na_attn(q_t, k, v_t, rpb, ctx_len):
    B, _, T = q_t.shape
    tq = ROW_TILE
    rows = (T - ctx_len) // GRID_W
    u0, type_of, ri, ci, valid, wk = _na_plan(rows)
    bias = jnp.where(valid[:, None], rpb[:, ri, ci].transpose(1, 0, 2, 3), MASK_VALUE).astype(F32)
    nb = T // tq
    assert ctx_len == tq

    grid_spec = pltpu.PrefetchScalarGridSpec(
        num_scalar_prefetch=2,
        grid=(B, nb),
        in_specs=[pl.BlockSpec((1, NA_HEADS * LANES, tq), lambda b, j, u, ty: (b, 0, j)),
                  pl.BlockSpec((1, T, NA_HEADS * LANES), lambda b, j, u, ty: (b, 0, 0)),
                  pl.BlockSpec((1, BRANCH_W, T), lambda b, j, u, ty: (b, 0, 0)),
                  pl.BlockSpec((1, NA_HEADS, wk, tq), lambda b, j, u, ty: (ty[jnp.maximum(j - 1, 0)], 0, 0, 0))],
        out_specs=pl.BlockSpec((1, tq, BRANCH_W), lambda b, j, u, ty: (b, j, 0)),
        scratch_shapes=[pltpu.VMEM((BRANCH_W, tq), F32)])
    return pl.pallas_call(
        functools.partial(_na_kernel, ctx_len=ctx_len),
        out_shape=jax.ShapeDtypeStruct((B, T, BRANCH_W), BF16),
        grid_spec=grid_spec,
        compiler_params=_cparams(("arbitrary", "arbitrary")),
        name="na_attn",
    )(jnp.asarray(u0), jnp.asarray(type_of), q_t, k, v_t, bias)


def _merge_kernel(x_ref, sc1_ref, sh1_ref, g1_ref, sc2_ref, sh2_ref, ona_ref, odf_ref, ogq_ref, oml_ref,
                  wg_ref, wb_ref, wo_ref, lng_ref, lnb_ref, rw_ref, rb_ref, tri_ref,
                  x1_ref, h2_ref, idx_ref, wts_ref, rank_ref, cnt_ref, carry_ref):
    tm = x_ref.shape[1]
    first = (pl.program_id(0) == 0) & (pl.program_id(1) == 0)

    @pl.when(first)
    def _():
        carry_ref[...] = jnp.zeros_like(carry_ref)

    x = x_ref[0]
    h1 = (x * (1.0 + sc1_ref[...]) + sh1_ref[...]).astype(BF16)
    y = None
    for i, o_ref in enumerate((ona_ref, odf_ref, ogq_ref, oml_ref)):
        gate = jax.nn.sigmoid(jnp.dot(h1, wg_ref[:, i * D_MODEL:(i + 1) * D_MODEL], preferred_element_type=F32))
        term = gate * jnp.dot(o_ref[0], wb_ref[i * BRANCH_W:(i + 1) * BRANCH_W, :], preferred_element_type=F32)
        y = term if y is None else y + term
    z = jnp.dot(y.astype(BF16), wo_ref[...], preferred_element_type=F32)
    r = DEEPNORM_ALPHA * x + g1_ref[...] * z
    mu = jnp.mean(r, axis=-1, keepdims=True)
    var = jnp.mean(jnp.square(r - mu), axis=-1, keepdims=True)
    x1 = (r - mu) * lax.rsqrt(var + EPS) * lng_ref[...] + lnb_ref[...]
    x1_ref[0] = x1
    h2 = x1 * (1.0 + sc2_ref[...]) + sh2_ref[...]
    for j in range(D_MODEL // LANES):
        h2_ref[:, j, :] = h2[:, j * LANES:(j + 1) * LANES]

    logits = lax.dot_general(rw_ref[...], h2, (((1,), (1,)), ((), ())), preferred_element_type=F32,
                             precision=lax.Precision.HIGHEST) + rb_ref[...]
    iota = lax.broadcasted_iota(I32, (N_EXPERTS, tm), 0)
    vals, idxs = [], []
    cur = logits
    for _ in range(TOP_K):
        m = jnp.max(cur, axis=0, keepdims=True)
        ik = jnp.min(jnp.where(cur == m, iota, N_EXPERTS), axis=0, keepdims=True)
        vals.append(m)
        idxs.append(ik)
        cur = jnp.where(iota == ik, -jnp.inf, cur)
    exps = [jnp.exp(v - vals[0]) for v in vals]
    denom = exps[0] + exps[1] + exps[2] + exps[3]
    wts_ref[0] = jnp.concatenate([e / denom for e in exps], axis=0)
    idx_ref[0] = jnp.concatenate(idxs, axis=0)

    onehot = jnp.zeros((N_EXPERTS, tm), F32)
    for ik in idxs:
        onehot = onehot + jnp.where(iota == ik, 1.0, 0.0)
    before = jnp.dot(onehot.astype(BF16), tri_ref[...], preferred_element_type=F32) + carry_ref[...]
    rank_ref[0] = jnp.concatenate(
        [jnp.sum(jnp.where(iota == ik, before, 0.0), axis=0, keepdims=True) for ik in idxs], axis=0).astype(I32)
    carry = carry_ref[...] + jnp.sum(onehot, axis=1, keepdims=True)
    carry_ref[...] = carry
    cnt_ref[...] = carry.astype(I32)


def _merge(xa, mod, o_na, o_df, o_gq, o_ml, wg, wb, wo, ln_g, ln_b, rw_t, rb, ctx_len, t_off):
    B, T, D = xa.shape
    tm = ROW_TILE
    nt = T // tm - t_off
    ctx_tiles = ctx_len // tm
    tq_rows = nt * tm

    def kind(t):
        return jnp.where(t + t_off >= ctx_tiles, 1, 0)

    def mod_spec(j):
        return pl.BlockSpec((None, None, None, 1, D), lambda b, t: (b, kind(t), j, 0, 0))

    def const(a):
        nd = a.ndim
        return pl.BlockSpec(a.shape, lambda b, t: (0,) * nd)

    def tok(cols):
        return pl.BlockSpec((1, tm, cols), lambda b, t: (b, t + t_off, 0))

    tri = jnp.asarray(np.triu(np.ones((tm, tm), np.float32), 1), BF16)
    rb_b = jnp.broadcast_to(rb[:, None], (N_EXPERTS, tm)).astype(F32)
    n_tiles = B * nt
    route_shape = jax.ShapeDtypeStruct((n_tiles, TOP_K, tm), I32)
    route_spec = pl.BlockSpec((1, TOP_K, tm), lambda b, t: (b * nt + t, 0, 0))
    return pl.pallas_call(
        _merge_kernel,
        out_shape=[jax.ShapeDtypeStruct((B, tq_rows, D), F32),
                   jax.ShapeDtypeStruct((B * tq_rows, D // LANES, LANES), F32),
                   route_shape, jax.ShapeDtypeStruct((n_tiles, TOP_K, tm), F32), route_shape,
                   jax.ShapeDtypeStruct((N_EXPERTS, tm), I32)],
        grid=(B, nt),
        in_specs=[tok(D), mod_spec(1), mod_spec(0), mod_spec(2), mod_spec(4), mod_spec(3),
                  tok(BRANCH_W), tok(BRANCH_W), tok(BRANCH_W), tok(BRANCH_W),
                  const(wg), const(wb), const(wo), const(ln_g), const(ln_b), const(rw_t), const(rb_b), const(tri)],
        out_specs=[pl.BlockSpec((1, tm, D), lambda b, t: (b, t, 0)),
                   pl.BlockSpec((tm, D // LANES, LANES), lambda b, t: (b * nt + t, 0, 0)),
                   route_spec, route_spec, route_spec,
                   pl.BlockSpec((N_EXPERTS, tm), lambda b, t: (0, 0))],
        scratch_shapes=[pltpu.VMEM((N_EXPERTS, tm), F32)],
        compiler_params=_cparams(("arbitrary", "arbitrary")),
        name="merge_router",
    )(xa, mod, mod, mod, mod, mod, o_na, o_df, o_gq, o_ml, wg, wb, wo, ln_g, ln_b, rw_t, rb_b, tri)


def _row_copy(src, src_row, dst, dst_row, sem):
    return pltpu.make_async_copy(src.at[pl.ds(src_row, 1)], dst.at[pl.ds(dst_row, 1)], sem)


def _dispatch_kernel(starts_ref, idx_ref, rank_ref, h_ref, xs_in_ref, xs_ref, sem):
    del xs_in_ref
    tm = idx_ref.shape[2]
    base = pl.program_id(0) * tm

    def issue(t, c):
        for k in range(TOP_K):
            d = starts_ref[idx_ref[0, k, t]] + rank_ref[0, k, t]
            _row_copy(h_ref, base + t, xs_ref, d, sem).start()
        return c

    lax.fori_loop(0, tm, issue, 0)

    def drain(t, c):
        for k in range(TOP_K):
            _row_copy(h_ref, 0, xs_ref, 0, sem).wait()
        return c

    lax.fori_loop(0, tm, drain, 0)


def _dispatch(starts, idx, rank, h2, n_slots):
    n_tiles, _, tm = idx.shape
    xs0 = jnp.zeros((n_slots,) + h2.shape[1:], F32)
    smem = lambda: pl.BlockSpec((1, TOP_K, tm), lambda i, s: (i, 0, 0), memory_space=pltpu.SMEM)
    grid_spec = pltpu.PrefetchScalarGridSpec(
        num_scalar_prefetch=1, grid=(n_tiles,),
        in_specs=[smem(), smem(), pl.BlockSpec(memory_space=pl.ANY), pl.BlockSpec(memory_space=pl.ANY)],
        out_specs=pl.BlockSpec(memory_space=pl.ANY),
        scratch_shapes=[pltpu.SemaphoreType.DMA(())])
    return pl.pallas_call(
        _dispatch_kernel,
        out_shape=jax.ShapeDtypeStruct(xs0.shape, F32),
        grid_spec=grid_spec,
        input_output_aliases={4: 0},
        compiler_params=_cparams(("arbitrary",)),
        name="moe_dispatch",
    )(starts, idx, rank, h2, xs0)


def _expert_kernel(te_ref, nu_ref, xs_ref, w1_ref, b1_ref, w2_ref, b2_ref, y_ref, w1s_ref, w2s_ref):
    i = pl.program_id(0)
    changed = (i == 0) | (te_ref[i] != te_ref[jnp.maximum(i - 1, 0)])

    @pl.when(changed)
    def _():
        w1s_ref[...] = w1_ref[...].astype(BF16)
        w2s_ref[...] = w2_ref[...].astype(BF16)

    @pl.when(i < nu_ref[0])
    def _():
        x = jnp.concatenate([xs_ref[:, j, :] for j in range(D_MODEL // LANES)], axis=1).astype(BF16)
        gu = jnp.dot(x, w1s_ref[...], preferred_element_type=F32) + b1_ref[...]
        gate = jnp.minimum(gu[:, :D_FF], SWIGLU_LIMIT)
        up = jnp.clip(gu[:, D_FF:], -SWIGLU_LIMIT, SWIGLU_LIMIT)
        act = (up + 1.0) * gate * jax.nn.sigmoid(SWIGLU_ALPHA * gate)
        y = jnp.dot(act.astype(BF16), w2s_ref[...], preferred_element_type=F32) + b2_ref[...]
        for j in range(D_MODEL // LANES):
            y_ref[:, j, :] = y[:, j * LANES:(j + 1) * LANES]

    @pl.when(i >= nu_ref[0])
    def _():
        y_ref[...] = jnp.zeros_like(y_ref)


def _experts(tile_expert, n_used, xs, w1, b1, w2, b2, layer):
    n_slots = xs.shape[0]
    tm = ROW_TILE
    nt = n_slots // tm
    E = w1.shape[1]
    grid_spec = pltpu.PrefetchScalarGridSpec(
        num_scalar_prefetch=2, grid=(nt,),
        in_specs=[pl.BlockSpec((tm, D_MODEL // LANES, LANES), lambda i, te, nu: (i, 0, 0)),
                  pl.BlockSpec((None, None, D_MODEL, 2 * D_FF), lambda i, te, nu: (layer, te[i], 0, 0)),
                  pl.BlockSpec((None, None, 1, 2 * D_FF), lambda i, te, nu: (layer, te[i], 0, 0)),
                  pl.BlockSpec((None, None, D_FF, D_MODEL), lambda i, te, nu: (layer, te[i], 0, 0)),
                  pl.BlockSpec((None, None, 1, D_MODEL), lambda i, te, nu: (layer, te[i], 0, 0))],
        out_specs=pl.BlockSpec((tm, D_MODEL // LANES, LANES), lambda i, te, nu: (i, 0, 0)),
        scratch_shapes=[pltpu.VMEM((D_MODEL, 2 * D_FF), BF16), pltpu.VMEM((D_FF, D_MODEL), BF16)])
    return pl.pallas_call(
        _expert_kernel,
        out_shape=jax.ShapeDtypeStruct(xs.shape, F32),
        grid_spec=grid_spec,
        compiler_params=_cparams(("arbitrary",)),
        name="moe_experts",
    )(tile_expert, n_used, xs, w1, b1.reshape(b1.shape[0], E, 1, -1), w2, b2.reshape(b2.shape[0], E, 1, -1))


def _combine_kernel(starts_ref, idx_ref, rank_ref, idxn_ref, rankn_ref, w_ref, x1_ref, g2_ref, lng_ref, lnb_ref,
                    y_ref, o_ref, buf_ref, sem):
    tm = idx_ref.shape[2]
    i = pl.program_id(0)
    n = pl.num_programs(0)
    slot = lax.rem(i, 2)

    def issue(ir, rr, s):
        def body(t, c):
            for k in range(TOP_K):
                d = starts_ref[ir[0, k, t]] + rr[0, k, t]
                pltpu.make_async_copy(y_ref.at[pl.ds(d, 1)], buf_ref.at[s, k, pl.ds(t, 1)], sem.at[s]).start()
            return c
        lax.fori_loop(0, tm, body, 0)

    @pl.when(i == 0)
    def _():
        issue(idx_ref, rank_ref, 0)

    @pl.when(i + 1 < n)
    def _():
        issue(idxn_ref, rankn_ref, 1 - slot)

    def drain(t, c):
        for k in range(TOP_K):
            pltpu.make_async_copy(y_ref.at[pl.ds(0, 1)], buf_ref.at[slot, k, pl.ds(0, 1)], sem.at[slot]).wait()
        return c

    lax.fori_loop(0, tm, drain, 0)

    cols = []
    for j in range(D_MODEL // LANES):
        acc = None
        for k in range(TOP_K):
            v = buf_ref[slot, k, :, j, :] * w_ref[0, :, k:k + 1]
            acc = v if acc is None else acc + v
        cols.append(acc)
    y2 = jnp.concatenate(cols, axis=1)
    r = DEEPNORM_ALPHA * x1_ref[0] + g2_ref[...] * y2
    mu = jnp.mean(r, axis=-1, keepdims=True)
    var = jnp.mean(jnp.square(r - mu), axis=-1, keepdims=True)
    o_ref[0] = (r - mu) * lax.rsqrt(var + EPS) * lng_ref[...] + lnb_ref[...]


def _combine(starts, idx, rank, wts_tm, x1, mod, ln_g, ln_b, y, ctx_len, t_off):
    B, tq_rows, D = x1.shape
    n_tiles, _, tm = idx.shape
    nt = n_tiles // B
    ctx_tiles = ctx_len // tm

    def kind(i):
        return jnp.where(i % nt + t_off >= ctx_tiles, 1, 0)

    smem = lambda f: pl.BlockSpec((1, TOP_K, tm), f, memory_space=pltpu.SMEM)
    cur = lambda i, s: (i, 0, 0)
    nxt = lambda i, s: (jnp.minimum(i + 1, n_tiles - 1), 0, 0)
    grid_spec = pltpu.PrefetchScalarGridSpec(
        num_scalar_prefetch=1, grid=(n_tiles,),
        in_specs=[smem(cur), smem(cur), smem(nxt), smem(nxt),
                  pl.BlockSpec((1, tm, TOP_K), lambda i, s: (i, 0, 0)),
                  pl.BlockSpec((1, tm, D), lambda i, s: (i // nt, i % nt, 0)),
                  pl.BlockSpec((None, None, None, 1, D), lambda i, s: (i // nt, kind(i), 5, 0, 0)),
                  pl.BlockSpec(ln_g.shape, lambda i, s: (0, 0)),
                  pl.BlockSpec(ln_b.shape, lambda i, s: (0, 0)),
                  pl.BlockSpec(memory_space=pl.ANY)],
        out_specs=pl.BlockSpec((1, tm, D), lambda i, s: (i // nt, i % nt, 0)),
        scratch_shapes=[pltpu.VMEM((2, TOP_K, tm, D // LANES, LANES), F32), pltpu.SemaphoreType.DMA((2,))])
    return pl.pallas_call(
        _combine_kernel,
        out_shape=jax.ShapeDtypeStruct((B, tq_rows, D), F32),
        grid_spec=grid_spec,
        compiler_params=_cparams(("arbitrary",)),
        name="moe_combine",
    )(starts, idx, rank, idx, rank, wts_tm, x1, mod, ln_g, ln_b, y)


_DF_HEADS_SPEC = tuple((((2 * i, i, False), (2 * i + 1, i, True)), i) for i in range(DF_HEADS))
_GQ_HEADS_SPEC = tuple((((i, i // (GQ_HEADS // GQ_KV_HEADS), False),), i // (GQ_HEADS // GQ_KV_HEADS))
                       for i in range(GQ_HEADS))
_ML_HEADS_SPEC = tuple((((i, i, False),), i) for i in range(ML_HEADS))


def _layer_weights(l, w_in, gq_qnorm, gq_knorm, ml_qa_norm, ml_wq_b, ml_kva_norm, ml_wkv_b):
    w = w_in[l]
    p32, s32 = _block_perm(256, 32)
    p64, s64 = _block_perm(256, 64)
    pk64, sk64 = _block_perm(128, 64)
    pr32, sr32 = _swap_perm(32)

    def cols(o, n):
        return w[:, o:o + n]

    def cols_sw(o, perm, sign):
        return w[:, o + perm] * sign[None, :]

    parts = [cols(O_NAQ, 256), cols(O_NAK, 256), cols(O_NAV, 256),
             cols(O_DFQ, 256), cols_sw(O_DFQ, p32, s32), cols(O_DFK, 256), cols_sw(O_DFK, p32, s32), cols(O_DFV, 256),
             cols(O_GQQ, 256), cols_sw(O_GQQ, p64, s64), cols(O_GQK, 128), cols_sw(O_GQK, pk64, sk64),
             cols(O_GQV, 128),
             cols(O_MLQA, 256), cols(O_MLKVA, 128), cols(O_MLKR, 32), cols_sw(O_MLKR, pr32, sr32)]
    w_t = jnp.concatenate(parts, axis=1).T.astype(BF16)
    assert w_t.shape[0] == PROJ_ROWS

    src64, _ = _swap_perm(64)
    tm = ROW_TILE
    gq_g = jnp.stack([gq_qnorm[l], gq_qnorm[l][src64], gq_knorm[l], gq_knorm[l][src64]])
    gq_g = jnp.broadcast_to(gq_g[:, :, None], (4, GQ_DIM, tm)).astype(F32)
    qa_g = jnp.broadcast_to(ml_qa_norm[l][:, None], (ML_Q_RANK, tm)).astype(F32)
    kva_g = jnp.broadcast_to(ml_kva_norm[l][:, None], (ML_KV_RANK, tm)).astype(F32)
    wq = ml_wq_b[l]
    qd = ML_NOPE + ML_ROPE
    wq_parts = []
    for i in range(ML_HEADS):
        rope_cols = wq[:, i * qd + ML_NOPE:(i + 1) * qd]
        wq_parts += [wq[:, i * qd:i * qd + ML_NOPE], rope_cols, rope_cols[:, pr32] * sr32[None, :]]
    wq_t = jnp.concatenate(wq_parts, axis=1).T.astype(BF16)
    wkv_t = ml_wkv_b[l].T.astype(BF16)
    return w_t, gq_g, qa_g, kva_g, wq_t, wkv_t


def kernel(x, c, ctx, c_ctx, w_ada, b_ada, w_in, na_rpb, df_lam, df_subln, gq_qnorm, gq_knorm, ml_qa_norm, ml_wq_b,
           ml_kva_norm, ml_wkv_b, w_branch, w_out, ln1_g, ln1_b, ln2_g, ln2_b, router_w, router_b, exp_w1, exp_b1,
           exp_w2, exp_b2):
    B, S, D = x.shape
    C = ctx.shape[1]
    T = C + S
    tm = ROW_TILE
    assert D == D_MODEL and C % tm == 0 and S % tm == 0 and C == tm

    xa = jnp.concatenate([ctx, x], axis=1)
    cc = jnp.zeros((16, D), F32).at[:B].set(c).at[B].set(c_ctx)
    mod_all = _ada(cc, w_ada, b_ada)
    cs32 = _rope_tables(32, C, S)
    cs64 = _rope_tables(64, C, S)

    for l in range(DEPTH):
        last = l == DEPTH - 1
        m = mod_all[l]
        mod = jnp.stack([jnp.broadcast_to(m[B][None], (B, 6 * D)), m[:B]], axis=1).reshape(B, 2, 6, 1, D)
        lam_init = 0.8 - 0.6 * math.exp(-0.3 * l)
        lp = df_lam[l].astype(F32)
        lam = (jnp.exp(jnp.sum(lp[0] * lp[1])) - jnp.exp(jnp.sum(lp[2] * lp[3])) + lam_init).reshape(1)
        w_t, gq_g, qa_g, kva_g, wq_t, wkv_t = _layer_weights(l, w_in, gq_qnorm, gq_knorm, ml_qa_norm, ml_wq_b,
                                                             ml_kva_norm, ml_wkv_b)
        (na_q, na_k, na_v, df_q, df_k, df_v, gq_q, gq_k, gq_v, ml_q, ml_k, ml_v) = _project(
            xa, mod, w_t, cs32, cs64, gq_g, qa_g, kva_g, wq_t, wkv_t, C)

        o_na = _na_attn(na_q, na_k, na_v, na_rpb[l], C)
        subln = jnp.broadcast_to((df_subln[l] * (1.0 - lam_init))[:, None], (DF_V, tm)).astype(F32)
        ones = jnp.ones((DF_V, tm), F32)
        o_df = _dense_attn(lam, df_q, df_k, df_v, subln, _DF_HEADS_SPEC, C, True, "df_attn")
        o_gq = _dense_attn(lam, gq_q, gq_k, gq_v, ones, _GQ_HEADS_SPEC, C, False, "gq_attn")
        o_ml = _dense_attn(lam, ml_q, ml_k, ml_v, ones, _ML_HEADS_SPEC, C, False, "ml_attn")

        t_off = C // tm if last else 0
        wg = w_in[l][:, O_GATES:].astype(BF16)
        wb = w_branch[l].reshape(N_BRANCH * BRANCH_W, D).astype(BF16)
        wo = w_out[l].astype(BF16)
        x1, h2, idx, wts, rank, cnt = _merge(xa, mod, o_na, o_df, o_gq, o_ml, wg, wb, wo,
                                             ln1_g[l][None], ln1_b[l][None], router_w[l].T, router_b[l], C, t_off)

        counts = cnt[:, 0]
        padded = ((counts + tm - 1) // tm) * tm
        ends = jnp.cumsum(padded)
        starts = (ends - padded).astype(I32)
        n_tok = h2.shape[0]
        n_exp_tiles = (n_tok * TOP_K) // tm + N_EXPERTS
        n_used = (ends[-1] // tm).astype(I32).reshape(1)
        tile_first = jnp.arange(n_exp_tiles, dtype=I32) * tm
        tile_expert = jnp.minimum(jnp.searchsorted(ends, tile_first, side="right"), N_EXPERTS - 1).astype(I32)
        tile_expert = jnp.where(tile_first < ends[-1], tile_expert, tile_expert[jnp.maximum(n_used[0] - 1, 0)])

        xs = _dispatch(starts, idx, rank, h2, n_exp_tiles * tm)
        ys = _experts(tile_expert, n_used, xs, exp_w1, exp_b1, exp_w2, exp_b2, l)
        xa = _combine(starts, idx, rank, wts.transpose(0, 2, 1), x1, mod, ln2_g[l][None], ln2_b[l][None], ys, C, t_off)
    return xa
```

```python
import functools
import math

import numpy as np
import jax
import jax.numpy as jnp
from jax import lax
from jax.experimental import pallas as pl
from jax.experimental.pallas import tpu as pltpu

F32, BF16, I32 = jnp.float32, jnp.bfloat16, jnp.int32

D_MODEL = 1024
DEPTH = 2
GRID_W = 64
ROPE_BASE = 10000.0
EPS = 1e-6
NA_HEADS, NA_DIM, NA_KH, NA_KW = 4, 64, 8, 16
DF_HEADS, DF_QK = 4, 32
DF_V = 2 * DF_QK
GQ_HEADS, GQ_KV_HEADS, GQ_DIM = 4, 2, 64
ML_HEADS, ML_NOPE, ML_ROPE, ML_V, ML_Q_RANK, ML_KV_RANK = 4, 64, 32, 64, 256, 128
N_BRANCH, BRANCH_W = 4, 256
N_EXPERTS, TOP_K = 32, 4
D_FF = D_MODEL
SWIGLU_LIMIT, SWIGLU_ALPHA = 7.0, 1.702
DEEPNORM_ALPHA = (2 * DEPTH) ** 0.25

IN_SIZES = (256, 256, 256, 256, 256, 256, 256, 128, 128, ML_Q_RANK, ML_KV_RANK, ML_ROPE, N_BRANCH * D_MODEL)
IN_OFFSETS = tuple(int(o) for o in np.cumsum((0,) + IN_SIZES)[:-1])
(O_NAQ, O_NAK, O_NAV, O_DFQ, O_DFK, O_DFV, O_GQQ, O_GQK, O_GQV, O_MLQA, O_MLKVA, O_MLKR, O_GATES) = IN_OFFSETS

LANES = 128
ROW_TILE = 256
MASK_VALUE = -1e30
VMEM_LIMIT = 56 * 1024 * 1024
LOG2E = math.log2(math.e)

_PROJ_GROUPS = (("na_q", 256), ("na_k", 256), ("na_v", 256),
                ("df_q", 256), ("df_q_sw", 256), ("df_k", 256), ("df_k_sw", 256), ("df_v", 256),
                ("gq_q", 256), ("gq_q_sw", 256), ("gq_k", 128), ("gq_k_sw", 128), ("gq_v", 128),
                ("ml_qa", 256), ("ml_kva", 128), ("ml_kr", 32), ("ml_kr_sw", 32))
_PROJ_OFF = {}
_o = 0
for _n, _r in _PROJ_GROUPS:
    _PROJ_OFF[_n] = (_o, _r)
    _o += _r
PROJ_ROWS = _o


def _cparams(sem):
    return pltpu.CompilerParams(dimension_semantics=sem, vmem_limit_bytes=VMEM_LIMIT)


def _swap_cols(w, n, signed=True):
    lead, width = w.shape[:-1], w.shape[-1]
    w5 = w.reshape(lead + (width // n, 2, 2, n // 4))
    lo, hi = w5[..., 0, :], w5[..., 1, :]
    out = jnp.stack([-hi if signed else hi, lo], axis=-2)
    return out.reshape(lead + (width,))


def _rope_tables(n, ctx_len, seq):
    h, q = n // 2, n // 4
    t = jnp.arange(seq, dtype=I32)
    rows, cols = (t // GRID_W).astype(F32), (t % GRID_W).astype(F32)
    inv_freq = ROPE_BASE ** (-jnp.arange(0, h, 2, dtype=F32) / h)
    ang_r = rows[:, None] * inv_freq[None, :]
    ang_c = cols[:, None] * inv_freq[None, :]
    ang = jnp.concatenate([ang_r, ang_r, ang_c, ang_c], axis=1)
    cos = jnp.concatenate([jnp.ones((ctx_len, n), F32), jnp.cos(ang)], axis=0).T
    sin = jnp.concatenate([jnp.zeros((ctx_len, n), F32), jnp.sin(ang)], axis=0).T
    return jnp.stack([cos, sin])


def _ada_kernel(c_ref, w_ref, b_ref, o_ref):
    cc = c_ref[...]
    h = (cc * jax.nn.sigmoid(cc)).astype(BF16)
    o_ref[0] = jnp.dot(h, w_ref[0].astype(BF16), preferred_element_type=F32) + b_ref[0]


def _ada(cc, w_ada, b_ada):
    L, D, N = w_ada.shape
    R = cc.shape[0]
    tn = 512
    return pl.pallas_call(
        _ada_kernel,
        out_shape=jax.ShapeDtypeStruct((L, R, N), F32),
        grid=(L, N // tn),
        in_specs=[pl.BlockSpec((R, D), lambda l, j: (0, 0)),
                  pl.BlockSpec((1, D, tn), lambda l, j: (l, 0, j)),
                  pl.BlockSpec((1, 1, tn), lambda l, j: (l, 0, j))],
        out_specs=pl.BlockSpec((1, R, tn), lambda l, j: (l, 0, j)),
        compiler_params=_cparams(("arbitrary", "arbitrary")),
        name="ada_mod",
    )(cc, w_ada, b_ada.reshape(L, 1, N))


def _proj_kernel(x_ref, sc_ref, sh_ref, w_ref, cs32_ref, cs64_ref, gqg_ref, qag_ref, kvag_ref, wq_ref, wkv_ref,
                 na_q, na_k, na_v, df_q, df_k, df_v, gq_q, gq_k, gq_v, ml_q, ml_k, ml_v):
    tm = x_ref.shape[1]
    h = (x_ref[0] * (1.0 + sc_ref[...]) + sh_ref[...]).astype(BF16)
    p = lax.dot_general(w_ref[...], h, (((1,), (1,)), ((), ())), preferred_element_type=F32)

    def grp(name):
        o, r = _PROJ_OFF[name]
        return p[o:o + r]

    def zeros(n):
        return jnp.zeros((n, tm), F32)

    def rope(a, a_sw, cs_ref):
        n = cs_ref.shape[1]
        r = a.shape[0]
        a3, s3 = a.reshape(r // n, n, tm), a_sw.reshape(r // n, n, tm)
        return (a3 * cs_ref[0][None] + s3 * cs_ref[1][None]).reshape(r, tm)

    def head_slots(a, width):
        pieces = []
        for i in range(a.shape[0] // width):
            pieces += [a[i * width:(i + 1) * width], zeros(LANES - width)]
        return jnp.concatenate(pieces, axis=0)

    na_q[0] = head_slots(grp("na_q") * (NA_DIM ** -0.5), NA_DIM).astype(BF16)
    na_k[0] = head_slots(grp("na_k"), NA_DIM).T.astype(BF16)
    na_v[0] = grp("na_v").astype(BF16)

    q = rope(grp("df_q"), grp("df_q_sw"), cs32_ref) * (DF_QK ** -0.5 * LOG2E)
    pieces = []
    for i in range(DF_HEADS):
        q1, q2 = q[i * 64:i * 64 + 32], q[i * 64 + 32:i * 64 + 64]
        pieces += [q1, zeros(LANES - 32), zeros(32), q2, zeros(LANES - 64)]
    df_q[0] = jnp.concatenate(pieces, axis=0).astype(BF16)
    df_k[0] = head_slots(rope(grp("df_k"), grp("df_k_sw"), cs32_ref), 2 * DF_QK).T.astype(BF16)
    df_v[0] = grp("df_v").astype(BF16)

    cos64, sin64 = cs64_ref[0], cs64_ref[1]

    def norm_rope(a, a_sw, g, g_sw, scale):
        hh = a.shape[0] // GQ_DIM
        a3, s3 = a.reshape(hh, GQ_DIM, tm), a_sw.reshape(hh, GQ_DIM, tm)
        r = lax.rsqrt(jnp.mean(a3 * a3, axis=1, keepdims=True) + EPS) * scale
        return ((a3 * (g * cos64)[None] + s3 * (g_sw * sin64)[None]) * r).reshape(hh * GQ_DIM, tm)

    gq_q[0] = head_slots(norm_rope(grp("gq_q"), grp("gq_q_sw"), gqg_ref[0], gqg_ref[1], GQ_DIM ** -0.5 * LOG2E),
                         GQ_DIM).astype(BF16)
    gq_k[0] = head_slots(norm_rope(grp("gq_k"), grp("gq_k_sw"), gqg_ref[2], gqg_ref[3], 1.0),
                         GQ_DIM).T.astype(BF16)
    gq_v[0] = grp("gq_v").astype(BF16)

    qa = grp("ml_qa")
    qan = (qa * lax.rsqrt(jnp.mean(qa * qa, axis=0, keepdims=True) + EPS) * qag_ref[...]).astype(BF16)
    qq = jnp.dot(wq_ref[...], qan, preferred_element_type=F32)
    kva = grp("ml_kva")
    kvan = (kva * lax.rsqrt(jnp.mean(kva * kva, axis=0, keepdims=True) + EPS) * kvag_ref[...]).astype(BF16)
    kv = jnp.dot(wkv_ref[...], kvan, preferred_element_type=F32)
    k_rope = rope(grp("ml_kr"), grp("ml_kr_sw"), cs32_ref)
    cos32, sin32 = cs32_ref[0], cs32_ref[1]
    ml_scale = (ML_NOPE + ML_ROPE) ** -0.5 * LOG2E
    qp, kp, vp = [], [], []
    for i in range(ML_HEADS):
        b = i * LANES
        q_rope = qq[b + 64:b + 96] * cos32 + qq[b + 96:b + 128] * sin32
        qp += [qq[b:b + 64] * ml_scale, q_rope * ml_scale, zeros(32)]
        kp += [kv[b:b + 64], k_rope, zeros(32)]
        vp += [kv[b + 64:b + 128]]
    ml_q[0] = jnp.concatenate(qp, axis=0).astype(BF16)
    ml_k[0] = jnp.concatenate(kp, axis=0).T.astype(BF16)
    ml_v[0] = jnp.concatenate(vp, axis=0).astype(BF16)


def _project(xa, mod, w_t, cs32, cs64, gq_g, qa_g, kva_g, wq_t, wkv_t, ctx_len):
    B, T, D = xa.shape
    tm = ROW_TILE
    nt = T // tm
    ctx_tiles = ctx_len // tm

    def kind(t):
        return jnp.where(t >= ctx_tiles, 1, 0)

    def mod_spec(j):
        return pl.BlockSpec((None, None, None, 1, D), lambda b, t: (b, kind(t), j, 0, 0))

    def const(a):
        nd = a.ndim
        return pl.BlockSpec(a.shape, lambda b, t: (0,) * nd)

    def fm(rows):
        return jax.ShapeDtypeStruct((B, rows, T), BF16), pl.BlockSpec((1, rows, tm), lambda b, t: (b, 0, t))

    def tmj(cols):
        return jax.ShapeDtypeStruct((B, T, cols), BF16), pl.BlockSpec((1, tm, cols), lambda b, t: (b, t, 0))

    outs = [fm(512), tmj(512), fm(256),
            fm(1024), tmj(512), fm(256),
            fm(512), tmj(256), fm(128),
            fm(512), tmj(512), fm(256)]
    return pl.pallas_call(
        _proj_kernel,
        out_shape=[o[0] for o in outs],
        grid=(B, nt),
        in_specs=[pl.BlockSpec((1, tm, D), lambda b, t: (b, t, 0)),
                  mod_spec(1), mod_spec(0),
                  const(w_t),
                  pl.BlockSpec((2, 32, tm), lambda b, t: (0, 0, t)),
                  pl.BlockSpec((2, 64, tm), lambda b, t: (0, 0, t)),
                  const(gq_g), const(qa_g), const(kva_g), const(wq_t), const(wkv_t)],
        out_specs=[o[1] for o in outs],
        compiler_params=_cparams(("arbitrary", "arbitrary")),
        name="in_proj",
    )(xa, mod, mod, w_t, cs32, cs64, gq_g, qa_g, kva_g, wq_t, wkv_t)


def _softmax_pv(k, q_t, v_t, exp_fn):
    s = jnp.dot(k, q_t, preferred_element_type=F32)
    m = jnp.max(s, axis=0, keepdims=True)
    p = exp_fn(s - m)
    l = jnp.sum(p, axis=0, keepdims=True)
    o = jnp.dot(v_t, p.astype(BF16), preferred_element_type=F32)
    return o, l


def _dense_attn_kernel(lam_ref, q_ref, k_ref, v_ref, post_ref, o_ref, ot_ref, *, heads, ctx_len, diff_norm):
    total = k_ref.shape[1]

    def run(n_keys):
        for i, (terms, v_idx) in enumerate(heads):
            v_t = v_ref[0, v_idx * 64:(v_idx + 1) * 64, 0:n_keys]
            acc = None
            for q_slot, k_slot, signed in terms:
                k = k_ref[0, 0:n_keys, k_slot * LANES:(k_slot + 1) * LANES]
                q_t = q_ref[0, q_slot * LANES:(q_slot + 1) * LANES, :]
                o, l = _softmax_pv(k, q_t, v_t, jnp.exp2)
                coef = (-lam_ref[0]) if signed else 1.0
                o = o * (coef / l)
                acc = o if acc is None else acc + o
            if diff_norm:
                acc = acc * lax.rsqrt(jnp.mean(acc * acc, axis=0, keepdims=True) + EPS) * post_ref[...]
            ot_ref[i * 64:(i + 1) * 64, :] = acc
        o_ref[0] = ot_ref[...].T.astype(BF16)

    qb = pl.program_id(1)

    @pl.when(qb * q_ref.shape[2] < ctx_len)
    def _():
        run(ctx_len)

    @pl.when(qb * q_ref.shape[2] >= ctx_len)
    def _():
        run(total)


def _dense_attn(lam, q_t, k, v_t, post, heads, ctx_len, diff_norm, name):
    B, qrows, T = q_t.shape
    tq = ROW_TILE
    kern = functools.partial(_dense_attn_kernel, heads=heads, ctx_len=ctx_len, diff_norm=diff_norm)
    return pl.pallas_call(
        kern,
        out_shape=jax.ShapeDtypeStruct((B, T, BRANCH_W), BF16),
        grid=(B, T // tq),
        in_specs=[pl.BlockSpec(memory_space=pltpu.SMEM),
                  pl.BlockSpec((1, qrows, tq), lambda b, t: (b, 0, t)),
                  pl.BlockSpec((1, T, k.shape[2]), lambda b, t: (b, 0, 0)),
                  pl.BlockSpec((1, v_t.shape[1], T), lambda b, t: (b, 0, 0)),
                  pl.BlockSpec(post.shape, lambda b, t: (0, 0))],
        out_specs=pl.BlockSpec((1, tq, BRANCH_W), lambda b, t: (b, t, 0)),
        scratch_shapes=[pltpu.VMEM((BRANCH_W, tq), F32)],
        compiler_params=_cparams(("arbitrary", "arbitrary")),
        name=name,
    )(lam, q_t, k, v_t, post)


def _na_plan(rows):
    kh, kw = min(NA_KH, rows), NA_KW
    q_rows = ROW_TILE // GRID_W
    win_rows = min(kh + q_rows, rows)
    nblk = rows // q_rows
    row_start = np.clip(np.arange(rows) - kh // 2, 0, rows - kh)
    col_start = np.clip(np.arange(GRID_W) - kw // 2, 0, GRID_W - kw)
    u0 = np.clip(np.arange(nblk) * q_rows - kh // 2, 0, rows - win_rows)
    wk, tq = win_rows * GRID_W, ROW_TILE
    kk, qq = np.arange(wk), np.arange(tq)
    kc, qc = kk % GRID_W, qq % GRID_W
    col_sel = (np.arange(2 * NA_KW - 1)[:, None, None]
               == (np.arange(GRID_W)[None, :, None] - np.arange(GRID_W)[None, None, :] + (NA_KW - 1))).astype(np.float32)
    types, type_of, row_sel_l, valid_l = {}, [], [], []
    for j in range(nblk):
        kr = u0[j] + kk // GRID_W
        qr = j * q_rows + qq // GRID_W
        rs = row_start[qr]
        valid = ((kr[:, None] >= rs[None, :]) & (kr[:, None] < rs[None, :] + kh)
                 & (kc[:, None] >= col_start[qc][None, :]) & (kc[:, None] < col_start[qc][None, :] + kw))
        assert valid.sum(axis=0).min() == kh * kw and valid.sum(axis=0).max() == kh * kw
        rel = (u0[j] + np.arange(win_rows))[:, None] - (j * q_rows + np.arange(q_rows))[None, :] + (NA_KH - 1)
        row_sel = (rel[:, :, None] == np.arange(2 * NA_KH - 1)[None, None, :]).astype(np.float32)
        key = (valid.tobytes(), row_sel.tobytes())
        if key not in types:
            types[key] = len(row_sel_l)
            row_sel_l.append(row_sel), valid_l.append(valid)
        type_of.append(types[key])
    assert all(int(u) * GRID_W % LANES == 0 for u in u0)
    return (np.asarray(u0, np.int32), np.asarray(type_of, np.int32), np.stack(row_sel_l), col_sel,
            np.stack(valid_l), wk)


def _na_kernel(u0_ref, ty_ref, q_ref, k_ref, v_ref, bias_ref, o_ref, ot_ref, *, ctx_len):
    del ty_ref
    wk = bias_ref.shape[2]
    j = pl.program_id(1)

    def finish():
        o_ref[0] = ot_ref[...].T.astype(BF16)

    @pl.when(j == 0)
    def _():
        for i in range(NA_HEADS):
            o, l = _softmax_pv(k_ref[0, 0:ctx_len, i * LANES:(i + 1) * LANES], q_ref[0, i * LANES:(i + 1) * LANES, :],
                               v_ref[0, i * 64:(i + 1) * 64, 0:ctx_len], jnp.exp)
            ot_ref[i * 64:(i + 1) * 64, :] = o * (1.0 / l)
        finish()

    @pl.when(j > 0)
    def _():
        ws = pl.multiple_of(ctx_len + u0_ref[j - 1] * GRID_W, LANES)
        for i in range(NA_HEADS):
            q_t = q_ref[0, i * LANES:(i + 1) * LANES, :]
            s_c = jnp.dot(k_ref[0, 0:ctx_len, i * LANES:(i + 1) * LANES], q_t, preferred_element_type=F32)
            s_w = jnp.dot(k_ref[0, pl.ds(ws, wk), i * LANES:(i + 1) * LANES], q_t,
                          preferred_element_type=F32) + bias_ref[0, i]
            m = jnp.maximum(jnp.max(s_c, axis=0, keepdims=True), jnp.max(s_w, axis=0, keepdims=True))
            p_c, p_w = jnp.exp(s_c - m), jnp.exp(s_w - m)
            l = jnp.sum(p_c, axis=0, keepdims=True) + jnp.sum(p_w, axis=0, keepdims=True)
            o = (jnp.dot(v_ref[0, i * 64:(i + 1) * 64, 0:ctx_len], p_c.astype(BF16), preferred_element_type=F32)
                 + jnp.dot(v_ref[0, i * 64:(i + 1) * 64, pl.ds(ws, wk)], p_w.astype(BF16),
                           preferred_element_type=F32))
            ot_ref[i * 64:(i + 1) * 64, :] = o * (1.0 / l)
        finish()


def _na_attn(q_t, k, v_t, rpb, ctx_len):
    B, _, T = q_t.shape
    tq = ROW_TILE
    rows = (T - ctx_len) // GRID_W
    u0, type_of, row_sel, col_sel, valid, wk = _na_plan(rows)
    hp = lax.Precision.HIGHEST
    by_col = jnp.einsum("hab,bcq->hacq", rpb.astype(F32), col_sel, precision=hp)
    bias = jnp.einsum("tkra,hacq->thkcrq", row_sel, by_col, precision=hp)
    bias = jnp.where(valid[:, None], bias.reshape(row_sel.shape[0], NA_HEADS, wk, tq), MASK_VALUE)
    nb = T // tq
    assert ctx_len == tq

    grid_spec = pltpu.PrefetchScalarGridSpec(
        num_scalar_prefetch=2,
        grid=(B, nb),
        in_specs=[pl.BlockSpec((1, NA_HEADS * LANES, tq), lambda b, j, u, ty: (b, 0, j)),
                  pl.BlockSpec((1, T, NA_HEADS * LANES), lambda b, j, u, ty: (b, 0, 0)),
                  pl.BlockSpec((1, BRANCH_W, T), lambda b, j, u, ty: (b, 0, 0)),
                  pl.BlockSpec((1, NA_HEADS, wk, tq), lambda b, j, u, ty: (ty[jnp.maximum(j - 1, 0)], 0, 0, 0))],
        out_specs=pl.BlockSpec((1, tq, BRANCH_W), lambda b, j, u, ty: (b, j, 0)),
        scratch_shapes=[pltpu.VMEM((BRANCH_W, tq), F32)])
    return pl.pallas_call(
        functools.partial(_na_kernel, ctx_len=ctx_len),
        out_shape=jax.ShapeDtypeStruct((B, T, BRANCH_W), BF16),
        grid_spec=grid_spec,
        compiler_params=_cparams(("arbitrary", "arbitrary")),
        name="na_attn",
    )(jnp.asarray(u0), jnp.asarray(type_of), q_t, k, v_t, bias)


def _merge_kernel(x_ref, sc1_ref, sh1_ref, g1_ref, sc2_ref, sh2_ref, ona_ref, odf_ref, ogq_ref, oml_ref,
                  wg_ref, wb_ref, wo_ref, lng_ref, lnb_ref, rw_ref, rb_ref, tri_ref,
                  x1_ref, h2_ref, idx_ref, wts_ref, rank_ref, cnt_ref, carry_ref):
    tm = x_ref.shape[1]
    first = (pl.program_id(0) == 0) & (pl.program_id(1) == 0)

    @pl.when(first)
    def _():
        carry_ref[...] = jnp.zeros_like(carry_ref)

    x = x_ref[0]
    h1 = (x * (1.0 + sc1_ref[...]) + sh1_ref[...]).astype(BF16)
    y = None
    for i, o_ref in enumerate((ona_ref, odf_ref, ogq_ref, oml_ref)):
        gate = jax.nn.sigmoid(jnp.dot(h1, wg_ref[:, i * D_MODEL:(i + 1) * D_MODEL], preferred_element_type=F32))
        term = gate * jnp.dot(o_ref[0], wb_ref[i * BRANCH_W:(i + 1) * BRANCH_W, :], preferred_element_type=F32)
        y = term if y is None else y + term
    z = jnp.dot(y.astype(BF16), wo_ref[...], preferred_element_type=F32)
    r = DEEPNORM_ALPHA * x + g1_ref[...] * z
    mu = jnp.mean(r, axis=-1, keepdims=True)
    var = jnp.mean(jnp.square(r - mu), axis=-1, keepdims=True)
    x1 = (r - mu) * lax.rsqrt(var + EPS) * lng_ref[...] + lnb_ref[...]
    x1_ref[0] = x1
    h2 = x1 * (1.0 + sc2_ref[...]) + sh2_ref[...]
    for j in range(D_MODEL // LANES):
        h2_ref[:, j, :] = h2[:, j * LANES:(j + 1) * LANES]

    logits = lax.dot_general(rw_ref[...], h2, (((1,), (1,)), ((), ())), preferred_element_type=F32,
                             precision=lax.Precision.HIGHEST) + rb_ref[...]
    iota = lax.broadcasted_iota(I32, (N_EXPERTS, tm), 0)
    vals, idxs = [], []
    cur = logits
    for _ in range(TOP_K):
        m = jnp.max(cur, axis=0, keepdims=True)
        ik = jnp.min(jnp.where(cur == m, iota, N_EXPERTS), axis=0, keepdims=True)
        vals.append(m)
        idxs.append(ik)
        cur = jnp.where(iota == ik, -jnp.inf, cur)
    exps = [jnp.exp(v - vals[0]) for v in vals]
    denom = exps[0] + exps[1] + exps[2] + exps[3]
    wts_ref[0] = jnp.concatenate([e / denom for e in exps], axis=0)
    idx_ref[0] = jnp.concatenate(idxs, axis=0)

    onehot = jnp.zeros((N_EXPERTS, tm), F32)
    for ik in idxs:
        onehot = onehot + jnp.where(iota == ik, 1.0, 0.0)
    before = jnp.dot(onehot.astype(BF16), tri_ref[...], preferred_element_type=F32) + carry_ref[...]
    rank_ref[0] = jnp.concatenate(
        [jnp.sum(jnp.where(iota == ik, before, 0.0), axis=0, keepdims=True) for ik in idxs], axis=0).astype(I32)
    carry = carry_ref[...] + jnp.sum(onehot, axis=1, keepdims=True)
    carry_ref[...] = carry
    cnt_ref[...] = carry.astype(I32)


def _merge(xa, mod, o_na, o_df, o_gq, o_ml, wg, wb, wo, ln_g, ln_b, rw_t, rb, ctx_len, t_off):
    B, T, D = xa.shape
    tm = ROW_TILE
    nt = T // tm - t_off
    ctx_tiles = ctx_len // tm
    tq_rows = nt * tm

    def kind(t):
        return jnp.where(t + t_off >= ctx_tiles, 1, 0)

    def mod_spec(j):
        return pl.BlockSpec((None, None, None, 1, D), lambda b, t: (b, kind(t), j, 0, 0))

    def const(a):
        nd = a.ndim
        return pl.BlockSpec(a.shape, lambda b, t: (0,) * nd)

    def tok(cols):
        return pl.BlockSpec((1, tm, cols), lambda b, t: (b, t + t_off, 0))

    tri = jnp.asarray(np.triu(np.ones((tm, tm), np.float32), 1), BF16)
    rb_b = jnp.broadcast_to(rb[:, None], (N_EXPERTS, tm)).astype(F32)
    n_tiles = B * nt
    route_shape = jax.ShapeDtypeStruct((n_tiles, TOP_K, tm), I32)
    route_spec = pl.BlockSpec((1, TOP_K, tm), lambda b, t: (b * nt + t, 0, 0))
    return pl.pallas_call(
        _merge_kernel,
        out_shape=[jax.ShapeDtypeStruct((B, tq_rows, D), F32),
                   jax.ShapeDtypeStruct((B * tq_rows, D // LANES, LANES), F32),
                   route_shape, jax.ShapeDtypeStruct((n_tiles, TOP_K, tm), F32), route_shape,
                   jax.ShapeDtypeStruct((N_EXPERTS, tm), I32)],
        grid=(B, nt),
        in_specs=[tok(D), mod_spec(1), mod_spec(0), mod_spec(2), mod_spec(4), mod_spec(3),
                  tok(BRANCH_W), tok(BRANCH_W), tok(BRANCH_W), tok(BRANCH_W),
                  const(wg), const(wb), const(wo), const(ln_g), const(ln_b), const(rw_t), const(rb_b), const(tri)],
        out_specs=[pl.BlockSpec((1, tm, D), lambda b, t: (b, t, 0)),
                   pl.BlockSpec((tm, D // LANES, LANES), lambda b, t: (b * nt + t, 0, 0)),
                   route_spec, route_spec, route_spec,
                   pl.BlockSpec((N_EXPERTS, tm), lambda b, t: (0, 0))],
        scratch_shapes=[pltpu.VMEM((N_EXPERTS, tm), F32)],
        compiler_params=_cparams(("arbitrary", "arbitrary")),
        name="merge_router",
    )(xa, mod, mod, mod, mod, mod, o_na, o_df, o_gq, o_ml, wg, wb, wo, ln_g, ln_b, rw_t, rb_b, tri)


def _row_copy(src, src_row, dst, dst_row, sem):
    return pltpu.make_async_copy(src.at[pl.ds(src_row, 1)], dst.at[pl.ds(dst_row, 1)], sem)


def _dispatch_kernel(starts_ref, idx_ref, rank_ref, h_ref, xs_in_ref, xs_ref, sem):
    del xs_in_ref
    tm = idx_ref.shape[2]

    def issue(t, c):
        for k in range(TOP_K):
            d = starts_ref[idx_ref[0, k, t]] + rank_ref[0, k, t]
            _row_copy(h_ref, t, xs_ref, d, sem).start()
        return c

    lax.fori_loop(0, tm, issue, 0)

    def drain(t, c):
        for k in range(TOP_K):
            _row_copy(h_ref, 0, xs_ref, 0, sem).wait()
        return c

    lax.fori_loop(0, tm, drain, 0)


def _dispatch(starts, idx, rank, h2, n_slots):
    n_tiles, _, tm = idx.shape
    xs0 = jnp.zeros((n_slots,) + h2.shape[1:], F32)
    smem = lambda: pl.BlockSpec((1, TOP_K, tm), lambda i, s: (i, 0, 0), memory_space=pltpu.SMEM)
    grid_spec = pltpu.PrefetchScalarGridSpec(
        num_scalar_prefetch=1, grid=(n_tiles,),
        in_specs=[smem(), smem(), pl.BlockSpec((tm,) + h2.shape[1:], lambda i, s: (i, 0, 0)),
                  pl.BlockSpec(memory_space=pl.ANY)],
        out_specs=pl.BlockSpec(memory_space=pl.ANY),
        scratch_shapes=[pltpu.SemaphoreType.DMA(())])
    return pl.pallas_call(
        _dispatch_kernel,
        out_shape=jax.ShapeDtypeStruct(xs0.shape, F32),
        grid_spec=grid_spec,
        input_output_aliases={4: 0},
        compiler_params=_cparams(("arbitrary",)),
        name="moe_dispatch",
    )(starts, idx, rank, h2, xs0)


def _expert_kernel(te_ref, nu_ref, xs_ref, w1_ref, b1_ref, w2_ref, b2_ref, y_ref, w1s_ref, w2s_ref):
    i = pl.program_id(0)
    changed = (i == 0) | (te_ref[i] != te_ref[jnp.maximum(i - 1, 0)])

    @pl.when(changed)
    def _():
        w1s_ref[...] = w1_ref[...].astype(BF16)
        w2s_ref[...] = w2_ref[...].astype(BF16)

    @pl.when(i < nu_ref[0])
    def _():
        x = jnp.concatenate([xs_ref[:, j, :] for j in range(D_MODEL // LANES)], axis=1).astype(BF16)
        gu = jnp.dot(x, w1s_ref[...], preferred_element_type=F32) + b1_ref[...]
        gate = jnp.minimum(gu[:, :D_FF], SWIGLU_LIMIT)
        up = jnp.clip(gu[:, D_FF:], -SWIGLU_LIMIT, SWIGLU_LIMIT)
        act = (up + 1.0) * gate * jax.nn.sigmoid(SWIGLU_ALPHA * gate)
        y = jnp.dot(act.astype(BF16), w2s_ref[...], preferred_element_type=F32) + b2_ref[...]
        for j in range(D_MODEL // LANES):
            y_ref[:, j, :] = y[:, j * LANES:(j + 1) * LANES]

    @pl.when(i >= nu_ref[0])
    def _():
        y_ref[...] = jnp.zeros_like(y_ref)


def _experts(tile_expert, n_used, xs, w1, b1, w2, b2, layer):
    n_slots = xs.shape[0]
    tm = ROW_TILE
    nt = n_slots // tm
    E = w1.shape[1]
    grid_spec = pltpu.PrefetchScalarGridSpec(
        num_scalar_prefetch=2, grid=(nt,),
        in_specs=[pl.BlockSpec((tm, D_MODEL // LANES, LANES), lambda i, te, nu: (i, 0, 0)),
                  pl.BlockSpec((None, None, D_MODEL, 2 * D_FF), lambda i, te, nu: (layer, te[i], 0, 0)),
                  pl.BlockSpec((None, None, 1, 2 * D_FF), lambda i, te, nu: (layer, te[i], 0, 0)),
                  pl.BlockSpec((None, None, D_FF, D_MODEL), lambda i, te, nu: (layer, te[i], 0, 0)),
                  pl.BlockSpec((None, None, 1, D_MODEL), lambda i, te, nu: (layer, te[i], 0, 0))],
        out_specs=pl.BlockSpec((tm, D_MODEL // LANES, LANES), lambda i, te, nu: (i, 0, 0)),
        scratch_shapes=[pltpu.VMEM((D_MODEL, 2 * D_FF), BF16), pltpu.VMEM((D_FF, D_MODEL), BF16)])
    return pl.pallas_call(
        _expert_kernel,
        out_shape=jax.ShapeDtypeStruct(xs.shape, F32),
        grid_spec=grid_spec,
        compiler_params=_cparams(("arbitrary",)),
        name="moe_experts",
    )(tile_expert, n_used, xs, w1, b1.reshape(b1.shape[0], E, 1, -1), w2, b2.reshape(b2.shape[0], E, 1, -1))


def _combine_kernel(starts_ref, idx_ref, rank_ref, idxn_ref, rankn_ref, w_ref, x1_ref, g2_ref, lng_ref, lnb_ref,
                    y_ref, o_ref, buf_ref, sem):
    tm = idx_ref.shape[2]
    i = pl.program_id(0)
    n = pl.num_programs(0)
    slot = lax.rem(i, 2)

    def issue(ir, rr, s):
        def body(t, c):
            for k in range(TOP_K):
                d = starts_ref[ir[0, k, t]] + rr[0, k, t]
                pltpu.make_async_copy(y_ref.at[pl.ds(d, 1)], buf_ref.at[s, k, pl.ds(t, 1)], sem.at[s]).start()
            return c
        lax.fori_loop(0, tm, body, 0)

    @pl.when(i == 0)
    def _():
        issue(idx_ref, rank_ref, 0)

    @pl.when(i + 1 < n)
    def _():
        issue(idxn_ref, rankn_ref, 1 - slot)

    def drain(t, c):
        for k in range(TOP_K):
            pltpu.make_async_copy(y_ref.at[pl.ds(0, 1)], buf_ref.at[slot, k, pl.ds(0, 1)], sem.at[slot]).wait()
        return c

    lax.fori_loop(0, tm, drain, 0)

    cols = []
    for j in range(D_MODEL // LANES):
        acc = None
        for k in range(TOP_K):
            v = buf_ref[slot, k, :, j, :] * w_ref[0, :, k:k + 1]
            acc = v if acc is None else acc + v
        cols.append(acc)
    y2 = jnp.concatenate(cols, axis=1)
    r = DEEPNORM_ALPHA * x1_ref[0] + g2_ref[...] * y2
    mu = jnp.mean(r, axis=-1, keepdims=True)
    var = jnp.mean(jnp.square(r - mu), axis=-1, keepdims=True)
    o_ref[0] = (r - mu) * lax.rsqrt(var + EPS) * lng_ref[...] + lnb_ref[...]


def _combine(starts, idx, rank, wts_tm, x1, mod, ln_g, ln_b, y, ctx_len, t_off):
    B, tq_rows, D = x1.shape
    n_tiles, _, tm = idx.shape
    nt = n_tiles // B
    ctx_tiles = ctx_len // tm

    def kind(i):
        return jnp.where(i % nt + t_off >= ctx_tiles, 1, 0)

    smem = lambda f: pl.BlockSpec((1, TOP_K, tm), f, memory_space=pltpu.SMEM)
    cur = lambda i, s: (i, 0, 0)
    nxt = lambda i, s: (jnp.minimum(i + 1, n_tiles - 1), 0, 0)
    grid_spec = pltpu.PrefetchScalarGridSpec(
        num_scalar_prefetch=1, grid=(n_tiles,),
        in_specs=[smem(cur), smem(cur), smem(nxt), smem(nxt),
                  pl.BlockSpec((1, tm, TOP_K), lambda i, s: (i, 0, 0)),
                  pl.BlockSpec((1, tm, D), lambda i, s: (i // nt, i % nt, 0)),
                  pl.BlockSpec((None, None, None, 1, D), lambda i, s: (i // nt, kind(i), 5, 0, 0)),
                  pl.BlockSpec(ln_g.shape, lambda i, s: (0, 0)),
                  pl.BlockSpec(ln_b.shape, lambda i, s: (0, 0)),
                  pl.BlockSpec(memory_space=pl.ANY)],
        out_specs=pl.BlockSpec((1, tm, D), lambda i, s: (i // nt, i % nt, 0)),
        scratch_shapes=[pltpu.VMEM((2, TOP_K, tm, D // LANES, LANES), F32), pltpu.SemaphoreType.DMA((2,))])
    return pl.pallas_call(
        _combine_kernel,
        out_shape=jax.ShapeDtypeStruct((B, tq_rows, D), F32),
        grid_spec=grid_spec,
        compiler_params=_cparams(("arbitrary",)),
        name="moe_combine",
    )(starts, idx, rank, idx, rank, wts_tm, x1, mod, ln_g, ln_b, y)


_DF_HEADS_SPEC = tuple((((2 * i, i, False), (2 * i + 1, i, True)), i) for i in range(DF_HEADS))
_GQ_HEADS_SPEC = tuple((((i, i // (GQ_HEADS // GQ_KV_HEADS), False),), i // (GQ_HEADS // GQ_KV_HEADS))
                       for i in range(GQ_HEADS))
_ML_HEADS_SPEC = tuple((((i, i, False),), i) for i in range(ML_HEADS))


def _layer_weights(l, w_in, gq_qnorm, gq_knorm, ml_qa_norm, ml_wq_b, ml_kva_norm, ml_wkv_b):
    w = w_in[l]

    def cols(o, n):
        return w[:, o:o + n]

    parts = [cols(O_NAQ, 256), cols(O_NAK, 256), cols(O_NAV, 256),
             cols(O_DFQ, 256), _swap_cols(cols(O_DFQ, 256), 32), cols(O_DFK, 256), _swap_cols(cols(O_DFK, 256), 32),
             cols(O_DFV, 256),
             cols(O_GQQ, 256), _swap_cols(cols(O_GQQ, 256), 64), cols(O_GQK, 128), _swap_cols(cols(O_GQK, 128), 64),
             cols(O_GQV, 128),
             cols(O_MLQA, 256), cols(O_MLKVA, 128), cols(O_MLKR, 32), _swap_cols(cols(O_MLKR, 32), 32)]
    w_t = jnp.concatenate(parts, axis=1).T.astype(BF16)
    assert w_t.shape[0] == PROJ_ROWS

    tm = ROW_TILE
    gq_g = jnp.stack([gq_qnorm[l], _swap_cols(gq_qnorm[l], 64, signed=False),
                      gq_knorm[l], _swap_cols(gq_knorm[l], 64, signed=False)])
    gq_g = jnp.broadcast_to(gq_g[:, :, None], (4, GQ_DIM, tm)).astype(F32)
    qa_g = jnp.broadcast_to(ml_qa_norm[l][:, None], (ML_Q_RANK, tm)).astype(F32)
    kva_g = jnp.broadcast_to(ml_kva_norm[l][:, None], (ML_KV_RANK, tm)).astype(F32)
    wq = ml_wq_b[l]
    qd = ML_NOPE + ML_ROPE
    wq_parts = []
    for i in range(ML_HEADS):
        rope_cols = wq[:, i * qd + ML_NOPE:(i + 1) * qd]
        wq_parts += [wq[:, i * qd:i * qd + ML_NOPE], rope_cols, _swap_cols(rope_cols, 32)]
    wq_t = jnp.concatenate(wq_parts, axis=1).T.astype(BF16)
    wkv_t = ml_wkv_b[l].T.astype(BF16)
    return w_t, gq_g, qa_g, kva_g, wq_t, wkv_t


def kernel(x, c, ctx, c_ctx, w_ada, b_ada, w_in, na_rpb, df_lam, df_subln, gq_qnorm, gq_knorm, ml_qa_norm, ml_wq_b,
           ml_kva_norm, ml_wkv_b, w_branch, w_out, ln1_g, ln1_b, ln2_g, ln2_b, router_w, router_b, exp_w1, exp_b1,
           exp_w2, exp_b2):
    B, S, D = x.shape
    C = ctx.shape[1]
    T = C + S
    tm = ROW_TILE
    assert D == D_MODEL and C % tm == 0 and S % tm == 0 and C == tm

    xa = jnp.concatenate([ctx, x], axis=1)
    cc = jnp.zeros((16, D), F32).at[:B].set(c).at[B].set(c_ctx)
    mod_all = _ada(cc, w_ada, b_ada)
    cs32 = _rope_tables(32, C, S)
    cs64 = _rope_tables(64, C, S)

    for l in range(DEPTH):
        last = l == DEPTH - 1
        m = mod_all[l]
        mod = jnp.stack([jnp.broadcast_to(m[B][None], (B, 6 * D)), m[:B]], axis=1).reshape(B, 2, 6, 1, D)
        lam_init = 0.8 - 0.6 * math.exp(-0.3 * l)
        lp = df_lam[l].astype(F32)
        lam = (jnp.exp(jnp.sum(lp[0] * lp[1])) - jnp.exp(jnp.sum(lp[2] * lp[3])) + lam_init).reshape(1)
        w_t, gq_g, qa_g, kva_g, wq_t, wkv_t = _layer_weights(l, w_in, gq_qnorm, gq_knorm, ml_qa_norm, ml_wq_b,
                                                             ml_kva_norm, ml_wkv_b)
        (na_q, na_k, na_v, df_q, df_k, df_v, gq_q, gq_k, gq_v, ml_q, ml_k, ml_v) = _project(
            xa, mod, w_t, cs32, cs64, gq_g, qa_g, kva_g, wq_t, wkv_t, C)

        o_na = _na_attn(na_q, na_k, na_v, na_rpb[l], C)
        subln = jnp.broadcast_to((df_subln[l] * (1.0 - lam_init))[:, None], (DF_V, tm)).astype(F32)
        ones = jnp.ones((DF_V, tm), F32)
        o_df = _dense_attn(lam, df_q, df_k, df_v, subln, _DF_HEADS_SPEC, C, True, "df_attn")
        o_gq = _dense_attn(lam, gq_q, gq_k, gq_v, ones, _GQ_HEADS_SPEC, C, False, "gq_attn")
        o_ml = _dense_attn(lam, ml_q, ml_k, ml_v, ones, _ML_HEADS_SPEC, C, False, "ml_attn")

        t_off = C // tm if last else 0
        wg = w_in[l][:, O_GATES:].astype(BF16)
        wb = w_branch[l].reshape(N_BRANCH * BRANCH_W, D).astype(BF16)
        wo = w_out[l].astype(BF16)
        x1, h2, idx, wts, rank, cnt = _merge(xa, mod, o_na, o_df, o_gq, o_ml, wg, wb, wo,
                                             ln1_g[l][None], ln1_b[l][None], router_w[l].T, router_b[l], C, t_off)

        counts = cnt[:, 0]
        padded = ((counts + tm - 1) // tm) * tm
        ends = jnp.cumsum(padded)
        starts = (ends - padded).astype(I32)
        n_tok = h2.shape[0]
        n_exp_tiles = (n_tok * TOP_K) // tm + N_EXPERTS
        n_used = (ends[-1] // tm).astype(I32).reshape(1)
        tile_first = jnp.arange(n_exp_tiles, dtype=I32) * tm
        tile_expert = jnp.minimum(jnp.sum((ends[None, :] <= tile_first[:, None]).astype(I32), axis=1), N_EXPERTS - 1)
        tile_expert = jnp.where(tile_first < ends[-1], tile_expert, tile_expert[jnp.maximum(n_used[0] - 1, 0)])

        xs = _dispatch(starts, idx, rank, h2, n_exp_tiles * tm)
        ys = _experts(tile_expert, n_used, xs, exp_w1, exp_b1, exp_w2, exp_b2, l)
        xa = _combine(starts, idx, rank, wts.transpose(0, 2, 1), x1, mod, ln2_g[l][None], ln2_b[l][None], ys, C, t_off)
    return xa
```

```python
import functools
import math

import numpy as np
import jax
import jax.numpy as jnp
from jax import lax
from jax.experimental import pallas as pl
from jax.experimental.pallas import tpu as pltpu

F32, BF16, I32 = jnp.float32, jnp.bfloat16, jnp.int32

D_MODEL = 1024
DEPTH = 2
GRID_W = 64
ROPE_BASE = 10000.0
EPS = 1e-6
NA_HEADS, NA_DIM, NA_KH, NA_KW = 4, 64, 8, 16
DF_HEADS, DF_QK = 4, 32
DF_V = 2 * DF_QK
GQ_HEADS, GQ_KV_HEADS, GQ_DIM = 4, 2, 64
ML_HEADS, ML_NOPE, ML_ROPE, ML_V, ML_Q_RANK, ML_KV_RANK = 4, 64, 32, 64, 256, 128
N_BRANCH, BRANCH_W = 4, 256
N_EXPERTS, TOP_K = 32, 4
D_FF = D_MODEL
SWIGLU_LIMIT, SWIGLU_ALPHA = 7.0, 1.702
DEEPNORM_ALPHA = (2 * DEPTH) ** 0.25

IN_SIZES = (256, 256, 256, 256, 256, 256, 256, 128, 128, ML_Q_RANK, ML_KV_RANK, ML_ROPE, N_BRANCH * D_MODEL)
IN_OFFSETS = tuple(int(o) for o in np.cumsum((0,) + IN_SIZES)[:-1])
(O_NAQ, O_NAK, O_NAV, O_DFQ, O_DFK, O_DFV, O_GQQ, O_GQK, O_GQV, O_MLQA, O_MLKVA, O_MLKR, O_GATES) = IN_OFFSETS

LANES = 128
ROW_TILE = 256
MASK_VALUE = -1e30
VMEM_LIMIT = 56 * 1024 * 1024
LOG2E = math.log2(math.e)
KEY_CHUNK = 256
ACC_ROWS = 16

_PROJ_GROUPS = (("na_q", 256), ("na_k", 256), ("na_v", 256),
                ("df_q", 256), ("df_q_sw", 256), ("df_k", 256), ("df_k_sw", 256), ("df_v", 256),
                ("gq_q", 256), ("gq_q_sw", 256), ("gq_k", 128), ("gq_k_sw", 128), ("gq_v", 128),
                ("ml_qa", 256), ("ml_kva", 128), ("ml_kr", 32), ("ml_kr_sw", 32))
_PROJ_OFF = {}
_o = 0
for _n, _r in _PROJ_GROUPS:
    _PROJ_OFF[_n] = (_o, _r)
    _o += _r
PROJ_ROWS = _o


def _cparams(sem):
    return pltpu.CompilerParams(dimension_semantics=sem, vmem_limit_bytes=VMEM_LIMIT)


def _swap_cols(w, n, signed=True):
    lead, width = w.shape[:-1], w.shape[-1]
    w5 = w.reshape(lead + (width // n, 2, 2, n // 4))
    lo, hi = w5[..., 0, :], w5[..., 1, :]
    out = jnp.stack([-hi if signed else hi, lo], axis=-2)
    return out.reshape(lead + (width,))


def _rope_tables(n, ctx_len, seq):
    h, q = n // 2, n // 4
    t = jnp.arange(seq, dtype=I32)
    rows, cols = (t // GRID_W).astype(F32), (t % GRID_W).astype(F32)
    inv_freq = ROPE_BASE ** (-jnp.arange(0, h, 2, dtype=F32) / h)
    ang_r = rows[:, None] * inv_freq[None, :]
    ang_c = cols[:, None] * inv_freq[None, :]
    ang = jnp.concatenate([ang_r, ang_r, ang_c, ang_c], axis=1)
    cos = jnp.concatenate([jnp.ones((ctx_len, n), F32), jnp.cos(ang)], axis=0).T
    sin = jnp.concatenate([jnp.zeros((ctx_len, n), F32), jnp.sin(ang)], axis=0).T
    return jnp.stack([cos, sin])


def _ada_kernel(c_ref, w_ref, b_ref, o_ref):
    cc = c_ref[...]
    h = (cc * jax.nn.sigmoid(cc)).astype(BF16)
    o_ref[0] = jnp.dot(h, w_ref[0].astype(BF16), preferred_element_type=F32) + b_ref[0]


def _ada(cc, w_ada, b_ada):
    L, D, N = w_ada.shape
    R = cc.shape[0]
    tn = 512
    return pl.pallas_call(
        _ada_kernel,
        out_shape=jax.ShapeDtypeStruct((L, R, N), F32),
        grid=(L, N // tn),
        in_specs=[pl.BlockSpec((R, D), lambda l, j: (0, 0)),
                  pl.BlockSpec((1, D, tn), lambda l, j: (l, 0, j)),
                  pl.BlockSpec((1, 1, tn), lambda l, j: (l, 0, j))],
        out_specs=pl.BlockSpec((1, R, tn), lambda l, j: (l, 0, j)),
        compiler_params=_cparams(("arbitrary", "arbitrary")),
        name="ada_mod",
    )(cc, w_ada, b_ada.reshape(L, 1, N))


def _proj_kernel(x_ref, sc_ref, sh_ref, w_ref, cs32_ref, cs64_ref, gqg_ref, qag_ref, kvag_ref, wq_ref, wkv_ref,
                 na_q, na_k, na_v, df_q, df_k, df_v, gq_q, gq_k, gq_v, ml_q, ml_k, ml_v):
    tm = x_ref.shape[1]
    h = (x_ref[0] * (1.0 + sc_ref[...]) + sh_ref[...]).astype(BF16)
    p = lax.dot_general(w_ref[...], h, (((1,), (1,)), ((), ())), preferred_element_type=F32)

    def grp(name):
        o, r = _PROJ_OFF[name]
        return p[o:o + r]

    def zeros(n):
        return jnp.zeros((n, tm), F32)

    def rope(a, a_sw, cs_ref):
        n = cs_ref.shape[1]
        r = a.shape[0]
        a3, s3 = a.reshape(r // n, n, tm), a_sw.reshape(r // n, n, tm)
        return (a3 * cs_ref[0][None] + s3 * cs_ref[1][None]).reshape(r, tm)

    def head_slots(a, width):
        pieces = []
        for i in range(a.shape[0] // width):
            pieces += [a[i * width:(i + 1) * width], zeros(LANES - width)]
        return jnp.concatenate(pieces, axis=0)

    na_q[0] = head_slots(grp("na_q") * (NA_DIM ** -0.5), NA_DIM).astype(BF16)
    na_k[0] = head_slots(grp("na_k"), NA_DIM).T.astype(BF16)
    na_v[0] = grp("na_v").astype(BF16)

    q = rope(grp("df_q"), grp("df_q_sw"), cs32_ref) * (DF_QK ** -0.5 * LOG2E)
    pieces = []
    for i in range(DF_HEADS):
        q1, q2 = q[i * 64:i * 64 + 32], q[i * 64 + 32:i * 64 + 64]
        pieces += [q1, zeros(LANES - 32), zeros(32), q2, zeros(LANES - 64)]
    df_q[0] = jnp.concatenate(pieces, axis=0).astype(BF16)
    df_k[0] = head_slots(rope(grp("df_k"), grp("df_k_sw"), cs32_ref), 2 * DF_QK).T.astype(BF16)
    df_v[0] = grp("df_v").astype(BF16)

    cos64, sin64 = cs64_ref[0], cs64_ref[1]

    def norm_rope(a, a_sw, g, g_sw, scale):
        hh = a.shape[0] // GQ_DIM
        a3, s3 = a.reshape(hh, GQ_DIM, tm), a_sw.reshape(hh, GQ_DIM, tm)
        r = lax.rsqrt(jnp.mean(a3 * a3, axis=1, keepdims=True) + EPS) * scale
        return ((a3 * (g * cos64)[None] + s3 * (g_sw * sin64)[None]) * r).reshape(hh * GQ_DIM, tm)

    gq_q[0] = head_slots(norm_rope(grp("gq_q"), grp("gq_q_sw"), gqg_ref[0], gqg_ref[1], GQ_DIM ** -0.5 * LOG2E),
                         GQ_DIM).astype(BF16)
    gq_k[0] = head_slots(norm_rope(grp("gq_k"), grp("gq_k_sw"), gqg_ref[2], gqg_ref[3], 1.0),
                         GQ_DIM).T.astype(BF16)
    gq_v[0] = grp("gq_v").astype(BF16)

    qa = grp("ml_qa")
    qan = (qa * lax.rsqrt(jnp.mean(qa * qa, axis=0, keepdims=True) + EPS) * qag_ref[...]).astype(BF16)
    qq = jnp.dot(wq_ref[...], qan, preferred_element_type=F32)
    kva = grp("ml_kva")
    kvan = (kva * lax.rsqrt(jnp.mean(kva * kva, axis=0, keepdims=True) + EPS) * kvag_ref[...]).astype(BF16)
    kv = jnp.dot(wkv_ref[...], kvan, preferred_element_type=F32)
    k_rope = rope(grp("ml_kr"), grp("ml_kr_sw"), cs32_ref)
    cos32, sin32 = cs32_ref[0], cs32_ref[1]
    ml_scale = (ML_NOPE + ML_ROPE) ** -0.5 * LOG2E
    qp, kp, vp = [], [], []
    for i in range(ML_HEADS):
        b = i * LANES
        q_rope = qq[b + 64:b + 96] * cos32 + qq[b + 96:b + 128] * sin32
        qp += [qq[b:b + 64] * ml_scale, q_rope * ml_scale, zeros(32)]
        kp += [kv[b:b + 64], k_rope, zeros(32)]
        vp += [kv[b + 64:b + 128]]
    ml_q[0] = jnp.concatenate(qp, axis=0).astype(BF16)
    ml_k[0] = jnp.concatenate(kp, axis=0).T.astype(BF16)
    ml_v[0] = jnp.concatenate(vp, axis=0).astype(BF16)


def _project(xa, mod, w_t, cs32, cs64, gq_g, qa_g, kva_g, wq_t, wkv_t, ctx_len):
    B, T, D = xa.shape
    tm = ROW_TILE
    nt = T // tm
    ctx_tiles = ctx_len // tm

    def kind(t):
        return jnp.where(t >= ctx_tiles, 1, 0)

    def mod_spec(j):
        return pl.BlockSpec((None, None, None, 1, D), lambda b, t: (b, kind(t), j, 0, 0))

    def const(a):
        nd = a.ndim
        return pl.BlockSpec(a.shape, lambda b, t: (0,) * nd)

    def fm(rows):
        return jax.ShapeDtypeStruct((B, rows, T), BF16), pl.BlockSpec((1, rows, tm), lambda b, t: (b, 0, t))

    def tmj(cols):
        return jax.ShapeDtypeStruct((B, T, cols), BF16), pl.BlockSpec((1, tm, cols), lambda b, t: (b, t, 0))

    outs = [fm(512), tmj(512), fm(256),
            fm(1024), tmj(512), fm(256),
            fm(512), tmj(256), fm(128),
            fm(512), tmj(512), fm(256)]
    return pl.pallas_call(
        _proj_kernel,
        out_shape=[o[0] for o in outs],
        grid=(B, nt),
        in_specs=[pl.BlockSpec((1, tm, D), lambda b, t: (b, t, 0)),
                  mod_spec(1), mod_spec(0),
                  const(w_t),
                  pl.BlockSpec((2, 32, tm), lambda b, t: (0, 0, t)),
                  pl.BlockSpec((2, 64, tm), lambda b, t: (0, 0, t)),
                  const(gq_g), const(qa_g), const(kva_g), const(wq_t), const(wkv_t)],
        out_specs=[o[1] for o in outs],
        compiler_params=_cparams(("arbitrary", "arbitrary")),
        name="in_proj",
    )(xa, mod, mod, w_t, cs32, cs64, gq_g, qa_g, kva_g, wq_t, wkv_t)


def _softmax_pv(k, q_t, v_t, exp_fn):
    s = jnp.dot(k, q_t, preferred_element_type=F32)
    m = jnp.max(s, axis=0, keepdims=True)
    p = exp_fn(s - m)
    l = jnp.sum(p, axis=0, keepdims=True)
    o = jnp.dot(v_t, p.astype(BF16), preferred_element_type=F32)
    return o, l


def _dense_attn_kernel(lam_ref, q_ref, k_ref, v_ref, post_ref, o_ref, ot_ref, s_ref, p_ref, *, heads, ctx_len,
                       diff_norm):
    total, tq = k_ref.shape[1], q_ref.shape[2]
    items = [(qs, ks, v_idx, i, signed) for i, (terms, v_idx) in enumerate(heads) for qs, ks, signed in terms]
    last_of_head = {i: max(j for j, it in enumerate(items) if it[3] == i) for i in range(len(heads))}

    def finish_head(i, acc):
        if diff_norm:
            acc = acc * lax.rsqrt(jnp.mean(acc * acc, axis=0, keepdims=True) + EPS) * post_ref[...]
        ot_ref[i * 64:(i + 1) * 64, :] = acc

    def run_context():
        head_acc = {}
        for j, (qs, ks, v_idx, i, signed) in enumerate(items):
            o, l = _softmax_pv(k_ref[0, 0:ctx_len, ks * LANES:(ks + 1) * LANES], q_ref[0, qs * LANES:(qs + 1) * LANES, :],
                               v_ref[0, v_idx * 64:(v_idx + 1) * 64, 0:ctx_len], jnp.exp2)
            o = o * (((-lam_ref[0]) if signed else 1.0) / l)
            head_acc[i] = o if i not in head_acc else head_acc[i] + o
            if last_of_head[i] == j:
                finish_head(i, head_acc.pop(i))

    def run_latent():
        n_chunks = total // KEY_CHUNK
        half = KEY_CHUNK // 2
        n_items = len(items)
        m_of, l_of, head_acc = {}, {}, {}
        for t in range(n_items + 2):
            sc = items[t] if t < n_items else None
            ex = items[t - 1] if 0 <= t - 1 < n_items else None
            pv = items[t - 2] if 0 <= t - 2 < n_items else None
            macc = jnp.full((ACC_ROWS, tq), -jnp.inf, F32)
            lacc = jnp.zeros((ACC_ROWS, tq), F32)
            oacc = jnp.zeros((64, tq), F32)
            for c in range(n_chunks):
                r0 = c * KEY_CHUNK
                if sc is not None:
                    q_t = q_ref[0, sc[0] * LANES:(sc[0] + 1) * LANES, :]
                    for r in (r0, r0 + half):
                        s = jnp.dot(k_ref[0, r:r + half, sc[1] * LANES:(sc[1] + 1) * LANES], q_t,
                                    preferred_element_type=F32)
                        s_ref[t % 2, r:r + half, :] = s
                        for a in range(half // ACC_ROWS):
                            macc = jnp.maximum(macc, s[a * ACC_ROWS:(a + 1) * ACC_ROWS])
                if ex is not None:
                    p = jnp.exp2(s_ref[(t - 1) % 2, r0:r0 + KEY_CHUNK, :] - m_of[t - 1])
                    for a in range(KEY_CHUNK // ACC_ROWS):
                        lacc = lacc + p[a * ACC_ROWS:(a + 1) * ACC_ROWS]
                    p_ref[(t - 1) % 2, r0:r0 + KEY_CHUNK, :] = p.astype(BF16)
                if pv is not None:
                    oacc = oacc + jnp.dot(v_ref[0, pv[2] * 64:(pv[2] + 1) * 64, r0:r0 + KEY_CHUNK],
                                          p_ref[(t - 2) % 2, r0:r0 + KEY_CHUNK, :], preferred_element_type=F32)
            if sc is not None:
                m_of[t] = jnp.max(macc, axis=0, keepdims=True)
            if ex is not None:
                l_of[t - 1] = jnp.sum(lacc, axis=0, keepdims=True)
            if pv is not None:
                i, signed = pv[3], pv[4]
                o = oacc * (((-lam_ref[0]) if signed else 1.0) / l_of.pop(t - 2))
                head_acc[i] = o if i not in head_acc else head_acc[i] + o
                if last_of_head[i] == t - 2:
                    finish_head(i, head_acc.pop(i))

    qb = pl.program_id(1)

    @pl.when(qb * tq < ctx_len)
    def _():
        run_context()
        o_ref[0] = ot_ref[...].T.astype(BF16)

    @pl.when(qb * tq >= ctx_len)
    def _():
        run_latent()
        o_ref[0] = ot_ref[...].T.astype(BF16)


def _dense_attn(lam, q_t, k, v_t, post, heads, ctx_len, diff_norm, name):
    B, qrows, T = q_t.shape
    tq = ROW_TILE
    kern = functools.partial(_dense_attn_kernel, heads=heads, ctx_len=ctx_len, diff_norm=diff_norm)
    return pl.pallas_call(
        kern,
        out_shape=jax.ShapeDtypeStruct((B, T, BRANCH_W), BF16),
        grid=(B, T // tq),
        in_specs=[pl.BlockSpec(memory_space=pltpu.SMEM),
                  pl.BlockSpec((1, qrows, tq), lambda b, t: (b, 0, t)),
                  pl.BlockSpec((1, T, k.shape[2]), lambda b, t: (b, 0, 0)),
                  pl.BlockSpec((1, v_t.shape[1], T), lambda b, t: (b, 0, 0)),
                  pl.BlockSpec(post.shape, lambda b, t: (0, 0))],
        out_specs=pl.BlockSpec((1, tq, BRANCH_W), lambda b, t: (b, t, 0)),
        scratch_shapes=[pltpu.VMEM((BRANCH_W, tq), F32), pltpu.VMEM((2, T, tq), F32), pltpu.VMEM((2, T, tq), BF16)],
        compiler_params=_cparams(("arbitrary", "arbitrary")),
        name=name,
    )(lam, q_t, k, v_t, post)


def _na_plan(rows):
    kh, kw = min(NA_KH, rows), NA_KW
    q_rows = ROW_TILE // GRID_W
    win_rows = min(kh + q_rows, rows)
    nblk = rows // q_rows
    row_start = np.clip(np.arange(rows) - kh // 2, 0, rows - kh)
    col_start = np.clip(np.arange(GRID_W) - kw // 2, 0, GRID_W - kw)
    u0 = np.clip(np.arange(nblk) * q_rows - kh // 2, 0, rows - win_rows)
    wk, tq = win_rows * GRID_W, ROW_TILE
    kk, qq = np.arange(wk), np.arange(tq)
    kc, qc = kk % GRID_W, qq % GRID_W
    col_sel = (np.arange(2 * NA_KW - 1)[:, None, None]
               == (np.arange(GRID_W)[None, :, None] - np.arange(GRID_W)[None, None, :] + (NA_KW - 1))).astype(np.float32)
    types, type_of, row_sel_l, valid_l = {}, [], [], []
    for j in range(nblk):
        kr = u0[j] + kk // GRID_W
        qr = j * q_rows + qq // GRID_W
        rs = row_start[qr]
        valid = ((kr[:, None] >= rs[None, :]) & (kr[:, None] < rs[None, :] + kh)
                 & (kc[:, None] >= col_start[qc][None, :]) & (kc[:, None] < col_start[qc][None, :] + kw))
        assert valid.sum(axis=0).min() == kh * kw and valid.sum(axis=0).max() == kh * kw
        rel = (u0[j] + np.arange(win_rows))[:, None] - (j * q_rows + np.arange(q_rows))[None, :] + (NA_KH - 1)
        row_sel = (rel[:, :, None] == np.arange(2 * NA_KH - 1)[None, None, :]).astype(np.float32)
        key = (valid.tobytes(), row_sel.tobytes())
        if key not in types:
            types[key] = len(row_sel_l)
            row_sel_l.append(row_sel), valid_l.append(valid)
        type_of.append(types[key])
    assert all(int(u) * GRID_W % LANES == 0 for u in u0)
    return (np.asarray(u0, np.int32), np.asarray(type_of, np.int32), np.stack(row_sel_l), col_sel,
            np.stack(valid_l), wk)


def _na_kernel(u0_ref, ty_ref, q_ref, k_ref, v_ref, bias_ref, o_ref, ot_ref, *, ctx_len):
    del ty_ref
    wk = bias_ref.shape[2]
    j = pl.program_id(1)

    def finish():
        o_ref[0] = ot_ref[...].T.astype(BF16)

    @pl.when(j == 0)
    def _():
        for i in range(NA_HEADS):
            o, l = _softmax_pv(k_ref[0, 0:ctx_len, i * LANES:(i + 1) * LANES], q_ref[0, i * LANES:(i + 1) * LANES, :],
                               v_ref[0, i * 64:(i + 1) * 64, 0:ctx_len], jnp.exp)
            ot_ref[i * 64:(i + 1) * 64, :] = o * (1.0 / l)
        finish()

    @pl.when(j > 0)
    def _():
        ws = pl.multiple_of(ctx_len + u0_ref[j - 1] * GRID_W, LANES)
        for i in range(NA_HEADS):
            q_t = q_ref[0, i * LANES:(i + 1) * LANES, :]
            s_c = jnp.dot(k_ref[0, 0:ctx_len, i * LANES:(i + 1) * LANES], q_t, preferred_element_type=F32)
            s_w = jnp.dot(k_ref[0, pl.ds(ws, wk), i * LANES:(i + 1) * LANES], q_t,
                          preferred_element_type=F32) + bias_ref[0, i]
            m = jnp.maximum(jnp.max(s_c, axis=0, keepdims=True), jnp.max(s_w, axis=0, keepdims=True))
            p_c, p_w = jnp.exp(s_c - m), jnp.exp(s_w - m)
            l = jnp.sum(p_c, axis=0, keepdims=True) + jnp.sum(p_w, axis=0, keepdims=True)
            o = (jnp.dot(v_ref[0, i * 64:(i + 1) * 64, 0:ctx_len], p_c.astype(BF16), preferred_element_type=F32)
                 + jnp.dot(v_ref[0, i * 64:(i + 1) * 64, pl.ds(ws, wk)], p_w.astype(BF16),
                           preferred_element_type=F32))
            ot_ref[i * 64:(i + 1) * 64, :] = o * (1.0 / l)
        finish()


def _na_attn(q_t, k, v_t, rpb, ctx_len):
    B, _, T = q_t.shape
    tq = ROW_TILE
    rows = (T - ctx_len) // GRID_W
    u0, type_of, row_sel, col_sel, valid, wk = _na_plan(rows)
    hp = lax.Precision.HIGHEST
    by_col = jnp.einsum("hab,bcq->hacq", rpb.astype(F32), col_sel, precision=hp)
    bias = jnp.einsum("tkra,hacq->thkcrq", row_sel, by_col, precision=hp)
    bias = jnp.where(valid[:, None], bias.reshape(row_sel.shape[0], NA_HEADS, wk, tq), MASK_VALUE)
    nb = T // tq
    assert ctx_len == tq

    grid_spec = pltpu.PrefetchScalarGridSpec(
        num_scalar_prefetch=2,
        grid=(B, nb),
        in_specs=[pl.BlockSpec((1, NA_HEADS * LANES, tq), lambda b, j, u, ty: (b, 0, j)),
                  pl.BlockSpec((1, T, NA_HEADS * LANES), lambda b, j, u, ty: (b, 0, 0)),
                  pl.BlockSpec((1, BRANCH_W, T), lambda b, j, u, ty: (b, 0, 0)),
                  pl.BlockSpec((1, NA_HEADS, wk, tq), lambda b, j, u, ty: (ty[jnp.maximum(j - 1, 0)], 0, 0, 0))],
        out_specs=pl.BlockSpec((1, tq, BRANCH_W), lambda b, j, u, ty: (b, j, 0)),
        scratch_shapes=[pltpu.VMEM((BRANCH_W, tq), F32)])
    return pl.pallas_call(
        functools.partial(_na_kernel, ctx_len=ctx_len),
        out_shape=jax.ShapeDtypeStruct((B, T, BRANCH_W), BF16),
        grid_spec=grid_spec,
        compiler_params=_cparams(("arbitrary", "arbitrary")),
        name="na_attn",
    )(jnp.asarray(u0), jnp.asarray(type_of), q_t, k, v_t, bias)


def _merge_kernel(x_ref, sc1_ref, sh1_ref, g1_ref, sc2_ref, sh2_ref, ona_ref, odf_ref, ogq_ref, oml_ref,
                  wg_ref, wb_ref, wo_ref, lng_ref, lnb_ref, rw_ref, rb_ref, tri_ref,
                  x1_ref, h2_ref, idx_ref, wts_ref, rank_ref, cnt_ref, carry_ref):
    tm = x_ref.shape[1]
    first = (pl.program_id(0) == 0) & (pl.program_id(1) == 0)

    @pl.when(first)
    def _():
        carry_ref[...] = jnp.zeros_like(carry_ref)

    x = x_ref[0]
    h1 = (x * (1.0 + sc1_ref[...]) + sh1_ref[...]).astype(BF16)
    y = None
    for i, o_ref in enumerate((ona_ref, odf_ref, ogq_ref, oml_ref)):
        gate = jax.nn.sigmoid(jnp.dot(h1, wg_ref[:, i * D_MODEL:(i + 1) * D_MODEL], preferred_element_type=F32))
        term = gate * jnp.dot(o_ref[0], wb_ref[i * BRANCH_W:(i + 1) * BRANCH_W, :], preferred_element_type=F32)
        y = term if y is None else y + term
    z = jnp.dot(y.astype(BF16), wo_ref[...], preferred_element_type=F32)
    r = DEEPNORM_ALPHA * x + g1_ref[...] * z
    mu = jnp.mean(r, axis=-1, keepdims=True)
    var = jnp.mean(jnp.square(r - mu), axis=-1, keepdims=True)
    x1 = (r - mu) * lax.rsqrt(var + EPS) * lng_ref[...] + lnb_ref[...]
    x1_ref[0] = x1
    h2 = x1 * (1.0 + sc2_ref[...]) + sh2_ref[...]
    for j in range(D_MODEL // LANES):
        h2_ref[:, j, :] = h2[:, j * LANES:(j + 1) * LANES]

    logits = lax.dot_general(rw_ref[...], h2, (((1,), (1,)), ((), ())), preferred_element_type=F32,
                             precision=lax.Precision.HIGHEST) + rb_ref[...]
    iota = lax.broadcasted_iota(I32, (N_EXPERTS, tm), 0)
    vals, idxs = [], []
    cur = logits
    for _ in range(TOP_K):
        m = jnp.max(cur, axis=0, keepdims=True)
        ik = jnp.min(jnp.where(cur == m, iota, N_EXPERTS), axis=0, keepdims=True)
        vals.append(m)
        idxs.append(ik)
        cur = jnp.where(iota == ik, -jnp.inf, cur)
    exps = [jnp.exp(v - vals[0]) for v in vals]
    denom = exps[0] + exps[1] + exps[2] + exps[3]
    wts_ref[0] = jnp.concatenate([e / denom for e in exps], axis=0)
    idx_ref[0] = jnp.concatenate(idxs, axis=0)

    onehot = jnp.zeros((N_EXPERTS, tm), F32)
    for ik in idxs:
        onehot = onehot + jnp.where(iota == ik, 1.0, 0.0)
    before = jnp.dot(onehot.astype(BF16), tri_ref[...], preferred_element_type=F32) + carry_ref[...]
    rank_ref[0] = jnp.concatenate(
        [jnp.sum(jnp.where(iota == ik, before, 0.0), axis=0, keepdims=True) for ik in idxs], axis=0).astype(I32)
    carry = carry_ref[...] + jnp.sum(onehot, axis=1, keepdims=True)
    carry_ref[...] = carry
    cnt_ref[...] = carry.astype(I32)


def _merge(xa, mod, o_na, o_df, o_gq, o_ml, wg, wb, wo, ln_g, ln_b, rw_t, rb, ctx_len, t_off):
    B, T, D = xa.shape
    tm = ROW_TILE
    nt = T // tm - t_off
    ctx_tiles = ctx_len // tm
    tq_rows = nt * tm

    def kind(t):
        return jnp.where(t + t_off >= ctx_tiles, 1, 0)

    def mod_spec(j):
        return pl.BlockSpec((None, None, None, 1, D), lambda b, t: (b, kind(t), j, 0, 0))

    def const(a):
        nd = a.ndim
        return pl.BlockSpec(a.shape, lambda b, t: (0,) * nd)

    def tok(cols):
        return pl.BlockSpec((1, tm, cols), lambda b, t: (b, t + t_off, 0))

    tri = jnp.asarray(np.triu(np.ones((tm, tm), np.float32), 1), BF16)
    rb_b = jnp.broadcast_to(rb[:, None], (N_EXPERTS, tm)).astype(F32)
    n_tiles = B * nt
    route_shape = jax.ShapeDtypeStruct((n_tiles, TOP_K, tm), I32)
    route_spec = pl.BlockSpec((1, TOP_K, tm), lambda b, t: (b * nt + t, 0, 0))
    return pl.pallas_call(
        _merge_kernel,
        out_shape=[jax.ShapeDtypeStruct((B, tq_rows, D), F32),
                   jax.ShapeDtypeStruct((B * tq_rows, D // LANES, LANES), F32),
                   route_shape, jax.ShapeDtypeStruct((n_tiles, TOP_K, tm), F32), route_shape,
                   jax.ShapeDtypeStruct((N_EXPERTS, tm), I32)],
        grid=(B, nt),
        in_specs=[tok(D), mod_spec(1), mod_spec(0), mod_spec(2), mod_spec(4), mod_spec(3),
                  tok(BRANCH_W), tok(BRANCH_W), tok(BRANCH_W), tok(BRANCH_W),
                  const(wg), const(wb), const(wo), const(ln_g), const(ln_b), const(rw_t), const(rb_b), const(tri)],
        out_specs=[pl.BlockSpec((1, tm, D), lambda b, t: (b, t, 0)),
                   pl.BlockSpec((tm, D // LANES, LANES), lambda b, t: (b * nt + t, 0, 0)),
                   route_spec, route_spec, route_spec,
                   pl.BlockSpec((N_EXPERTS, tm), lambda b, t: (0, 0))],
        scratch_shapes=[pltpu.VMEM((N_EXPERTS, tm), F32)],
        compiler_params=_cparams(("arbitrary", "arbitrary")),
        name="merge_router",
    )(xa, mod, mod, mod, mod, mod, o_na, o_df, o_gq, o_ml, wg, wb, wo, ln_g, ln_b, rw_t, rb_b, tri)


def _row_copy(src, src_row, dst, dst_row, sem):
    return pltpu.make_async_copy(src.at[pl.ds(src_row, 1)], dst.at[pl.ds(dst_row, 1)], sem)


def _dispatch_kernel(starts_ref, idx_ref, rank_ref, h_ref, xs_in_ref, xs_ref, sem):
    del xs_in_ref
    tm = idx_ref.shape[2]

    def issue(t, c):
        for k in range(TOP_K):
            d = starts_ref[idx_ref[0, k, t]] + rank_ref[0, k, t]
            _row_copy(h_ref, t, xs_ref, d, sem).start()
        return c

    lax.fori_loop(0, tm, issue, 0)

    def drain(t, c):
        for k in range(TOP_K):
            _row_copy(h_ref, 0, xs_ref, 0, sem).wait()
        return c

    lax.fori_loop(0, tm, drain, 0)


def _dispatch(starts, idx, rank, h2, n_slots):
    n_tiles, _, tm = idx.shape
    xs0 = jnp.zeros((n_slots,) + h2.shape[1:], F32)
    smem = lambda: pl.BlockSpec((1, TOP_K, tm), lambda i, s: (i, 0, 0), memory_space=pltpu.SMEM)
    grid_spec = pltpu.PrefetchScalarGridSpec(
        num_scalar_prefetch=1, grid=(n_tiles,),
        in_specs=[smem(), smem(), pl.BlockSpec((tm,) + h2.shape[1:], lambda i, s: (i, 0, 0)),
                  pl.BlockSpec(memory_space=pl.ANY)],
        out_specs=pl.BlockSpec(memory_space=pl.ANY),
        scratch_shapes=[pltpu.SemaphoreType.DMA(())])
    return pl.pallas_call(
        _dispatch_kernel,
        out_shape=jax.ShapeDtypeStruct(xs0.shape, F32),
        grid_spec=grid_spec,
        input_output_aliases={4: 0},
        compiler_params=_cparams(("arbitrary",)),
        name="moe_dispatch",
    )(starts, idx, rank, h2, xs0)


def _expert_kernel(te_ref, nu_ref, xs_ref, w1_ref, b1_ref, w2_ref, b2_ref, y_ref, w1s_ref, w2s_ref):
    i = pl.program_id(0)
    changed = (i == 0) | (te_ref[i] != te_ref[jnp.maximum(i - 1, 0)])

    @pl.when(changed)
    def _():
        w1s_ref[...] = w1_ref[...].astype(BF16)
        w2s_ref[...] = w2_ref[...].astype(BF16)

    @pl.when(i < nu_ref[0])
    def _():
        x = jnp.concatenate([xs_ref[:, j, :] for j in range(D_MODEL // LANES)], axis=1).astype(BF16)
        gu = jnp.dot(x, w1s_ref[...], preferred_element_type=F32) + b1_ref[...]
        gate = jnp.minimum(gu[:, :D_FF], SWIGLU_LIMIT)
        up = jnp.clip(gu[:, D_FF:], -SWIGLU_LIMIT, SWIGLU_LIMIT)
        act = (up + 1.0) * gate * jax.nn.sigmoid(SWIGLU_ALPHA * gate)
        y = jnp.dot(act.astype(BF16), w2s_ref[...], preferred_element_type=F32) + b2_ref[...]
        for j in range(D_MODEL // LANES):
            y_ref[:, j, :] = y[:, j * LANES:(j + 1) * LANES]

    @pl.when(i >= nu_ref[0])
    def _():
        y_ref[...] = jnp.zeros_like(y_ref)


def _experts(tile_expert, n_used, xs, w1, b1, w2, b2, layer):
    n_slots = xs.shape[0]
    tm = ROW_TILE
    nt = n_slots // tm
    E = w1.shape[1]
    grid_spec = pltpu.PrefetchScalarGridSpec(
        num_scalar_prefetch=2, grid=(nt,),
        in_specs=[pl.BlockSpec((tm, D_MODEL // LANES, LANES), lambda i, te, nu: (i, 0, 0)),
                  pl.BlockSpec((None, None, D_MODEL, 2 * D_FF), lambda i, te, nu: (layer, te[i], 0, 0)),
                  pl.BlockSpec((None, None, 1, 2 * D_FF), lambda i, te, nu: (layer, te[i], 0, 0)),
                  pl.BlockSpec((None, None, D_FF, D_MODEL), lambda i, te, nu: (layer, te[i], 0, 0)),
                  pl.BlockSpec((None, None, 1, D_MODEL), lambda i, te, nu: (layer, te[i], 0, 0))],
        out_specs=pl.BlockSpec((tm, D_MODEL // LANES, LANES), lambda i, te, nu: (i, 0, 0)),
        scratch_shapes=[pltpu.VMEM((D_MODEL, 2 * D_FF), BF16), pltpu.VMEM((D_FF, D_MODEL), BF16)])
    return pl.pallas_call(
        _expert_kernel,
        out_shape=jax.ShapeDtypeStruct(xs.shape, F32),
        grid_spec=grid_spec,
        compiler_params=_cparams(("arbitrary",)),
        name="moe_experts",
    )(tile_expert, n_used, xs, w1, b1.reshape(b1.shape[0], E, 1, -1), w2, b2.reshape(b2.shape[0], E, 1, -1))


def _combine_kernel(starts_ref, idx_ref, rank_ref, idxn_ref, rankn_ref, w_ref, x1_ref, g2_ref, lng_ref, lnb_ref,
                    y_ref, o_ref, buf_ref, sem):
    tm = idx_ref.shape[2]
    i = pl.program_id(0)
    n = pl.num_programs(0)
    slot = lax.rem(i, 2)

    def issue(ir, rr, s):
        def body(t, c):
            for k in range(TOP_K):
                d = starts_ref[ir[0, k, t]] + rr[0, k, t]
                pltpu.make_async_copy(y_ref.at[pl.ds(d, 1)], buf_ref.at[s, k, pl.ds(t, 1)], sem.at[s]).start()
            return c
        lax.fori_loop(0, tm, body, 0)

    @pl.when(i == 0)
    def _():
        issue(idx_ref, rank_ref, 0)

    @pl.when(i + 1 < n)
    def _():
        issue(idxn_ref, rankn_ref, 1 - slot)

    def drain(t, c):
        for k in range(TOP_K):
            pltpu.make_async_copy(y_ref.at[pl.ds(0, 1)], buf_ref.at[slot, k, pl.ds(0, 1)], sem.at[slot]).wait()
        return c

    lax.fori_loop(0, tm, drain, 0)

    cols = []
    for j in range(D_MODEL // LANES):
        acc = None
        for k in range(TOP_K):
            v = buf_ref[slot, k, :, j, :] * w_ref[0, :, k:k + 1]
            acc = v if acc is None else acc + v
        cols.append(acc)
    y2 = jnp.concatenate(cols, axis=1)
    r = DEEPNORM_ALPHA * x1_ref[0] + g2_ref[...] * y2
    mu = jnp.mean(r, axis=-1, keepdims=True)
    var = jnp.mean(jnp.square(r - mu), axis=-1, keepdims=True)
    o_ref[0] = (r - mu) * lax.rsqrt(var + EPS) * lng_ref[...] + lnb_ref[...]


def _combine(starts, idx, rank, wts_tm, x1, mod, ln_g, ln_b, y, ctx_len, t_off):
    B, tq_rows, D = x1.shape
    n_tiles, _, tm = idx.shape
    nt = n_tiles // B
    ctx_tiles = ctx_len // tm

    def kind(i):
        return jnp.where(i % nt + t_off >= ctx_tiles, 1, 0)

    smem = lambda f: pl.BlockSpec((1, TOP_K, tm), f, memory_space=pltpu.SMEM)
    cur = lambda i, s: (i, 0, 0)
    nxt = lambda i, s: (jnp.minimum(i + 1, n_tiles - 1), 0, 0)
    grid_spec = pltpu.PrefetchScalarGridSpec(
        num_scalar_prefetch=1, grid=(n_tiles,),
        in_specs=[smem(cur), smem(cur), smem(nxt), smem(nxt),
                  pl.BlockSpec((1, tm, TOP_K), lambda i, s: (i, 0, 0)),
                  pl.BlockSpec((1, tm, D), lambda i, s: (i // nt, i % nt, 0)),
                  pl.BlockSpec((None, None, None, 1, D), lambda i, s: (i // nt, kind(i), 5, 0, 0)),
                  pl.BlockSpec(ln_g.shape, lambda i, s: (0, 0)),
                  pl.BlockSpec(ln_b.shape, lambda i, s: (0, 0)),
                  pl.BlockSpec(memory_space=pl.ANY)],
        out_specs=pl.BlockSpec((1, tm, D), lambda i, s: (i // nt, i % nt, 0)),
        scratch_shapes=[pltpu.VMEM((2, TOP_K, tm, D // LANES, LANES), F32), pltpu.SemaphoreType.DMA((2,))])
    return pl.pallas_call(
        _combine_kernel,
        out_shape=jax.ShapeDtypeStruct((B, tq_rows, D), F32),
        grid_spec=grid_spec,
        compiler_params=_cparams(("arbitrary",)),
        name="moe_combine",
    )(starts, idx, rank, idx, rank, wts_tm, x1, mod, ln_g, ln_b, y)


_DF_HEADS_SPEC = tuple((((2 * i, i, False), (2 * i + 1, i, True)), i) for i in range(DF_HEADS))
_GQ_HEADS_SPEC = tuple((((i, i // (GQ_HEADS // GQ_KV_HEADS), False),), i // (GQ_HEADS // GQ_KV_HEADS))
                       for i in range(GQ_HEADS))
_ML_HEADS_SPEC = tuple((((i, i, False),), i) for i in range(ML_HEADS))


def _layer_weights(l, w_in, gq_qnorm, gq_knorm, ml_qa_norm, ml_wq_b, ml_kva_norm, ml_wkv_b):
    w = w_in[l]

    def cols(o, n):
        return w[:, o:o + n]

    parts = [cols(O_NAQ, 256), cols(O_NAK, 256), cols(O_NAV, 256),
             cols(O_DFQ, 256), _swap_cols(cols(O_DFQ, 256), 32), cols(O_DFK, 256), _swap_cols(cols(O_DFK, 256), 32),
             cols(O_DFV, 256),
             cols(O_GQQ, 256), _swap_cols(cols(O_GQQ, 256), 64), cols(O_GQK, 128), _swap_cols(cols(O_GQK, 128), 64),
             cols(O_GQV, 128),
             cols(O_MLQA, 256), cols(O_MLKVA, 128), cols(O_MLKR, 32), _swap_cols(cols(O_MLKR, 32), 32)]
    w_t = jnp.concatenate(parts, axis=1).T.astype(BF16)
    assert w_t.shape[0] == PROJ_ROWS

    tm = ROW_TILE
    gq_g = jnp.stack([gq_qnorm[l], _swap_cols(gq_qnorm[l], 64, signed=False),
                      gq_knorm[l], _swap_cols(gq_knorm[l], 64, signed=False)])
    gq_g = jnp.broadcast_to(gq_g[:, :, None], (4, GQ_DIM, tm)).astype(F32)
    qa_g = jnp.broadcast_to(ml_qa_norm[l][:, None], (ML_Q_RANK, tm)).astype(F32)
    kva_g = jnp.broadcast_to(ml_kva_norm[l][:, None], (ML_KV_RANK, tm)).astype(F32)
    wq = ml_wq_b[l]
    qd = ML_NOPE + ML_ROPE
    wq_parts = []
    for i in range(ML_HEADS):
        rope_cols = wq[:, i * qd + ML_NOPE:(i + 1) * qd]
        wq_parts += [wq[:, i * qd:i * qd + ML_NOPE], rope_cols, _swap_cols(rope_cols, 32)]
    wq_t = jnp.concatenate(wq_parts, axis=1).T.astype(BF16)
    wkv_t = ml_wkv_b[l].T.astype(BF16)
    return w_t, gq_g, qa_g, kva_g, wq_t, wkv_t


def kernel(x, c, ctx, c_ctx, w_ada, b_ada, w_in, na_rpb, df_lam, df_subln, gq_qnorm, gq_knorm, ml_qa_norm, ml_wq_b,
           ml_kva_norm, ml_wkv_b, w_branch, w_out, ln1_g, ln1_b, ln2_g, ln2_b, router_w, router_b, exp_w1, exp_b1,
           exp_w2, exp_b2):
    B, S, D = x.shape
    C = ctx.shape[1]
    T = C + S
    tm = ROW_TILE
    assert D == D_MODEL and C % tm == 0 and S % tm == 0 and C == tm

    xa = jnp.concatenate([ctx, x], axis=1)
    cc = jnp.zeros((16, D), F32).at[:B].set(c).at[B].set(c_ctx)
    mod_all = _ada(cc, w_ada, b_ada)
    cs32 = _rope_tables(32, C, S)
    cs64 = _rope_tables(64, C, S)

    for l in range(DEPTH):
        last = l == DEPTH - 1
        m = mod_all[l]
        mod = jnp.stack([jnp.broadcast_to(m[B][None], (B, 6 * D)), m[:B]], axis=1).reshape(B, 2, 6, 1, D)
        lam_init = 0.8 - 0.6 * math.exp(-0.3 * l)
        lp = df_lam[l].astype(F32)
        lam = (jnp.exp(jnp.sum(lp[0] * lp[1])) - jnp.exp(jnp.sum(lp[2] * lp[3])) + lam_init).reshape(1)
        w_t, gq_g, qa_g, kva_g, wq_t, wkv_t = _layer_weights(l, w_in, gq_qnorm, gq_knorm, ml_qa_norm, ml_wq_b,
                                                             ml_kva_norm, ml_wkv_b)
        (na_q, na_k, na_v, df_q, df_k, df_v, gq_q, gq_k, gq_v, ml_q, ml_k, ml_v) = _project(
            xa, mod, w_t, cs32, cs64, gq_g, qa_g, kva_g, wq_t, wkv_t, C)

        o_na = _na_attn(na_q, na_k, na_v, na_rpb[l], C)
        subln = jnp.broadcast_to((df_subln[l] * (1.0 - lam_init))[:, None], (DF_V, tm)).astype(F32)
        ones = jnp.ones((DF_V, tm), F32)
        o_df = _dense_attn(lam, df_q, df_k, df_v, subln, _DF_HEADS_SPEC, C, True, "df_attn")
        o_gq = _dense_attn(lam, gq_q, gq_k, gq_v, ones, _GQ_HEADS_SPEC, C, False, "gq_attn")
        o_ml = _dense_attn(lam, ml_q, ml_k, ml_v, ones, _ML_HEADS_SPEC, C, False, "ml_attn")

        t_off = C // tm if last else 0
        wg = w_in[l][:, O_GATES:].astype(BF16)
        wb = w_branch[l].reshape(N_BRANCH * BRANCH_W, D).astype(BF16)
        wo = w_out[l].astype(BF16)
        x1, h2, idx, wts, rank, cnt = _merge(xa, mod, o_na, o_df, o_gq, o_ml, wg, wb, wo,
                                             ln1_g[l][None], ln1_b[l][None], router_w[l].T, router_b[l], C, t_off)

        counts = cnt[:, 0]
        padded = ((counts + tm - 1) // tm) * tm
        ends = jnp.cumsum(padded)
        starts = (ends - padded).astype(I32)
        n_tok = h2.shape[0]
        n_exp_tiles = (n_tok * TOP_K) // tm + N_EXPERTS
        n_used = (ends[-1] // tm).astype(I32).reshape(1)
        tile_first = jnp.arange(n_exp_tiles, dtype=I32) * tm
        tile_expert = jnp.minimum(jnp.sum((ends[None, :] <= tile_first[:, None]).astype(I32), axis=1), N_EXPERTS - 1)
        tile_expert = jnp.where(tile_first < ends[-1], tile_expert, tile_expert[jnp.maximum(n_used[0] - 1, 0)])

        xs = _dispatch(starts, idx, rank, h2, n_exp_tiles * tm)
        ys = _experts(tile_expert, n_used, xs, exp_w1, exp_b1, exp_w2, exp_b2, l)
        xa = _combine(starts, idx, rank, wts.transpose(0, 2, 1), x1, mod, ln2_g[l][None], ln2_b[l][None], ys, C, t_off)
    return xa
```

```python
import functools
import math

import numpy as np
import jax
import jax.numpy as jnp
from jax import lax
from jax.experimental import pallas as pl
from jax.experimental.pallas import tpu as pltpu

F32, BF16, I32 = jnp.float32, jnp.bfloat16, jnp.int32

D_MODEL = 1024
DEPTH = 2
GRID_W = 64
ROPE_BASE = 10000.0
EPS = 1e-6
NA_HEADS, NA_DIM, NA_KH, NA_KW = 4, 64, 8, 16
DF_HEADS, DF_QK = 4, 32
DF_V = 2 * DF_QK
GQ_HEADS, GQ_KV_HEADS, GQ_DIM = 4, 2, 64
ML_HEADS, ML_NOPE, ML_ROPE, ML_V, ML_Q_RANK, ML_KV_RANK = 4, 64, 32, 64, 256, 128
N_BRANCH, BRANCH_W = 4, 256
N_EXPERTS, TOP_K = 32, 4
D_FF = D_MODEL
SWIGLU_LIMIT, SWIGLU_ALPHA = 7.0, 1.702
DEEPNORM_ALPHA = (2 * DEPTH) ** 0.25

IN_SIZES = (256, 256, 256, 256, 256, 256, 256, 128, 128, ML_Q_RANK, ML_KV_RANK, ML_ROPE, N_BRANCH * D_MODEL)
IN_OFFSETS = tuple(int(o) for o in np.cumsum((0,) + IN_SIZES)[:-1])
(O_NAQ, O_NAK, O_NAV, O_DFQ, O_DFK, O_DFV, O_GQQ, O_GQK, O_GQV, O_MLQA, O_MLKVA, O_MLKR, O_GATES) = IN_OFFSETS

LANES = 128
ROW_TILE = 256
MASK_VALUE = -1e30
VMEM_LIMIT = 56 * 1024 * 1024
LOG2E = math.log2(math.e)
KEY_CHUNK = 256
ACC_ROWS = 16

_PROJ_GROUPS = (("na_q", 256), ("na_k", 256), ("na_v", 256),
                ("df_q", 256), ("df_q_sw", 256), ("df_k", 256), ("df_k_sw", 256), ("df_v", 256),
                ("gq_q", 256), ("gq_q_sw", 256), ("gq_k", 128), ("gq_k_sw", 128), ("gq_v", 128),
                ("ml_qa", 256), ("ml_kva", 128), ("ml_kr", 32), ("ml_kr_sw", 32))
_PROJ_OFF = {}
_o = 0
for _n, _r in _PROJ_GROUPS:
    _PROJ_OFF[_n] = (_o, _r)
    _o += _r
PROJ_ROWS = _o


def _cparams(sem):
    return pltpu.CompilerParams(dimension_semantics=sem, vmem_limit_bytes=VMEM_LIMIT)


def _swap_cols(w, n, signed=True):
    lead, width = w.shape[:-1], w.shape[-1]
    w5 = w.reshape(lead + (width // n, 2, 2, n // 4))
    lo, hi = w5[..., 0, :], w5[..., 1, :]
    out = jnp.stack([-hi if signed else hi, lo], axis=-2)
    return out.reshape(lead + (width,))


def _rope_tables(n, ctx_len, seq):
    h, q = n // 2, n // 4
    t = jnp.arange(seq, dtype=I32)
    rows, cols = (t // GRID_W).astype(F32), (t % GRID_W).astype(F32)
    inv_freq = ROPE_BASE ** (-jnp.arange(0, h, 2, dtype=F32) / h)
    ang_r = rows[:, None] * inv_freq[None, :]
    ang_c = cols[:, None] * inv_freq[None, :]
    ang = jnp.concatenate([ang_r, ang_r, ang_c, ang_c], axis=1)
    cos = jnp.concatenate([jnp.ones((ctx_len, n), F32), jnp.cos(ang)], axis=0).T
    sin = jnp.concatenate([jnp.zeros((ctx_len, n), F32), jnp.sin(ang)], axis=0).T
    return jnp.stack([cos, sin])


def _ada_kernel(c_ref, w_ref, b_ref, o_ref):
    cc = c_ref[...]
    h = (cc * jax.nn.sigmoid(cc)).astype(BF16)
    o_ref[0] = jnp.dot(h, w_ref[0].astype(BF16), preferred_element_type=F32) + b_ref[0]


def _ada(cc, w_ada, b_ada):
    L, D, N = w_ada.shape
    R = cc.shape[0]
    tn = 512
    return pl.pallas_call(
        _ada_kernel,
        out_shape=jax.ShapeDtypeStruct((L, R, N), F32),
        grid=(L, N // tn),
        in_specs=[pl.BlockSpec((R, D), lambda l, j: (0, 0)),
                  pl.BlockSpec((1, D, tn), lambda l, j: (l, 0, j)),
                  pl.BlockSpec((1, 1, tn), lambda l, j: (l, 0, j))],
        out_specs=pl.BlockSpec((1, R, tn), lambda l, j: (l, 0, j)),
        compiler_params=_cparams(("arbitrary", "arbitrary")),
        name="ada_mod",
    )(cc, w_ada, b_ada.reshape(L, 1, N))


def _proj_kernel(x_ref, sc_ref, sh_ref, w_ref, cs32_ref, cs64_ref, gqg_ref, qag_ref, kvag_ref, wq_ref, wkv_ref,
                 na_q, na_k, na_v, df_q, df_k, df_v, gq_q, gq_k, gq_v, ml_q, ml_k, ml_v):
    tm = x_ref.shape[1]
    h = (x_ref[0] * (1.0 + sc_ref[...]) + sh_ref[...]).astype(BF16)
    p = lax.dot_general(w_ref[...], h, (((1,), (1,)), ((), ())), preferred_element_type=F32)

    def grp(name):
        o, r = _PROJ_OFF[name]
        return p[o:o + r]

    def zeros(n):
        return jnp.zeros((n, tm), F32)

    def rope(a, a_sw, cs_ref):
        n = cs_ref.shape[1]
        r = a.shape[0]
        a3, s3 = a.reshape(r // n, n, tm), a_sw.reshape(r // n, n, tm)
        return (a3 * cs_ref[0][None] + s3 * cs_ref[1][None]).reshape(r, tm)

    def head_slots(a, width):
        pieces = []
        for i in range(a.shape[0] // width):
            pieces += [a[i * width:(i + 1) * width], zeros(LANES - width)]
        return jnp.concatenate(pieces, axis=0)

    na_q[0] = head_slots(grp("na_q") * (NA_DIM ** -0.5), NA_DIM).astype(BF16)
    na_k[0] = head_slots(grp("na_k"), NA_DIM).T.astype(BF16)
    na_v[0] = grp("na_v").astype(BF16)

    q = rope(grp("df_q"), grp("df_q_sw"), cs32_ref) * (DF_QK ** -0.5 * LOG2E)
    pieces = []
    for i in range(DF_HEADS):
        q1, q2 = q[i * 64:i * 64 + 32], q[i * 64 + 32:i * 64 + 64]
        pieces += [q1, zeros(LANES - 32), zeros(32), q2, zeros(LANES - 64)]
    df_q[0] = jnp.concatenate(pieces, axis=0).astype(BF16)
    df_k[0] = head_slots(rope(grp("df_k"), grp("df_k_sw"), cs32_ref), 2 * DF_QK).T.astype(BF16)
    df_v[0] = grp("df_v").astype(BF16)

    cos64, sin64 = cs64_ref[0], cs64_ref[1]

    def norm_rope(a, a_sw, g, g_sw, scale):
        hh = a.shape[0] // GQ_DIM
        a3, s3 = a.reshape(hh, GQ_DIM, tm), a_sw.reshape(hh, GQ_DIM, tm)
        r = lax.rsqrt(jnp.mean(a3 * a3, axis=1, keepdims=True) + EPS) * scale
        return ((a3 * (g * cos64)[None] + s3 * (g_sw * sin64)[None]) * r).reshape(hh * GQ_DIM, tm)

    gq_q[0] = head_slots(norm_rope(grp("gq_q"), grp("gq_q_sw"), gqg_ref[0], gqg_ref[1], GQ_DIM ** -0.5 * LOG2E),
                         GQ_DIM).astype(BF16)
    gq_k[0] = head_slots(norm_rope(grp("gq_k"), grp("gq_k_sw"), gqg_ref[2], gqg_ref[3], 1.0),
                         GQ_DIM).T.astype(BF16)
    gq_v[0] = grp("gq_v").astype(BF16)

    qa = grp("ml_qa")
    qan = (qa * lax.rsqrt(jnp.mean(qa * qa, axis=0, keepdims=True) + EPS) * qag_ref[...]).astype(BF16)
    qq = jnp.dot(wq_ref[...], qan, preferred_element_type=F32)
    kva = grp("ml_kva")
    kvan = (kva * lax.rsqrt(jnp.mean(kva * kva, axis=0, keepdims=True) + EPS) * kvag_ref[...]).astype(BF16)
    kv = jnp.dot(wkv_ref[...], kvan, preferred_element_type=F32)
    k_rope = rope(grp("ml_kr"), grp("ml_kr_sw"), cs32_ref)
    cos32, sin32 = cs32_ref[0], cs32_ref[1]
    ml_scale = (ML_NOPE + ML_ROPE) ** -0.5 * LOG2E
    qp, kp, vp = [], [], []
    for i in range(ML_HEADS):
        b = i * LANES
        q_rope = qq[b + 64:b + 96] * cos32 + qq[b + 96:b + 128] * sin32
        qp += [qq[b:b + 64] * ml_scale, q_rope * ml_scale, zeros(32)]
        kp += [kv[b:b + 64], k_rope, zeros(32)]
        vp += [kv[b + 64:b + 128]]
    ml_q[0] = jnp.concatenate(qp, axis=0).astype(BF16)
    ml_k[0] = jnp.concatenate(kp, axis=0).T.astype(BF16)
    ml_v[0] = jnp.concatenate(vp, axis=0).astype(BF16)


def _project(xa, mod, w_t, cs32, cs64, gq_g, qa_g, kva_g, wq_t, wkv_t, ctx_len):
    B, T, D = xa.shape
    tm = ROW_TILE
    nt = T // tm
    ctx_tiles = ctx_len // tm

    def kind(t):
        return jnp.where(t >= ctx_tiles, 1, 0)

    def mod_spec(j):
        return pl.BlockSpec((None, None, None, 1, D), lambda b, t: (b, kind(t), j, 0, 0))

    def const(a):
        nd = a.ndim
        return pl.BlockSpec(a.shape, lambda b, t: (0,) * nd)

    def fm(rows):
        return jax.ShapeDtypeStruct((B, rows, T), BF16), pl.BlockSpec((1, rows, tm), lambda b, t: (b, 0, t))

    def tmj(cols):
        return jax.ShapeDtypeStruct((B, T, cols), BF16), pl.BlockSpec((1, tm, cols), lambda b, t: (b, t, 0))

    outs = [fm(512), tmj(512), fm(256),
            fm(1024), tmj(512), fm(256),
            fm(512), tmj(256), fm(128),
            fm(512), tmj(512), fm(256)]
    return pl.pallas_call(
        _proj_kernel,
        out_shape=[o[0] for o in outs],
        grid=(B, nt),
        in_specs=[pl.BlockSpec((1, tm, D), lambda b, t: (b, t, 0)),
                  mod_spec(1), mod_spec(0),
                  const(w_t),
                  pl.BlockSpec((2, 32, tm), lambda b, t: (0, 0, t)),
                  pl.BlockSpec((2, 64, tm), lambda b, t: (0, 0, t)),
                  const(gq_g), const(qa_g), const(kva_g), const(wq_t), const(wkv_t)],
        out_specs=[o[1] for o in outs],
        compiler_params=_cparams(("arbitrary", "arbitrary")),
        name="in_proj",
    )(xa, mod, mod, w_t, cs32, cs64, gq_g, qa_g, kva_g, wq_t, wkv_t)


def _softmax_pv(k, q_t, v_t, exp_fn):
    s = jnp.dot(k, q_t, preferred_element_type=F32)
    m = jnp.max(s, axis=0, keepdims=True)
    p = exp_fn(s - m)
    l = jnp.sum(p, axis=0, keepdims=True)
    o = jnp.dot(v_t, p.astype(BF16), preferred_element_type=F32)
    return o, l


def _dense_attn_kernel(lam_ref, q_ref, k_ref, v_ref, post_ref, o_ref, ot_ref, s_ref, p_ref, *, heads, ctx_len,
                       diff_norm):
    total, tq = k_ref.shape[1], q_ref.shape[2]
    items = [(qs, ks, v_idx, i, signed) for i, (terms, v_idx) in enumerate(heads) for qs, ks, signed in terms]
    last_of_head = {i: max(j for j, it in enumerate(items) if it[3] == i) for i in range(len(heads))}

    def finish_head(i, acc):
        if diff_norm:
            acc = acc * lax.rsqrt(jnp.mean(acc * acc, axis=0, keepdims=True) + EPS) * post_ref[...]
        ot_ref[i * 64:(i + 1) * 64, :] = acc

    def run_context():
        head_acc = {}
        for j, (qs, ks, v_idx, i, signed) in enumerate(items):
            o, l = _softmax_pv(k_ref[0, 0:ctx_len, ks * LANES:(ks + 1) * LANES], q_ref[0, qs * LANES:(qs + 1) * LANES, :],
                               v_ref[0, v_idx * 64:(v_idx + 1) * 64, 0:ctx_len], jnp.exp2)
            o = o * (((-lam_ref[0]) if signed else 1.0) / l)
            head_acc[i] = o if i not in head_acc else head_acc[i] + o
            if last_of_head[i] == j:
                finish_head(i, head_acc.pop(i))

    def run_latent():
        n_chunks = total // KEY_CHUNK
        half = KEY_CHUNK // 2
        n_items = len(items)
        m_of, l_of, head_acc = {}, {}, {}
        for t in range(n_items + 2):
            sc = items[t] if t < n_items else None
            ex = items[t - 1] if 0 <= t - 1 < n_items else None
            pv = items[t - 2] if 0 <= t - 2 < n_items else None
            macc = jnp.full((ACC_ROWS, tq), -jnp.inf, F32)
            lacc = jnp.zeros((ACC_ROWS, tq), F32)
            oacc = jnp.zeros((64, tq), F32)
            for c in range(n_chunks):
                r0 = c * KEY_CHUNK
                if sc is not None:
                    q_t = q_ref[0, sc[0] * LANES:(sc[0] + 1) * LANES, :]
                    for r in (r0, r0 + half):
                        s = jnp.dot(k_ref[0, r:r + half, sc[1] * LANES:(sc[1] + 1) * LANES], q_t,
                                    preferred_element_type=F32)
                        s_ref[t % 2, r:r + half, :] = s
                        for a in range(half // ACC_ROWS):
                            macc = jnp.maximum(macc, s[a * ACC_ROWS:(a + 1) * ACC_ROWS])
                if ex is not None:
                    p = jnp.exp2(s_ref[(t - 1) % 2, r0:r0 + KEY_CHUNK, :] - m_of[t - 1])
                    for a in range(KEY_CHUNK // ACC_ROWS):
                        lacc = lacc + p[a * ACC_ROWS:(a + 1) * ACC_ROWS]
                    p_ref[(t - 1) % 2, r0:r0 + KEY_CHUNK, :] = p.astype(BF16)
                if pv is not None:
                    oacc = oacc + jnp.dot(v_ref[0, pv[2] * 64:(pv[2] + 1) * 64, r0:r0 + KEY_CHUNK],
                                          p_ref[(t - 2) % 2, r0:r0 + KEY_CHUNK, :], preferred_element_type=F32)
            if sc is not None:
                m_of[t] = jnp.max(macc, axis=0, keepdims=True)
            if ex is not None:
                l_of[t - 1] = jnp.sum(lacc, axis=0, keepdims=True)
            if pv is not None:
                i, signed = pv[3], pv[4]
                o = oacc * (((-lam_ref[0]) if signed else 1.0) / l_of.pop(t - 2))
                head_acc[i] = o if i not in head_acc else head_acc[i] + o
                if last_of_head[i] == t - 2:
                    finish_head(i, head_acc.pop(i))

    qb = pl.program_id(1)

    @pl.when(qb * tq < ctx_len)
    def _():
        run_context()
        o_ref[0] = ot_ref[...].T.astype(BF16)

    @pl.when(qb * tq >= ctx_len)
    def _():
        run_latent()
        o_ref[0] = ot_ref[...].T.astype(BF16)


def _dense_attn(lam, q_t, k, v_t, post, heads, ctx_len, diff_norm, name):
    B, qrows, T = q_t.shape
    tq = ROW_TILE
    kern = functools.partial(_dense_attn_kernel, heads=heads, ctx_len=ctx_len, diff_norm=diff_norm)
    return pl.pallas_call(
        kern,
        out_shape=jax.ShapeDtypeStruct((B, T, BRANCH_W), BF16),
        grid=(B, T // tq),
        in_specs=[pl.BlockSpec(memory_space=pltpu.SMEM),
                  pl.BlockSpec((1, qrows, tq), lambda b, t: (b, 0, t)),
                  pl.BlockSpec((1, T, k.shape[2]), lambda b, t: (b, 0, 0)),
                  pl.BlockSpec((1, v_t.shape[1], T), lambda b, t: (b, 0, 0)),
                  pl.BlockSpec(post.shape, lambda b, t: (0, 0))],
        out_specs=pl.BlockSpec((1, tq, BRANCH_W), lambda b, t: (b, t, 0)),
        scratch_shapes=[pltpu.VMEM((BRANCH_W, tq), F32), pltpu.VMEM((2, T, tq), F32), pltpu.VMEM((2, T, tq), BF16)],
        compiler_params=_cparams(("arbitrary", "arbitrary")),
        name=name,
    )(lam, q_t, k, v_t, post)


def _na_plan(rows):
    kh, kw = min(NA_KH, rows), NA_KW
    q_rows = ROW_TILE // GRID_W
    win_rows = min(kh + q_rows, rows)
    nblk = rows // q_rows
    row_start = np.clip(np.arange(rows) - kh // 2, 0, rows - kh)
    col_start = np.clip(np.arange(GRID_W) - kw // 2, 0, GRID_W - kw)
    u0 = np.clip(np.arange(nblk) * q_rows - kh // 2, 0, rows - win_rows)
    wk, tq = win_rows * GRID_W, ROW_TILE
    kk, qq = np.arange(wk), np.arange(tq)
    kc, qc = kk % GRID_W, qq % GRID_W
    col_sel = (np.arange(2 * NA_KW - 1)[:, None, None]
               == (np.arange(GRID_W)[None, :, None] - np.arange(GRID_W)[None, None, :] + (NA_KW - 1))).astype(np.float32)
    types, type_of, row_sel_l, valid_l = {}, [], [], []
    for j in range(nblk):
        kr = u0[j] + kk // GRID_W
        qr = j * q_rows + qq // GRID_W
        rs = row_start[qr]
        valid = ((kr[:, None] >= rs[None, :]) & (kr[:, None] < rs[None, :] + kh)
                 & (kc[:, None] >= col_start[qc][None, :]) & (kc[:, None] < col_start[qc][None, :] + kw))
        assert valid.sum(axis=0).min() == kh * kw and valid.sum(axis=0).max() == kh * kw
        rel = (u0[j] + np.arange(win_rows))[:, None] - (j * q_rows + np.arange(q_rows))[None, :] + (NA_KH - 1)
        row_sel = (rel[:, :, None] == np.arange(2 * NA_KH - 1)[None, None, :]).astype(np.float32)
        key = (valid.tobytes(), row_sel.tobytes())
        if key not in types:
            types[key] = len(row_sel_l)
            row_sel_l.append(row_sel), valid_l.append(valid)
        type_of.append(types[key])
    assert all(int(u) * GRID_W % LANES == 0 for u in u0)
    return (np.asarray(u0, np.int32), np.asarray(type_of, np.int32), np.stack(row_sel_l), col_sel,
            np.stack(valid_l), wk)


def _na_kernel(u0_ref, ty_ref, q_ref, k_ref, v_ref, bias_ref, o_ref, ot_ref, *, ctx_len):
    del ty_ref
    wk = bias_ref.shape[2]
    j = pl.program_id(1)

    def finish():
        o_ref[0] = ot_ref[...].T.astype(BF16)

    @pl.when(j == 0)
    def _():
        for i in range(NA_HEADS):
            o, l = _softmax_pv(k_ref[0, 0:ctx_len, i * LANES:(i + 1) * LANES], q_ref[0, i * LANES:(i + 1) * LANES, :],
                               v_ref[0, i * 64:(i + 1) * 64, 0:ctx_len], jnp.exp)
            ot_ref[i * 64:(i + 1) * 64, :] = o * (1.0 / l)
        finish()

    @pl.when(j > 0)
    def _():
        ws = pl.multiple_of(ctx_len + u0_ref[j - 1] * GRID_W, LANES)
        for i in range(NA_HEADS):
            q_t = q_ref[0, i * LANES:(i + 1) * LANES, :]
            s_c = jnp.dot(k_ref[0, 0:ctx_len, i * LANES:(i + 1) * LANES], q_t, preferred_element_type=F32)
            s_w = jnp.dot(k_ref[0, pl.ds(ws, wk), i * LANES:(i + 1) * LANES], q_t,
                          preferred_element_type=F32) + bias_ref[0, i]
            m = jnp.maximum(jnp.max(s_c, axis=0, keepdims=True), jnp.max(s_w, axis=0, keepdims=True))
            p_c, p_w = jnp.exp(s_c - m), jnp.exp(s_w - m)
            l = jnp.sum(p_c, axis=0, keepdims=True) + jnp.sum(p_w, axis=0, keepdims=True)
            o = (jnp.dot(v_ref[0, i * 64:(i + 1) * 64, 0:ctx_len], p_c.astype(BF16), preferred_element_type=F32)
                 + jnp.dot(v_ref[0, i * 64:(i + 1) * 64, pl.ds(ws, wk)], p_w.astype(BF16),
                           preferred_element_type=F32))
            ot_ref[i * 64:(i + 1) * 64, :] = o * (1.0 / l)
        finish()


def _na_attn(q_t, k, v_t, rpb, ctx_len):
    B, _, T = q_t.shape
    tq = ROW_TILE
    rows = (T - ctx_len) // GRID_W
    u0, type_of, row_sel, col_sel, valid, wk = _na_plan(rows)
    hp = lax.Precision.HIGHEST
    by_col = jnp.einsum("hab,bcq->hacq", rpb.astype(F32), col_sel, precision=hp)
    bias = jnp.einsum("tkra,hacq->thkcrq", row_sel, by_col, precision=hp)
    bias = jnp.where(valid[:, None], bias.reshape(row_sel.shape[0], NA_HEADS, wk, tq), MASK_VALUE)
    nb = T // tq
    assert ctx_len == tq

    grid_spec = pltpu.PrefetchScalarGridSpec(
        num_scalar_prefetch=2,
        grid=(B, nb),
        in_specs=[pl.BlockSpec((1, NA_HEADS * LANES, tq), lambda b, j, u, ty: (b, 0, j)),
                  pl.BlockSpec((1, T, NA_HEADS * LANES), lambda b, j, u, ty: (b, 0, 0)),
                  pl.BlockSpec((1, BRANCH_W, T), lambda b, j, u, ty: (b, 0, 0)),
                  pl.BlockSpec((1, NA_HEADS, wk, tq), lambda b, j, u, ty: (ty[jnp.maximum(j - 1, 0)], 0, 0, 0))],
        out_specs=pl.BlockSpec((1, tq, BRANCH_W), lambda b, j, u, ty: (b, j, 0)),
        scratch_shapes=[pltpu.VMEM((BRANCH_W, tq), F32)])
    return pl.pallas_call(
        functools.partial(_na_kernel, ctx_len=ctx_len),
        out_shape=jax.ShapeDtypeStruct((B, T, BRANCH_W), BF16),
        grid_spec=grid_spec,
        compiler_params=_cparams(("arbitrary", "arbitrary")),
        name="na_attn",
    )(jnp.asarray(u0), jnp.asarray(type_of), q_t, k, v_t, bias)


def _merge_kernel(x_ref, sc1_ref, sh1_ref, g1_ref, sc2_ref, sh2_ref, ona_ref, odf_ref, ogq_ref, oml_ref,
                  wg_ref, wb_ref, wo_ref, lng_ref, lnb_ref, rw_ref, rb_ref, tri_ref,
                  x1_ref, h2_ref, idx_ref, wts_ref, rank_ref, cnt_ref, carry_ref):
    tm = x_ref.shape[1]
    first = (pl.program_id(0) == 0) & (pl.program_id(1) == 0)

    @pl.when(first)
    def _():
        carry_ref[...] = jnp.zeros_like(carry_ref)

    x = x_ref[0]
    h1 = (x * (1.0 + sc1_ref[...]) + sh1_ref[...]).astype(BF16)
    y = None
    for i, o_ref in enumerate((ona_ref, odf_ref, ogq_ref, oml_ref)):
        gate = jax.nn.sigmoid(jnp.dot(h1, wg_ref[:, i * D_MODEL:(i + 1) * D_MODEL], preferred_element_type=F32))
        term = gate * jnp.dot(o_ref[0], wb_ref[i * BRANCH_W:(i + 1) * BRANCH_W, :], preferred_element_type=F32)
        y = term if y is None else y + term
    z = jnp.dot(y.astype(BF16), wo_ref[...], preferred_element_type=F32)
    r = DEEPNORM_ALPHA * x + g1_ref[...] * z
    mu = jnp.mean(r, axis=-1, keepdims=True)
    var = jnp.mean(jnp.square(r - mu), axis=-1, keepdims=True)
    x1 = (r - mu) * lax.rsqrt(var + EPS) * lng_ref[...] + lnb_ref[...]
    x1_ref[0] = x1
    h2 = x1 * (1.0 + sc2_ref[...]) + sh2_ref[...]
    for j in range(D_MODEL // LANES):
        h2_ref[:, j, :] = h2[:, j * LANES:(j + 1) * LANES]

    logits = lax.dot_general(rw_ref[...], h2, (((1,), (1,)), ((), ())), preferred_element_type=F32,
                             precision=lax.Precision.HIGHEST) + rb_ref[...]
    iota = lax.broadcasted_iota(I32, (N_EXPERTS, tm), 0)
    vals, idxs = [], []
    cur = logits
    for _ in range(TOP_K):
        m = jnp.max(cur, axis=0, keepdims=True)
        ik = jnp.min(jnp.where(cur == m, iota, N_EXPERTS), axis=0, keepdims=True)
        vals.append(m)
        idxs.append(ik)
        cur = jnp.where(iota == ik, -jnp.inf, cur)
    exps = [jnp.exp(v - vals[0]) for v in vals]
    denom = exps[0] + exps[1] + exps[2] + exps[3]
    wts_ref[0] = jnp.concatenate([e / denom for e in exps], axis=0)
    idx_ref[0] = jnp.concatenate(idxs, axis=0)

    onehot = jnp.zeros((N_EXPERTS, tm), F32)
    for ik in idxs:
        onehot = onehot + jnp.where(iota == ik, 1.0, 0.0)
    before = jnp.dot(onehot.astype(BF16), tri_ref[...], preferred_element_type=F32) + carry_ref[...]
    rank_ref[0] = jnp.concatenate(
        [jnp.sum(jnp.where(iota == ik, before, 0.0), axis=0, keepdims=True) for ik in idxs], axis=0).astype(I32)
    carry = carry_ref[...] + jnp.sum(onehot, axis=1, keepdims=True)
    carry_ref[...] = carry
    cnt_ref[...] = carry.astype(I32)


def _merge(xa, mod, o_na, o_df, o_gq, o_ml, wg, wb, wo, ln_g, ln_b, rw_t, rb, ctx_len, t_off):
    B, T, D = xa.shape
    tm = ROW_TILE
    nt = T // tm - t_off
    ctx_tiles = ctx_len // tm
    tq_rows = nt * tm

    def kind(t):
        return jnp.where(t + t_off >= ctx_tiles, 1, 0)

    def mod_spec(j):
        return pl.BlockSpec((None, None, None, 1, D), lambda b, t: (b, kind(t), j, 0, 0))

    def const(a):
        nd = a.ndim
        return pl.BlockSpec(a.shape, lambda b, t: (0,) * nd)

    def tok(cols):
        return pl.BlockSpec((1, tm, cols), lambda b, t: (b, t + t_off, 0))

    tri = jnp.asarray(np.triu(np.ones((tm, tm), np.float32), 1), BF16)
    rb_b = jnp.broadcast_to(rb[:, None], (N_EXPERTS, tm)).astype(F32)
    n_tiles = B * nt
    route_shape = jax.ShapeDtypeStruct((n_tiles, TOP_K, tm), I32)
    route_spec = pl.BlockSpec((1, TOP_K, tm), lambda b, t: (b * nt + t, 0, 0))
    return pl.pallas_call(
        _merge_kernel,
        out_shape=[jax.ShapeDtypeStruct((B, tq_rows, D), F32),
                   jax.ShapeDtypeStruct((B * tq_rows, D // LANES, LANES), F32),
                   route_shape, jax.ShapeDtypeStruct((n_tiles, TOP_K, tm), F32), route_shape,
                   jax.ShapeDtypeStruct((N_EXPERTS, tm), I32)],
        grid=(B, nt),
        in_specs=[tok(D), mod_spec(1), mod_spec(0), mod_spec(2), mod_spec(4), mod_spec(3),
                  tok(BRANCH_W), tok(BRANCH_W), tok(BRANCH_W), tok(BRANCH_W),
                  const(wg), const(wb), const(wo), const(ln_g), const(ln_b), const(rw_t), const(rb_b), const(tri)],
        out_specs=[pl.BlockSpec((1, tm, D), lambda b, t: (b, t, 0)),
                   pl.BlockSpec((tm, D // LANES, LANES), lambda b, t: (b * nt + t, 0, 0)),
                   route_spec, route_spec, route_spec,
                   pl.BlockSpec((N_EXPERTS, tm), lambda b, t: (0, 0))],
        scratch_shapes=[pltpu.VMEM((N_EXPERTS, tm), F32)],
        compiler_params=_cparams(("arbitrary", "arbitrary")),
        name="merge_router",
    )(xa, mod, mod, mod, mod, mod, o_na, o_df, o_gq, o_ml, wg, wb, wo, ln_g, ln_b, rw_t, rb_b, tri)


def _row_copy(src, src_row, dst, dst_row, sem):
    return pltpu.make_async_copy(src.at[pl.ds(src_row, 1)], dst.at[pl.ds(dst_row, 1)], sem)


def _dispatch_kernel(dest_ref, h_ref, xs_in_ref, xs_ref, sem):
    del xs_in_ref
    tm = dest_ref.shape[2]

    def issue(t, c):
        for k in range(TOP_K):
            _row_copy(h_ref, t, xs_ref, dest_ref[0, k, t], sem).start()
        return c

    lax.fori_loop(0, tm, issue, 0, unroll=2)
    for k in range(TOP_K):
        pltpu.make_async_copy(h_ref, xs_ref.at[pl.ds(0, tm)], sem).wait()


def _dispatch(dest, h2, n_slots):
    n_tiles, _, tm = dest.shape
    xs0 = jnp.zeros((n_slots,) + h2.shape[1:], F32)
    return pl.pallas_call(
        _dispatch_kernel,
        out_shape=jax.ShapeDtypeStruct(xs0.shape, F32),
        grid=(n_tiles,),
        in_specs=[pl.BlockSpec((1, TOP_K, tm), lambda i: (i, 0, 0), memory_space=pltpu.SMEM),
                  pl.BlockSpec((tm,) + h2.shape[1:], lambda i: (i, 0, 0)),
                  pl.BlockSpec(memory_space=pl.ANY)],
        out_specs=pl.BlockSpec(memory_space=pl.ANY),
        scratch_shapes=[pltpu.SemaphoreType.DMA(())],
        input_output_aliases={2: 0},
        compiler_params=_cparams(("arbitrary",)),
        name="moe_dispatch",
    )(dest, h2, xs0)


def _expert_kernel(te_ref, nu_ref, xs_ref, w1_ref, b1_ref, w2_ref, b2_ref, y_ref, w1s_ref, w2s_ref):
    i = pl.program_id(0)
    changed = (i == 0) | (te_ref[i] != te_ref[jnp.maximum(i - 1, 0)])

    @pl.when(changed)
    def _():
        w1s_ref[...] = w1_ref[...].astype(BF16)
        w2s_ref[...] = w2_ref[...].astype(BF16)

    @pl.when(i < nu_ref[0])
    def _():
        x = jnp.concatenate([xs_ref[:, j, :] for j in range(D_MODEL // LANES)], axis=1).astype(BF16)
        gu = jnp.dot(x, w1s_ref[...], preferred_element_type=F32) + b1_ref[...]
        gate = jnp.minimum(gu[:, :D_FF], SWIGLU_LIMIT)
        up = jnp.clip(gu[:, D_FF:], -SWIGLU_LIMIT, SWIGLU_LIMIT)
        act = (up + 1.0) * gate * jax.nn.sigmoid(SWIGLU_ALPHA * gate)
        y = jnp.dot(act.astype(BF16), w2s_ref[...], preferred_element_type=F32) + b2_ref[...]
        for j in range(D_MODEL // LANES):
            y_ref[:, j, :] = y[:, j * LANES:(j + 1) * LANES]

    @pl.when(i >= nu_ref[0])
    def _():
        y_ref[...] = jnp.zeros_like(y_ref)


def _experts(tile_expert, n_used, xs, w1, b1, w2, b2, layer):
    n_slots = xs.shape[0]
    tm = ROW_TILE
    nt = n_slots // tm
    E = w1.shape[1]
    grid_spec = pltpu.PrefetchScalarGridSpec(
        num_scalar_prefetch=2, grid=(nt,),
        in_specs=[pl.BlockSpec((tm, D_MODEL // LANES, LANES), lambda i, te, nu: (i, 0, 0)),
                  pl.BlockSpec((None, None, D_MODEL, 2 * D_FF), lambda i, te, nu: (layer, te[i], 0, 0)),
                  pl.BlockSpec((None, None, 1, 2 * D_FF), lambda i, te, nu: (layer, te[i], 0, 0)),
                  pl.BlockSpec((None, None, D_FF, D_MODEL), lambda i, te, nu: (layer, te[i], 0, 0)),
                  pl.BlockSpec((None, None, 1, D_MODEL), lambda i, te, nu: (layer, te[i], 0, 0))],
        out_specs=pl.BlockSpec((tm, D_MODEL // LANES, LANES), lambda i, te, nu: (i, 0, 0)),
        scratch_shapes=[pltpu.VMEM((D_MODEL, 2 * D_FF), BF16), pltpu.VMEM((D_FF, D_MODEL), BF16)])
    return pl.pallas_call(
        _expert_kernel,
        out_shape=jax.ShapeDtypeStruct(xs.shape, F32),
        grid_spec=grid_spec,
        compiler_params=_cparams(("arbitrary",)),
        name="moe_experts",
    )(tile_expert, n_used, xs, w1, b1.reshape(b1.shape[0], E, 1, -1), w2, b2.reshape(b2.shape[0], E, 1, -1))


def _combine_kernel(dest_ref, destn_ref, w_ref, x1_ref, g2_ref, lng_ref, lnb_ref, y_ref, o_ref, buf_ref, y3_ref,
                    sem):
    tm = dest_ref.shape[2]
    i = pl.program_id(0)
    n = pl.num_programs(0)
    slot = lax.rem(i, 2)

    def issue(dr, s):
        def body(t, c):
            for k in range(TOP_K):
                pltpu.make_async_copy(y_ref.at[pl.ds(dr[0, k, t], 1)], buf_ref.at[s, k, pl.ds(t, 1)],
                                      sem.at[s]).start()
            return c
        lax.fori_loop(0, tm, body, 0, unroll=2)

    @pl.when(i == 0)
    def _():
        issue(dest_ref, 0)

    @pl.when(i + 1 < n)
    def _():
        issue(destn_ref, 1 - slot)

    for k in range(TOP_K):
        pltpu.make_async_copy(y_ref.at[pl.ds(0, tm)], buf_ref.at[slot, k], sem.at[slot]).wait()

    def weigh(t, c):
        acc = buf_ref[slot, 0, t] * w_ref[0, 0, t]
        for k in range(1, TOP_K):
            acc = acc + buf_ref[slot, k, t] * w_ref[0, k, t]
        y3_ref[t] = acc
        return c

    lax.fori_loop(0, tm, weigh, 0, unroll=8)
    y2 = jnp.concatenate([y3_ref[:, j, :] for j in range(D_MODEL // LANES)], axis=1)
    r = DEEPNORM_ALPHA * x1_ref[0] + g2_ref[...] * y2
    mu = jnp.mean(r, axis=-1, keepdims=True)
    var = jnp.mean(jnp.square(r - mu), axis=-1, keepdims=True)
    o_ref[0] = (r - mu) * lax.rsqrt(var + EPS) * lng_ref[...] + lnb_ref[...]


def _combine(dest, wts, x1, mod, ln_g, ln_b, y, ctx_len, t_off):
    B, tq_rows, D = x1.shape
    n_tiles, _, tm = dest.shape
    nt = n_tiles // B
    ctx_tiles = ctx_len // tm

    def kind(i):
        return jnp.where(i % nt + t_off >= ctx_tiles, 1, 0)

    smem = lambda f: pl.BlockSpec((1, TOP_K, tm), f, memory_space=pltpu.SMEM)
    cur = lambda i: (i, 0, 0)
    nxt = lambda i: (jnp.minimum(i + 1, n_tiles - 1), 0, 0)
    return pl.pallas_call(
        _combine_kernel,
        out_shape=jax.ShapeDtypeStruct((B, tq_rows, D), F32),
        grid=(n_tiles,),
        in_specs=[smem(cur), smem(nxt), smem(cur),
                  pl.BlockSpec((1, tm, D), lambda i: (i // nt, i % nt, 0)),
                  pl.BlockSpec((None, None, None, 1, D), lambda i: (i // nt, kind(i), 5, 0, 0)),
                  pl.BlockSpec(ln_g.shape, lambda i: (0, 0)),
                  pl.BlockSpec(ln_b.shape, lambda i: (0, 0)),
                  pl.BlockSpec(memory_space=pl.ANY)],
        out_specs=pl.BlockSpec((1, tm, D), lambda i: (i // nt, i % nt, 0)),
        scratch_shapes=[pltpu.VMEM((2, TOP_K, tm, D // LANES, LANES), F32),
                        pltpu.VMEM((tm, D // LANES, LANES), F32), pltpu.SemaphoreType.DMA((2,))],
        compiler_params=_cparams(("arbitrary",)),
        name="moe_combine",
    )(dest, dest, wts, x1, mod, ln_g, ln_b, y)


_DF_HEADS_SPEC = tuple((((2 * i, i, False), (2 * i + 1, i, True)), i) for i in range(DF_HEADS))
_GQ_HEADS_SPEC = tuple((((i, i // (GQ_HEADS // GQ_KV_HEADS), False),), i // (GQ_HEADS // GQ_KV_HEADS))
                       for i in range(GQ_HEADS))
_ML_HEADS_SPEC = tuple((((i, i, False),), i) for i in range(ML_HEADS))


def _layer_weights(l, w_in, gq_qnorm, gq_knorm, ml_qa_norm, ml_wq_b, ml_kva_norm, ml_wkv_b):
    w = w_in[l]

    def cols(o, n):
        return w[:, o:o + n]

    parts = [cols(O_NAQ, 256), cols(O_NAK, 256), cols(O_NAV, 256),
             cols(O_DFQ, 256), _swap_cols(cols(O_DFQ, 256), 32), cols(O_DFK, 256), _swap_cols(cols(O_DFK, 256), 32),
             cols(O_DFV, 256),
             cols(O_GQQ, 256), _swap_cols(cols(O_GQQ, 256), 64), cols(O_GQK, 128), _swap_cols(cols(O_GQK, 128), 64),
             cols(O_GQV, 128),
             cols(O_MLQA, 256), cols(O_MLKVA, 128), cols(O_MLKR, 32), _swap_cols(cols(O_MLKR, 32), 32)]
    w_t = jnp.concatenate(parts, axis=1).T.astype(BF16)
    assert w_t.shape[0] == PROJ_ROWS

    tm = ROW_TILE
    gq_g = jnp.stack([gq_qnorm[l], _swap_cols(gq_qnorm[l], 64, signed=False),
                      gq_knorm[l], _swap_cols(gq_knorm[l], 64, signed=False)])
    gq_g = jnp.broadcast_to(gq_g[:, :, None], (4, GQ_DIM, tm)).astype(F32)
    qa_g = jnp.broadcast_to(ml_qa_norm[l][:, None], (ML_Q_RANK, tm)).astype(F32)
    kva_g = jnp.broadcast_to(ml_kva_norm[l][:, None], (ML_KV_RANK, tm)).astype(F32)
    wq = ml_wq_b[l]
    qd = ML_NOPE + ML_ROPE
    wq_parts = []
    for i in range(ML_HEADS):
        rope_cols = wq[:, i * qd + ML_NOPE:(i + 1) * qd]
        wq_parts += [wq[:, i * qd:i * qd + ML_NOPE], rope_cols, _swap_cols(rope_cols, 32)]
    wq_t = jnp.concatenate(wq_parts, axis=1).T.astype(BF16)
    wkv_t = ml_wkv_b[l].T.astype(BF16)
    return w_t, gq_g, qa_g, kva_g, wq_t, wkv_t


def kernel(x, c, ctx, c_ctx, w_ada, b_ada, w_in, na_rpb, df_lam, df_subln, gq_qnorm, gq_knorm, ml_qa_norm, ml_wq_b,
           ml_kva_norm, ml_wkv_b, w_branch, w_out, ln1_g, ln1_b, ln2_g, ln2_b, router_w, router_b, exp_w1, exp_b1,
           exp_w2, exp_b2):
    B, S, D = x.shape
    C = ctx.shape[1]
    T = C + S
    tm = ROW_TILE
    assert D == D_MODEL and C % tm == 0 and S % tm == 0 and C == tm

    xa = jnp.concatenate([ctx, x], axis=1)
    cc = jnp.zeros((16, D), F32).at[:B].set(c).at[B].set(c_ctx)
    mod_all = _ada(cc, w_ada, b_ada)
    cs32 = _rope_tables(32, C, S)
    cs64 = _rope_tables(64, C, S)

    for l in range(DEPTH):
        last = l == DEPTH - 1
        m = mod_all[l]
        mod = jnp.stack([jnp.broadcast_to(m[B][None], (B, 6 * D)), m[:B]], axis=1).reshape(B, 2, 6, 1, D)
        lam_init = 0.8 - 0.6 * math.exp(-0.3 * l)
        lp = df_lam[l].astype(F32)
        lam = (jnp.exp(jnp.sum(lp[0] * lp[1])) - jnp.exp(jnp.sum(lp[2] * lp[3])) + lam_init).reshape(1)
        w_t, gq_g, qa_g, kva_g, wq_t, wkv_t = _layer_weights(l, w_in, gq_qnorm, gq_knorm, ml_qa_norm, ml_wq_b,
                                                             ml_kva_norm, ml_wkv_b)
        (na_q, na_k, na_v, df_q, df_k, df_v, gq_q, gq_k, gq_v, ml_q, ml_k, ml_v) = _project(
            xa, mod, w_t, cs32, cs64, gq_g, qa_g, kva_g, wq_t, wkv_t, C)

        o_na = _na_attn(na_q, na_k, na_v, na_rpb[l], C)
        subln = jnp.broadcast_to((df_subln[l] * (1.0 - lam_init))[:, None], (DF_V, tm)).astype(F32)
        ones = jnp.ones((DF_V, tm), F32)
        o_df = _dense_attn(lam, df_q, df_k, df_v, subln, _DF_HEADS_SPEC, C, True, "df_attn")
        o_gq = _dense_attn(lam, gq_q, gq_k, gq_v, ones, _GQ_HEADS_SPEC, C, False, "gq_attn")
        o_ml = _dense_attn(lam, ml_q, ml_k, ml_v, ones, _ML_HEADS_SPEC, C, False, "ml_attn")

        t_off = C // tm if last else 0
        wg = w_in[l][:, O_GATES:].astype(BF16)
        wb = w_branch[l].reshape(N_BRANCH * BRANCH_W, D).astype(BF16)
        wo = w_out[l].astype(BF16)
        x1, h2, idx, wts, rank, cnt = _merge(xa, mod, o_na, o_df, o_gq, o_ml, wg, wb, wo,
                                             ln1_g[l][None], ln1_b[l][None], router_w[l].T, router_b[l], C, t_off)

        counts = cnt[:, 0]
        padded = ((counts + tm - 1) // tm) * tm
        ends = jnp.cumsum(padded)
        starts = (ends - padded).astype(I32)
        n_tok = h2.shape[0]
        n_exp_tiles = (n_tok * TOP_K) // tm + N_EXPERTS
        n_used = (ends[-1] // tm).astype(I32).reshape(1)
        tile_first = jnp.arange(n_exp_tiles, dtype=I32) * tm
        tile_expert = jnp.minimum(jnp.sum((ends[None, :] <= tile_first[:, None]).astype(I32), axis=1), N_EXPERTS - 1)
        tile_expert = jnp.where(tile_first < ends[-1], tile_expert, tile_expert[jnp.maximum(n_used[0] - 1, 0)])

        dest = jnp.take(starts, idx) + rank
        xs = _dispatch(dest, h2, n_exp_tiles * tm)
        ys = _experts(tile_expert, n_used, xs, exp_w1, exp_b1, exp_w2, exp_b2, l)
        xa = _combine(dest, wts, x1, mod, ln2_g[l][None], ln2_b[l][None], ys, C, t_off)
    return xa
```

```python
import functools
import math

import numpy as np
import jax
import jax.numpy as jnp
from jax import lax
from jax.experimental import pallas as pl
from jax.experimental.pallas import tpu as pltpu

F32, BF16, I32 = jnp.float32, jnp.bfloat16, jnp.int32

D_MODEL = 1024
DEPTH = 2
GRID_W = 64
ROPE_BASE = 10000.0
EPS = 1e-6
NA_HEADS, NA_DIM, NA_KH, NA_KW = 4, 64, 8, 16
DF_HEADS, DF_QK = 4, 32
DF_V = 2 * DF_QK
GQ_HEADS, GQ_KV_HEADS, GQ_DIM = 4, 2, 64
ML_HEADS, ML_NOPE, ML_ROPE, ML_V, ML_Q_RANK, ML_KV_RANK = 4, 64, 32, 64, 256, 128
N_BRANCH, BRANCH_W = 4, 256
N_EXPERTS, TOP_K = 32, 4
D_FF = D_MODEL
SWIGLU_LIMIT, SWIGLU_ALPHA = 7.0, 1.702
DEEPNORM_ALPHA = (2 * DEPTH) ** 0.25

IN_SIZES = (256, 256, 256, 256, 256, 256, 256, 128, 128, ML_Q_RANK, ML_KV_RANK, ML_ROPE, N_BRANCH * D_MODEL)
IN_OFFSETS = tuple(int(o) for o in np.cumsum((0,) + IN_SIZES)[:-1])
(O_NAQ, O_NAK, O_NAV, O_DFQ, O_DFK, O_DFV, O_GQQ, O_GQK, O_GQV, O_MLQA, O_MLKVA, O_MLKR, O_GATES) = IN_OFFSETS

LANES = 128
ROW_TILE = 256
MASK_VALUE = -1e30
VMEM_LIMIT = 56 * 1024 * 1024
LOG2E = math.log2(math.e)
KEY_CHUNK = 256
ACC_ROWS = 16

_PROJ_GROUPS = (("na_q", 256), ("na_k", 256), ("na_v", 256),
                ("df_q", 256), ("df_q_sw", 256), ("df_k", 256), ("df_k_sw", 256), ("df_v", 256),
                ("gq_q", 256), ("gq_q_sw", 256), ("gq_k", 128), ("gq_k_sw", 128), ("gq_v", 128),
                ("ml_qa", 256), ("ml_kva", 128), ("ml_kr", 32), ("ml_kr_sw", 32))
_PROJ_OFF = {}
_o = 0
for _n, _r in _PROJ_GROUPS:
    _PROJ_OFF[_n] = (_o, _r)
    _o += _r
PROJ_ROWS = _o


def _cparams(sem):
    return pltpu.CompilerParams(dimension_semantics=sem, vmem_limit_bytes=VMEM_LIMIT)


def _swap_cols(w, n, signed=True):
    lead, width = w.shape[:-1], w.shape[-1]
    w5 = w.reshape(lead + (width // n, 2, 2, n // 4))
    lo, hi = w5[..., 0, :], w5[..., 1, :]
    out = jnp.stack([-hi if signed else hi, lo], axis=-2)
    return out.reshape(lead + (width,))


def _rope_tables(n, ctx_len, seq):
    h, q = n // 2, n // 4
    t = jnp.arange(seq, dtype=I32)
    rows, cols = (t // GRID_W).astype(F32), (t % GRID_W).astype(F32)
    inv_freq = ROPE_BASE ** (-jnp.arange(0, h, 2, dtype=F32) / h)
    ang_r = rows[:, None] * inv_freq[None, :]
    ang_c = cols[:, None] * inv_freq[None, :]
    ang = jnp.concatenate([ang_r, ang_r, ang_c, ang_c], axis=1)
    cos = jnp.concatenate([jnp.ones((ctx_len, n), F32), jnp.cos(ang)], axis=0).T
    sin = jnp.concatenate([jnp.zeros((ctx_len, n), F32), jnp.sin(ang)], axis=0).T
    return jnp.stack([cos, sin])


def _ada_kernel(c_ref, w_ref, b_ref, o_ref):
    cc = c_ref[...]
    h = (cc * jax.nn.sigmoid(cc)).astype(BF16)
    o_ref[0] = jnp.dot(h, w_ref[0].astype(BF16), preferred_element_type=F32) + b_ref[0]


def _ada(cc, w_ada, b_ada):
    L, D, N = w_ada.shape
    R = cc.shape[0]
    tn = 512
    return pl.pallas_call(
        _ada_kernel,
        out_shape=jax.ShapeDtypeStruct((L, R, N), F32),
        grid=(L, N // tn),
        in_specs=[pl.BlockSpec((R, D), lambda l, j: (0, 0)),
                  pl.BlockSpec((1, D, tn), lambda l, j: (l, 0, j)),
                  pl.BlockSpec((1, 1, tn), lambda l, j: (l, 0, j))],
        out_specs=pl.BlockSpec((1, R, tn), lambda l, j: (l, 0, j)),
        compiler_params=_cparams(("arbitrary", "arbitrary")),
        name="ada_mod",
    )(cc, w_ada, b_ada.reshape(L, 1, N))


def _proj_kernel(x_ref, sc_ref, sh_ref, w_ref, cs32_ref, cs64_ref, gqg_ref, qag_ref, kvag_ref, wq_ref, wkv_ref,
                 na_q, na_k, na_v, df_q, df_k, df_v, gq_q, gq_k, gq_v, ml_q, ml_k, ml_v):
    tm = x_ref.shape[1]
    h = (x_ref[0] * (1.0 + sc_ref[...]) + sh_ref[...]).astype(BF16)
    p = lax.dot_general(w_ref[...], h, (((1,), (1,)), ((), ())), preferred_element_type=F32)

    def grp(name):
        o, r = _PROJ_OFF[name]
        return p[o:o + r]

    def zeros(n):
        return jnp.zeros((n, tm), F32)

    def rope(a, a_sw, cs_ref):
        n = cs_ref.shape[1]
        r = a.shape[0]
        a3, s3 = a.reshape(r // n, n, tm), a_sw.reshape(r // n, n, tm)
        return (a3 * cs_ref[0][None] + s3 * cs_ref[1][None]).reshape(r, tm)

    def head_slots(a, width):
        pieces = []
        for i in range(a.shape[0] // width):
            pieces += [a[i * width:(i + 1) * width], zeros(LANES - width)]
        return jnp.concatenate(pieces, axis=0)

    na_q[0] = head_slots(grp("na_q") * (NA_DIM ** -0.5), NA_DIM).astype(BF16)
    na_k[0] = head_slots(grp("na_k"), NA_DIM).T.astype(BF16)
    na_v[0] = grp("na_v").astype(BF16)

    q = rope(grp("df_q"), grp("df_q_sw"), cs32_ref) * (DF_QK ** -0.5 * LOG2E)
    pieces = []
    for i in range(DF_HEADS):
        q1, q2 = q[i * 64:i * 64 + 32], q[i * 64 + 32:i * 64 + 64]
        pieces += [q1, zeros(LANES - 32), zeros(32), q2, zeros(LANES - 64)]
    df_q[0] = jnp.concatenate(pieces, axis=0).astype(BF16)
    df_k[0] = head_slots(rope(grp("df_k"), grp("df_k_sw"), cs32_ref), 2 * DF_QK).T.astype(BF16)
    df_v[0] = grp("df_v").astype(BF16)

    cos64, sin64 = cs64_ref[0], cs64_ref[1]

    def norm_rope(a, a_sw, g, g_sw, scale):
        hh = a.shape[0] // GQ_DIM
        a3, s3 = a.reshape(hh, GQ_DIM, tm), a_sw.reshape(hh, GQ_DIM, tm)
        r = lax.rsqrt(jnp.mean(a3 * a3, axis=1, keepdims=True) + EPS) * scale
        return ((a3 * (g * cos64)[None] + s3 * (g_sw * sin64)[None]) * r).reshape(hh * GQ_DIM, tm)

    gq_q[0] = head_slots(norm_rope(grp("gq_q"), grp("gq_q_sw"), gqg_ref[0], gqg_ref[1], GQ_DIM ** -0.5 * LOG2E),
                         GQ_DIM).astype(BF16)
    gq_k[0] = head_slots(norm_rope(grp("gq_k"), grp("gq_k_sw"), gqg_ref[2], gqg_ref[3], 1.0),
                         GQ_DIM).T.astype(BF16)
    gq_v[0] = grp("gq_v").astype(BF16)

    qa = grp("ml_qa")
    qan = (qa * lax.rsqrt(jnp.mean(qa * qa, axis=0, keepdims=True) + EPS) * qag_ref[...]).astype(BF16)
    qq = jnp.dot(wq_ref[...], qan, preferred_element_type=F32)
    kva = grp("ml_kva")
    kvan = (kva * lax.rsqrt(jnp.mean(kva * kva, axis=0, keepdims=True) + EPS) * kvag_ref[...]).astype(BF16)
    kv = jnp.dot(wkv_ref[...], kvan, preferred_element_type=F32)
    k_rope = rope(grp("ml_kr"), grp("ml_kr_sw"), cs32_ref)
    cos32, sin32 = cs32_ref[0], cs32_ref[1]
    ml_scale = (ML_NOPE + ML_ROPE) ** -0.5 * LOG2E
    qp, kp, vp = [], [], []
    for i in range(ML_HEADS):
        b = i * LANES
        q_rope = qq[b + 64:b + 96] * cos32 + qq[b + 96:b + 128] * sin32
        qp += [qq[b:b + 64] * ml_scale, q_rope * ml_scale, zeros(32)]
        kp += [kv[b:b + 64], k_rope, zeros(32)]
        vp += [kv[b + 64:b + 128]]
    ml_q[0] = jnp.concatenate(qp, axis=0).astype(BF16)
    ml_k[0] = jnp.concatenate(kp, axis=0).T.astype(BF16)
    ml_v[0] = jnp.concatenate(vp, axis=0).astype(BF16)


def _project(xa, mod, w_t, cs32, cs64, gq_g, qa_g, kva_g, wq_t, wkv_t, ctx_len):
    B, T, D = xa.shape
    tm = ROW_TILE
    nt = T // tm
    ctx_tiles = ctx_len // tm

    def kind(t):
        return jnp.where(t >= ctx_tiles, 1, 0)

    def mod_spec(j):
        return pl.BlockSpec((None, None, None, 1, D), lambda b, t: (b, kind(t), j, 0, 0))

    def const(a):
        nd = a.ndim
        return pl.BlockSpec(a.shape, lambda b, t: (0,) * nd)

    def fm(rows):
        return jax.ShapeDtypeStruct((B, rows, T), BF16), pl.BlockSpec((1, rows, tm), lambda b, t: (b, 0, t))

    def tmj(cols):
        return jax.ShapeDtypeStruct((B, T, cols), BF16), pl.BlockSpec((1, tm, cols), lambda b, t: (b, t, 0))

    outs = [fm(512), tmj(512), fm(256),
            fm(1024), tmj(512), fm(256),
            fm(512), tmj(256), fm(128),
            fm(512), tmj(512), fm(256)]
    return pl.pallas_call(
        _proj_kernel,
        out_shape=[o[0] for o in outs],
        grid=(B, nt),
        in_specs=[pl.BlockSpec((1, tm, D), lambda b, t: (b, t, 0)),
                  mod_spec(1), mod_spec(0),
                  const(w_t),
                  pl.BlockSpec((2, 32, tm), lambda b, t: (0, 0, t)),
                  pl.BlockSpec((2, 64, tm), lambda b, t: (0, 0, t)),
                  const(gq_g), const(qa_g), const(kva_g), const(wq_t), const(wkv_t)],
        out_specs=[o[1] for o in outs],
        compiler_params=_cparams(("arbitrary", "arbitrary")),
        name="in_proj",
    )(xa, mod, mod, w_t, cs32, cs64, gq_g, qa_g, kva_g, wq_t, wkv_t)


def _softmax_pv(k, q_t, v_t, exp_fn):
    s = jnp.dot(k, q_t, preferred_element_type=F32)
    m = jnp.max(s, axis=0, keepdims=True)
    p = exp_fn(s - m)
    l = jnp.sum(p, axis=0, keepdims=True)
    o = jnp.dot(v_t, p.astype(BF16), preferred_element_type=F32)
    return o, l


def _dense_attn_kernel(lam_ref, q_ref, k_ref, v_ref, post_ref, o_ref, ot_ref, s_ref, p_ref, *, heads, ctx_len,
                       diff_norm):
    total, tq = k_ref.shape[1], q_ref.shape[2]
    items = [(qs, ks, v_idx, i, signed) for i, (terms, v_idx) in enumerate(heads) for qs, ks, signed in terms]
    last_of_head = {i: max(j for j, it in enumerate(items) if it[3] == i) for i in range(len(heads))}

    def finish_head(i, acc):
        if diff_norm:
            acc = acc * lax.rsqrt(jnp.mean(acc * acc, axis=0, keepdims=True) + EPS) * post_ref[...]
        ot_ref[i * 64:(i + 1) * 64, :] = acc

    def run_context():
        head_acc = {}
        for j, (qs, ks, v_idx, i, signed) in enumerate(items):
            o, l = _softmax_pv(k_ref[0, 0:ctx_len, ks * LANES:(ks + 1) * LANES], q_ref[0, qs * LANES:(qs + 1) * LANES, :],
                               v_ref[0, v_idx * 64:(v_idx + 1) * 64, 0:ctx_len], jnp.exp2)
            o = o * (((-lam_ref[0]) if signed else 1.0) / l)
            head_acc[i] = o if i not in head_acc else head_acc[i] + o
            if last_of_head[i] == j:
                finish_head(i, head_acc.pop(i))

    def run_latent():
        n_chunks = total // KEY_CHUNK
        half = KEY_CHUNK // 2
        n_items = len(items)
        m_of, l_of, head_acc = {}, {}, {}
        for t in range(n_items + 2):
            sc = items[t] if t < n_items else None
            ex = items[t - 1] if 0 <= t - 1 < n_items else None
            pv = items[t - 2] if 0 <= t - 2 < n_items else None
            macc = jnp.full((ACC_ROWS, tq), -jnp.inf, F32)
            lacc = jnp.zeros((ACC_ROWS, tq), F32)
            oacc = jnp.zeros((64, tq), F32)
            for c in range(n_chunks):
                r0 = c * KEY_CHUNK
                if sc is not None:
                    q_t = q_ref[0, sc[0] * LANES:(sc[0] + 1) * LANES, :]
                    for r in (r0, r0 + half):
                        s = jnp.dot(k_ref[0, r:r + half, sc[1] * LANES:(sc[1] + 1) * LANES], q_t,
                                    preferred_element_type=F32)
                        s_ref[t % 2, r:r + half, :] = s
                        for a in range(half // ACC_ROWS):
                            macc = jnp.maximum(macc, s[a * ACC_ROWS:(a + 1) * ACC_ROWS])
                if ex is not None:
                    p = jnp.exp2(s_ref[(t - 1) % 2, r0:r0 + KEY_CHUNK, :] - m_of[t - 1])
                    for a in range(KEY_CHUNK // ACC_ROWS):
                        lacc = lacc + p[a * ACC_ROWS:(a + 1) * ACC_ROWS]
                    p_ref[(t - 1) % 2, r0:r0 + KEY_CHUNK, :] = p.astype(BF16)
                if pv is not None:
                    oacc = oacc + jnp.dot(v_ref[0, pv[2] * 64:(pv[2] + 1) * 64, r0:r0 + KEY_CHUNK],
                                          p_ref[(t - 2) % 2, r0:r0 + KEY_CHUNK, :], preferred_element_type=F32)
            if sc is not None:
                m_of[t] = jnp.max(macc, axis=0, keepdims=True)
            if ex is not None:
                l_of[t - 1] = jnp.sum(lacc, axis=0, keepdims=True)
            if pv is not None:
                i, signed = pv[3], pv[4]
                o = oacc * (((-lam_ref[0]) if signed else 1.0) / l_of.pop(t - 2))
                head_acc[i] = o if i not in head_acc else head_acc[i] + o
                if last_of_head[i] == t - 2:
                    finish_head(i, head_acc.pop(i))

    qb = pl.program_id(1)

    @pl.when(qb * tq < ctx_len)
    def _():
        run_context()
        o_ref[0] = ot_ref[...].T.astype(BF16)

    @pl.when(qb * tq >= ctx_len)
    def _():
        run_latent()
        o_ref[0] = ot_ref[...].T.astype(BF16)


def _dense_attn(lam, q_t, k, v_t, post, heads, ctx_len, diff_norm, name):
    B, qrows, T = q_t.shape
    tq = ROW_TILE
    kern = functools.partial(_dense_attn_kernel, heads=heads, ctx_len=ctx_len, diff_norm=diff_norm)
    return pl.pallas_call(
        kern,
        out_shape=jax.ShapeDtypeStruct((B, T, BRANCH_W), BF16),
        grid=(B, T // tq),
        in_specs=[pl.BlockSpec(memory_space=pltpu.SMEM),
                  pl.BlockSpec((1, qrows, tq), lambda b, t: (b, 0, t)),
                  pl.BlockSpec((1, T, k.shape[2]), lambda b, t: (b, 0, 0)),
                  pl.BlockSpec((1, v_t.shape[1], T), lambda b, t: (b, 0, 0)),
                  pl.BlockSpec(post.shape, lambda b, t: (0, 0))],
        out_specs=pl.BlockSpec((1, tq, BRANCH_W), lambda b, t: (b, t, 0)),
        scratch_shapes=[pltpu.VMEM((BRANCH_W, tq), F32), pltpu.VMEM((2, T, tq), F32), pltpu.VMEM((2, T, tq), BF16)],
        compiler_params=_cparams(("arbitrary", "arbitrary")),
        name=name,
    )(lam, q_t, k, v_t, post)


def _na_plan(rows):
    kh, kw = min(NA_KH, rows), NA_KW
    q_rows = ROW_TILE // GRID_W
    win_rows = min(kh + q_rows, rows)
    nblk = rows // q_rows
    row_start = np.clip(np.arange(rows) - kh // 2, 0, rows - kh)
    col_start = np.clip(np.arange(GRID_W) - kw // 2, 0, GRID_W - kw)
    u0 = np.clip(np.arange(nblk) * q_rows - kh // 2, 0, rows - win_rows)
    wk, tq = win_rows * GRID_W, ROW_TILE
    kk, qq = np.arange(wk), np.arange(tq)
    kc, qc = kk % GRID_W, qq % GRID_W
    col_sel = (np.arange(2 * NA_KW - 1)[:, None, None]
               == (np.arange(GRID_W)[None, :, None] - np.arange(GRID_W)[None, None, :] + (NA_KW - 1))).astype(np.float32)
    types, type_of, row_sel_l, valid_l = {}, [], [], []
    for j in range(nblk):
        kr = u0[j] + kk // GRID_W
        qr = j * q_rows + qq // GRID_W
        rs = row_start[qr]
        valid = ((kr[:, None] >= rs[None, :]) & (kr[:, None] < rs[None, :] + kh)
                 & (kc[:, None] >= col_start[qc][None, :]) & (kc[:, None] < col_start[qc][None, :] + kw))
        assert valid.sum(axis=0).min() == kh * kw and valid.sum(axis=0).max() == kh * kw
        rel = (u0[j] + np.arange(win_rows))[:, None] - (j * q_rows + np.arange(q_rows))[None, :] + (NA_KH - 1)
        row_sel = (rel[:, :, None] == np.arange(2 * NA_KH - 1)[None, None, :]).astype(np.float32)
        key = (valid.tobytes(), row_sel.tobytes())
        if key not in types:
            types[key] = len(row_sel_l)
            row_sel_l.append(row_sel), valid_l.append(valid)
        type_of.append(types[key])
    assert all(int(u) * GRID_W % LANES == 0 for u in u0)
    return (np.asarray(u0, np.int32), np.asarray(type_of, np.int32), np.stack(row_sel_l), col_sel,
            np.stack(valid_l), wk)


def _na_kernel(u0_ref, ty_ref, q_ref, k_ref, v_ref, bias_ref, o_ref, ot_ref, *, ctx_len):
    del ty_ref
    wk = bias_ref.shape[2]
    j = pl.program_id(1)

    def finish():
        o_ref[0] = ot_ref[...].T.astype(BF16)

    @pl.when(j == 0)
    def _():
        for i in range(NA_HEADS):
            o, l = _softmax_pv(k_ref[0, 0:ctx_len, i * LANES:(i + 1) * LANES], q_ref[0, i * LANES:(i + 1) * LANES, :],
                               v_ref[0, i * 64:(i + 1) * 64, 0:ctx_len], jnp.exp)
            ot_ref[i * 64:(i + 1) * 64, :] = o * (1.0 / l)
        finish()

    @pl.when(j > 0)
    def _():
        ws = pl.multiple_of(ctx_len + u0_ref[j - 1] * GRID_W, LANES)
        for i in range(NA_HEADS):
            q_t = q_ref[0, i * LANES:(i + 1) * LANES, :]
            s_c = jnp.dot(k_ref[0, 0:ctx_len, i * LANES:(i + 1) * LANES], q_t, preferred_element_type=F32)
            s_w = jnp.dot(k_ref[0, pl.ds(ws, wk), i * LANES:(i + 1) * LANES], q_t,
                          preferred_element_type=F32) + bias_ref[0, i]
            m = jnp.maximum(jnp.max(s_c, axis=0, keepdims=True), jnp.max(s_w, axis=0, keepdims=True))
            p_c, p_w = jnp.exp(s_c - m), jnp.exp(s_w - m)
            l = jnp.sum(p_c, axis=0, keepdims=True) + jnp.sum(p_w, axis=0, keepdims=True)
            o = (jnp.dot(v_ref[0, i * 64:(i + 1) * 64, 0:ctx_len], p_c.astype(BF16), preferred_element_type=F32)
                 + jnp.dot(v_ref[0, i * 64:(i + 1) * 64, pl.ds(ws, wk)], p_w.astype(BF16),
                           preferred_element_type=F32))
            ot_ref[i * 64:(i + 1) * 64, :] = o * (1.0 / l)
        finish()


def _na_attn(q_t, k, v_t, rpb, ctx_len):
    B, _, T = q_t.shape
    tq = ROW_TILE
    rows = (T - ctx_len) // GRID_W
    u0, type_of, row_sel, col_sel, valid, wk = _na_plan(rows)
    hp = lax.Precision.HIGHEST
    by_col = jnp.einsum("hab,bcq->hacq", rpb.astype(F32), col_sel, precision=hp)
    bias = jnp.einsum("tkra,hacq->thkcrq", row_sel, by_col, precision=hp)
    bias = jnp.where(valid[:, None], bias.reshape(row_sel.shape[0], NA_HEADS, wk, tq), MASK_VALUE)
    nb = T // tq
    assert ctx_len == tq

    grid_spec = pltpu.PrefetchScalarGridSpec(
        num_scalar_prefetch=2,
        grid=(B, nb),
        in_specs=[pl.BlockSpec((1, NA_HEADS * LANES, tq), lambda b, j, u, ty: (b, 0, j)),
                  pl.BlockSpec((1, T, NA_HEADS * LANES), lambda b, j, u, ty: (b, 0, 0)),
                  pl.BlockSpec((1, BRANCH_W, T), lambda b, j, u, ty: (b, 0, 0)),
                  pl.BlockSpec((1, NA_HEADS, wk, tq), lambda b, j, u, ty: (ty[jnp.maximum(j - 1, 0)], 0, 0, 0))],
        out_specs=pl.BlockSpec((1, tq, BRANCH_W), lambda b, j, u, ty: (b, j, 0)),
        scratch_shapes=[pltpu.VMEM((BRANCH_W, tq), F32)])
    return pl.pallas_call(
        functools.partial(_na_kernel, ctx_len=ctx_len),
        out_shape=jax.ShapeDtypeStruct((B, T, BRANCH_W), BF16),
        grid_spec=grid_spec,
        compiler_params=_cparams(("arbitrary", "arbitrary")),
        name="na_attn",
    )(jnp.asarray(u0), jnp.asarray(type_of), q_t, k, v_t, bias)


def _merge_kernel(x_ref, sc1_ref, sh1_ref, g1_ref, sc2_ref, sh2_ref, ona_ref, odf_ref, ogq_ref, oml_ref,
                  wg_ref, wb_ref, wo_ref, lng_ref, lnb_ref, rw_ref, rb_ref, tri_ref,
                  x1_ref, h2_ref, idx_ref, wts_ref, rank_ref, cnt_ref, carry_ref):
    tm = x_ref.shape[1]
    first = (pl.program_id(0) == 0) & (pl.program_id(1) == 0)

    @pl.when(first)
    def _():
        carry_ref[...] = jnp.zeros_like(carry_ref)

    x = x_ref[0]
    h1 = (x * (1.0 + sc1_ref[...]) + sh1_ref[...]).astype(BF16)
    y = None
    for i, o_ref in enumerate((ona_ref, odf_ref, ogq_ref, oml_ref)):
        gate = jax.nn.sigmoid(jnp.dot(h1, wg_ref[:, i * D_MODEL:(i + 1) * D_MODEL], preferred_element_type=F32))
        term = gate * jnp.dot(o_ref[0], wb_ref[i * BRANCH_W:(i + 1) * BRANCH_W, :], preferred_element_type=F32)
        y = term if y is None else y + term
    z = jnp.dot(y.astype(BF16), wo_ref[...], preferred_element_type=F32)
    r = DEEPNORM_ALPHA * x + g1_ref[...] * z
    mu = jnp.mean(r, axis=-1, keepdims=True)
    var = jnp.mean(jnp.square(r - mu), axis=-1, keepdims=True)
    x1 = (r - mu) * lax.rsqrt(var + EPS) * lng_ref[...] + lnb_ref[...]
    x1_ref[0] = x1
    h2 = x1 * (1.0 + sc2_ref[...]) + sh2_ref[...]
    for j in range(D_MODEL // LANES):
        h2_ref[:, j, :] = h2[:, j * LANES:(j + 1) * LANES]

    logits = lax.dot_general(rw_ref[...], h2, (((1,), (1,)), ((), ())), preferred_element_type=F32,
                             precision=lax.Precision.HIGHEST) + rb_ref[...]
    iota = lax.broadcasted_iota(I32, (N_EXPERTS, tm), 0)
    vals, idxs = [], []
    cur = logits
    for _ in range(TOP_K):
        m = jnp.max(cur, axis=0, keepdims=True)
        ik = jnp.min(jnp.where(cur == m, iota, N_EXPERTS), axis=0, keepdims=True)
        vals.append(m)
        idxs.append(ik)
        cur = jnp.where(iota == ik, -jnp.inf, cur)
    exps = [jnp.exp(v - vals[0]) for v in vals]
    denom = exps[0] + exps[1] + exps[2] + exps[3]
    wts_ref[0] = jnp.concatenate([e / denom for e in exps], axis=0)
    idx_ref[0] = jnp.concatenate(idxs, axis=0)

    onehot = jnp.zeros((N_EXPERTS, tm), F32)
    for ik in idxs:
        onehot = onehot + jnp.where(iota == ik, 1.0, 0.0)
    before = jnp.dot(onehot.astype(BF16), tri_ref[...], preferred_element_type=F32) + carry_ref[...]
    rank_ref[0] = jnp.concatenate(
        [jnp.sum(jnp.where(iota == ik, before, 0.0), axis=0, keepdims=True) for ik in idxs], axis=0).astype(I32)
    carry = carry_ref[...] + jnp.sum(onehot, axis=1, keepdims=True)
    carry_ref[...] = carry
    cnt_ref[...] = carry.astype(I32)


def _merge(xa, mod, o_na, o_df, o_gq, o_ml, wg, wb, wo, ln_g, ln_b, rw_t, rb, ctx_len, t_off):
    B, T, D = xa.shape
    tm = ROW_TILE
    nt = T // tm - t_off
    ctx_tiles = ctx_len // tm
    tq_rows = nt * tm

    def kind(t):
        return jnp.where(t + t_off >= ctx_tiles, 1, 0)

    def mod_spec(j):
        return pl.BlockSpec((None, None, None, 1, D), lambda b, t: (b, kind(t), j, 0, 0))

    def const(a):
        nd = a.ndim
        return pl.BlockSpec(a.shape, lambda b, t: (0,) * nd)

    def tok(cols):
        return pl.BlockSpec((1, tm, cols), lambda b, t: (b, t + t_off, 0))

    tri = jnp.asarray(np.triu(np.ones((tm, tm), np.float32), 1), BF16)
    rb_b = jnp.broadcast_to(rb[:, None], (N_EXPERTS, tm)).astype(F32)
    n_tiles = B * nt
    route_shape = jax.ShapeDtypeStruct((n_tiles, TOP_K, tm), I32)
    route_spec = pl.BlockSpec((1, TOP_K, tm), lambda b, t: (b * nt + t, 0, 0))
    return pl.pallas_call(
        _merge_kernel,
        out_shape=[jax.ShapeDtypeStruct((B, tq_rows, D), F32),
                   jax.ShapeDtypeStruct((B * tq_rows, D // LANES, LANES), F32),
                   route_shape, jax.ShapeDtypeStruct((n_tiles, TOP_K, tm), F32), route_shape,
                   jax.ShapeDtypeStruct((N_EXPERTS, tm), I32)],
        grid=(B, nt),
        in_specs=[tok(D), mod_spec(1), mod_spec(0), mod_spec(2), mod_spec(4), mod_spec(3),
                  tok(BRANCH_W), tok(BRANCH_W), tok(BRANCH_W), tok(BRANCH_W),
                  const(wg), const(wb), const(wo), const(ln_g), const(ln_b), const(rw_t), const(rb_b), const(tri)],
        out_specs=[pl.BlockSpec((1, tm, D), lambda b, t: (b, t, 0)),
                   pl.BlockSpec((tm, D // LANES, LANES), lambda b, t: (b * nt + t, 0, 0)),
                   route_spec, route_spec, route_spec,
                   pl.BlockSpec((N_EXPERTS, tm), lambda b, t: (0, 0))],
        scratch_shapes=[pltpu.VMEM((N_EXPERTS, tm), F32)],
        compiler_params=_cparams(("arbitrary", "arbitrary")),
        name="merge_router",
    )(xa, mod, mod, mod, mod, mod, o_na, o_df, o_gq, o_ml, wg, wb, wo, ln_g, ln_b, rw_t, rb_b, tri)


def _row_copy(src, src_row, dst, dst_row, sem):
    return pltpu.make_async_copy(src.at[pl.ds(src_row, 1)], dst.at[pl.ds(dst_row, 1)], sem)


def _dispatch_kernel(dest_ref, h_ref, xs_in_ref, xs_ref, sem):
    del xs_in_ref
    tm = dest_ref.shape[2]

    def issue(t, c):
        for k in range(TOP_K):
            _row_copy(h_ref, t, xs_ref, dest_ref[0, k, t], sem).start()
        return c

    lax.fori_loop(0, tm, issue, 0, unroll=2)
    for k in range(TOP_K):
        pltpu.make_async_copy(h_ref, xs_ref.at[pl.ds(0, tm)], sem).wait()


def _dispatch(dest, h2, n_slots):
    n_tiles, _, tm = dest.shape
    xs0 = jnp.zeros((n_slots,) + h2.shape[1:], F32)
    return pl.pallas_call(
        _dispatch_kernel,
        out_shape=jax.ShapeDtypeStruct(xs0.shape, F32),
        grid=(n_tiles,),
        in_specs=[pl.BlockSpec((1, TOP_K, tm), lambda i: (i, 0, 0), memory_space=pltpu.SMEM),
                  pl.BlockSpec((tm,) + h2.shape[1:], lambda i: (i, 0, 0)),
                  pl.BlockSpec(memory_space=pl.ANY)],
        out_specs=pl.BlockSpec(memory_space=pl.ANY),
        scratch_shapes=[pltpu.SemaphoreType.DMA(())],
        input_output_aliases={2: 0},
        compiler_params=_cparams(("arbitrary",)),
        name="moe_dispatch",
    )(dest, h2, xs0)


def _tile_relayout_copies(tiled_hbm, flat_vmem, tile, sem):
    tm = flat_vmem.shape[0]
    r0 = pl.multiple_of(tile * tm, tm)
    return [(tiled_hbm.at[pl.ds(r0, tm), j, :], flat_vmem.at[:, pl.ds(j * LANES, LANES)], sem)
            for j in range(D_MODEL // LANES)]


def _expert_kernel(te_ref, nu_ref, xs_ref, w1_ref, b1_ref, w2_ref, b2_ref, y_ref, w1s_ref, w2s_ref, x_buf, y_buf,
                   sem_in, sem_out):
    i = pl.program_id(0)
    n_used = nu_ref[0]
    slot = lax.rem(i, 2)

    def fetch(tile, s, wait):
        for hbm, vmem, sem in _tile_relayout_copies(xs_ref, x_buf.at[s], tile, sem_in.at[s]):
            cp = pltpu.make_async_copy(hbm, vmem, sem)
            cp.wait() if wait else cp.start()

    def write_back(tile, s, wait):
        for hbm, vmem, sem in _tile_relayout_copies(y_ref, y_buf.at[s], tile, sem_out.at[s]):
            cp = pltpu.make_async_copy(vmem, hbm, sem)
            cp.wait() if wait else cp.start()

    @pl.when(i == 0)
    def _():
        fetch(0, 0, False)

    @pl.when(i + 1 < n_used)
    def _():
        fetch(i + 1, 1 - slot, False)

    @pl.when((i == 0) | (te_ref[i] != te_ref[jnp.maximum(i - 1, 0)]))
    def _():
        w1s_ref[...] = w1_ref[...].astype(BF16)
        w2s_ref[...] = w2_ref[...].astype(BF16)

    @pl.when(i >= 2)
    def _():
        write_back(i - 2, slot, True)

    @pl.when(i < n_used)
    def _():
        fetch(i, slot, True)
        gu = jnp.dot(x_buf[slot].astype(BF16), w1s_ref[...], preferred_element_type=F32) + b1_ref[...]
        gate = jnp.minimum(gu[:, :D_FF], SWIGLU_LIMIT)
        up = jnp.clip(gu[:, D_FF:], -SWIGLU_LIMIT, SWIGLU_LIMIT)
        act = (up + 1.0) * gate * jax.nn.sigmoid(SWIGLU_ALPHA * gate)
        y_buf[slot] = jnp.dot(act.astype(BF16), w2s_ref[...], preferred_element_type=F32) + b2_ref[...]

    @pl.when(i >= n_used)
    def _():
        y_buf[slot] = jnp.zeros(y_buf.shape[1:], F32)

    write_back(i, slot, False)

    @pl.when(i == pl.num_programs(0) - 1)
    def _():
        write_back(i, slot, True)

        @pl.when(i >= 1)
        def _():
            write_back(i - 1, 1 - slot, True)


def _experts(tile_expert, n_used, xs, w1, b1, w2, b2, layer):
    n_slots = xs.shape[0]
    tm = ROW_TILE
    nt = n_slots // tm
    E = w1.shape[1]
    grid_spec = pltpu.PrefetchScalarGridSpec(
        num_scalar_prefetch=2, grid=(nt,),
        in_specs=[pl.BlockSpec(memory_space=pl.ANY),
                  pl.BlockSpec((None, None, D_MODEL, 2 * D_FF), lambda i, te, nu: (layer, te[i], 0, 0)),
                  pl.BlockSpec((None, None, 1, 2 * D_FF), lambda i, te, nu: (layer, te[i], 0, 0)),
                  pl.BlockSpec((None, None, D_FF, D_MODEL), lambda i, te, nu: (layer, te[i], 0, 0)),
                  pl.BlockSpec((None, None, 1, D_MODEL), lambda i, te, nu: (layer, te[i], 0, 0))],
        out_specs=pl.BlockSpec(memory_space=pl.ANY),
        scratch_shapes=[pltpu.VMEM((D_MODEL, 2 * D_FF), BF16), pltpu.VMEM((D_FF, D_MODEL), BF16),
                        pltpu.VMEM((2, tm, D_MODEL), F32), pltpu.VMEM((2, tm, D_MODEL), F32),
                        pltpu.SemaphoreType.DMA((2,)), pltpu.SemaphoreType.DMA((2,))])
    return pl.pallas_call(
        _expert_kernel,
        out_shape=jax.ShapeDtypeStruct(xs.shape, F32),
        grid_spec=grid_spec,
        compiler_params=_cparams(("arbitrary",)),
        name="moe_experts",
    )(tile_expert, n_used, xs, w1, b1.reshape(b1.shape[0], E, 1, -1), w2, b2.reshape(b2.shape[0], E, 1, -1))


def _combine_kernel(dest_ref, destn_ref, w_ref, x1_ref, g2_ref, lng_ref, lnb_ref, y_ref, o_ref, buf_ref, y3_ref,
                    sem):
    tm = dest_ref.shape[2]
    i = pl.program_id(0)
    n = pl.num_programs(0)
    slot = lax.rem(i, 2)

    def issue(dr, s):
        def body(t, c):
            for k in range(TOP_K):
                pltpu.make_async_copy(y_ref.at[pl.ds(dr[0, k, t], 1)], buf_ref.at[s, k, pl.ds(t, 1)],
                                      sem.at[s]).start()
            return c
        lax.fori_loop(0, tm, body, 0, unroll=2)

    @pl.when(i == 0)
    def _():
        issue(dest_ref, 0)

    @pl.when(i + 1 < n)
    def _():
        issue(destn_ref, 1 - slot)

    for k in range(TOP_K):
        pltpu.make_async_copy(y_ref.at[pl.ds(0, tm)], buf_ref.at[slot, k], sem.at[slot]).wait()

    def weigh(t, c):
        acc = buf_ref[slot, 0, t] * w_ref[0, 0, t]
        for k in range(1, TOP_K):
            acc = acc + buf_ref[slot, k, t] * w_ref[0, k, t]
        y3_ref[t] = acc
        return c

    lax.fori_loop(0, tm, weigh, 0, unroll=8)
    y2 = jnp.concatenate([y3_ref[:, j, :] for j in range(D_MODEL // LANES)], axis=1)
    r = DEEPNORM_ALPHA * x1_ref[0] + g2_ref[...] * y2
    mu = jnp.mean(r, axis=-1, keepdims=True)
    var = jnp.mean(jnp.square(r - mu), axis=-1, keepdims=True)
    o_ref[0] = (r - mu) * lax.rsqrt(var + EPS) * lng_ref[...] + lnb_ref[...]


def _combine(dest, wts, x1, mod, ln_g, ln_b, y, ctx_len, t_off):
    B, tq_rows, D = x1.shape
    n_tiles, _, tm = dest.shape
    nt = n_tiles // B
    ctx_tiles = ctx_len // tm

    def kind(i):
        return jnp.where(i % nt + t_off >= ctx_tiles, 1, 0)

    smem = lambda f: pl.BlockSpec((1, TOP_K, tm), f, memory_space=pltpu.SMEM)
    cur = lambda i: (i, 0, 0)
    nxt = lambda i: (jnp.minimum(i + 1, n_tiles - 1), 0, 0)
    return pl.pallas_call(
        _combine_kernel,
        out_shape=jax.ShapeDtypeStruct((B, tq_rows, D), F32),
        grid=(n_tiles,),
        in_specs=[smem(cur), smem(nxt), smem(cur),
                  pl.BlockSpec((1, tm, D), lambda i: (i // nt, i % nt, 0)),
                  pl.BlockSpec((None, None, None, 1, D), lambda i: (i // nt, kind(i), 5, 0, 0)),
                  pl.BlockSpec(ln_g.shape, lambda i: (0, 0)),
                  pl.BlockSpec(ln_b.shape, lambda i: (0, 0)),
                  pl.BlockSpec(memory_space=pl.ANY)],
        out_specs=pl.BlockSpec((1, tm, D), lambda i: (i // nt, i % nt, 0)),
        scratch_shapes=[pltpu.VMEM((2, TOP_K, tm, D // LANES, LANES), F32),
                        pltpu.VMEM((tm, D // LANES, LANES), F32), pltpu.SemaphoreType.DMA((2,))],
        compiler_params=_cparams(("arbitrary",)),
        name="moe_combine",
    )(dest, dest, wts, x1, mod, ln_g, ln_b, y)


_DF_HEADS_SPEC = tuple((((2 * i, i, False), (2 * i + 1, i, True)), i) for i in range(DF_HEADS))
_GQ_HEADS_SPEC = tuple((((i, i // (GQ_HEADS // GQ_KV_HEADS), False),), i // (GQ_HEADS // GQ_KV_HEADS))
                       for i in range(GQ_HEADS))
_ML_HEADS_SPEC = tuple((((i, i, False),), i) for i in range(ML_HEADS))


def _layer_weights(l, w_in, gq_qnorm, gq_knorm, ml_qa_norm, ml_wq_b, ml_kva_norm, ml_wkv_b):
    w = w_in[l]

    def cols(o, n):
        return w[:, o:o + n]

    parts = [cols(O_NAQ, 256), cols(O_NAK, 256), cols(O_NAV, 256),
             cols(O_DFQ, 256), _swap_cols(cols(O_DFQ, 256), 32), cols(O_DFK, 256), _swap_cols(cols(O_DFK, 256), 32),
             cols(O_DFV, 256),
             cols(O_GQQ, 256), _swap_cols(cols(O_GQQ, 256), 64), cols(O_GQK, 128), _swap_cols(cols(O_GQK, 128), 64),
             cols(O_GQV, 128),
             cols(O_MLQA, 256), cols(O_MLKVA, 128), cols(O_MLKR, 32), _swap_cols(cols(O_MLKR, 32), 32)]
    w_t = jnp.concatenate(parts, axis=1).T.astype(BF16)
    assert w_t.shape[0] == PROJ_ROWS

    tm = ROW_TILE
    gq_g = jnp.stack([gq_qnorm[l], _swap_cols(gq_qnorm[l], 64, signed=False),
                      gq_knorm[l], _swap_cols(gq_knorm[l], 64, signed=False)])
    gq_g = jnp.broadcast_to(gq_g[:, :, None], (4, GQ_DIM, tm)).astype(F32)
    qa_g = jnp.broadcast_to(ml_qa_norm[l][:, None], (ML_Q_RANK, tm)).astype(F32)
    kva_g = jnp.broadcast_to(ml_kva_norm[l][:, None], (ML_KV_RANK, tm)).astype(F32)
    wq = ml_wq_b[l]
    qd = ML_NOPE + ML_ROPE
    wq_parts = []
    for i in range(ML_HEADS):
        rope_cols = wq[:, i * qd + ML_NOPE:(i + 1) * qd]
        wq_parts += [wq[:, i * qd:i * qd + ML_NOPE], rope_cols, _swap_cols(rope_cols, 32)]
    wq_t = jnp.concatenate(wq_parts, axis=1).T.astype(BF16)
    wkv_t = ml_wkv_b[l].T.astype(BF16)
    return w_t, gq_g, qa_g, kva_g, wq_t, wkv_t


def kernel(x, c, ctx, c_ctx, w_ada, b_ada, w_in, na_rpb, df_lam, df_subln, gq_qnorm, gq_knorm, ml_qa_norm, ml_wq_b,
           ml_kva_norm, ml_wkv_b, w_branch, w_out, ln1_g, ln1_b, ln2_g, ln2_b, router_w, router_b, exp_w1, exp_b1,
           exp_w2, exp_b2):
    B, S, D = x.shape
    C = ctx.shape[1]
    T = C + S
    tm = ROW_TILE
    assert D == D_MODEL and C % tm == 0 and S % tm == 0 and C == tm

    xa = jnp.concatenate([ctx, x], axis=1)
    cc = jnp.zeros((16, D), F32).at[:B].set(c).at[B].set(c_ctx)
    mod_all = _ada(cc, w_ada, b_ada)
    cs32 = _rope_tables(32, C, S)
    cs64 = _rope_tables(64, C, S)

    for l in range(DEPTH):
        last = l == DEPTH - 1
        m = mod_all[l]
        mod = jnp.stack([jnp.broadcast_to(m[B][None], (B, 6 * D)), m[:B]], axis=1).reshape(B, 2, 6, 1, D)
        lam_init = 0.8 - 0.6 * math.exp(-0.3 * l)
        lp = df_lam[l].astype(F32)
        lam = (jnp.exp(jnp.sum(lp[0] * lp[1])) - jnp.exp(jnp.sum(lp[2] * lp[3])) + lam_init).reshape(1)
        w_t, gq_g, qa_g, kva_g, wq_t, wkv_t = _layer_weights(l, w_in, gq_qnorm, gq_knorm, ml_qa_norm, ml_wq_b,
                                                             ml_kva_norm, ml_wkv_b)
        (na_q, na_k, na_v, df_q, df_k, df_v, gq_q, gq_k, gq_v, ml_q, ml_k, ml_v) = _project(
            xa, mod, w_t, cs32, cs64, gq_g, qa_g, kva_g, wq_t, wkv_t, C)

        o_na = _na_attn(na_q, na_k, na_v, na_rpb[l], C)
        subln = jnp.broadcast_to((df_subln[l] * (1.0 - lam_init))[:, None], (DF_V, tm)).astype(F32)
        ones = jnp.ones((DF_V, tm), F32)
        o_df = _dense_attn(lam, df_q, df_k, df_v, subln, _DF_HEADS_SPEC, C, True, "df_attn")
        o_gq = _dense_attn(lam, gq_q, gq_k, gq_v, ones, _GQ_HEADS_SPEC, C, False, "gq_attn")
        o_ml = _dense_attn(lam, ml_q, ml_k, ml_v, ones, _ML_HEADS_SPEC, C, False, "ml_attn")

        t_off = C // tm if last else 0
        wg = w_in[l][:, O_GATES:].astype(BF16)
        wb = w_branch[l].reshape(N_BRANCH * BRANCH_W, D).astype(BF16)
        wo = w_out[l].astype(BF16)
        x1, h2, idx, wts, rank, cnt = _merge(xa, mod, o_na, o_df, o_gq, o_ml, wg, wb, wo,
                                             ln1_g[l][None], ln1_b[l][None], router_w[l].T, router_b[l], C, t_off)

        counts = cnt[:, 0]
        padded = ((counts + tm - 1) // tm) * tm
        ends = jnp.cumsum(padded)
        starts = (ends - padded).astype(I32)
        n_tok = h2.shape[0]
        n_exp_tiles = (n_tok * TOP_K) // tm + N_EXPERTS
        n_used = (ends[-1] // tm).astype(I32).reshape(1)
        tile_first = jnp.arange(n_exp_tiles, dtype=I32) * tm
        tile_expert = jnp.minimum(jnp.sum((ends[None, :] <= tile_first[:, None]).astype(I32), axis=1), N_EXPERTS - 1)
        tile_expert = jnp.where(tile_first < ends[-1], tile_expert, tile_expert[jnp.maximum(n_used[0] - 1, 0)])

        dest = jnp.sum(jnp.where(idx[..., None] == jnp.arange(N_EXPERTS, dtype=I32), starts, 0), axis=-1) + rank
        xs = _dispatch(dest, h2, n_exp_tiles * tm)
        ys = _experts(tile_expert, n_used, xs, exp_w1, exp_b1, exp_w2, exp_b2, l)
        xa = _combine(dest, wts, x1, mod, ln2_g[l][None], ln2_b[l][None], ys, C, t_off)
    return xa
```

```python
import functools
import math

import numpy as np
import jax
import jax.numpy as jnp
from jax import lax
from jax.experimental import pallas as pl
from jax.experimental.pallas import tpu as pltpu

F32, BF16, I32 = jnp.float32, jnp.bfloat16, jnp.int32

D_MODEL = 1024
DEPTH = 2
GRID_W = 64
ROPE_BASE = 10000.0
EPS = 1e-6
NA_HEADS, NA_DIM, NA_KH, NA_KW = 4, 64, 8, 16
DF_HEADS, DF_QK = 4, 32
DF_V = 2 * DF_QK
GQ_HEADS, GQ_KV_HEADS, GQ_DIM = 4, 2, 64
ML_HEADS, ML_NOPE, ML_ROPE, ML_V, ML_Q_RANK, ML_KV_RANK = 4, 64, 32, 64, 256, 128
N_BRANCH, BRANCH_W = 4, 256
N_EXPERTS, TOP_K = 32, 4
D_FF = D_MODEL
SWIGLU_LIMIT, SWIGLU_ALPHA = 7.0, 1.702
DEEPNORM_ALPHA = (2 * DEPTH) ** 0.25

IN_SIZES = (256, 256, 256, 256, 256, 256, 256, 128, 128, ML_Q_RANK, ML_KV_RANK, ML_ROPE, N_BRANCH * D_MODEL)
IN_OFFSETS = tuple(int(o) for o in np.cumsum((0,) + IN_SIZES)[:-1])
(O_NAQ, O_NAK, O_NAV, O_DFQ, O_DFK, O_DFV, O_GQQ, O_GQK, O_GQV, O_MLQA, O_MLKVA, O_MLKR, O_GATES) = IN_OFFSETS

LANES = 128
ROW_TILE = 256
MASK_VALUE = -1e30
VMEM_LIMIT = 56 * 1024 * 1024
LOG2E = math.log2(math.e)
KEY_CHUNK = 256
ACC_ROWS = 16
SCORE_SPLIT = 1

_PROJ_GROUPS = (("na_q", 256), ("na_k", 256), ("na_v", 256),
                ("df_q", 256), ("df_q_sw", 256), ("df_k", 256), ("df_k_sw", 256), ("df_v", 256),
                ("gq_q", 256), ("gq_q_sw", 256), ("gq_k", 128), ("gq_k_sw", 128), ("gq_v", 128),
                ("ml_qa", 256), ("ml_kva", 128), ("ml_kr", 32), ("ml_kr_sw", 32))
_PROJ_OFF = {}
_o = 0
for _n, _r in _PROJ_GROUPS:
    _PROJ_OFF[_n] = (_o, _r)
    _o += _r
PROJ_ROWS = _o


def _cparams(sem):
    return pltpu.CompilerParams(dimension_semantics=sem, vmem_limit_bytes=VMEM_LIMIT)


def _swap_cols(w, n, signed=True):
    lead, width = w.shape[:-1], w.shape[-1]
    w5 = w.reshape(lead + (width // n, 2, 2, n // 4))
    lo, hi = w5[..., 0, :], w5[..., 1, :]
    out = jnp.stack([-hi if signed else hi, lo], axis=-2)
    return out.reshape(lead + (width,))


def _rope_tables(n, ctx_len, seq):
    h, q = n // 2, n // 4
    t = jnp.arange(seq, dtype=I32)
    rows, cols = (t // GRID_W).astype(F32), (t % GRID_W).astype(F32)
    inv_freq = ROPE_BASE ** (-jnp.arange(0, h, 2, dtype=F32) / h)
    ang_r = rows[:, None] * inv_freq[None, :]
    ang_c = cols[:, None] * inv_freq[None, :]
    ang = jnp.concatenate([ang_r, ang_r, ang_c, ang_c], axis=1)
    cos = jnp.concatenate([jnp.ones((ctx_len, n), F32), jnp.cos(ang)], axis=0).T
    sin = jnp.concatenate([jnp.zeros((ctx_len, n), F32), jnp.sin(ang)], axis=0).T
    return jnp.stack([cos, sin])


def _ada_kernel(c_ref, w_ref, b_ref, o_ref):
    cc = c_ref[...]
    h = (cc * jax.nn.sigmoid(cc)).astype(BF16)
    o_ref[0] = jnp.dot(h, w_ref[0].astype(BF16), preferred_element_type=F32) + b_ref[0]


def _ada(cc, w_ada, b_ada):
    L, D, N = w_ada.shape
    R = cc.shape[0]
    tn = 512
    return pl.pallas_call(
        _ada_kernel,
        out_shape=jax.ShapeDtypeStruct((L, R, N), F32),
        grid=(L, N // tn),
        in_specs=[pl.BlockSpec((R, D), lambda l, j: (0, 0)),
                  pl.BlockSpec((1, D, tn), lambda l, j: (l, 0, j)),
                  pl.BlockSpec((1, 1, tn), lambda l, j: (l, 0, j))],
        out_specs=pl.BlockSpec((1, R, tn), lambda l, j: (l, 0, j)),
        compiler_params=_cparams(("arbitrary", "arbitrary")),
        name="ada_mod",
    )(cc, w_ada, b_ada.reshape(L, 1, N))


def _proj_kernel(x_ref, sc_ref, sh_ref, w_ref, cs32_ref, cs64_ref, gqg_ref, qag_ref, kvag_ref, wq_ref, wkv_ref,
                 na_q, na_k, na_v, df_q, df_k, df_v, gq_q, gq_k, gq_v, ml_q, ml_k, ml_v):
    tm = x_ref.shape[1]
    h = (x_ref[0] * (1.0 + sc_ref[...]) + sh_ref[...]).astype(BF16)
    p = lax.dot_general(w_ref[...], h, (((1,), (1,)), ((), ())), preferred_element_type=F32)

    def grp(name):
        o, r = _PROJ_OFF[name]
        return p[o:o + r]

    def zeros(n):
        return jnp.zeros((n, tm), F32)

    def rope(a, a_sw, cs_ref):
        n = cs_ref.shape[1]
        r = a.shape[0]
        a3, s3 = a.reshape(r // n, n, tm), a_sw.reshape(r // n, n, tm)
        return (a3 * cs_ref[0][None] + s3 * cs_ref[1][None]).reshape(r, tm)

    def head_slots(a, width):
        pieces = []
        for i in range(a.shape[0] // width):
            pieces += [a[i * width:(i + 1) * width], zeros(LANES - width)]
        return jnp.concatenate(pieces, axis=0)

    na_q[0] = head_slots(grp("na_q") * (NA_DIM ** -0.5), NA_DIM).astype(BF16)
    na_k[0] = head_slots(grp("na_k"), NA_DIM).T.astype(BF16)
    na_v[0] = grp("na_v").astype(BF16)

    q = rope(grp("df_q"), grp("df_q_sw"), cs32_ref) * (DF_QK ** -0.5 * LOG2E)
    pieces = []
    for i in range(DF_HEADS):
        q1, q2 = q[i * 64:i * 64 + 32], q[i * 64 + 32:i * 64 + 64]
        pieces += [q1, zeros(LANES - 32), zeros(32), q2, zeros(LANES - 64)]
    df_q[0] = jnp.concatenate(pieces, axis=0).astype(BF16)
    df_k[0] = head_slots(rope(grp("df_k"), grp("df_k_sw"), cs32_ref), 2 * DF_QK).T.astype(BF16)
    df_v[0] = grp("df_v").astype(BF16)

    cos64, sin64 = cs64_ref[0], cs64_ref[1]

    def norm_rope(a, a_sw, g, g_sw, scale):
        hh = a.shape[0] // GQ_DIM
        a3, s3 = a.reshape(hh, GQ_DIM, tm), a_sw.reshape(hh, GQ_DIM, tm)
        r = lax.rsqrt(jnp.mean(a3 * a3, axis=1, keepdims=True) + EPS) * scale
        return ((a3 * (g * cos64)[None] + s3 * (g_sw * sin64)[None]) * r).reshape(hh * GQ_DIM, tm)

    gq_q[0] = head_slots(norm_rope(grp("gq_q"), grp("gq_q_sw"), gqg_ref[0], gqg_ref[1], GQ_DIM ** -0.5 * LOG2E),
                         GQ_DIM).astype(BF16)
    gq_k[0] = head_slots(norm_rope(grp("gq_k"), grp("gq_k_sw"), gqg_ref[2], gqg_ref[3], 1.0),
                         GQ_DIM).T.astype(BF16)
    gq_v[0] = grp("gq_v").astype(BF16)

    qa = grp("ml_qa")
    qan = (qa * lax.rsqrt(jnp.mean(qa * qa, axis=0, keepdims=True) + EPS) * qag_ref[...]).astype(BF16)
    qq = jnp.dot(wq_ref[...], qan, preferred_element_type=F32)
    kva = grp("ml_kva")
    kvan = (kva * lax.rsqrt(jnp.mean(kva * kva, axis=0, keepdims=True) + EPS) * kvag_ref[...]).astype(BF16)
    kv = jnp.dot(wkv_ref[...], kvan, preferred_element_type=F32)
    k_rope = rope(grp("ml_kr"), grp("ml_kr_sw"), cs32_ref)
    cos32, sin32 = cs32_ref[0], cs32_ref[1]
    ml_scale = (ML_NOPE + ML_ROPE) ** -0.5 * LOG2E
    qp, kp, vp = [], [], []
    for i in range(ML_HEADS):
        b = i * LANES
        q_rope = qq[b + 64:b + 96] * cos32 + qq[b + 96:b + 128] * sin32
        qp += [qq[b:b + 64] * ml_scale, q_rope * ml_scale, zeros(32)]
        kp += [kv[b:b + 64], k_rope, zeros(32)]
        vp += [kv[b + 64:b + 128]]
    ml_q[0] = jnp.concatenate(qp, axis=0).astype(BF16)
    ml_k[0] = jnp.concatenate(kp, axis=0).T.astype(BF16)
    ml_v[0] = jnp.concatenate(vp, axis=0).astype(BF16)


def _project(xa, mod, w_t, cs32, cs64, gq_g, qa_g, kva_g, wq_t, wkv_t, ctx_len):
    B, T, D = xa.shape
    tm = ROW_TILE
    nt = T // tm
    ctx_tiles = ctx_len // tm

    def kind(t):
        return jnp.where(t >= ctx_tiles, 1, 0)

    def mod_spec(j):
        return pl.BlockSpec((None, None, None, 1, D), lambda b, t: (b, kind(t), j, 0, 0))

    def const(a):
        nd = a.ndim
        return pl.BlockSpec(a.shape, lambda b, t: (0,) * nd)

    def fm(rows):
        return jax.ShapeDtypeStruct((B, rows, T), BF16), pl.BlockSpec((1, rows, tm), lambda b, t: (b, 0, t))

    def tmj(cols):
        return jax.ShapeDtypeStruct((B, T, cols), BF16), pl.BlockSpec((1, tm, cols), lambda b, t: (b, t, 0))

    outs = [fm(512), tmj(512), fm(256),
            fm(1024), tmj(512), fm(256),
            fm(512), tmj(256), fm(128),
            fm(512), tmj(512), fm(256)]
    return pl.pallas_call(
        _proj_kernel,
        out_shape=[o[0] for o in outs],
        grid=(B, nt),
        in_specs=[pl.BlockSpec((1, tm, D), lambda b, t: (b, t, 0)),
                  mod_spec(1), mod_spec(0),
                  const(w_t),
                  pl.BlockSpec((2, 32, tm), lambda b, t: (0, 0, t)),
                  pl.BlockSpec((2, 64, tm), lambda b, t: (0, 0, t)),
                  const(gq_g), const(qa_g), const(kva_g), const(wq_t), const(wkv_t)],
        out_specs=[o[1] for o in outs],
        compiler_params=_cparams(("arbitrary", "arbitrary")),
        name="in_proj",
    )(xa, mod, mod, w_t, cs32, cs64, gq_g, qa_g, kva_g, wq_t, wkv_t)


def _softmax_pv(k, q_t, v_t, exp_fn):
    s = jnp.dot(k, q_t, preferred_element_type=F32)
    m = jnp.max(s, axis=0, keepdims=True)
    p = exp_fn(s - m)
    l = jnp.sum(p, axis=0, keepdims=True)
    o = jnp.dot(v_t, p.astype(BF16), preferred_element_type=F32)
    return o, l


def _dense_attn_kernel(lam_ref, q_ref, k_ref, v_ref, post_ref, o_ref, ot_ref, s_ref, p_ref, *, heads, ctx_len,
                       diff_norm):
    total, tq = k_ref.shape[1], q_ref.shape[2]
    items = [(qs, ks, v_idx, i, signed) for i, (terms, v_idx) in enumerate(heads) for qs, ks, signed in terms]
    last_of_head = {i: max(j for j, it in enumerate(items) if it[3] == i) for i in range(len(heads))}

    def finish_head(i, acc):
        if diff_norm:
            acc = acc * lax.rsqrt(jnp.mean(acc * acc, axis=0, keepdims=True) + EPS) * post_ref[...]
        ot_ref[i * 64:(i + 1) * 64, :] = acc

    def run_context():
        head_acc = {}
        for j, (qs, ks, v_idx, i, signed) in enumerate(items):
            o, l = _softmax_pv(k_ref[0, 0:ctx_len, ks * LANES:(ks + 1) * LANES], q_ref[0, qs * LANES:(qs + 1) * LANES, :],
                               v_ref[0, v_idx * 64:(v_idx + 1) * 64, 0:ctx_len], jnp.exp2)
            o = o * (((-lam_ref[0]) if signed else 1.0) / l)
            head_acc[i] = o if i not in head_acc else head_acc[i] + o
            if last_of_head[i] == j:
                finish_head(i, head_acc.pop(i))

    def run_latent():
        n_chunks = total // KEY_CHUNK
        half = KEY_CHUNK // SCORE_SPLIT
        n_items = len(items)
        m_of, l_of, head_acc = {}, {}, {}
        for t in range(n_items + 2):
            sc = items[t] if t < n_items else None
            ex = items[t - 1] if 0 <= t - 1 < n_items else None
            pv = items[t - 2] if 0 <= t - 2 < n_items else None
            macc = jnp.full((ACC_ROWS, tq), -jnp.inf, F32)
            lacc = jnp.zeros((ACC_ROWS, tq), F32)
            oacc = jnp.zeros((64, tq), F32)
            for c in range(n_chunks):
                r0 = c * KEY_CHUNK
                if sc is not None:
                    q_t = q_ref[0, sc[0] * LANES:(sc[0] + 1) * LANES, :]
                    for r in ((r0,) if half == KEY_CHUNK else (r0, r0 + half)):
                        s = jnp.dot(k_ref[0, r:r + half, sc[1] * LANES:(sc[1] + 1) * LANES], q_t,
                                    preferred_element_type=F32)
                        s_ref[t % 2, r:r + half, :] = s
                        for a in range(half // ACC_ROWS):
                            macc = jnp.maximum(macc, s[a * ACC_ROWS:(a + 1) * ACC_ROWS])
                if ex is not None:
                    p = jnp.exp2(s_ref[(t - 1) % 2, r0:r0 + KEY_CHUNK, :] - m_of[t - 1])
                    for a in range(KEY_CHUNK // ACC_ROWS):
                        lacc = lacc + p[a * ACC_ROWS:(a + 1) * ACC_ROWS]
                    p_ref[(t - 1) % 2, r0:r0 + KEY_CHUNK, :] = p.astype(BF16)
                if pv is not None:
                    oacc = oacc + jnp.dot(v_ref[0, pv[2] * 64:(pv[2] + 1) * 64, r0:r0 + KEY_CHUNK],
                                          p_ref[(t - 2) % 2, r0:r0 + KEY_CHUNK, :], preferred_element_type=F32)
            if sc is not None:
                m_of[t] = jnp.max(macc, axis=0, keepdims=True)
            if ex is not None:
                l_of[t - 1] = jnp.sum(lacc, axis=0, keepdims=True)
            if pv is not None:
                i, signed = pv[3], pv[4]
                o = oacc * (((-lam_ref[0]) if signed else 1.0) / l_of.pop(t - 2))
                head_acc[i] = o if i not in head_acc else head_acc[i] + o
                if last_of_head[i] == t - 2:
                    finish_head(i, head_acc.pop(i))

    qb = pl.program_id(1)

    @pl.when(qb * tq < ctx_len)
    def _():
        run_context()
        o_ref[0] = ot_ref[...].T.astype(BF16)

    @pl.when(qb * tq >= ctx_len)
    def _():
        run_latent()
        o_ref[0] = ot_ref[...].T.astype(BF16)


def _dense_attn(lam, q_t, k, v_t, post, heads, ctx_len, diff_norm, name):
    B, qrows, T = q_t.shape
    tq = ROW_TILE
    kern = functools.partial(_dense_attn_kernel, heads=heads, ctx_len=ctx_len, diff_norm=diff_norm)
    return pl.pallas_call(
        kern,
        out_shape=jax.ShapeDtypeStruct((B, T, BRANCH_W), BF16),
        grid=(B, T // tq),
        in_specs=[pl.BlockSpec(memory_space=pltpu.SMEM),
                  pl.BlockSpec((1, qrows, tq), lambda b, t: (b, 0, t)),
                  pl.BlockSpec((1, T, k.shape[2]), lambda b, t: (b, 0, 0)),
                  pl.BlockSpec((1, v_t.shape[1], T), lambda b, t: (b, 0, 0)),
                  pl.BlockSpec(post.shape, lambda b, t: (0, 0))],
        out_specs=pl.BlockSpec((1, tq, BRANCH_W), lambda b, t: (b, t, 0)),
        scratch_shapes=[pltpu.VMEM((BRANCH_W, tq), F32), pltpu.VMEM((2, T, tq), F32), pltpu.VMEM((2, T, tq), BF16)],
        compiler_params=_cparams(("arbitrary", "arbitrary")),
        name=name,
    )(lam, q_t, k, v_t, post)


def _na_plan(rows):
    kh, kw = min(NA_KH, rows), NA_KW
    q_rows = ROW_TILE // GRID_W
    win_rows = min(kh + q_rows, rows)
    nblk = rows // q_rows
    row_start = np.clip(np.arange(rows) - kh // 2, 0, rows - kh)
    col_start = np.clip(np.arange(GRID_W) - kw // 2, 0, GRID_W - kw)
    u0 = np.clip(np.arange(nblk) * q_rows - kh // 2, 0, rows - win_rows)
    wk, tq = win_rows * GRID_W, ROW_TILE
    kk, qq = np.arange(wk), np.arange(tq)
    kc, qc = kk % GRID_W, qq % GRID_W
    col_sel = (np.arange(2 * NA_KW - 1)[:, None, None]
               == (np.arange(GRID_W)[None, :, None] - np.arange(GRID_W)[None, None, :] + (NA_KW - 1))).astype(np.float32)
    types, type_of, row_sel_l, valid_l = {}, [], [], []
    for j in range(nblk):
        kr = u0[j] + kk // GRID_W
        qr = j * q_rows + qq // GRID_W
        rs = row_start[qr]
        valid = ((kr[:, None] >= rs[None, :]) & (kr[:, None] < rs[None, :] + kh)
                 & (kc[:, None] >= col_start[qc][None, :]) & (kc[:, None] < col_start[qc][None, :] + kw))
        assert valid.sum(axis=0).min() == kh * kw and valid.sum(axis=0).max() == kh * kw
        rel = (u0[j] + np.arange(win_rows))[:, None] - (j * q_rows + np.arange(q_rows))[None, :] + (NA_KH - 1)
        row_sel = (rel[:, :, None] == np.arange(2 * NA_KH - 1)[None, None, :]).astype(np.float32)
        key = (valid.tobytes(), row_sel.tobytes())
        if key not in types:
            types[key] = len(row_sel_l)
            row_sel_l.append(row_sel), valid_l.append(valid)
        type_of.append(types[key])
    assert all(int(u) * GRID_W % LANES == 0 for u in u0)
    return (np.asarray(u0, np.int32), np.asarray(type_of, np.int32), np.stack(row_sel_l), col_sel,
            np.stack(valid_l), wk)


def _na_kernel(u0_ref, ty_ref, q_ref, k_ref, v_ref, bias_ref, o_ref, ot_ref, *, ctx_len):
    del ty_ref
    wk = bias_ref.shape[2]
    j = pl.program_id(1)

    def finish():
        o_ref[0] = ot_ref[...].T.astype(BF16)

    @pl.when(j == 0)
    def _():
        for i in range(NA_HEADS):
            o, l = _softmax_pv(k_ref[0, 0:ctx_len, i * LANES:(i + 1) * LANES], q_ref[0, i * LANES:(i + 1) * LANES, :],
                               v_ref[0, i * 64:(i + 1) * 64, 0:ctx_len], jnp.exp)
            ot_ref[i * 64:(i + 1) * 64, :] = o * (1.0 / l)
        finish()

    @pl.when(j > 0)
    def _():
        ws = pl.multiple_of(ctx_len + u0_ref[j - 1] * GRID_W, LANES)
        for i in range(NA_HEADS):
            q_t = q_ref[0, i * LANES:(i + 1) * LANES, :]
            s_c = jnp.dot(k_ref[0, 0:ctx_len, i * LANES:(i + 1) * LANES], q_t, preferred_element_type=F32)
            s_w = jnp.dot(k_ref[0, pl.ds(ws, wk), i * LANES:(i + 1) * LANES], q_t,
                          preferred_element_type=F32) + bias_ref[0, i]
            m = jnp.maximum(jnp.max(s_c, axis=0, keepdims=True), jnp.max(s_w, axis=0, keepdims=True))
            p_c, p_w = jnp.exp(s_c - m), jnp.exp(s_w - m)
            l = jnp.sum(p_c, axis=0, keepdims=True) + jnp.sum(p_w, axis=0, keepdims=True)
            o = (jnp.dot(v_ref[0, i * 64:(i + 1) * 64, 0:ctx_len], p_c.astype(BF16), preferred_element_type=F32)
                 + jnp.dot(v_ref[0, i * 64:(i + 1) * 64, pl.ds(ws, wk)], p_w.astype(BF16),
                           preferred_element_type=F32))
            ot_ref[i * 64:(i + 1) * 64, :] = o * (1.0 / l)
        finish()


def _na_attn(q_t, k, v_t, rpb, ctx_len):
    B, _, T = q_t.shape
    tq = ROW_TILE
    rows = (T - ctx_len) // GRID_W
    u0, type_of, row_sel, col_sel, valid, wk = _na_plan(rows)
    hp = lax.Precision.HIGHEST
    by_col = jnp.einsum("hab,bcq->hacq", rpb.astype(F32), col_sel, precision=hp)
    bias = jnp.einsum("tkra,hacq->thkcrq", row_sel, by_col, precision=hp)
    bias = jnp.where(valid[:, None], bias.reshape(row_sel.shape[0], NA_HEADS, wk, tq), MASK_VALUE)
    nb = T // tq
    assert ctx_len == tq

    grid_spec = pltpu.PrefetchScalarGridSpec(
        num_scalar_prefetch=2,
        grid=(B, nb),
        in_specs=[pl.BlockSpec((1, NA_HEADS * LANES, tq), lambda b, j, u, ty: (b, 0, j)),
                  pl.BlockSpec((1, T, NA_HEADS * LANES), lambda b, j, u, ty: (b, 0, 0)),
                  pl.BlockSpec((1, BRANCH_W, T), lambda b, j, u, ty: (b, 0, 0)),
                  pl.BlockSpec((1, NA_HEADS, wk, tq), lambda b, j, u, ty: (ty[jnp.maximum(j - 1, 0)], 0, 0, 0))],
        out_specs=pl.BlockSpec((1, tq, BRANCH_W), lambda b, j, u, ty: (b, j, 0)),
        scratch_shapes=[pltpu.VMEM((BRANCH_W, tq), F32)])
    return pl.pallas_call(
        functools.partial(_na_kernel, ctx_len=ctx_len),
        out_shape=jax.ShapeDtypeStruct((B, T, BRANCH_W), BF16),
        grid_spec=grid_spec,
        compiler_params=_cparams(("arbitrary", "arbitrary")),
        name="na_attn",
    )(jnp.asarray(u0), jnp.asarray(type_of), q_t, k, v_t, bias)


def _merge_kernel(x_ref, sc1_ref, sh1_ref, g1_ref, sc2_ref, sh2_ref, ona_ref, odf_ref, ogq_ref, oml_ref,
                  wg_ref, wb_ref, wo_ref, lng_ref, lnb_ref, rw_ref, rb_ref, tri_ref,
                  x1_ref, h2_ref, idx_ref, wts_ref, rank_ref, cnt_ref, carry_ref):
    tm = x_ref.shape[1]
    first = (pl.program_id(0) == 0) & (pl.program_id(1) == 0)

    @pl.when(first)
    def _():
        carry_ref[...] = jnp.zeros_like(carry_ref)

    x = x_ref[0]
    h1 = (x * (1.0 + sc1_ref[...]) + sh1_ref[...]).astype(BF16)
    y = None
    for i, o_ref in enumerate((ona_ref, odf_ref, ogq_ref, oml_ref)):
        gate = jax.nn.sigmoid(jnp.dot(h1, wg_ref[:, i * D_MODEL:(i + 1) * D_MODEL], preferred_element_type=F32))
        term = gate * jnp.dot(o_ref[0], wb_ref[i * BRANCH_W:(i + 1) * BRANCH_W, :], preferred_element_type=F32)
        y = term if y is None else y + term
    z = jnp.dot(y.astype(BF16), wo_ref[...], preferred_element_type=F32)
    r = DEEPNORM_ALPHA * x + g1_ref[...] * z
    mu = jnp.mean(r, axis=-1, keepdims=True)
    var = jnp.mean(jnp.square(r - mu), axis=-1, keepdims=True)
    x1 = (r - mu) * lax.rsqrt(var + EPS) * lng_ref[...] + lnb_ref[...]
    x1_ref[0] = x1
    h2 = x1 * (1.0 + sc2_ref[...]) + sh2_ref[...]
    for j in range(D_MODEL // LANES):
        h2_ref[:, j, :] = h2[:, j * LANES:(j + 1) * LANES]

    logits = lax.dot_general(rw_ref[...], h2, (((1,), (1,)), ((), ())), preferred_element_type=F32,
                             precision=lax.Precision.HIGHEST) + rb_ref[...]
    iota = lax.broadcasted_iota(I32, (N_EXPERTS, tm), 0)
    vals, idxs = [], []
    cur = logits
    for _ in range(TOP_K):
        m = jnp.max(cur, axis=0, keepdims=True)
        ik = jnp.min(jnp.where(cur == m, iota, N_EXPERTS), axis=0, keepdims=True)
        vals.append(m)
        idxs.append(ik)
        cur = jnp.where(iota == ik, -jnp.inf, cur)
    exps = [jnp.exp(v - vals[0]) for v in vals]
    denom = exps[0] + exps[1] + exps[2] + exps[3]
    wts_ref[0] = jnp.concatenate([e / denom for e in exps], axis=0)
    idx_ref[0] = jnp.concatenate(idxs, axis=0)

    onehot = jnp.zeros((N_EXPERTS, tm), F32)
    for ik in idxs:
        onehot = onehot + jnp.where(iota == ik, 1.0, 0.0)
    before = jnp.dot(onehot.astype(BF16), tri_ref[...], preferred_element_type=F32) + carry_ref[...]
    rank_ref[0] = jnp.concatenate(
        [jnp.sum(jnp.where(iota == ik, before, 0.0), axis=0, keepdims=True) for ik in idxs], axis=0).astype(I32)
    carry = carry_ref[...] + jnp.sum(onehot, axis=1, keepdims=True)
    carry_ref[...] = carry
    cnt_ref[...] = carry.astype(I32)


def _merge(xa, mod, o_na, o_df, o_gq, o_ml, wg, wb, wo, ln_g, ln_b, rw_t, rb, ctx_len, t_off):
    B, T, D = xa.shape
    tm = ROW_TILE
    nt = T // tm - t_off
    ctx_tiles = ctx_len // tm
    tq_rows = nt * tm

    def kind(t):
        return jnp.where(t + t_off >= ctx_tiles, 1, 0)

    def mod_spec(j):
        return pl.BlockSpec((None, None, None, 1, D), lambda b, t: (b, kind(t), j, 0, 0))

    def const(a):
        nd = a.ndim
        return pl.BlockSpec(a.shape, lambda b, t: (0,) * nd)

    def tok(cols):
        return pl.BlockSpec((1, tm, cols), lambda b, t: (b, t + t_off, 0))

    tri = jnp.asarray(np.triu(np.ones((tm, tm), np.float32), 1), BF16)
    rb_b = jnp.broadcast_to(rb[:, None], (N_EXPERTS, tm)).astype(F32)
    n_tiles = B * nt
    route_shape = jax.ShapeDtypeStruct((n_tiles, TOP_K, tm), I32)
    route_spec = pl.BlockSpec((1, TOP_K, tm), lambda b, t: (b * nt + t, 0, 0))
    return pl.pallas_call(
        _merge_kernel,
        out_shape=[jax.ShapeDtypeStruct((B, tq_rows, D), F32),
                   jax.ShapeDtypeStruct((B * tq_rows, D // LANES, LANES), F32),
                   route_shape, jax.ShapeDtypeStruct((n_tiles, TOP_K, tm), F32), route_shape,
                   jax.ShapeDtypeStruct((N_EXPERTS, tm), I32)],
        grid=(B, nt),
        in_specs=[tok(D), mod_spec(1), mod_spec(0), mod_spec(2), mod_spec(4), mod_spec(3),
                  tok(BRANCH_W), tok(BRANCH_W), tok(BRANCH_W), tok(BRANCH_W),
                  const(wg), const(wb), const(wo), const(ln_g), const(ln_b), const(rw_t), const(rb_b), const(tri)],
        out_specs=[pl.BlockSpec((1, tm, D), lambda b, t: (b, t, 0)),
                   pl.BlockSpec((tm, D // LANES, LANES), lambda b, t: (b * nt + t, 0, 0)),
                   route_spec, route_spec, route_spec,
                   pl.BlockSpec((N_EXPERTS, tm), lambda b, t: (0, 0))],
        scratch_shapes=[pltpu.VMEM((N_EXPERTS, tm), F32)],
        compiler_params=_cparams(("arbitrary", "arbitrary")),
        name="merge_router",
    )(xa, mod, mod, mod, mod, mod, o_na, o_df, o_gq, o_ml, wg, wb, wo, ln_g, ln_b, rw_t, rb_b, tri)


def _row_copy(src, src_row, dst, dst_row, sem):
    return pltpu.make_async_copy(src.at[pl.ds(src_row, 1)], dst.at[pl.ds(dst_row, 1)], sem)


def _dispatch_kernel(dest_ref, h_ref, xs_in_ref, xs_ref, sem):
    del xs_in_ref
    tm = dest_ref.shape[2]

    def issue(t, c):
        for k in range(TOP_K):
            _row_copy(h_ref, t, xs_ref, dest_ref[0, k, t], sem).start(priority=k % 2)
        return c

    lax.fori_loop(0, tm, issue, 0, unroll=2)
    for k in range(TOP_K):
        pltpu.make_async_copy(h_ref, xs_ref.at[pl.ds(0, tm)], sem).wait()


def _dispatch(dest, h2, n_slots):
    n_tiles, _, tm = dest.shape
    xs0 = jnp.zeros((n_slots,) + h2.shape[1:], F32)
    return pl.pallas_call(
        _dispatch_kernel,
        out_shape=jax.ShapeDtypeStruct(xs0.shape, F32),
        grid=(n_tiles,),
        in_specs=[pl.BlockSpec((1, TOP_K, tm), lambda i: (i, 0, 0), memory_space=pltpu.SMEM),
                  pl.BlockSpec((tm,) + h2.shape[1:], lambda i: (i, 0, 0)),
                  pl.BlockSpec(memory_space=pl.ANY)],
        out_specs=pl.BlockSpec(memory_space=pl.ANY),
        scratch_shapes=[pltpu.SemaphoreType.DMA(())],
        input_output_aliases={2: 0},
        compiler_params=_cparams(("arbitrary",)),
        name="moe_dispatch",
    )(dest, h2, xs0)


def _tile_relayout_copies(tiled_hbm, flat_vmem, tile, sem):
    tm = flat_vmem.shape[0]
    r0 = pl.multiple_of(tile * tm, tm)
    return [(tiled_hbm.at[pl.ds(r0, tm), j, :], flat_vmem.at[:, pl.ds(j * LANES, LANES)], sem)
            for j in range(D_MODEL // LANES)]


def _expert_kernel(te_ref, nu_ref, xs_ref, w1_ref, b1_ref, w2_ref, b2_ref, y_ref, w1s_ref, w2s_ref, x_buf, y_buf,
                   sem_in, sem_out):
    i = pl.program_id(0)
    n_used = nu_ref[0]
    slot = lax.rem(i, 2)

    def fetch(tile, s, wait):
        for hbm, vmem, sem in _tile_relayout_copies(xs_ref, x_buf.at[s], tile, sem_in.at[s]):
            cp = pltpu.make_async_copy(hbm, vmem, sem)
            cp.wait() if wait else cp.start()

    def write_back(tile, s, wait):
        for hbm, vmem, sem in _tile_relayout_copies(y_ref, y_buf.at[s], tile, sem_out.at[s]):
            cp = pltpu.make_async_copy(vmem, hbm, sem)
            cp.wait() if wait else cp.start()

    @pl.when(i == 0)
    def _():
        fetch(0, 0, False)

    @pl.when(i + 1 < n_used)
    def _():
        fetch(i + 1, 1 - slot, False)

    @pl.when((i == 0) | (te_ref[i] != te_ref[jnp.maximum(i - 1, 0)]))
    def _():
        w1s_ref[...] = w1_ref[...].astype(BF16)
        w2s_ref[...] = w2_ref[...].astype(BF16)

    @pl.when(i >= 2)
    def _():
        write_back(i - 2, slot, True)

    @pl.when(i < n_used)
    def _():
        fetch(i, slot, True)
        gu = jnp.dot(x_buf[slot].astype(BF16), w1s_ref[...], preferred_element_type=F32) + b1_ref[...]
        gate = jnp.minimum(gu[:, :D_FF], SWIGLU_LIMIT)
        up = jnp.clip(gu[:, D_FF:], -SWIGLU_LIMIT, SWIGLU_LIMIT)
        act = (up + 1.0) * gate * jax.nn.sigmoid(SWIGLU_ALPHA * gate)
        y_buf[slot] = jnp.dot(act.astype(BF16), w2s_ref[...], preferred_element_type=F32) + b2_ref[...]

    @pl.when(i >= n_used)
    def _():
        y_buf[slot] = jnp.zeros(y_buf.shape[1:], F32)

    write_back(i, slot, False)

    @pl.when(i == pl.num_programs(0) - 1)
    def _():
        write_back(i, slot, True)

        @pl.when(i >= 1)
        def _():
            write_back(i - 1, 1 - slot, True)


def _experts(tile_expert, n_used, xs, w1, b1, w2, b2, layer):
    n_slots = xs.shape[0]
    tm = ROW_TILE
    nt = n_slots // tm
    E = w1.shape[1]
    grid_spec = pltpu.PrefetchScalarGridSpec(
        num_scalar_prefetch=2, grid=(nt,),
        in_specs=[pl.BlockSpec(memory_space=pl.ANY),
                  pl.BlockSpec((None, None, D_MODEL, 2 * D_FF), lambda i, te, nu: (layer, te[i], 0, 0)),
                  pl.BlockSpec((None, None, 1, 2 * D_FF), lambda i, te, nu: (layer, te[i], 0, 0)),
                  pl.BlockSpec((None, None, D_FF, D_MODEL), lambda i, te, nu: (layer, te[i], 0, 0)),
                  pl.BlockSpec((None, None, 1, D_MODEL), lambda i, te, nu: (layer, te[i], 0, 0))],
        out_specs=pl.BlockSpec(memory_space=pl.ANY),
        scratch_shapes=[pltpu.VMEM((D_MODEL, 2 * D_FF), BF16), pltpu.VMEM((D_FF, D_MODEL), BF16),
                        pltpu.VMEM((2, tm, D_MODEL), F32), pltpu.VMEM((2, tm, D_MODEL), F32),
                        pltpu.SemaphoreType.DMA((2,)), pltpu.SemaphoreType.DMA((2,))])
    return pl.pallas_call(
        _expert_kernel,
        out_shape=jax.ShapeDtypeStruct(xs.shape, F32),
        grid_spec=grid_spec,
        compiler_params=_cparams(("arbitrary",)),
        name="moe_experts",
    )(tile_expert, n_used, xs, w1, b1.reshape(b1.shape[0], E, 1, -1), w2, b2.reshape(b2.shape[0], E, 1, -1))


def _combine_kernel(dest_ref, destn_ref, w_ref, x1_ref, g2_ref, lng_ref, lnb_ref, y_ref, o_ref, buf_ref, y3_ref,
                    sem):
    tm = dest_ref.shape[2]
    i = pl.program_id(0)
    n = pl.num_programs(0)
    slot = lax.rem(i, 2)

    def issue(dr, s):
        def body(t, c):
            for k in range(TOP_K):
                pltpu.make_async_copy(y_ref.at[pl.ds(dr[0, k, t], 1)], buf_ref.at[s, k, pl.ds(t, 1)],
                                      sem.at[s]).start(priority=k % 2)
            return c
        lax.fori_loop(0, tm, body, 0, unroll=2)

    @pl.when(i == 0)
    def _():
        issue(dest_ref, 0)

    @pl.when(i + 1 < n)
    def _():
        issue(destn_ref, 1 - slot)

    for k in range(TOP_K):
        pltpu.make_async_copy(y_ref.at[pl.ds(0, tm)], buf_ref.at[slot, k], sem.at[slot]).wait()

    def weigh(t, c):
        acc = buf_ref[slot, 0, t] * w_ref[0, 0, t]
        for k in range(1, TOP_K):
            acc = acc + buf_ref[slot, k, t] * w_ref[0, k, t]
        y3_ref[t] = acc
        return c

    lax.fori_loop(0, tm, weigh, 0, unroll=8)
    y2 = jnp.concatenate([y3_ref[:, j, :] for j in range(D_MODEL // LANES)], axis=1)
    r = DEEPNORM_ALPHA * x1_ref[0] + g2_ref[...] * y2
    mu = jnp.mean(r, axis=-1, keepdims=True)
    var = jnp.mean(jnp.square(r - mu), axis=-1, keepdims=True)
    o_ref[0] = (r - mu) * lax.rsqrt(var + EPS) * lng_ref[...] + lnb_ref[...]


def _combine(dest, wts, x1, mod, ln_g, ln_b, y, ctx_len, t_off):
    B, tq_rows, D = x1.shape
    n_tiles, _, tm = dest.shape
    nt = n_tiles // B
    ctx_tiles = ctx_len // tm

    def kind(i):
        return jnp.where(i % nt + t_off >= ctx_tiles, 1, 0)

    smem = lambda f: pl.BlockSpec((1, TOP_K, tm), f, memory_space=pltpu.SMEM)
    cur = lambda i: (i, 0, 0)
    nxt = lambda i: (jnp.minimum(i + 1, n_tiles - 1), 0, 0)
    return pl.pallas_call(
        _combine_kernel,
        out_shape=jax.ShapeDtypeStruct((B, tq_rows, D), F32),
        grid=(n_tiles,),
        in_specs=[smem(cur), smem(nxt), smem(cur),
                  pl.BlockSpec((1, tm, D), lambda i: (i // nt, i % nt, 0)),
                  pl.BlockSpec((None, None, None, 1, D), lambda i: (i // nt, kind(i), 5, 0, 0)),
                  pl.BlockSpec(ln_g.shape, lambda i: (0, 0)),
                  pl.BlockSpec(ln_b.shape, lambda i: (0, 0)),
                  pl.BlockSpec(memory_space=pl.ANY)],
        out_specs=pl.BlockSpec((1, tm, D), lambda i: (i // nt, i % nt, 0)),
        scratch_shapes=[pltpu.VMEM((2, TOP_K, tm, D // LANES, LANES), F32),
                        pltpu.VMEM((tm, D // LANES, LANES), F32), pltpu.SemaphoreType.DMA((2,))],
        compiler_params=_cparams(("arbitrary",)),
        name="moe_combine",
    )(dest, dest, wts, x1, mod, ln_g, ln_b, y)


_DF_HEADS_SPEC = tuple((((2 * i, i, False), (2 * i + 1, i, True)), i) for i in range(DF_HEADS))
_GQ_HEADS_SPEC = tuple((((i, i // (GQ_HEADS // GQ_KV_HEADS), False),), i // (GQ_HEADS // GQ_KV_HEADS))
                       for i in range(GQ_HEADS))
_ML_HEADS_SPEC = tuple((((i, i, False),), i) for i in range(ML_HEADS))


def _layer_weights(l, w_in, gq_qnorm, gq_knorm, ml_qa_norm, ml_wq_b, ml_kva_norm, ml_wkv_b):
    w = w_in[l]

    def cols(o, n):
        return w[:, o:o + n]

    parts = [cols(O_NAQ, 256), cols(O_NAK, 256), cols(O_NAV, 256),
             cols(O_DFQ, 256), _swap_cols(cols(O_DFQ, 256), 32), cols(O_DFK, 256), _swap_cols(cols(O_DFK, 256), 32),
             cols(O_DFV, 256),
             cols(O_GQQ, 256), _swap_cols(cols(O_GQQ, 256), 64), cols(O_GQK, 128), _swap_cols(cols(O_GQK, 128), 64),
             cols(O_GQV, 128),
             cols(O_MLQA, 256), cols(O_MLKVA, 128), cols(O_MLKR, 32), _swap_cols(cols(O_MLKR, 32), 32)]
    w_t = jnp.concatenate(parts, axis=1).T.astype(BF16)
    assert w_t.shape[0] == PROJ_ROWS

    tm = ROW_TILE
    gq_g = jnp.stack([gq_qnorm[l], _swap_cols(gq_qnorm[l], 64, signed=False),
                      gq_knorm[l], _swap_cols(gq_knorm[l], 64, signed=False)])
    gq_g = jnp.broadcast_to(gq_g[:, :, None], (4, GQ_DIM, tm)).astype(F32)
    qa_g = jnp.broadcast_to(ml_qa_norm[l][:, None], (ML_Q_RANK, tm)).astype(F32)
    kva_g = jnp.broadcast_to(ml_kva_norm[l][:, None], (ML_KV_RANK, tm)).astype(F32)
    wq = ml_wq_b[l]
    qd = ML_NOPE + ML_ROPE
    wq_parts = []
    for i in range(ML_HEADS):
        rope_cols = wq[:, i * qd + ML_NOPE:(i + 1) * qd]
        wq_parts += [wq[:, i * qd:i * qd + ML_NOPE], rope_cols, _swap_cols(rope_cols, 32)]
    wq_t = jnp.concatenate(wq_parts, axis=1).T.astype(BF16)
    wkv_t = ml_wkv_b[l].T.astype(BF16)
    return w_t, gq_g, qa_g, kva_g, wq_t, wkv_t


def kernel(x, c, ctx, c_ctx, w_ada, b_ada, w_in, na_rpb, df_lam, df_subln, gq_qnorm, gq_knorm, ml_qa_norm, ml_wq_b,
           ml_kva_norm, ml_wkv_b, w_branch, w_out, ln1_g, ln1_b, ln2_g, ln2_b, router_w, router_b, exp_w1, exp_b1,
           exp_w2, exp_b2):
    B, S, D = x.shape
    C = ctx.shape[1]
    T = C + S
    tm = ROW_TILE
    assert D == D_MODEL and C % tm == 0 and S % tm == 0 and C == tm

    xa = jnp.concatenate([ctx, x], axis=1)
    cc = jnp.concatenate([c, c_ctx[None], jnp.zeros((16 - B - 1, D), F32)], axis=0)
    mod_all = _ada(cc, w_ada, b_ada)
    cs32 = _rope_tables(32, C, S)
    cs64 = _rope_tables(64, C, S)

    for l in range(DEPTH):
        last = l == DEPTH - 1
        m = mod_all[l]
        mod = jnp.stack([jnp.broadcast_to(m[B][None], (B, 6 * D)), m[:B]], axis=1).reshape(B, 2, 6, 1, D)
        lam_init = 0.8 - 0.6 * math.exp(-0.3 * l)
        lp = df_lam[l].astype(F32)
        lam = (jnp.exp(jnp.sum(lp[0] * lp[1])) - jnp.exp(jnp.sum(lp[2] * lp[3])) + lam_init).reshape(1)
        w_t, gq_g, qa_g, kva_g, wq_t, wkv_t = _layer_weights(l, w_in, gq_qnorm, gq_knorm, ml_qa_norm, ml_wq_b,
                                                             ml_kva_norm, ml_wkv_b)
        (na_q, na_k, na_v, df_q, df_k, df_v, gq_q, gq_k, gq_v, ml_q, ml_k, ml_v) = _project(
            xa, mod, w_t, cs32, cs64, gq_g, qa_g, kva_g, wq_t, wkv_t, C)

        o_na = _na_attn(na_q, na_k, na_v, na_rpb[l], C)
        subln = jnp.broadcast_to((df_subln[l] * (1.0 - lam_init))[:, None], (DF_V, tm)).astype(F32)
        ones = jnp.ones((DF_V, tm), F32)
        o_df = _dense_attn(lam, df_q, df_k, df_v, subln, _DF_HEADS_SPEC, C, True, "df_attn")
        o_gq = _dense_attn(lam, gq_q, gq_k, gq_v, ones, _GQ_HEADS_SPEC, C, False, "gq_attn")
        o_ml = _dense_attn(lam, ml_q, ml_k, ml_v, ones, _ML_HEADS_SPEC, C, False, "ml_attn")

        t_off = C // tm if last else 0
        wg = w_in[l][:, O_GATES:].astype(BF16)
        wb = w_branch[l].reshape(N_BRANCH * BRANCH_W, D).astype(BF16)
        wo = w_out[l].astype(BF16)
        x1, h2, idx, wts, rank, cnt = _merge(xa, mod, o_na, o_df, o_gq, o_ml, wg, wb, wo,
                                             ln1_g[l][None], ln1_b[l][None], router_w[l].T, router_b[l], C, t_off)

        counts = cnt[:, 0]
        padded = ((counts + tm - 1) // tm) * tm
        ends = jnp.cumsum(padded)
        starts = (ends - padded).astype(I32)
        n_tok = h2.shape[0]
        n_exp_tiles = (n_tok * TOP_K) // tm + N_EXPERTS
        n_used = (ends[-1] // tm).astype(I32).reshape(1)
        tile_first = jnp.arange(n_exp_tiles, dtype=I32) * tm
        tile_expert = jnp.minimum(jnp.sum((ends[None, :] <= tile_first[:, None]).astype(I32), axis=1), N_EXPERTS - 1)
        tile_expert = jnp.where(tile_first < ends[-1], tile_expert, tile_expert[jnp.maximum(n_used[0] - 1, 0)])

        dest = jnp.sum(jnp.where(idx[..., None] == jnp.arange(N_EXPERTS, dtype=I32), starts, 0), axis=-1) + rank
        xs = _dispatch(dest, h2, n_exp_tiles * tm)
        ys = _experts(tile_expert, n_used, xs, exp_w1, exp_b1, exp_w2, exp_b2, l)
        xa = _combine(dest, wts, x1, mod, ln2_g[l][None], ln2_b[l][None], ys, C, t_off)
    return xa
```

```python
import functools
import math

import numpy as np
import jax
import jax.numpy as jnp
from jax import lax
from jax.experimental import pallas as pl
from jax.experimental.pallas import tpu as pltpu

F32, BF16, I32 = jnp.float32, jnp.bfloat16, jnp.int32

D_MODEL = 1024
DEPTH = 2
GRID_W = 64
ROPE_BASE = 10000.0
EPS = 1e-6
NA_HEADS, NA_DIM, NA_KH, NA_KW = 4, 64, 8, 16
DF_HEADS, DF_QK = 4, 32
DF_V = 2 * DF_QK
GQ_HEADS, GQ_KV_HEADS, GQ_DIM = 4, 2, 64
ML_HEADS, ML_NOPE, ML_ROPE, ML_V, ML_Q_RANK, ML_KV_RANK = 4, 64, 32, 64, 256, 128
N_BRANCH, BRANCH_W = 4, 256
N_EXPERTS, TOP_K = 32, 4
D_FF = D_MODEL
SWIGLU_LIMIT, SWIGLU_ALPHA = 7.0, 1.702
DEEPNORM_ALPHA = (2 * DEPTH) ** 0.25

IN_SIZES = (256, 256, 256, 256, 256, 256, 256, 128, 128, ML_Q_RANK, ML_KV_RANK, ML_ROPE, N_BRANCH * D_MODEL)
IN_OFFSETS = tuple(int(o) for o in np.cumsum((0,) + IN_SIZES)[:-1])
(O_NAQ, O_NAK, O_NAV, O_DFQ, O_DFK, O_DFV, O_GQQ, O_GQK, O_GQV, O_MLQA, O_MLKVA, O_MLKR, O_GATES) = IN_OFFSETS

LANES = 128
ROW_TILE = 256
MASK_VALUE = -1e30
VMEM_LIMIT = 56 * 1024 * 1024
LOG2E = math.log2(math.e)
KEY_CHUNK = 256
ACC_ROWS = 16
SCORE_SPLIT = 2

_PROJ_GROUPS = (("na_q", 256), ("na_k", 256), ("na_v", 256),
                ("df_q", 256), ("df_q_sw", 256), ("df_k", 256), ("df_k_sw", 256), ("df_v", 256),
                ("gq_q", 256), ("gq_q_sw", 256), ("gq_k", 128), ("gq_k_sw", 128), ("gq_v", 128),
                ("ml_qa", 256), ("ml_kva", 128), ("ml_kr", 32), ("ml_kr_sw", 32))
_PROJ_OFF = {}
_o = 0
for _n, _r in _PROJ_GROUPS:
    _PROJ_OFF[_n] = (_o, _r)
    _o += _r
PROJ_ROWS = _o


def _cparams(sem):
    return pltpu.CompilerParams(dimension_semantics=sem, vmem_limit_bytes=VMEM_LIMIT)


def _swap_cols(w, n, signed=True):
    lead, width = w.shape[:-1], w.shape[-1]
    w5 = w.reshape(lead + (width // n, 2, 2, n // 4))
    lo, hi = w5[..., 0, :], w5[..., 1, :]
    out = jnp.stack([-hi if signed else hi, lo], axis=-2)
    return out.reshape(lead + (width,))


def _rope_tables(n, ctx_len, seq):
    h = n // 2
    t = np.arange(seq)
    rows, cols = (t // GRID_W).astype(np.float32), (t % GRID_W).astype(np.float32)
    inv_freq = (np.float32(ROPE_BASE) ** (-np.arange(0, h, 2, dtype=np.float32) / np.float32(h))).astype(np.float32)
    ang_r = rows[:, None] * inv_freq[None, :]
    ang_c = cols[:, None] * inv_freq[None, :]
    ang = np.concatenate([ang_r, ang_r, ang_c, ang_c], axis=1)
    cos = np.concatenate([np.ones((ctx_len, n), np.float32), np.cos(ang)], axis=0).T
    sin = np.concatenate([np.zeros((ctx_len, n), np.float32), np.sin(ang)], axis=0).T
    return jnp.asarray(np.stack([cos, sin]).astype(np.float32))


def _ada_kernel(c_ref, w_ref, b_ref, o_ref):
    cc = c_ref[...]
    h = (cc * jax.nn.sigmoid(cc)).astype(BF16)
    o_ref[0] = jnp.dot(h, w_ref[0].astype(BF16), preferred_element_type=F32) + b_ref[0]


def _ada(cc, w_ada, b_ada):
    L, D, N = w_ada.shape
    R = cc.shape[0]
    tn = 512
    return pl.pallas_call(
        _ada_kernel,
        out_shape=jax.ShapeDtypeStruct((L, R, N), F32),
        grid=(L, N // tn),
        in_specs=[pl.BlockSpec((R, D), lambda l, j: (0, 0)),
                  pl.BlockSpec((1, D, tn), lambda l, j: (l, 0, j)),
                  pl.BlockSpec((1, 1, tn), lambda l, j: (l, 0, j))],
        out_specs=pl.BlockSpec((1, R, tn), lambda l, j: (l, 0, j)),
        compiler_params=_cparams(("arbitrary", "arbitrary")),
        name="ada_mod",
    )(cc, w_ada, b_ada.reshape(L, 1, N))


def _proj_kernel(x_ref, sc_ref, sh_ref, w_ref, cs32_ref, cs64_ref, gqg_ref, qag_ref, kvag_ref, wq_ref, wkv_ref,
                 na_q, na_k, na_v, df_q, df_k, df_v, gq_q, gq_k, gq_v, ml_q, ml_k, ml_v):
    tm = x_ref.shape[1]
    h = (x_ref[0] * (1.0 + sc_ref[...]) + sh_ref[...]).astype(BF16)
    p = lax.dot_general(w_ref[...], h, (((1,), (1,)), ((), ())), preferred_element_type=F32)

    def grp(name):
        o, r = _PROJ_OFF[name]
        return p[o:o + r]

    def zeros(n):
        return jnp.zeros((n, tm), F32)

    def rope(a, a_sw, cs_ref):
        n = cs_ref.shape[1]
        r = a.shape[0]
        a3, s3 = a.reshape(r // n, n, tm), a_sw.reshape(r // n, n, tm)
        return (a3 * cs_ref[0][None] + s3 * cs_ref[1][None]).reshape(r, tm)

    def head_slots(a, width):
        pieces = []
        for i in range(a.shape[0] // width):
            pieces += [a[i * width:(i + 1) * width], zeros(LANES - width)]
        return jnp.concatenate(pieces, axis=0)

    na_q[0] = head_slots(grp("na_q") * (NA_DIM ** -0.5), NA_DIM).astype(BF16)
    na_k[0] = head_slots(grp("na_k"), NA_DIM).T.astype(BF16)
    na_v[0] = grp("na_v").astype(BF16)

    q = rope(grp("df_q"), grp("df_q_sw"), cs32_ref) * (DF_QK ** -0.5 * LOG2E)
    pieces = []
    for i in range(DF_HEADS):
        q1, q2 = q[i * 64:i * 64 + 32], q[i * 64 + 32:i * 64 + 64]
        pieces += [q1, zeros(LANES - 32), zeros(32), q2, zeros(LANES - 64)]
    df_q[0] = jnp.concatenate(pieces, axis=0).astype(BF16)
    df_k[0] = head_slots(rope(grp("df_k"), grp("df_k_sw"), cs32_ref), 2 * DF_QK).T.astype(BF16)
    df_v[0] = grp("df_v").astype(BF16)

    cos64, sin64 = cs64_ref[0], cs64_ref[1]

    def norm_rope(a, a_sw, g, g_sw, scale):
        hh = a.shape[0] // GQ_DIM
        a3, s3 = a.reshape(hh, GQ_DIM, tm), a_sw.reshape(hh, GQ_DIM, tm)
        r = lax.rsqrt(jnp.mean(a3 * a3, axis=1, keepdims=True) + EPS) * scale
        return ((a3 * (g * cos64)[None] + s3 * (g_sw * sin64)[None]) * r).reshape(hh * GQ_DIM, tm)

    gq_q[0] = head_slots(norm_rope(grp("gq_q"), grp("gq_q_sw"), gqg_ref[0], gqg_ref[1], GQ_DIM ** -0.5 * LOG2E),
                         GQ_DIM).astype(BF16)
    gq_k[0] = head_slots(norm_rope(grp("gq_k"), grp("gq_k_sw"), gqg_ref[2], gqg_ref[3], 1.0),
                         GQ_DIM).T.astype(BF16)
    gq_v[0] = grp("gq_v").astype(BF16)

    qa = grp("ml_qa")
    qan = (qa * lax.rsqrt(jnp.mean(qa * qa, axis=0, keepdims=True) + EPS) * qag_ref[...]).astype(BF16)
    qq = jnp.dot(wq_ref[...], qan, preferred_element_type=F32)
    kva = grp("ml_kva")
    kvan = (kva * lax.rsqrt(jnp.mean(kva * kva, axis=0, keepdims=True) + EPS) * kvag_ref[...]).astype(BF16)
    kv = jnp.dot(wkv_ref[...], kvan, preferred_element_type=F32)
    k_rope = rope(grp("ml_kr"), grp("ml_kr_sw"), cs32_ref)
    cos32, sin32 = cs32_ref[0], cs32_ref[1]
    ml_scale = (ML_NOPE + ML_ROPE) ** -0.5 * LOG2E
    qp, kp, vp = [], [], []
    for i in range(ML_HEADS):
        b = i * LANES
        q_rope = qq[b + 64:b + 96] * cos32 + qq[b + 96:b + 128] * sin32
        qp += [qq[b:b + 64] * ml_scale, q_rope * ml_scale, zeros(32)]
        kp += [kv[b:b + 64], k_rope, zeros(32)]
        vp += [kv[b + 64:b + 128]]
    ml_q[0] = jnp.concatenate(qp, axis=0).astype(BF16)
    ml_k[0] = jnp.concatenate(kp, axis=0).T.astype(BF16)
    ml_v[0] = jnp.concatenate(vp, axis=0).astype(BF16)


def _project(xa, mod, w_t, cs32, cs64, gq_g, qa_g, kva_g, wq_t, wkv_t, ctx_len):
    B, T, D = xa.shape
    tm = ROW_TILE
    nt = T // tm
    ctx_tiles = ctx_len // tm

    def kind(t):
        return jnp.where(t >= ctx_tiles, 1, 0)

    def mod_spec(j):
        return pl.BlockSpec((None, None, None, 1, D), lambda b, t: (b, kind(t), j, 0, 0))

    def const(a):
        nd = a.ndim
        return pl.BlockSpec(a.shape, lambda b, t: (0,) * nd)

    def fm(rows):
        return jax.ShapeDtypeStruct((B, rows, T), BF16), pl.BlockSpec((1, rows, tm), lambda b, t: (b, 0, t))

    def tmj(cols):
        return jax.ShapeDtypeStruct((B, T, cols), BF16), pl.BlockSpec((1, tm, cols), lambda b, t: (b, t, 0))

    outs = [fm(512), tmj(512), fm(256),
            fm(1024), tmj(512), fm(256),
            fm(512), tmj(256), fm(128),
            fm(512), tmj(512), fm(256)]
    return pl.pallas_call(
        _proj_kernel,
        out_shape=[o[0] for o in outs],
        grid=(B, nt),
        in_specs=[pl.BlockSpec((1, tm, D), lambda b, t: (b, t, 0)),
                  mod_spec(1), mod_spec(0),
                  const(w_t),
                  pl.BlockSpec((2, 32, tm), lambda b, t: (0, 0, t)),
                  pl.BlockSpec((2, 64, tm), lambda b, t: (0, 0, t)),
                  const(gq_g), const(qa_g), const(kva_g), const(wq_t), const(wkv_t)],
        out_specs=[o[1] for o in outs],
        compiler_params=_cparams(("arbitrary", "arbitrary")),
        name="in_proj",
    )(xa, mod, mod, w_t, cs32, cs64, gq_g, qa_g, kva_g, wq_t, wkv_t)


def _softmax_pv(k, q_t, v_t, exp_fn):
    s = jnp.dot(k, q_t, preferred_element_type=F32)
    m = jnp.max(s, axis=0, keepdims=True)
    p = exp_fn(s - m)
    l = jnp.sum(p, axis=0, keepdims=True)
    o = jnp.dot(v_t, p.astype(BF16), preferred_element_type=F32)
    return o, l


def _dense_attn_kernel(lam_ref, q_ref, k_ref, v_ref, post_ref, o_ref, ot_ref, s_ref, p_ref, *, heads, ctx_len,
                       diff_norm):
    total, tq = k_ref.shape[1], q_ref.shape[2]
    items = [(qs, ks, v_idx, i, signed) for i, (terms, v_idx) in enumerate(heads) for qs, ks, signed in terms]
    last_of_head = {i: max(j for j, it in enumerate(items) if it[3] == i) for i in range(len(heads))}

    def finish_head(i, acc):
        if diff_norm:
            acc = acc * lax.rsqrt(jnp.mean(acc * acc, axis=0, keepdims=True) + EPS) * post_ref[...]
        ot_ref[i * 64:(i + 1) * 64, :] = acc

    def run_context():
        head_acc = {}
        for j, (qs, ks, v_idx, i, signed) in enumerate(items):
            o, l = _softmax_pv(k_ref[0, 0:ctx_len, ks * LANES:(ks + 1) * LANES], q_ref[0, qs * LANES:(qs + 1) * LANES, :],
                               v_ref[0, v_idx * 64:(v_idx + 1) * 64, 0:ctx_len], jnp.exp2)
            o = o * (((-lam_ref[0]) if signed else 1.0) / l)
            head_acc[i] = o if i not in head_acc else head_acc[i] + o
            if last_of_head[i] == j:
                finish_head(i, head_acc.pop(i))

    def run_latent():
        n_chunks = total // KEY_CHUNK
        half = KEY_CHUNK // SCORE_SPLIT
        n_items = len(items)
        m_of, l_of, head_acc = {}, {}, {}
        for t in range(n_items + 2):
            sc = items[t] if t < n_items else None
            ex = items[t - 1] if 0 <= t - 1 < n_items else None
            pv = items[t - 2] if 0 <= t - 2 < n_items else None
            macc = jnp.full((ACC_ROWS, tq), -jnp.inf, F32)
            lacc = jnp.zeros((ACC_ROWS, tq), F32)
            oacc = jnp.zeros((64, tq), F32)
            for c in range(n_chunks):
                r0 = c * KEY_CHUNK
                if sc is not None:
                    q_t = q_ref[0, sc[0] * LANES:(sc[0] + 1) * LANES, :]
                    for r in ((r0,) if half == KEY_CHUNK else (r0, r0 + half)):
                        s = jnp.dot(k_ref[0, r:r + half, sc[1] * LANES:(sc[1] + 1) * LANES], q_t,
                                    preferred_element_type=F32)
                        s_ref[t % 2, r:r + half, :] = s
                        for a in range(half // ACC_ROWS):
                            macc = jnp.maximum(macc, s[a * ACC_ROWS:(a + 1) * ACC_ROWS])
                if ex is not None:
                    p = jnp.exp2(s_ref[(t - 1) % 2, r0:r0 + KEY_CHUNK, :] - m_of[t - 1])
                    for a in range(KEY_CHUNK // ACC_ROWS):
                        lacc = lacc + p[a * ACC_ROWS:(a + 1) * ACC_ROWS]
                    p_ref[(t - 1) % 2, r0:r0 + KEY_CHUNK, :] = p.astype(BF16)
                if pv is not None:
                    oacc = oacc + jnp.dot(v_ref[0, pv[2] * 64:(pv[2] + 1) * 64, r0:r0 + KEY_CHUNK],
                                          p_ref[(t - 2) % 2, r0:r0 + KEY_CHUNK, :], preferred_element_type=F32)
            if sc is not None:
                m_of[t] = jnp.max(macc, axis=0, keepdims=True)
            if ex is not None:
                l_of[t - 1] = jnp.sum(lacc, axis=0, keepdims=True)
            if pv is not None:
                i, signed = pv[3], pv[4]
                o = oacc * (((-lam_ref[0]) if signed else 1.0) / l_of.pop(t - 2))
                head_acc[i] = o if i not in head_acc else head_acc[i] + o
                if last_of_head[i] == t - 2:
                    finish_head(i, head_acc.pop(i))

    qb = pl.program_id(1)

    @pl.when(qb * tq < ctx_len)
    def _():
        run_context()
        o_ref[0] = ot_ref[...].T.astype(BF16)

    @pl.when(qb * tq >= ctx_len)
    def _():
        run_latent()
        o_ref[0] = ot_ref[...].T.astype(BF16)


def _dense_attn(lam, q_t, k, v_t, post, heads, ctx_len, diff_norm, name):
    B, qrows, T = q_t.shape
    tq = ROW_TILE
    kern = functools.partial(_dense_attn_kernel, heads=heads, ctx_len=ctx_len, diff_norm=diff_norm)
    return pl.pallas_call(
        kern,
        out_shape=jax.ShapeDtypeStruct((B, T, BRANCH_W), BF16),
        grid=(B, T // tq),
        in_specs=[pl.BlockSpec(memory_space=pltpu.SMEM),
                  pl.BlockSpec((1, qrows, tq), lambda b, t: (b, 0, t)),
                  pl.BlockSpec((1, T, k.shape[2]), lambda b, t: (b, 0, 0)),
                  pl.BlockSpec((1, v_t.shape[1], T), lambda b, t: (b, 0, 0)),
                  pl.BlockSpec(post.shape, lambda b, t: (0, 0))],
        out_specs=pl.BlockSpec((1, tq, BRANCH_W), lambda b, t: (b, t, 0)),
        scratch_shapes=[pltpu.VMEM((BRANCH_W, tq), F32), pltpu.VMEM((2, T, tq), F32), pltpu.VMEM((2, T, tq), BF16)],
        compiler_params=_cparams(("arbitrary", "arbitrary")),
        name=name,
    )(lam, q_t, k, v_t, post)


def _na_plan(rows):
    kh, kw = min(NA_KH, rows), NA_KW
    q_rows = ROW_TILE // GRID_W
    win_rows = min(kh + q_rows, rows)
    nblk = rows // q_rows
    row_start = np.clip(np.arange(rows) - kh // 2, 0, rows - kh)
    col_start = np.clip(np.arange(GRID_W) - kw // 2, 0, GRID_W - kw)
    u0 = np.clip(np.arange(nblk) * q_rows - kh // 2, 0, rows - win_rows)
    wk, tq = win_rows * GRID_W, ROW_TILE
    kk, qq = np.arange(wk), np.arange(tq)
    kc, qc = kk % GRID_W, qq % GRID_W
    col_sel = (np.arange(2 * NA_KW - 1)[:, None, None]
               == (np.arange(GRID_W)[None, :, None] - np.arange(GRID_W)[None, None, :] + (NA_KW - 1))).astype(np.float32)
    types, type_of, row_sel_l, valid_l = {}, [], [], []
    for j in range(nblk):
        kr = u0[j] + kk // GRID_W
        qr = j * q_rows + qq // GRID_W
        rs = row_start[qr]
        valid = ((kr[:, None] >= rs[None, :]) & (kr[:, None] < rs[None, :] + kh)
                 & (kc[:, None] >= col_start[qc][None, :]) & (kc[:, None] < col_start[qc][None, :] + kw))
        assert valid.sum(axis=0).min() == kh * kw and valid.sum(axis=0).max() == kh * kw
        rel = (u0[j] + np.arange(win_rows))[:, None] - (j * q_rows + np.arange(q_rows))[None, :] + (NA_KH - 1)
        row_sel = (rel[:, :, None] == np.arange(2 * NA_KH - 1)[None, None, :]).astype(np.float32)
        key = (valid.tobytes(), row_sel.tobytes())
        if key not in types:
            types[key] = len(row_sel_l)
            row_sel_l.append(row_sel), valid_l.append(valid)
        type_of.append(types[key])
    assert all(int(u) * GRID_W % LANES == 0 for u in u0)
    return (np.asarray(u0, np.int32), np.asarray(type_of, np.int32), np.stack(row_sel_l), col_sel,
            np.stack(valid_l), wk)


def _na_kernel(u0_ref, ty_ref, q_ref, k_ref, v_ref, bias_ref, o_ref, ot_ref, *, ctx_len):
    del ty_ref
    wk = bias_ref.shape[2]
    j = pl.program_id(1)

    def finish():
        o_ref[0] = ot_ref[...].T.astype(BF16)

    @pl.when(j == 0)
    def _():
        for i in range(NA_HEADS):
            o, l = _softmax_pv(k_ref[0, 0:ctx_len, i * LANES:(i + 1) * LANES], q_ref[0, i * LANES:(i + 1) * LANES, :],
                               v_ref[0, i * 64:(i + 1) * 64, 0:ctx_len], jnp.exp)
            ot_ref[i * 64:(i + 1) * 64, :] = o * (1.0 / l)
        finish()

    @pl.when(j > 0)
    def _():
        ws = pl.multiple_of(ctx_len + u0_ref[j - 1] * GRID_W, LANES)
        for i in range(NA_HEADS):
            q_t = q_ref[0, i * LANES:(i + 1) * LANES, :]
            s_c = jnp.dot(k_ref[0, 0:ctx_len, i * LANES:(i + 1) * LANES], q_t, preferred_element_type=F32)
            s_w = jnp.dot(k_ref[0, pl.ds(ws, wk), i * LANES:(i + 1) * LANES], q_t,
                          preferred_element_type=F32) + bias_ref[0, i]
            m = jnp.maximum(jnp.max(s_c, axis=0, keepdims=True), jnp.max(s_w, axis=0, keepdims=True))
            p_c, p_w = jnp.exp(s_c - m), jnp.exp(s_w - m)
            l = jnp.sum(p_c, axis=0, keepdims=True) + jnp.sum(p_w, axis=0, keepdims=True)
            o = (jnp.dot(v_ref[0, i * 64:(i + 1) * 64, 0:ctx_len], p_c.astype(BF16), preferred_element_type=F32)
                 + jnp.dot(v_ref[0, i * 64:(i + 1) * 64, pl.ds(ws, wk)], p_w.astype(BF16),
                           preferred_element_type=F32))
            ot_ref[i * 64:(i + 1) * 64, :] = o * (1.0 / l)
        finish()


def _na_attn(q_t, k, v_t, rpb, ctx_len):
    B, _, T = q_t.shape
    tq = ROW_TILE
    rows = (T - ctx_len) // GRID_W
    u0, type_of, row_sel, col_sel, valid, wk = _na_plan(rows)
    hp = lax.Precision.HIGHEST
    by_col = jnp.einsum("hab,bcq->hacq", rpb.astype(F32), col_sel, precision=hp)
    bias = jnp.einsum("tkra,hacq->thkcrq", row_sel, by_col, precision=hp)
    bias = jnp.where(valid[:, None], bias.reshape(row_sel.shape[0], NA_HEADS, wk, tq), MASK_VALUE)
    nb = T // tq
    assert ctx_len == tq

    grid_spec = pltpu.PrefetchScalarGridSpec(
        num_scalar_prefetch=2,
        grid=(B, nb),
        in_specs=[pl.BlockSpec((1, NA_HEADS * LANES, tq), lambda b, j, u, ty: (b, 0, j)),
                  pl.BlockSpec((1, T, NA_HEADS * LANES), lambda b, j, u, ty: (b, 0, 0)),
                  pl.BlockSpec((1, BRANCH_W, T), lambda b, j, u, ty: (b, 0, 0)),
                  pl.BlockSpec((1, NA_HEADS, wk, tq), lambda b, j, u, ty: (ty[jnp.maximum(j - 1, 0)], 0, 0, 0))],
        out_specs=pl.BlockSpec((1, tq, BRANCH_W), lambda b, j, u, ty: (b, j, 0)),
        scratch_shapes=[pltpu.VMEM((BRANCH_W, tq), F32)])
    return pl.pallas_call(
        functools.partial(_na_kernel, ctx_len=ctx_len),
        out_shape=jax.ShapeDtypeStruct((B, T, BRANCH_W), BF16),
        grid_spec=grid_spec,
        compiler_params=_cparams(("arbitrary", "arbitrary")),
        name="na_attn",
    )(jnp.asarray(u0), jnp.asarray(type_of), q_t, k, v_t, bias)


def _merge_kernel(x_ref, sc1_ref, sh1_ref, g1_ref, sc2_ref, sh2_ref, ona_ref, odf_ref, ogq_ref, oml_ref,
                  wg_ref, wb_ref, wo_ref, lng_ref, lnb_ref, rw_ref, rb_ref, tri_ref,
                  x1_ref, h2_ref, idx_ref, wts_ref, rank_ref, cnt_ref, carry_ref, hprev_ref):
    tm = x_ref.shape[1]
    step = pl.program_id(0)

    @pl.when(step == 0)
    def _():
        carry_ref[...] = jnp.zeros_like(carry_ref)
        hprev_ref[...] = jnp.zeros_like(hprev_ref)

    iota = lax.broadcasted_iota(I32, (N_EXPERTS, tm), 0)
    route = {}

    def route_logits():
        route["cur"] = lax.dot_general(rw_ref[...], hprev_ref[...], (((1,), (1,)), ((), ())),
                                       preferred_element_type=F32, precision=lax.Precision.HIGHEST) + rb_ref[...]
        route["vals"], route["idxs"] = [], []

    def route_pick():
        cur = route["cur"]
        m = jnp.max(cur, axis=0, keepdims=True)
        ik = jnp.min(jnp.where(cur == m, iota, N_EXPERTS), axis=0, keepdims=True)
        route["vals"].append(m)
        route["idxs"].append(ik)
        route["cur"] = jnp.where(iota == ik, -jnp.inf, cur)

    def route_finish():
        vals, idxs = route["vals"], route["idxs"]
        exps = [jnp.exp(v - vals[0]) for v in vals]
        denom = exps[0] + exps[1] + exps[2] + exps[3]
        wts_ref[0] = jnp.concatenate([e / denom for e in exps], axis=0)
        idx_ref[0] = jnp.concatenate(idxs, axis=0)
        live = jnp.where(step >= 1, 1.0, 0.0)
        onehot = jnp.zeros((N_EXPERTS, tm), F32)
        for ik in idxs:
            onehot = onehot + jnp.where(iota == ik, live, 0.0)
        before = jnp.dot(onehot.astype(BF16), tri_ref[...], preferred_element_type=F32) + carry_ref[...]
        rank_ref[0] = jnp.concatenate(
            [jnp.sum(jnp.where(iota == ik, before, 0.0), axis=0, keepdims=True) for ik in idxs], axis=0).astype(I32)
        carry = carry_ref[...] + jnp.sum(onehot, axis=1, keepdims=True)
        carry_ref[...] = carry
        cnt_ref[...] = carry.astype(I32)

    route_pieces = [[route_logits], [route_pick, route_pick], [route_pick, route_pick], [route_finish]]

    x = x_ref[0]
    h1 = (x * (1.0 + sc1_ref[...]) + sh1_ref[...]).astype(BF16)
    y = None
    for i, o_ref in enumerate((ona_ref, odf_ref, ogq_ref, oml_ref)):
        gate = jax.nn.sigmoid(jnp.dot(h1, wg_ref[:, i * D_MODEL:(i + 1) * D_MODEL], preferred_element_type=F32))
        term = gate * jnp.dot(o_ref[0], wb_ref[i * BRANCH_W:(i + 1) * BRANCH_W, :], preferred_element_type=F32)
        y = term if y is None else y + term
        for piece in route_pieces[i]:
            piece()
    z = jnp.dot(y.astype(BF16), wo_ref[...], preferred_element_type=F32)
    r = DEEPNORM_ALPHA * x + g1_ref[...] * z
    mu = jnp.mean(r, axis=-1, keepdims=True)
    var = jnp.mean(jnp.square(r - mu), axis=-1, keepdims=True)
    x1 = (r - mu) * lax.rsqrt(var + EPS) * lng_ref[...] + lnb_ref[...]
    x1_ref[0] = x1
    h2 = x1 * (1.0 + sc2_ref[...]) + sh2_ref[...]
    for j in range(D_MODEL // LANES):
        h2_ref[:, j, :] = h2[:, j * LANES:(j + 1) * LANES]
    hprev_ref[...] = h2


def _merge(xa, mod, o_na, o_df, o_gq, o_ml, wg, wb, wo, ln_g, ln_b, rw_t, rb, ctx_len, t_off):
    B, T, D = xa.shape
    tm = ROW_TILE
    nt = T // tm - t_off
    ctx_tiles = ctx_len // tm
    tq_rows = nt * tm
    n_tiles = B * nt

    def bt(i):
        ti = jnp.minimum(i, n_tiles - 1)
        return ti // nt, ti % nt

    def kind(t):
        return jnp.where(t + t_off >= ctx_tiles, 1, 0)

    def mod_spec(j):
        return pl.BlockSpec((None, None, None, 1, D), lambda i: (bt(i)[0], kind(bt(i)[1]), j, 0, 0))

    def const(a):
        nd = a.ndim
        return pl.BlockSpec(a.shape, lambda i: (0,) * nd)

    def tok(cols):
        return pl.BlockSpec((1, tm, cols), lambda i: (bt(i)[0], bt(i)[1] + t_off, 0))

    tri = jnp.asarray(np.triu(np.ones((tm, tm), np.float32), 1), BF16)
    rb_b = jnp.broadcast_to(rb[:, None], (N_EXPERTS, tm)).astype(F32)
    route_shape = jax.ShapeDtypeStruct((n_tiles, TOP_K, tm), I32)
    route_spec = pl.BlockSpec((1, TOP_K, tm), lambda i: (jnp.maximum(i - 1, 0), 0, 0))
    return pl.pallas_call(
        _merge_kernel,
        out_shape=[jax.ShapeDtypeStruct((B, tq_rows, D), F32),
                   jax.ShapeDtypeStruct((B * tq_rows, D // LANES, LANES), F32),
                   route_shape, jax.ShapeDtypeStruct((n_tiles, TOP_K, tm), F32), route_shape,
                   jax.ShapeDtypeStruct((N_EXPERTS, tm), I32)],
        grid=(n_tiles + 1,),
        in_specs=[tok(D), mod_spec(1), mod_spec(0), mod_spec(2), mod_spec(4), mod_spec(3),
                  tok(BRANCH_W), tok(BRANCH_W), tok(BRANCH_W), tok(BRANCH_W),
                  const(wg), const(wb), const(wo), const(ln_g), const(ln_b), const(rw_t), const(rb_b), const(tri)],
        out_specs=[pl.BlockSpec((1, tm, D), lambda i: (bt(i)[0], bt(i)[1], 0)),
                   pl.BlockSpec((tm, D // LANES, LANES), lambda i: (jnp.minimum(i, n_tiles - 1), 0, 0)),
                   route_spec, route_spec, route_spec,
                   pl.BlockSpec((N_EXPERTS, tm), lambda i: (0, 0))],
        scratch_shapes=[pltpu.VMEM((N_EXPERTS, tm), F32), pltpu.VMEM((tm, D), F32)],
        compiler_params=_cparams(("arbitrary",)),
        name="merge_router",
    )(xa, mod, mod, mod, mod, mod, o_na, o_df, o_gq, o_ml, wg, wb, wo, ln_g, ln_b, rw_t, rb_b, tri)


def _row_copy(src, src_row, dst, dst_row, sem):
    return pltpu.make_async_copy(src.at[pl.ds(src_row, 1)], dst.at[pl.ds(dst_row, 1)], sem)


def _dispatch_kernel(dest_ref, h_ref, xs_in_ref, xs_ref, sem):
    del xs_in_ref
    tm = dest_ref.shape[2]

    def issue(t, c):
        for k in range(TOP_K):
            _row_copy(h_ref, t, xs_ref, dest_ref[0, k, t], sem).start(priority=k % 2)
        return c

    lax.fori_loop(0, tm, issue, 0, unroll=2)
    for k in range(TOP_K):
        pltpu.make_async_copy(h_ref, xs_ref.at[pl.ds(0, tm)], sem).wait()


def _dispatch(dest, h2, xs0):
    n_tiles, _, tm = dest.shape
    return pl.pallas_call(
        _dispatch_kernel,
        out_shape=jax.ShapeDtypeStruct(xs0.shape, F32),
        grid=(n_tiles,),
        in_specs=[pl.BlockSpec((1, TOP_K, tm), lambda i: (i, 0, 0), memory_space=pltpu.SMEM),
                  pl.BlockSpec((tm,) + h2.shape[1:], lambda i: (i, 0, 0)),
                  pl.BlockSpec(memory_space=pl.ANY)],
        out_specs=pl.BlockSpec(memory_space=pl.ANY),
        scratch_shapes=[pltpu.SemaphoreType.DMA(())],
        input_output_aliases={2: 0},
        compiler_params=_cparams(("arbitrary",)),
        name="moe_dispatch",
    )(dest, h2, xs0)


def _tile_relayout_copies(tiled_hbm, flat_vmem, tile, sem):
    tm = flat_vmem.shape[0]
    r0 = pl.multiple_of(tile * tm, tm)
    return [(tiled_hbm.at[pl.ds(r0, tm), j, :], flat_vmem.at[:, pl.ds(j * LANES, LANES)], sem)
            for j in range(D_MODEL // LANES)]


def _expert_kernel(te_ref, nu_ref, xs_ref, w1_ref, b1_ref, w2_ref, b2_ref, y_ref, w1s_ref, w2s_ref, x_buf, y_buf,
                   sem_in, sem_out):
    i = pl.program_id(0)
    n_used = nu_ref[0]
    slot = lax.rem(i, 2)

    def fetch(tile, s, wait):
        for hbm, vmem, sem in _tile_relayout_copies(xs_ref, x_buf.at[s], tile, sem_in.at[s]):
            cp = pltpu.make_async_copy(hbm, vmem, sem)
            cp.wait() if wait else cp.start()

    def write_back(tile, s, wait):
        for hbm, vmem, sem in _tile_relayout_copies(y_ref, y_buf.at[s], tile, sem_out.at[s]):
            cp = pltpu.make_async_copy(vmem, hbm, sem)
            cp.wait() if wait else cp.start()

    @pl.when(i == 0)
    def _():
        fetch(0, 0, False)

    @pl.when(i + 1 < n_used)
    def _():
        fetch(i + 1, 1 - slot, False)

    @pl.when((i == 0) | (te_ref[i] != te_ref[jnp.maximum(i - 1, 0)]))
    def _():
        w1s_ref[...] = w1_ref[...].astype(BF16)
        w2s_ref[...] = w2_ref[...].astype(BF16)

    @pl.when(i >= 2)
    def _():
        write_back(i - 2, slot, True)

    @pl.when(i < n_used)
    def _():
        fetch(i, slot, True)
        gu = jnp.dot(x_buf[slot].astype(BF16), w1s_ref[...], preferred_element_type=F32) + b1_ref[...]
        gate = jnp.minimum(gu[:, :D_FF], SWIGLU_LIMIT)
        up = jnp.clip(gu[:, D_FF:], -SWIGLU_LIMIT, SWIGLU_LIMIT)
        act = (up + 1.0) * gate * jax.nn.sigmoid(SWIGLU_ALPHA * gate)
        y_buf[slot] = jnp.dot(act.astype(BF16), w2s_ref[...], preferred_element_type=F32) + b2_ref[...]

    @pl.when(i >= n_used)
    def _():
        y_buf[slot] = jnp.zeros(y_buf.shape[1:], F32)

    write_back(i, slot, False)

    @pl.when(i == pl.num_programs(0) - 1)
    def _():
        write_back(i, slot, True)

        @pl.when(i >= 1)
        def _():
            write_back(i - 1, 1 - slot, True)


def _experts(tile_expert, n_used, xs, w1, b1, w2, b2, layer):
    n_slots = xs.shape[0]
    tm = ROW_TILE
    nt = n_slots // tm
    E = w1.shape[1]
    grid_spec = pltpu.PrefetchScalarGridSpec(
        num_scalar_prefetch=2, grid=(nt,),
        in_specs=[pl.BlockSpec(memory_space=pl.ANY),
                  pl.BlockSpec((None, None, D_MODEL, 2 * D_FF), lambda i, te, nu: (layer, te[i], 0, 0)),
                  pl.BlockSpec((None, None, 1, 2 * D_FF), lambda i, te, nu: (layer, te[i], 0, 0)),
                  pl.BlockSpec((None, None, D_FF, D_MODEL), lambda i, te, nu: (layer, te[i], 0, 0)),
                  pl.BlockSpec((None, None, 1, D_MODEL), lambda i, te, nu: (layer, te[i], 0, 0))],
        out_specs=pl.BlockSpec(memory_space=pl.ANY),
        scratch_shapes=[pltpu.VMEM((D_MODEL, 2 * D_FF), BF16), pltpu.VMEM((D_FF, D_MODEL), BF16),
                        pltpu.VMEM((2, tm, D_MODEL), F32), pltpu.VMEM((2, tm, D_MODEL), F32),
                        pltpu.SemaphoreType.DMA((2,)), pltpu.SemaphoreType.DMA((2,))])
    return pl.pallas_call(
        _expert_kernel,
        out_shape=jax.ShapeDtypeStruct(xs.shape, F32),
        grid_spec=grid_spec,
        compiler_params=_cparams(("arbitrary",)),
        name="moe_experts",
    )(tile_expert, n_used, xs, w1, b1.reshape(b1.shape[0], E, 1, -1), w2, b2.reshape(b2.shape[0], E, 1, -1))


def _combine_kernel(dest_ref, destn_ref, w_ref, x1_ref, g2_ref, lng_ref, lnb_ref, y_ref, o_ref, buf_ref, y3_ref,
                    sem):
    tm = dest_ref.shape[2]
    i = pl.program_id(0)
    n = pl.num_programs(0)
    slot = lax.rem(i, 2)

    def issue(dr, s):
        def body(t, c):
            for k in range(TOP_K):
                pltpu.make_async_copy(y_ref.at[pl.ds(dr[0, k, t], 1)], buf_ref.at[s, k, pl.ds(t, 1)],
                                      sem.at[s]).start(priority=k % 2)
            return c
        lax.fori_loop(0, tm, body, 0, unroll=2)

    @pl.when(i == 0)
    def _():
        issue(dest_ref, 0)

    @pl.when(i + 1 < n)
    def _():
        issue(destn_ref, 1 - slot)

    for k in range(TOP_K):
        pltpu.make_async_copy(y_ref.at[pl.ds(0, tm)], buf_ref.at[slot, k], sem.at[slot]).wait()

    def weigh(t, c):
        acc = buf_ref[slot, 0, t] * w_ref[0, 0, t]
        for k in range(1, TOP_K):
            acc = acc + buf_ref[slot, k, t] * w_ref[0, k, t]
        y3_ref[t] = acc
        return c

    lax.fori_loop(0, tm, weigh, 0, unroll=8)
    y2 = jnp.concatenate([y3_ref[:, j, :] for j in range(D_MODEL // LANES)], axis=1)
    r = DEEPNORM_ALPHA * x1_ref[0] + g2_ref[...] * y2
    mu = jnp.mean(r, axis=-1, keepdims=True)
    var = jnp.mean(jnp.square(r - mu), axis=-1, keepdims=True)
    o_ref[0] = (r - mu) * lax.rsqrt(var + EPS) * lng_ref[...] + lnb_ref[...]


def _combine(dest, wts, x1, mod, ln_g, ln_b, y, ctx_len, t_off):
    B, tq_rows, D = x1.shape
    n_tiles, _, tm = dest.shape
    nt = n_tiles // B
    ctx_tiles = ctx_len // tm

    def kind(i):
        return jnp.where(i % nt + t_off >= ctx_tiles, 1, 0)

    smem = lambda f: pl.BlockSpec((1, TOP_K, tm), f, memory_space=pltpu.SMEM)
    cur = lambda i: (i, 0, 0)
    nxt = lambda i: (jnp.minimum(i + 1, n_tiles - 1), 0, 0)
    return pl.pallas_call(
        _combine_kernel,
        out_shape=jax.ShapeDtypeStruct((B, tq_rows, D), F32),
        grid=(n_tiles,),
        in_specs=[smem(cur), smem(nxt), smem(cur),
                  pl.BlockSpec((1, tm, D), lambda i: (i // nt, i % nt, 0)),
                  pl.BlockSpec((None, None, None, 1, D), lambda i: (i // nt, kind(i), 5, 0, 0)),
                  pl.BlockSpec(ln_g.shape, lambda i: (0, 0)),
                  pl.BlockSpec(ln_b.shape, lambda i: (0, 0)),
                  pl.BlockSpec(memory_space=pl.ANY)],
        out_specs=pl.BlockSpec((1, tm, D), lambda i: (i // nt, i % nt, 0)),
        scratch_shapes=[pltpu.VMEM((2, TOP_K, tm, D // LANES, LANES), F32),
                        pltpu.VMEM((tm, D // LANES, LANES), F32), pltpu.SemaphoreType.DMA((2,))],
        compiler_params=_cparams(("arbitrary",)),
        name="moe_combine",
    )(dest, dest, wts, x1, mod, ln_g, ln_b, y)


_DF_HEADS_SPEC = tuple((((2 * i, i, False), (2 * i + 1, i, True)), i) for i in range(DF_HEADS))
_GQ_HEADS_SPEC = tuple((((i, i // (GQ_HEADS // GQ_KV_HEADS), False),), i // (GQ_HEADS // GQ_KV_HEADS))
                       for i in range(GQ_HEADS))
_ML_HEADS_SPEC = tuple((((i, i, False),), i) for i in range(ML_HEADS))


def _layer_weights(l, w_in, gq_qnorm, gq_knorm, ml_qa_norm, ml_wq_b, ml_kva_norm, ml_wkv_b):
    w = w_in[l]

    def cols(o, n):
        return w[:, o:o + n]

    parts = [cols(O_NAQ, 256), cols(O_NAK, 256), cols(O_NAV, 256),
             cols(O_DFQ, 256), _swap_cols(cols(O_DFQ, 256), 32), cols(O_DFK, 256), _swap_cols(cols(O_DFK, 256), 32),
             cols(O_DFV, 256),
             cols(O_GQQ, 256), _swap_cols(cols(O_GQQ, 256), 64), cols(O_GQK, 128), _swap_cols(cols(O_GQK, 128), 64),
             cols(O_GQV, 128),
             cols(O_MLQA, 256), cols(O_MLKVA, 128), cols(O_MLKR, 32), _swap_cols(cols(O_MLKR, 32), 32)]
    w_t = jnp.concatenate(parts, axis=1).T.astype(BF16)
    assert w_t.shape[0] == PROJ_ROWS

    tm = ROW_TILE
    gq_g = jnp.stack([gq_qnorm[l], _swap_cols(gq_qnorm[l], 64, signed=False),
                      gq_knorm[l], _swap_cols(gq_knorm[l], 64, signed=False)])
    gq_g = jnp.broadcast_to(gq_g[:, :, None], (4, GQ_DIM, tm)).astype(F32)
    qa_g = jnp.broadcast_to(ml_qa_norm[l][:, None], (ML_Q_RANK, tm)).astype(F32)
    kva_g = jnp.broadcast_to(ml_kva_norm[l][:, None], (ML_KV_RANK, tm)).astype(F32)
    wq = ml_wq_b[l]
    qd = ML_NOPE + ML_ROPE
    wq_parts = []
    for i in range(ML_HEADS):
        rope_cols = wq[:, i * qd + ML_NOPE:(i + 1) * qd]
        wq_parts += [wq[:, i * qd:i * qd + ML_NOPE], rope_cols, _swap_cols(rope_cols, 32)]
    wq_t = jnp.concatenate(wq_parts, axis=1).T.astype(BF16)
    wkv_t = ml_wkv_b[l].T.astype(BF16)
    return w_t, gq_g, qa_g, kva_g, wq_t, wkv_t


def kernel(x, c, ctx, c_ctx, w_ada, b_ada, w_in, na_rpb, df_lam, df_subln, gq_qnorm, gq_knorm, ml_qa_norm, ml_wq_b,
           ml_kva_norm, ml_wkv_b, w_branch, w_out, ln1_g, ln1_b, ln2_g, ln2_b, router_w, router_b, exp_w1, exp_b1,
           exp_w2, exp_b2):
    B, S, D = x.shape
    C = ctx.shape[1]
    T = C + S
    tm = ROW_TILE
    assert D == D_MODEL and C % tm == 0 and S % tm == 0 and C == tm

    xa = jnp.concatenate([ctx, x], axis=1)
    cc = jnp.concatenate([c, c_ctx[None], jnp.zeros((16 - B - 1, D), F32)], axis=0)
    mod_all = _ada(cc, w_ada, b_ada)
    cs32 = _rope_tables(32, C, S)
    cs64 = _rope_tables(64, C, S)

    xs = None
    for l in range(DEPTH):
        last = l == DEPTH - 1
        m = mod_all[l]
        mod = jnp.stack([jnp.broadcast_to(m[B][None], (B, 6 * D)), m[:B]], axis=1).reshape(B, 2, 6, 1, D)
        lam_init = 0.8 - 0.6 * math.exp(-0.3 * l)
        lp = df_lam[l].astype(F32)
        lam = (jnp.exp(jnp.sum(lp[0] * lp[1])) - jnp.exp(jnp.sum(lp[2] * lp[3])) + lam_init).reshape(1)
        w_t, gq_g, qa_g, kva_g, wq_t, wkv_t = _layer_weights(l, w_in, gq_qnorm, gq_knorm, ml_qa_norm, ml_wq_b,
                                                             ml_kva_norm, ml_wkv_b)
        (na_q, na_k, na_v, df_q, df_k, df_v, gq_q, gq_k, gq_v, ml_q, ml_k, ml_v) = _project(
            xa, mod, w_t, cs32, cs64, gq_g, qa_g, kva_g, wq_t, wkv_t, C)

        o_na = _na_attn(na_q, na_k, na_v, na_rpb[l], C)
        subln = jnp.broadcast_to((df_subln[l] * (1.0 - lam_init))[:, None], (DF_V, tm)).astype(F32)
        ones = jnp.ones((DF_V, tm), F32)
        o_df = _dense_attn(lam, df_q, df_k, df_v, subln, _DF_HEADS_SPEC, C, True, "df_attn")
        o_gq = _dense_attn(lam, gq_q, gq_k, gq_v, ones, _GQ_HEADS_SPEC, C, False, "gq_attn")
        o_ml = _dense_attn(lam, ml_q, ml_k, ml_v, ones, _ML_HEADS_SPEC, C, False, "ml_attn")

        t_off = C // tm if last else 0
        wg = w_in[l][:, O_GATES:].astype(BF16)
        wb = w_branch[l].reshape(N_BRANCH * BRANCH_W, D).astype(BF16)
        wo = w_out[l].astype(BF16)
        x1, h2, idx, wts, rank, cnt = _merge(xa, mod, o_na, o_df, o_gq, o_ml, wg, wb, wo,
                                             ln1_g[l][None], ln1_b[l][None], router_w[l].T, router_b[l], C, t_off)

        counts = cnt[:, 0]
        padded = ((counts + tm - 1) // tm) * tm
        ends = jnp.cumsum(padded)
        starts = (ends - padded).astype(I32)
        n_exp_tiles = (B * T * TOP_K) // tm + N_EXPERTS
        n_used = (ends[-1] // tm).astype(I32).reshape(1)
        tile_first = jnp.arange(n_exp_tiles, dtype=I32) * tm
        tile_expert = jnp.minimum(jnp.sum((ends[None, :] <= tile_first[:, None]).astype(I32), axis=1), N_EXPERTS - 1)
        tile_expert = jnp.where(tile_first < ends[-1], tile_expert, tile_expert[jnp.maximum(n_used[0] - 1, 0)])

        dest = jnp.sum(jnp.where(idx[..., None] == jnp.arange(N_EXPERTS, dtype=I32), starts, 0), axis=-1) + rank
        if xs is None:
            xs = jnp.zeros((n_exp_tiles * tm,) + h2.shape[1:], F32)
        xs = _dispatch(dest, h2, xs)
        ys = _experts(tile_expert, n_used, xs, exp_w1, exp_b1, exp_w2, exp_b2, l)
        xa = _combine(dest, wts, x1, mod, ln2_g[l][None], ln2_b[l][None], ys, C, t_off)
    return xa
```

```python
import functools
import math

import numpy as np
import jax
import jax.numpy as jnp
from jax import lax
from jax.experimental import pallas as pl
from jax.experimental.pallas import tpu as pltpu

F32, BF16, I32 = jnp.float32, jnp.bfloat16, jnp.int32

D_MODEL = 1024
DEPTH = 2
GRID_W = 64
ROPE_BASE = 10000.0
EPS = 1e-6
NA_HEADS, NA_DIM, NA_KH, NA_KW = 4, 64, 8, 16
DF_HEADS, DF_QK = 4, 32
DF_V = 2 * DF_QK
GQ_HEADS, GQ_KV_HEADS, GQ_DIM = 4, 2, 64
ML_HEADS, ML_NOPE, ML_ROPE, ML_V, ML_Q_RANK, ML_KV_RANK = 4, 64, 32, 64, 256, 128
N_BRANCH, BRANCH_W = 4, 256
N_EXPERTS, TOP_K = 32, 4
D_FF = D_MODEL
SWIGLU_LIMIT, SWIGLU_ALPHA = 7.0, 1.702
DEEPNORM_ALPHA = (2 * DEPTH) ** 0.25

IN_SIZES = (256, 256, 256, 256, 256, 256, 256, 128, 128, ML_Q_RANK, ML_KV_RANK, ML_ROPE, N_BRANCH * D_MODEL)
IN_OFFSETS = tuple(int(o) for o in np.cumsum((0,) + IN_SIZES)[:-1])
(O_NAQ, O_NAK, O_NAV, O_DFQ, O_DFK, O_DFV, O_GQQ, O_GQK, O_GQV, O_MLQA, O_MLKVA, O_MLKR, O_GATES) = IN_OFFSETS

LANES = 128
ROW_TILE = 256
MASK_VALUE = -1e30
VMEM_LIMIT = 56 * 1024 * 1024
LOG2E = math.log2(math.e)
KEY_CHUNK = 256
ACC_ROWS = 16
SCORE_SPLIT = 2

_PROJ_GROUPS = (("na_q", 256), ("na_k", 256), ("na_v", 256),
                ("df_q", 256), ("df_q_sw", 256), ("df_k", 256), ("df_k_sw", 256), ("df_v", 256),
                ("gq_q", 256), ("gq_q_sw", 256), ("gq_k", 128), ("gq_k_sw", 128), ("gq_v", 128),
                ("ml_qa", 256), ("ml_kva", 128), ("ml_kr", 32), ("ml_kr_sw", 32))
_PROJ_OFF = {}
_o = 0
for _n, _r in _PROJ_GROUPS:
    _PROJ_OFF[_n] = (_o, _r)
    _o += _r
PROJ_ROWS = _o


def _cparams(sem):
    return pltpu.CompilerParams(dimension_semantics=sem, vmem_limit_bytes=VMEM_LIMIT)


def _swap_cols(w, n, signed=True):
    lead, width = w.shape[:-1], w.shape[-1]
    w5 = w.reshape(lead + (width // n, 2, 2, n // 4))
    lo, hi = w5[..., 0, :], w5[..., 1, :]
    out = jnp.stack([-hi if signed else hi, lo], axis=-2)
    return out.reshape(lead + (width,))


def _rope_tables(n, ctx_len, seq):
    h = n // 2
    t = np.arange(seq)
    rows, cols = (t // GRID_W).astype(np.float32), (t % GRID_W).astype(np.float32)
    inv_freq = (np.float32(ROPE_BASE) ** (-np.arange(0, h, 2, dtype=np.float32) / np.float32(h))).astype(np.float32)
    ang_r = rows[:, None] * inv_freq[None, :]
    ang_c = cols[:, None] * inv_freq[None, :]
    ang = np.concatenate([ang_r, ang_r, ang_c, ang_c], axis=1)
    cos = np.concatenate([np.ones((ctx_len, n), np.float32), np.cos(ang)], axis=0).T
    sin = np.concatenate([np.zeros((ctx_len, n), np.float32), np.sin(ang)], axis=0).T
    return jnp.asarray(np.stack([cos, sin]).astype(np.float32))


def _ada_kernel(c_ref, w_ref, b_ref, o_ref):
    cc = c_ref[...]
    h = (cc * jax.nn.sigmoid(cc)).astype(BF16)
    o_ref[0] = jnp.dot(h, w_ref[0].astype(BF16), preferred_element_type=F32) + b_ref[0]


def _ada(cc, w_ada, b_ada):
    L, D, N = w_ada.shape
    R = cc.shape[0]
    tn = 512
    return pl.pallas_call(
        _ada_kernel,
        out_shape=jax.ShapeDtypeStruct((L, R, N), F32),
        grid=(L, N // tn),
        in_specs=[pl.BlockSpec((R, D), lambda l, j: (0, 0)),
                  pl.BlockSpec((1, D, tn), lambda l, j: (l, 0, j)),
                  pl.BlockSpec((1, 1, tn), lambda l, j: (l, 0, j))],
        out_specs=pl.BlockSpec((1, R, tn), lambda l, j: (l, 0, j)),
        compiler_params=_cparams(("arbitrary", "arbitrary")),
        name="ada_mod",
    )(cc, w_ada, b_ada.reshape(L, 1, N))


def _proj_kernel(x_ref, sc_ref, sh_ref, w_ref, cs32_ref, cs64_ref, gqg_ref, qag_ref, kvag_ref, wq_ref, wkv_ref,
                 na_q, na_k, na_v, df_q, df_k, df_v, gq_q, gq_k, gq_v, ml_q, ml_k, ml_v):
    tm = x_ref.shape[1]
    h = (x_ref[0] * (1.0 + sc_ref[...]) + sh_ref[...]).astype(BF16)
    p = lax.dot_general(w_ref[...], h, (((1,), (1,)), ((), ())), preferred_element_type=F32)

    def grp(name):
        o, r = _PROJ_OFF[name]
        return p[o:o + r]

    def zeros(n):
        return jnp.zeros((n, tm), F32)

    def rope(a, a_sw, cs_ref):
        n = cs_ref.shape[1]
        r = a.shape[0]
        a3, s3 = a.reshape(r // n, n, tm), a_sw.reshape(r // n, n, tm)
        return (a3 * cs_ref[0][None] + s3 * cs_ref[1][None]).reshape(r, tm)

    def head_slots(a, width):
        pieces = []
        for i in range(a.shape[0] // width):
            pieces += [a[i * width:(i + 1) * width], zeros(LANES - width)]
        return jnp.concatenate(pieces, axis=0)

    na_q[0] = head_slots(grp("na_q") * (NA_DIM ** -0.5), NA_DIM).astype(BF16)
    na_k[0] = head_slots(grp("na_k"), NA_DIM).T.astype(BF16)
    na_v[0] = grp("na_v").astype(BF16)

    q = rope(grp("df_q"), grp("df_q_sw"), cs32_ref) * (DF_QK ** -0.5 * LOG2E)
    pieces = []
    for i in range(DF_HEADS):
        q1, q2 = q[i * 64:i * 64 + 32], q[i * 64 + 32:i * 64 + 64]
        pieces += [q1, zeros(LANES - 32), zeros(32), q2, zeros(LANES - 64)]
    df_q[0] = jnp.concatenate(pieces, axis=0).astype(BF16)
    df_k[0] = head_slots(rope(grp("df_k"), grp("df_k_sw"), cs32_ref), 2 * DF_QK).T.astype(BF16)
    df_v[0] = grp("df_v").astype(BF16)

    cos64, sin64 = cs64_ref[0], cs64_ref[1]

    def norm_rope(a, a_sw, g, g_sw, scale):
        hh = a.shape[0] // GQ_DIM
        a3, s3 = a.reshape(hh, GQ_DIM, tm), a_sw.reshape(hh, GQ_DIM, tm)
        r = lax.rsqrt(jnp.mean(a3 * a3, axis=1, keepdims=True) + EPS) * scale
        return ((a3 * (g * cos64)[None] + s3 * (g_sw * sin64)[None]) * r).reshape(hh * GQ_DIM, tm)

    gq_q[0] = head_slots(norm_rope(grp("gq_q"), grp("gq_q_sw"), gqg_ref[0], gqg_ref[1], GQ_DIM ** -0.5 * LOG2E),
                         GQ_DIM).astype(BF16)
    gq_k[0] = head_slots(norm_rope(grp("gq_k"), grp("gq_k_sw"), gqg_ref[2], gqg_ref[3], 1.0),
                         GQ_DIM).T.astype(BF16)
    gq_v[0] = grp("gq_v").astype(BF16)

    qa = grp("ml_qa")
    qan = (qa * lax.rsqrt(jnp.mean(qa * qa, axis=0, keepdims=True) + EPS) * qag_ref[...]).astype(BF16)
    qq = jnp.dot(wq_ref[...], qan, preferred_element_type=F32)
    kva = grp("ml_kva")
    kvan = (kva * lax.rsqrt(jnp.mean(kva * kva, axis=0, keepdims=True) + EPS) * kvag_ref[...]).astype(BF16)
    kv = jnp.dot(wkv_ref[...], kvan, preferred_element_type=F32)
    k_rope = rope(grp("ml_kr"), grp("ml_kr_sw"), cs32_ref)
    cos32, sin32 = cs32_ref[0], cs32_ref[1]
    ml_scale = (ML_NOPE + ML_ROPE) ** -0.5 * LOG2E
    qp, kp, vp = [], [], []
    for i in range(ML_HEADS):
        b = i * LANES
        q_rope = qq[b + 64:b + 96] * cos32 + qq[b + 96:b + 128] * sin32
        qp += [qq[b:b + 64] * ml_scale, q_rope * ml_scale, zeros(32)]
        kp += [kv[b:b + 64], k_rope, zeros(32)]
        vp += [kv[b + 64:b + 128]]
    ml_q[0] = jnp.concatenate(qp, axis=0).astype(BF16)
    ml_k[0] = jnp.concatenate(kp, axis=0).T.astype(BF16)
    ml_v[0] = jnp.concatenate(vp, axis=0).astype(BF16)


def _project(xa, mod, w_t, cs32, cs64, gq_g, qa_g, kva_g, wq_t, wkv_t, ctx_len):
    B, T, D = xa.shape
    tm = ROW_TILE
    nt = T // tm
    ctx_tiles = ctx_len // tm

    def kind(t):
        return jnp.where(t >= ctx_tiles, 1, 0)

    def mod_spec(j):
        return pl.BlockSpec((None, None, None, 1, D), lambda b, t: (b, kind(t), j, 0, 0))

    def const(a):
        nd = a.ndim
        return pl.BlockSpec(a.shape, lambda b, t: (0,) * nd)

    def fm(rows):
        return jax.ShapeDtypeStruct((B, rows, T), BF16), pl.BlockSpec((1, rows, tm), lambda b, t: (b, 0, t))

    def tmj(cols):
        return jax.ShapeDtypeStruct((B, T, cols), BF16), pl.BlockSpec((1, tm, cols), lambda b, t: (b, t, 0))

    outs = [fm(512), tmj(512), fm(256),
            fm(1024), tmj(512), fm(256),
            fm(512), tmj(256), fm(128),
            fm(512), tmj(512), fm(256)]
    return pl.pallas_call(
        _proj_kernel,
        out_shape=[o[0] for o in outs],
        grid=(B, nt),
        in_specs=[pl.BlockSpec((1, tm, D), lambda b, t: (b, t, 0)),
                  mod_spec(1), mod_spec(0),
                  const(w_t),
                  pl.BlockSpec((2, 32, tm), lambda b, t: (0, 0, t)),
                  pl.BlockSpec((2, 64, tm), lambda b, t: (0, 0, t)),
                  const(gq_g), const(qa_g), const(kva_g), const(wq_t), const(wkv_t)],
        out_specs=[o[1] for o in outs],
        compiler_params=_cparams(("arbitrary", "arbitrary")),
        name="in_proj",
    )(xa, mod, mod, w_t, cs32, cs64, gq_g, qa_g, kva_g, wq_t, wkv_t)


def _softmax_pv(k, q_t, v_t, exp_fn):
    s = jnp.dot(k, q_t, preferred_element_type=F32)
    m = jnp.max(s, axis=0, keepdims=True)
    p = exp_fn(s - m)
    l = jnp.sum(p, axis=0, keepdims=True)
    o = jnp.dot(v_t, p.astype(BF16), preferred_element_type=F32)
    return o, l


def _dense_attn_kernel(lam_ref, q_ref, k_ref, v_ref, post_ref, o_ref, ot_ref, s_ref, p_ref, *, heads, ctx_len,
                       diff_norm):
    total, tq = k_ref.shape[1], q_ref.shape[2]
    items = [(qs, ks, v_idx, i, signed) for i, (terms, v_idx) in enumerate(heads) for qs, ks, signed in terms]
    last_of_head = {i: max(j for j, it in enumerate(items) if it[3] == i) for i in range(len(heads))}

    def finish_head(i, acc):
        if diff_norm:
            acc = acc * lax.rsqrt(jnp.mean(acc * acc, axis=0, keepdims=True) + EPS) * post_ref[...]
        ot_ref[i * 64:(i + 1) * 64, :] = acc

    def run_context():
        head_acc = {}
        for j, (qs, ks, v_idx, i, signed) in enumerate(items):
            o, l = _softmax_pv(k_ref[0, 0:ctx_len, ks * LANES:(ks + 1) * LANES], q_ref[0, qs * LANES:(qs + 1) * LANES, :],
                               v_ref[0, v_idx * 64:(v_idx + 1) * 64, 0:ctx_len], jnp.exp2)
            o = o * (((-lam_ref[0]) if signed else 1.0) / l)
            head_acc[i] = o if i not in head_acc else head_acc[i] + o
            if last_of_head[i] == j:
                finish_head(i, head_acc.pop(i))

    def run_latent():
        n_chunks = total // KEY_CHUNK
        half = KEY_CHUNK // SCORE_SPLIT
        n_items = len(items)
        m_of, l_of, head_acc = {}, {}, {}
        for t in range(n_items + 2):
            sc = items[t] if t < n_items else None
            ex = items[t - 1] if 0 <= t - 1 < n_items else None
            pv = items[t - 2] if 0 <= t - 2 < n_items else None
            macc = jnp.full((ACC_ROWS, tq), -jnp.inf, F32)
            lacc = jnp.zeros((ACC_ROWS, tq), F32)
            oacc = jnp.zeros((64, tq), F32)
            for c in range(n_chunks):
                r0 = c * KEY_CHUNK
                if sc is not None:
                    q_t = q_ref[0, sc[0] * LANES:(sc[0] + 1) * LANES, :]
                    for r in ((r0,) if half == KEY_CHUNK else (r0, r0 + half)):
                        s = jnp.dot(k_ref[0, r:r + half, sc[1] * LANES:(sc[1] + 1) * LANES], q_t,
                                    preferred_element_type=F32)
                        s_ref[t % 2, r:r + half, :] = s
                        for a in range(half // ACC_ROWS):
                            macc = jnp.maximum(macc, s[a * ACC_ROWS:(a + 1) * ACC_ROWS])
                if ex is not None:
                    p = jnp.exp2(s_ref[(t - 1) % 2, r0:r0 + KEY_CHUNK, :] - m_of[t - 1])
                    for a in range(KEY_CHUNK // ACC_ROWS):
                        lacc = lacc + p[a * ACC_ROWS:(a + 1) * ACC_ROWS]
                    p_ref[(t - 1) % 2, r0:r0 + KEY_CHUNK, :] = p.astype(BF16)
                if pv is not None:
                    oacc = oacc + jnp.dot(v_ref[0, pv[2] * 64:(pv[2] + 1) * 64, r0:r0 + KEY_CHUNK],
                                          p_ref[(t - 2) % 2, r0:r0 + KEY_CHUNK, :], preferred_element_type=F32)
            if sc is not None:
                m_of[t] = jnp.max(macc, axis=0, keepdims=True)
            if ex is not None:
                l_of[t - 1] = jnp.sum(lacc, axis=0, keepdims=True)
            if pv is not None:
                i, signed = pv[3], pv[4]
                o = oacc * (((-lam_ref[0]) if signed else 1.0) / l_of.pop(t - 2))
                head_acc[i] = o if i not in head_acc else head_acc[i] + o
                if last_of_head[i] == t - 2:
                    finish_head(i, head_acc.pop(i))

    qb = pl.program_id(1)

    @pl.when(qb * tq < ctx_len)
    def _():
        run_context()
        o_ref[0] = ot_ref[...].T.astype(BF16)

    @pl.when(qb * tq >= ctx_len)
    def _():
        run_latent()
        o_ref[0] = ot_ref[...].T.astype(BF16)


def _dense_attn(lam, q_t, k, v_t, post, heads, ctx_len, diff_norm, name):
    B, qrows, T = q_t.shape
    tq = ROW_TILE
    kern = functools.partial(_dense_attn_kernel, heads=heads, ctx_len=ctx_len, diff_norm=diff_norm)
    return pl.pallas_call(
        kern,
        out_shape=jax.ShapeDtypeStruct((B, T, BRANCH_W), BF16),
        grid=(B, T // tq),
        in_specs=[pl.BlockSpec(memory_space=pltpu.SMEM),
                  pl.BlockSpec((1, qrows, tq), lambda b, t: (b, 0, t)),
                  pl.BlockSpec((1, T, k.shape[2]), lambda b, t: (b, 0, 0)),
                  pl.BlockSpec((1, v_t.shape[1], T), lambda b, t: (b, 0, 0)),
                  pl.BlockSpec(post.shape, lambda b, t: (0, 0))],
        out_specs=pl.BlockSpec((1, tq, BRANCH_W), lambda b, t: (b, t, 0)),
        scratch_shapes=[pltpu.VMEM((BRANCH_W, tq), F32), pltpu.VMEM((2, T, tq), F32), pltpu.VMEM((2, T, tq), BF16)],
        compiler_params=_cparams(("arbitrary", "arbitrary")),
        name=name,
    )(lam, q_t, k, v_t, post)


def _na_plan(rows):
    kh, kw = min(NA_KH, rows), NA_KW
    q_rows = ROW_TILE // GRID_W
    win_rows = min(kh + q_rows, rows)
    nblk = rows // q_rows
    row_start = np.clip(np.arange(rows) - kh // 2, 0, rows - kh)
    col_start = np.clip(np.arange(GRID_W) - kw // 2, 0, GRID_W - kw)
    u0 = np.clip(np.arange(nblk) * q_rows - kh // 2, 0, rows - win_rows)
    wk, tq = win_rows * GRID_W, ROW_TILE
    kk, qq = np.arange(wk), np.arange(tq)
    kc, qc = kk % GRID_W, qq % GRID_W
    col_sel = (np.arange(2 * NA_KW - 1)[:, None, None]
               == (np.arange(GRID_W)[None, :, None] - np.arange(GRID_W)[None, None, :] + (NA_KW - 1))).astype(np.float32)
    types, type_of, row_sel_l, valid_l = {}, [], [], []
    for j in range(nblk):
        kr = u0[j] + kk // GRID_W
        qr = j * q_rows + qq // GRID_W
        rs = row_start[qr]
        valid = ((kr[:, None] >= rs[None, :]) & (kr[:, None] < rs[None, :] + kh)
                 & (kc[:, None] >= col_start[qc][None, :]) & (kc[:, None] < col_start[qc][None, :] + kw))
        assert valid.sum(axis=0).min() == kh * kw and valid.sum(axis=0).max() == kh * kw
        rel = (u0[j] + np.arange(win_rows))[:, None] - (j * q_rows + np.arange(q_rows))[None, :] + (NA_KH - 1)
        row_sel = (rel[:, :, None] == np.arange(2 * NA_KH - 1)[None, None, :]).astype(np.float32)
        key = (valid.tobytes(), row_sel.tobytes())
        if key not in types:
            types[key] = len(row_sel_l)
            row_sel_l.append(row_sel), valid_l.append(valid)
        type_of.append(types[key])
    assert all(int(u) * GRID_W % LANES == 0 for u in u0)
    return (np.asarray(u0, np.int32), np.asarray(type_of, np.int32), np.stack(row_sel_l), col_sel,
            np.stack(valid_l), wk)


def _na_kernel(u0_ref, ty_ref, q_ref, k_ref, v_ref, bias_ref, o_ref, ot_ref, *, ctx_len):
    del ty_ref
    wk = bias_ref.shape[2]
    j = pl.program_id(1)

    def finish():
        o_ref[0] = ot_ref[...].T.astype(BF16)

    @pl.when(j == 0)
    def _():
        for i in range(NA_HEADS):
            o, l = _softmax_pv(k_ref[0, 0:ctx_len, i * LANES:(i + 1) * LANES], q_ref[0, i * LANES:(i + 1) * LANES, :],
                               v_ref[0, i * 64:(i + 1) * 64, 0:ctx_len], jnp.exp)
            ot_ref[i * 64:(i + 1) * 64, :] = o * (1.0 / l)
        finish()

    @pl.when(j > 0)
    def _():
        ws = pl.multiple_of(ctx_len + u0_ref[j - 1] * GRID_W, LANES)
        for i in range(NA_HEADS):
            q_t = q_ref[0, i * LANES:(i + 1) * LANES, :]
            s_c = jnp.dot(k_ref[0, 0:ctx_len, i * LANES:(i + 1) * LANES], q_t, preferred_element_type=F32)
            s_w = jnp.dot(k_ref[0, pl.ds(ws, wk), i * LANES:(i + 1) * LANES], q_t,
                          preferred_element_type=F32) + bias_ref[0, i]
            m = jnp.maximum(jnp.max(s_c, axis=0, keepdims=True), jnp.max(s_w, axis=0, keepdims=True))
            p_c, p_w = jnp.exp(s_c - m), jnp.exp(s_w - m)
            l = jnp.sum(p_c, axis=0, keepdims=True) + jnp.sum(p_w, axis=0, keepdims=True)
            o = (jnp.dot(v_ref[0, i * 64:(i + 1) * 64, 0:ctx_len], p_c.astype(BF16), preferred_element_type=F32)
                 + jnp.dot(v_ref[0, i * 64:(i + 1) * 64, pl.ds(ws, wk)], p_w.astype(BF16),
                           preferred_element_type=F32))
            ot_ref[i * 64:(i + 1) * 64, :] = o * (1.0 / l)
        finish()


def _na_attn(q_t, k, v_t, rpb, ctx_len):
    B, _, T = q_t.shape
    tq = ROW_TILE
    rows = (T - ctx_len) // GRID_W
    u0, type_of, row_sel, col_sel, valid, wk = _na_plan(rows)
    hp = lax.Precision.HIGHEST
    by_col = jnp.einsum("hab,bcq->hacq", rpb.astype(F32), col_sel, precision=hp)
    bias = jnp.einsum("tkra,hacq->thkcrq", row_sel, by_col, precision=hp)
    bias = jnp.where(valid[:, None], bias.reshape(row_sel.shape[0], NA_HEADS, wk, tq), MASK_VALUE)
    nb = T // tq
    assert ctx_len == tq

    grid_spec = pltpu.PrefetchScalarGridSpec(
        num_scalar_prefetch=2,
        grid=(B, nb),
        in_specs=[pl.BlockSpec((1, NA_HEADS * LANES, tq), lambda b, j, u, ty: (b, 0, j)),
                  pl.BlockSpec((1, T, NA_HEADS * LANES), lambda b, j, u, ty: (b, 0, 0)),
                  pl.BlockSpec((1, BRANCH_W, T), lambda b, j, u, ty: (b, 0, 0)),
                  pl.BlockSpec((1, NA_HEADS, wk, tq), lambda b, j, u, ty: (ty[jnp.maximum(j - 1, 0)], 0, 0, 0))],
        out_specs=pl.BlockSpec((1, tq, BRANCH_W), lambda b, j, u, ty: (b, j, 0)),
        scratch_shapes=[pltpu.VMEM((BRANCH_W, tq), F32)])
    return pl.pallas_call(
        functools.partial(_na_kernel, ctx_len=ctx_len),
        out_shape=jax.ShapeDtypeStruct((B, T, BRANCH_W), BF16),
        grid_spec=grid_spec,
        compiler_params=_cparams(("arbitrary", "arbitrary")),
        name="na_attn",
    )(jnp.asarray(u0), jnp.asarray(type_of), q_t, k, v_t, bias)


def _merge_kernel(x_ref, sc1_ref, sh1_ref, g1_ref, sc2_ref, sh2_ref, ona_ref, odf_ref, ogq_ref, oml_ref,
                  wg_ref, wb_ref, wo_ref, lng_ref, lnb_ref, rw_ref, rb_ref, tri_ref,
                  x1_ref, h2_ref, idx_ref, wts_ref, rank_ref, cnt_ref, carry_ref, hprev_ref):
    tm = x_ref.shape[1]
    step = pl.program_id(0)

    @pl.when(step == 0)
    def _():
        carry_ref[...] = jnp.zeros_like(carry_ref)
        hprev_ref[...] = jnp.zeros_like(hprev_ref)

    iota = lax.broadcasted_iota(I32, (N_EXPERTS, tm), 0)
    route = {}

    def route_logits():
        route["cur"] = lax.dot_general(rw_ref[...], hprev_ref[...], (((1,), (1,)), ((), ())),
                                       preferred_element_type=F32, precision=lax.Precision.HIGHEST) + rb_ref[...]
        route["vals"], route["idxs"] = [], []

    def route_pick():
        cur = route["cur"]
        m = jnp.max(cur, axis=0, keepdims=True)
        ik = jnp.min(jnp.where(cur == m, iota, N_EXPERTS), axis=0, keepdims=True)
        route["vals"].append(m)
        route["idxs"].append(ik)
        route["cur"] = jnp.where(iota == ik, -jnp.inf, cur)

    def route_finish():
        vals, idxs = route["vals"], route["idxs"]
        exps = [jnp.exp(v - vals[0]) for v in vals]
        denom = exps[0] + exps[1] + exps[2] + exps[3]
        wts_ref[0] = jnp.concatenate([e / denom for e in exps], axis=0)
        idx_ref[0] = jnp.concatenate(idxs, axis=0)
        live = jnp.where(step >= 1, 1.0, 0.0)
        onehot = jnp.zeros((N_EXPERTS, tm), F32)
        for ik in idxs:
            onehot = onehot + jnp.where(iota == ik, live, 0.0)
        before = jnp.dot(onehot.astype(BF16), tri_ref[...], preferred_element_type=F32) + carry_ref[...]
        rank_ref[0] = jnp.concatenate(
            [jnp.sum(jnp.where(iota == ik, before, 0.0), axis=0, keepdims=True) for ik in idxs], axis=0).astype(I32)
        carry = carry_ref[...] + jnp.sum(onehot, axis=1, keepdims=True)
        carry_ref[...] = carry
        cnt_ref[...] = carry.astype(I32)

    route_pieces = [[route_logits], [route_pick, route_pick], [route_pick, route_pick], [route_finish]]

    x = x_ref[0]
    h1 = (x * (1.0 + sc1_ref[...]) + sh1_ref[...]).astype(BF16)
    y = None
    for i, o_ref in enumerate((ona_ref, odf_ref, ogq_ref, oml_ref)):
        gate = jax.nn.sigmoid(jnp.dot(h1, wg_ref[:, i * D_MODEL:(i + 1) * D_MODEL], preferred_element_type=F32))
        term = gate * jnp.dot(o_ref[0], wb_ref[i * BRANCH_W:(i + 1) * BRANCH_W, :], preferred_element_type=F32)
        y = term if y is None else y + term
        for piece in route_pieces[i]:
            piece()
    z = jnp.dot(y.astype(BF16), wo_ref[...], preferred_element_type=F32)
    r = DEEPNORM_ALPHA * x + g1_ref[...] * z
    mu = jnp.mean(r, axis=-1, keepdims=True)
    var = jnp.mean(jnp.square(r - mu), axis=-1, keepdims=True)
    x1 = (r - mu) * lax.rsqrt(var + EPS) * lng_ref[...] + lnb_ref[...]
    x1_ref[0] = x1
    h2 = x1 * (1.0 + sc2_ref[...]) + sh2_ref[...]
    for j in range(D_MODEL // LANES):
        h2_ref[:, j, :] = h2[:, j * LANES:(j + 1) * LANES]
    hprev_ref[...] = h2


def _merge(xa, mod, o_na, o_df, o_gq, o_ml, wg, wb, wo, ln_g, ln_b, rw_t, rb, ctx_len, t_off):
    B, T, D = xa.shape
    tm = ROW_TILE
    nt = T // tm - t_off
    ctx_tiles = ctx_len // tm
    tq_rows = nt * tm
    n_tiles = B * nt

    def bt(i):
        ti = jnp.minimum(i, n_tiles - 1)
        return ti // nt, ti % nt

    def kind(t):
        return jnp.where(t + t_off >= ctx_tiles, 1, 0)

    def mod_spec(j):
        return pl.BlockSpec((None, None, None, 1, D), lambda i: (bt(i)[0], kind(bt(i)[1]), j, 0, 0))

    def const(a):
        nd = a.ndim
        return pl.BlockSpec(a.shape, lambda i: (0,) * nd)

    def tok(cols):
        return pl.BlockSpec((1, tm, cols), lambda i: (bt(i)[0], bt(i)[1] + t_off, 0))

    tri = jnp.asarray(np.triu(np.ones((tm, tm), np.float32), 1), BF16)
    rb_b = jnp.broadcast_to(rb[:, None], (N_EXPERTS, tm)).astype(F32)
    route_shape = jax.ShapeDtypeStruct((n_tiles, TOP_K, tm), I32)
    route_spec = pl.BlockSpec((1, TOP_K, tm), lambda i: (jnp.maximum(i - 1, 0), 0, 0))
    return pl.pallas_call(
        _merge_kernel,
        out_shape=[jax.ShapeDtypeStruct((B, tq_rows, D), F32),
                   jax.ShapeDtypeStruct((B * tq_rows, D // LANES, LANES), F32),
                   route_shape, jax.ShapeDtypeStruct((n_tiles, TOP_K, tm), F32), route_shape,
                   jax.ShapeDtypeStruct((N_EXPERTS, tm), I32)],
        grid=(n_tiles + 1,),
        in_specs=[tok(D), mod_spec(1), mod_spec(0), mod_spec(2), mod_spec(4), mod_spec(3),
                  tok(BRANCH_W), tok(BRANCH_W), tok(BRANCH_W), tok(BRANCH_W),
                  const(wg), const(wb), const(wo), const(ln_g), const(ln_b), const(rw_t), const(rb_b), const(tri)],
        out_specs=[pl.BlockSpec((1, tm, D), lambda i: (bt(i)[0], bt(i)[1], 0)),
                   pl.BlockSpec((tm, D // LANES, LANES), lambda i: (jnp.minimum(i, n_tiles - 1), 0, 0)),
                   route_spec, route_spec, route_spec,
                   pl.BlockSpec((N_EXPERTS, tm), lambda i: (0, 0))],
        scratch_shapes=[pltpu.VMEM((N_EXPERTS, tm), F32), pltpu.VMEM((tm, D), F32)],
        compiler_params=_cparams(("arbitrary",)),
        name="merge_router",
    )(xa, mod, mod, mod, mod, mod, o_na, o_df, o_gq, o_ml, wg, wb, wo, ln_g, ln_b, rw_t, rb_b, tri)


def _row_copy(src, src_row, dst, dst_row, sem):
    return pltpu.make_async_copy(src.at[pl.ds(src_row, 1)], dst.at[pl.ds(dst_row, 1)], sem)


def _dispatch_kernel(dest_ref, h_ref, xs_in_ref, xs_ref, sem):
    del xs_in_ref
    tm = dest_ref.shape[2]

    def issue(t, c):
        for k in range(TOP_K):
            _row_copy(h_ref, t, xs_ref, dest_ref[0, k, t], sem).start(priority=k % 2)
        return c

    lax.fori_loop(0, tm, issue, 0, unroll=2)
    for k in range(TOP_K):
        pltpu.make_async_copy(h_ref, xs_ref.at[pl.ds(0, tm)], sem).wait()


def _dispatch(dest, h2, xs0):
    n_tiles, _, tm = dest.shape
    return pl.pallas_call(
        _dispatch_kernel,
        out_shape=jax.ShapeDtypeStruct(xs0.shape, F32),
        grid=(n_tiles,),
        in_specs=[pl.BlockSpec((1, TOP_K, tm), lambda i: (i, 0, 0), memory_space=pltpu.SMEM),
                  pl.BlockSpec((tm,) + h2.shape[1:], lambda i: (i, 0, 0)),
                  pl.BlockSpec(memory_space=pl.ANY)],
        out_specs=pl.BlockSpec(memory_space=pl.ANY),
        scratch_shapes=[pltpu.SemaphoreType.DMA(())],
        input_output_aliases={2: 0},
        compiler_params=_cparams(("arbitrary",)),
        name="moe_dispatch",
    )(dest, h2, xs0)


def _tile_relayout_copies(tiled_hbm, flat_vmem, tile, sem):
    tm = flat_vmem.shape[0]
    r0 = pl.multiple_of(tile * tm, tm)
    return [(tiled_hbm.at[pl.ds(r0, tm), j, :], flat_vmem.at[:, pl.ds(j * LANES, LANES)], sem)
            for j in range(D_MODEL // LANES)]


def _expert_kernel(te_ref, nu_ref, xs_ref, w1_ref, b1_ref, w2_ref, b2_ref, y_ref, w1s_ref, w2s_ref, x_buf, y_buf,
                   sem_in, sem_out, *, n_steps):
    i = pl.program_id(0)
    n_used = nu_ref[0]
    slot = lax.rem(i, 2)

    def fetch(tile, s, wait):
        for hbm, vmem, sem in _tile_relayout_copies(xs_ref, x_buf.at[s], tile, sem_in.at[s]):
            cp = pltpu.make_async_copy(hbm, vmem, sem)
            cp.wait() if wait else cp.start()

    def write_back(tile, s, wait):
        for hbm, vmem, sem in _tile_relayout_copies(y_ref, y_buf.at[s], tile, sem_out.at[s]):
            cp = pltpu.make_async_copy(vmem, hbm, sem)
            cp.wait() if wait else cp.start(priority=1)

    @pl.when(i == 0)
    def _():
        fetch(0, 0, False)

    @pl.when(i + 1 < n_used)
    def _():
        fetch(i + 1, 1 - slot, False)

    @pl.when((i == 0) | (te_ref[i] != te_ref[jnp.maximum(i - 1, 0)]))
    def _():
        w1s_ref[...] = w1_ref[...].astype(BF16)
        w2s_ref[...] = w2_ref[...].astype(BF16)

    @pl.when(i >= 2)
    def _():
        write_back(i - 2, slot, True)

    @pl.when(i < n_used)
    def _():
        fetch(i, slot, True)
        gu = jnp.dot(x_buf[slot].astype(BF16), w1s_ref[...], preferred_element_type=F32) + b1_ref[...]
        gate = jnp.minimum(gu[:, :D_FF], SWIGLU_LIMIT)
        up = jnp.clip(gu[:, D_FF:], -SWIGLU_LIMIT, SWIGLU_LIMIT)
        act = (up + 1.0) * gate * jax.nn.sigmoid(SWIGLU_ALPHA * gate)
        y_buf[slot] = jnp.dot(act.astype(BF16), w2s_ref[...], preferred_element_type=F32) + b2_ref[...]

    @pl.when(i >= n_used)
    def _():
        y_buf[slot] = jnp.zeros(y_buf.shape[1:], F32)

    write_back(i, slot, False)

    @pl.when(i == n_steps - 1)
    def _():
        write_back(i, slot, True)

        @pl.when(i >= 1)
        def _():
            write_back(i - 1, 1 - slot, True)


def _experts(tile_expert, n_used, xs, w1, b1, w2, b2, layer):
    n_slots = xs.shape[0]
    tm = ROW_TILE
    nt = n_slots // tm
    E = w1.shape[1]
    grid_spec = pltpu.PrefetchScalarGridSpec(
        num_scalar_prefetch=2, grid=(nt,),
        in_specs=[pl.BlockSpec(memory_space=pl.ANY),
                  pl.BlockSpec((None, None, D_MODEL, 2 * D_FF), lambda i, te, nu: (layer, te[i], 0, 0)),
                  pl.BlockSpec((None, None, 1, 2 * D_FF), lambda i, te, nu: (layer, te[i], 0, 0)),
                  pl.BlockSpec((None, None, D_FF, D_MODEL), lambda i, te, nu: (layer, te[i], 0, 0)),
                  pl.BlockSpec((None, None, 1, D_MODEL), lambda i, te, nu: (layer, te[i], 0, 0))],
        out_specs=pl.BlockSpec(memory_space=pl.ANY),
        scratch_shapes=[pltpu.VMEM((D_MODEL, 2 * D_FF), BF16), pltpu.VMEM((D_FF, D_MODEL), BF16),
                        pltpu.VMEM((2, tm, D_MODEL), F32), pltpu.VMEM((2, tm, D_MODEL), F32),
                        pltpu.SemaphoreType.DMA((2,)), pltpu.SemaphoreType.DMA((2,))])
    return pl.pallas_call(
        functools.partial(_expert_kernel, n_steps=nt),
        out_shape=jax.ShapeDtypeStruct(xs.shape, F32),
        grid_spec=grid_spec,
        compiler_params=_cparams(("arbitrary",)),
        name="moe_experts",
    )(tile_expert, n_used, xs, w1, b1.reshape(b1.shape[0], E, 1, -1), w2, b2.reshape(b2.shape[0], E, 1, -1))


def _combine_kernel(dest_ref, destn_ref, w_ref, x1_ref, g2_ref, lng_ref, lnb_ref, y_ref, o_ref, buf_ref, y3_ref,
                    sem, *, n_steps):
    tm = dest_ref.shape[2]
    i = pl.program_id(0)
    n = n_steps
    slot = lax.rem(i, 2)

    def issue(dr, s, t):
        for k in range(TOP_K):
            pltpu.make_async_copy(y_ref.at[pl.ds(dr[0, k, t], 1)], buf_ref.at[s, k, pl.ds(t, 1)],
                                  sem.at[s]).start(priority=k % 2)

    def weigh(t):
        acc = buf_ref[slot, 0, t] * w_ref[0, 0, t]
        for k in range(1, TOP_K):
            acc = acc + buf_ref[slot, k, t] * w_ref[0, k, t]
        y3_ref[t] = acc

    def loop(body):
        lax.fori_loop(0, tm, lambda t, c: (body(t), c)[1], 0, unroll=2)

    @pl.when(i == 0)
    def _():
        loop(lambda t: issue(dest_ref, 0, t))

    for k in range(TOP_K):
        pltpu.make_async_copy(y_ref.at[pl.ds(0, tm)], buf_ref.at[slot, k], sem.at[slot]).wait()

    @pl.when(i + 1 < n)
    def _():
        loop(lambda t: (issue(destn_ref, 1 - slot, t), weigh(t)))

    @pl.when(i + 1 >= n)
    def _():
        loop(weigh)
    y2 = jnp.concatenate([y3_ref[:, j, :] for j in range(D_MODEL // LANES)], axis=1)
    r = DEEPNORM_ALPHA * x1_ref[0] + g2_ref[...] * y2
    mu = jnp.mean(r, axis=-1, keepdims=True)
    var = jnp.mean(jnp.square(r - mu), axis=-1, keepdims=True)
    o_ref[0] = (r - mu) * lax.rsqrt(var + EPS) * lng_ref[...] + lnb_ref[...]


def _combine(dest, wts, x1, mod, ln_g, ln_b, y, ctx_len, t_off):
    B, tq_rows, D = x1.shape
    n_tiles, _, tm = dest.shape
    nt = n_tiles // B
    ctx_tiles = ctx_len // tm

    def kind(i):
        return jnp.where(i % nt + t_off >= ctx_tiles, 1, 0)

    smem = lambda f: pl.BlockSpec((1, TOP_K, tm), f, memory_space=pltpu.SMEM)
    cur = lambda i: (i, 0, 0)
    nxt = lambda i: (jnp.minimum(i + 1, n_tiles - 1), 0, 0)
    return pl.pallas_call(
        functools.partial(_combine_kernel, n_steps=n_tiles),
        out_shape=jax.ShapeDtypeStruct((B, tq_rows, D), F32),
        grid=(n_tiles,),
        in_specs=[smem(cur), smem(nxt), smem(cur),
                  pl.BlockSpec((1, tm, D), lambda i: (i // nt, i % nt, 0)),
                  pl.BlockSpec((None, None, None, 1, D), lambda i: (i // nt, kind(i), 5, 0, 0)),
                  pl.BlockSpec(ln_g.shape, lambda i: (0, 0)),
                  pl.BlockSpec(ln_b.shape, lambda i: (0, 0)),
                  pl.BlockSpec(memory_space=pl.ANY)],
        out_specs=pl.BlockSpec((1, tm, D), lambda i: (i // nt, i % nt, 0)),
        scratch_shapes=[pltpu.VMEM((2, TOP_K, tm, D // LANES, LANES), F32),
                        pltpu.VMEM((tm, D // LANES, LANES), F32), pltpu.SemaphoreType.DMA((2,))],
        compiler_params=_cparams(("arbitrary",)),
        name="moe_combine",
    )(dest, dest, wts, x1, mod, ln_g, ln_b, y)


_DF_HEADS_SPEC = tuple((((2 * i, i, False), (2 * i + 1, i, True)), i) for i in range(DF_HEADS))
_GQ_HEADS_SPEC = tuple((((i, i // (GQ_HEADS // GQ_KV_HEADS), False),), i // (GQ_HEADS // GQ_KV_HEADS))
                       for i in range(GQ_HEADS))
_ML_HEADS_SPEC = tuple((((i, i, False),), i) for i in range(ML_HEADS))


def _layer_weights(l, w_in, gq_qnorm, gq_knorm, ml_qa_norm, ml_wq_b, ml_kva_norm, ml_wkv_b):
    w = w_in[l]

    def cols(o, n):
        return w[:, o:o + n]

    parts = [cols(O_NAQ, 256), cols(O_NAK, 256), cols(O_NAV, 256),
             cols(O_DFQ, 256), _swap_cols(cols(O_DFQ, 256), 32), cols(O_DFK, 256), _swap_cols(cols(O_DFK, 256), 32),
             cols(O_DFV, 256),
             cols(O_GQQ, 256), _swap_cols(cols(O_GQQ, 256), 64), cols(O_GQK, 128), _swap_cols(cols(O_GQK, 128), 64),
             cols(O_GQV, 128),
             cols(O_MLQA, 256), cols(O_MLKVA, 128), cols(O_MLKR, 32), _swap_cols(cols(O_MLKR, 32), 32)]
    w_t = jnp.concatenate(parts, axis=1).T.astype(BF16)
    assert w_t.shape[0] == PROJ_ROWS

    tm = ROW_TILE
    gq_g = jnp.stack([gq_qnorm[l], _swap_cols(gq_qnorm[l], 64, signed=False),
                      gq_knorm[l], _swap_cols(gq_knorm[l], 64, signed=False)])
    gq_g = jnp.broadcast_to(gq_g[:, :, None], (4, GQ_DIM, tm)).astype(F32)
    qa_g = jnp.broadcast_to(ml_qa_norm[l][:, None], (ML_Q_RANK, tm)).astype(F32)
    kva_g = jnp.broadcast_to(ml_kva_norm[l][:, None], (ML_KV_RANK, tm)).astype(F32)
    wq = ml_wq_b[l]
    qd = ML_NOPE + ML_ROPE
    wq_parts = []
    for i in range(ML_HEADS):
        rope_cols = wq[:, i * qd + ML_NOPE:(i + 1) * qd]
        wq_parts += [wq[:, i * qd:i * qd + ML_NOPE], rope_cols, _swap_cols(rope_cols, 32)]
    wq_t = jnp.concatenate(wq_parts, axis=1).T.astype(BF16)
    wkv_t = ml_wkv_b[l].T.astype(BF16)
    return w_t, gq_g, qa_g, kva_g, wq_t, wkv_t


def kernel(x, c, ctx, c_ctx, w_ada, b_ada, w_in, na_rpb, df_lam, df_subln, gq_qnorm, gq_knorm, ml_qa_norm, ml_wq_b,
           ml_kva_norm, ml_wkv_b, w_branch, w_out, ln1_g, ln1_b, ln2_g, ln2_b, router_w, router_b, exp_w1, exp_b1,
           exp_w2, exp_b2):
    B, S, D = x.shape
    C = ctx.shape[1]
    T = C + S
    tm = ROW_TILE
    assert D == D_MODEL and C % tm == 0 and S % tm == 0 and C == tm

    xa = jnp.concatenate([ctx, x], axis=1)
    cc = jnp.concatenate([c, c_ctx[None], jnp.zeros((16 - B - 1, D), F32)], axis=0)
    mod_all = _ada(cc, w_ada, b_ada)
    cs32 = _rope_tables(32, C, S)
    cs64 = _rope_tables(64, C, S)

    xs = None
    for l in range(DEPTH):
        last = l == DEPTH - 1
        m = mod_all[l]
        mod = jnp.stack([jnp.broadcast_to(m[B][None], (B, 6 * D)), m[:B]], axis=1).reshape(B, 2, 6, 1, D)
        lam_init = 0.8 - 0.6 * math.exp(-0.3 * l)
        lp = df_lam[l].astype(F32)
        lam = (jnp.exp(jnp.sum(lp[0] * lp[1])) - jnp.exp(jnp.sum(lp[2] * lp[3])) + lam_init).reshape(1)
        w_t, gq_g, qa_g, kva_g, wq_t, wkv_t = _layer_weights(l, w_in, gq_qnorm, gq_knorm, ml_qa_norm, ml_wq_b,
                                                             ml_kva_norm, ml_wkv_b)
        (na_q, na_k, na_v, df_q, df_k, df_v, gq_q, gq_k, gq_v, ml_q, ml_k, ml_v) = _project(
            xa, mod, w_t, cs32, cs64, gq_g, qa_g, kva_g, wq_t, wkv_t, C)

        o_na = _na_attn(na_q, na_k, na_v, na_rpb[l], C)
        subln = jnp.broadcast_to((df_subln[l] * (1.0 - lam_init))[:, None], (DF_V, tm)).astype(F32)
        ones = jnp.ones((DF_V, tm), F32)
        o_df = _dense_attn(lam, df_q, df_k, df_v, subln, _DF_HEADS_SPEC, C, True, "df_attn")
        o_gq = _dense_attn(lam, gq_q, gq_k, gq_v, ones, _GQ_HEADS_SPEC, C, False, "gq_attn")
        o_ml = _dense_attn(lam, ml_q, ml_k, ml_v, ones, _ML_HEADS_SPEC, C, False, "ml_attn")

        t_off = C // tm if last else 0
        wg = w_in[l][:, O_GATES:].astype(BF16)
        wb = w_branch[l].reshape(N_BRANCH * BRANCH_W, D).astype(BF16)
        wo = w_out[l].astype(BF16)
        x1, h2, idx, wts, rank, cnt = _merge(xa, mod, o_na, o_df, o_gq, o_ml, wg, wb, wo,
                                             ln1_g[l][None], ln1_b[l][None], router_w[l].T, router_b[l], C, t_off)

        counts = cnt[:, 0]
        padded = ((counts + tm - 1) // tm) * tm
        ends = jnp.cumsum(padded)
        starts = (ends - padded).astype(I32)
        n_exp_tiles = (B * T * TOP_K) // tm + N_EXPERTS
        n_used = (ends[-1] // tm).astype(I32).reshape(1)
        tile_first = jnp.arange(n_exp_tiles, dtype=I32) * tm
        tile_expert = jnp.minimum(jnp.sum((ends[None, :] <= tile_first[:, None]).astype(I32), axis=1), N_EXPERTS - 1)
        tile_expert = jnp.where(tile_first < ends[-1], tile_expert, tile_expert[jnp.maximum(n_used[0] - 1, 0)])

        dest = jnp.sum(jnp.where(idx[..., None] == jnp.arange(N_EXPERTS, dtype=I32), starts, 0), axis=-1) + rank
        if xs is None:
            xs = jnp.zeros((n_exp_tiles * tm,) + h2.shape[1:], F32)
        xs = _dispatch(dest, h2, xs)
        ys = _experts(tile_expert, n_used, xs, exp_w1, exp_b1, exp_w2, exp_b2, l)
        xa = _combine(dest, wts, x1, mod, ln2_g[l][None], ln2_b[l][None], ys, C, t_off)
    return xa
```

```python
import functools
import math

import numpy as np
import jax
import jax.numpy as jnp
from jax import lax
from jax.experimental import pallas as pl
from jax.experimental.pallas import tpu as pltpu

F32, BF16, I32 = jnp.float32, jnp.bfloat16, jnp.int32

D_MODEL = 1024
DEPTH = 2
GRID_W = 64
ROPE_BASE = 10000.0
EPS = 1e-6
NA_HEADS, NA_DIM, NA_KH, NA_KW = 4, 64, 8, 16
DF_HEADS, DF_QK = 4, 32
DF_V = 2 * DF_QK
GQ_HEADS, GQ_KV_HEADS, GQ_DIM = 4, 2, 64
ML_HEADS, ML_NOPE, ML_ROPE, ML_V, ML_Q_RANK, ML_KV_RANK = 4, 64, 32, 64, 256, 128
N_BRANCH, BRANCH_W = 4, 256
N_EXPERTS, TOP_K = 32, 4
D_FF = D_MODEL
SWIGLU_LIMIT, SWIGLU_ALPHA = 7.0, 1.702
DEEPNORM_ALPHA = (2 * DEPTH) ** 0.25

IN_SIZES = (256, 256, 256, 256, 256, 256, 256, 128, 128, ML_Q_RANK, ML_KV_RANK, ML_ROPE, N_BRANCH * D_MODEL)
IN_OFFSETS = tuple(int(o) for o in np.cumsum((0,) + IN_SIZES)[:-1])
(O_NAQ, O_NAK, O_NAV, O_DFQ, O_DFK, O_DFV, O_GQQ, O_GQK, O_GQV, O_MLQA, O_MLKVA, O_MLKR, O_GATES) = IN_OFFSETS

LANES = 128
ROW_TILE = 256
MASK_VALUE = -1e30
VMEM_LIMIT = 56 * 1024 * 1024
LOG2E = math.log2(math.e)
KEY_CHUNK = 256
ACC_ROWS = 16
SCORE_SPLIT = 2

_PROJ_GROUPS = (("na_q", 256), ("na_k", 256), ("na_v", 256),
                ("df_q", 256), ("df_k", 256), ("df_v", 256),
                ("gq_q", 256), ("gq_k", 128), ("gq_v", 128),
                ("ml_qa", 256), ("ml_kva", 128), ("ml_kr", 32))
_PROJ_OFF = {}
_o = 0
for _n, _r in _PROJ_GROUPS:
    _PROJ_OFF[_n] = (_o, _r)
    _o += _r
PROJ_ROWS = _o


def _cparams(sem):
    return pltpu.CompilerParams(dimension_semantics=sem, vmem_limit_bytes=VMEM_LIMIT)


def _swap_cols(w, n, signed=True):
    lead, width = w.shape[:-1], w.shape[-1]
    w5 = w.reshape(lead + (width // n, 2, 2, n // 4))
    lo, hi = w5[..., 0, :], w5[..., 1, :]
    out = jnp.stack([-hi if signed else hi, lo], axis=-2)
    return out.reshape(lead + (width,))


def _rope_tables(n, ctx_len, seq):
    h = n // 2
    t = np.arange(seq)
    rows, cols = (t // GRID_W).astype(np.float32), (t % GRID_W).astype(np.float32)
    inv_freq = (np.float32(ROPE_BASE) ** (-np.arange(0, h, 2, dtype=np.float32) / np.float32(h))).astype(np.float32)
    ang_r = rows[:, None] * inv_freq[None, :]
    ang_c = cols[:, None] * inv_freq[None, :]
    ang = np.concatenate([ang_r, ang_r, ang_c, ang_c], axis=1)
    cos = np.concatenate([np.ones((ctx_len, n), np.float32), np.cos(ang)], axis=0).T
    sin = np.concatenate([np.zeros((ctx_len, n), np.float32), np.sin(ang)], axis=0).T
    return jnp.asarray(np.stack([cos, sin]).astype(np.float32))


def _ada_kernel(c_ref, w_ref, b_ref, o_ref):
    cc = c_ref[...]
    h = (cc * jax.nn.sigmoid(cc)).astype(BF16)
    o_ref[0] = jnp.dot(h, w_ref[0].astype(BF16), preferred_element_type=F32) + b_ref[0]


def _ada(cc, w_ada, b_ada):
    L, D, N = w_ada.shape
    R = cc.shape[0]
    tn = 512
    return pl.pallas_call(
        _ada_kernel,
        out_shape=jax.ShapeDtypeStruct((L, R, N), F32),
        grid=(L, N // tn),
        in_specs=[pl.BlockSpec((R, D), lambda l, j: (0, 0)),
                  pl.BlockSpec((1, D, tn), lambda l, j: (l, 0, j)),
                  pl.BlockSpec((1, 1, tn), lambda l, j: (l, 0, j))],
        out_specs=pl.BlockSpec((1, R, tn), lambda l, j: (l, 0, j)),
        compiler_params=_cparams(("arbitrary", "arbitrary")),
        name="ada_mod",
    )(cc, w_ada, b_ada.reshape(L, 1, N))


def _proj_kernel(x_ref, sc_ref, sh_ref, w_ref, cs32_ref, cs64_ref, gqg_ref, qag_ref, kvag_ref, wq_ref, wkv_ref,
                 na_q, na_k, na_v, df_q, df_k, df_v, gq_q, gq_k, gq_v, ml_q, ml_k, ml_v):
    tm = x_ref.shape[1]
    h = (x_ref[0] * (1.0 + sc_ref[...]) + sh_ref[...]).astype(BF16)
    p = lax.dot_general(w_ref[...], h, (((1,), (1,)), ((), ())), preferred_element_type=F32)

    def grp(name):
        o, r = _PROJ_OFF[name]
        return p[o:o + r]

    def zeros(n):
        return jnp.zeros((n, tm), F32)

    def swap(a, n):
        r, q = a.shape[0], n // 4
        a4 = a.reshape(r // (2 * q), 2, q, tm)
        return jnp.concatenate([-a4[:, 1:2], a4[:, 0:1]], axis=1).reshape(r, tm)

    def rope(a, cs_ref):
        n = cs_ref.shape[1]
        r = a.shape[0]
        a3, s3 = a.reshape(r // n, n, tm), swap(a, n).reshape(r // n, n, tm)
        return (a3 * cs_ref[0][None] + s3 * cs_ref[1][None]).reshape(r, tm)

    def head_slots(a, width):
        pieces = []
        for i in range(a.shape[0] // width):
            pieces += [a[i * width:(i + 1) * width], zeros(LANES - width)]
        return jnp.concatenate(pieces, axis=0)

    na_q[0] = head_slots(grp("na_q") * (NA_DIM ** -0.5), NA_DIM).astype(BF16)
    na_k[0] = head_slots(grp("na_k"), NA_DIM).T.astype(BF16)
    na_v[0] = grp("na_v").astype(BF16)

    q = rope(grp("df_q"), cs32_ref) * (DF_QK ** -0.5 * LOG2E)
    pieces = []
    for i in range(DF_HEADS):
        q1, q2 = q[i * 64:i * 64 + 32], q[i * 64 + 32:i * 64 + 64]
        pieces += [q1, zeros(LANES - 32), zeros(32), q2, zeros(LANES - 64)]
    df_q[0] = jnp.concatenate(pieces, axis=0).astype(BF16)
    df_k[0] = head_slots(rope(grp("df_k"), cs32_ref), 2 * DF_QK).T.astype(BF16)
    df_v[0] = grp("df_v").astype(BF16)

    cos64, sin64 = cs64_ref[0], cs64_ref[1]

    def norm_rope(a, g, g_sw, scale):
        hh = a.shape[0] // GQ_DIM
        a3, s3 = a.reshape(hh, GQ_DIM, tm), swap(a, GQ_DIM).reshape(hh, GQ_DIM, tm)
        r = lax.rsqrt(jnp.mean(a3 * a3, axis=1, keepdims=True) + EPS) * scale
        return ((a3 * (g * cos64)[None] + s3 * (g_sw * sin64)[None]) * r).reshape(hh * GQ_DIM, tm)

    gq_q[0] = head_slots(norm_rope(grp("gq_q"), gqg_ref[0], gqg_ref[1], GQ_DIM ** -0.5 * LOG2E),
                         GQ_DIM).astype(BF16)
    gq_k[0] = head_slots(norm_rope(grp("gq_k"), gqg_ref[2], gqg_ref[3], 1.0), GQ_DIM).T.astype(BF16)
    gq_v[0] = grp("gq_v").astype(BF16)

    qa = grp("ml_qa")
    qan = (qa * lax.rsqrt(jnp.mean(qa * qa, axis=0, keepdims=True) + EPS) * qag_ref[...]).astype(BF16)
    qq = jnp.dot(wq_ref[...], qan, preferred_element_type=F32)
    kva = grp("ml_kva")
    kvan = (kva * lax.rsqrt(jnp.mean(kva * kva, axis=0, keepdims=True) + EPS) * kvag_ref[...]).astype(BF16)
    kv = jnp.dot(wkv_ref[...], kvan, preferred_element_type=F32)
    k_rope = rope(grp("ml_kr"), cs32_ref)
    ml_scale = (ML_NOPE + ML_ROPE) ** -0.5 * LOG2E
    qd = ML_NOPE + ML_ROPE
    qp, kp, vp = [], [], []
    for i in range(ML_HEADS):
        b = i * LANES
        q_rope = rope(qq[i * qd + ML_NOPE:(i + 1) * qd], cs32_ref)
        qp += [qq[i * qd:i * qd + ML_NOPE] * ml_scale, q_rope * ml_scale, zeros(32)]
        kp += [kv[b:b + 64], k_rope, zeros(32)]
        vp += [kv[b + 64:b + 128]]
    ml_q[0] = jnp.concatenate(qp, axis=0).astype(BF16)
    ml_k[0] = jnp.concatenate(kp, axis=0).T.astype(BF16)
    ml_v[0] = jnp.concatenate(vp, axis=0).astype(BF16)


def _project(xa, mod, w_t, cs32, cs64, gq_g, qa_g, kva_g, wq_t, wkv_t, ctx_len):
    B, T, D = xa.shape
    tm = ROW_TILE
    nt = T // tm
    ctx_tiles = ctx_len // tm

    def kind(t):
        return jnp.where(t >= ctx_tiles, 1, 0)

    def mod_spec(j):
        return pl.BlockSpec((None, None, None, 1, D), lambda b, t: (b, kind(t), j, 0, 0))

    def const(a):
        nd = a.ndim
        return pl.BlockSpec(a.shape, lambda b, t: (0,) * nd)

    def fm(rows):
        return jax.ShapeDtypeStruct((B, rows, T), BF16), pl.BlockSpec((1, rows, tm), lambda b, t: (b, 0, t))

    def tmj(cols):
        return jax.ShapeDtypeStruct((B, T, cols), BF16), pl.BlockSpec((1, tm, cols), lambda b, t: (b, t, 0))

    outs = [fm(512), tmj(512), fm(256),
            fm(1024), tmj(512), fm(256),
            fm(512), tmj(256), fm(128),
            fm(512), tmj(512), fm(256)]
    return pl.pallas_call(
        _proj_kernel,
        out_shape=[o[0] for o in outs],
        grid=(B, nt),
        in_specs=[pl.BlockSpec((1, tm, D), lambda b, t: (b, t, 0)),
                  mod_spec(1), mod_spec(0),
                  const(w_t),
                  pl.BlockSpec((2, 32, tm), lambda b, t: (0, 0, t)),
                  pl.BlockSpec((2, 64, tm), lambda b, t: (0, 0, t)),
                  const(gq_g), const(qa_g), const(kva_g), const(wq_t), const(wkv_t)],
        out_specs=[o[1] for o in outs],
        compiler_params=_cparams(("arbitrary", "arbitrary")),
        name="in_proj",
    )(xa, mod, mod, w_t, cs32, cs64, gq_g, qa_g, kva_g, wq_t, wkv_t)


def _softmax_pv(k, q_t, v_t, exp_fn):
    s = jnp.dot(k, q_t, preferred_element_type=F32)
    m = jnp.max(s, axis=0, keepdims=True)
    p = exp_fn(s - m)
    l = jnp.sum(p, axis=0, keepdims=True)
    o = jnp.dot(v_t, p.astype(BF16), preferred_element_type=F32)
    return o, l


def _dense_attn_kernel(lam_ref, q_ref, k_ref, v_ref, post_ref, o_ref, ot_ref, s_ref, p_ref, *, heads, ctx_len,
                       diff_norm):
    total, tq = k_ref.shape[1], q_ref.shape[2]
    items = [(qs, ks, v_idx, i, signed) for i, (terms, v_idx) in enumerate(heads) for qs, ks, signed in terms]
    last_of_head = {i: max(j for j, it in enumerate(items) if it[3] == i) for i in range(len(heads))}

    def finish_head(i, acc):
        if diff_norm:
            acc = acc * lax.rsqrt(jnp.mean(acc * acc, axis=0, keepdims=True) + EPS) * post_ref[...]
        ot_ref[i * 64:(i + 1) * 64, :] = acc

    def run_context():
        head_acc = {}
        for j, (qs, ks, v_idx, i, signed) in enumerate(items):
            o, l = _softmax_pv(k_ref[0, 0:ctx_len, ks * LANES:(ks + 1) * LANES], q_ref[0, qs * LANES:(qs + 1) * LANES, :],
                               v_ref[0, v_idx * 64:(v_idx + 1) * 64, 0:ctx_len], jnp.exp2)
            o = o * (((-lam_ref[0]) if signed else 1.0) / l)
            head_acc[i] = o if i not in head_acc else head_acc[i] + o
            if last_of_head[i] == j:
                finish_head(i, head_acc.pop(i))

    def run_latent():
        n_chunks = total // KEY_CHUNK
        half = KEY_CHUNK // SCORE_SPLIT
        n_items = len(items)
        m_of, l_of, head_acc = {}, {}, {}
        for t in range(n_items + 2):
            sc = items[t] if t < n_items else None
            ex = items[t - 1] if 0 <= t - 1 < n_items else None
            pv = items[t - 2] if 0 <= t - 2 < n_items else None
            macc = jnp.full((ACC_ROWS, tq), -jnp.inf, F32)
            lacc = jnp.zeros((ACC_ROWS, tq), F32)
            oacc = jnp.zeros((64, tq), F32)
            for c in range(n_chunks):
                r0 = c * KEY_CHUNK
                if sc is not None:
                    q_t = q_ref[0, sc[0] * LANES:(sc[0] + 1) * LANES, :]
                    for r in ((r0,) if half == KEY_CHUNK else (r0, r0 + half)):
                        s = jnp.dot(k_ref[0, r:r + half, sc[1] * LANES:(sc[1] + 1) * LANES], q_t,
                                    preferred_element_type=F32)
                        s_ref[t % 2, r:r + half, :] = s
                        for a in range(half // ACC_ROWS):
                            macc = jnp.maximum(macc, s[a * ACC_ROWS:(a + 1) * ACC_ROWS])
                if ex is not None:
                    p = jnp.exp2(s_ref[(t - 1) % 2, r0:r0 + KEY_CHUNK, :] - m_of[t - 1])
                    for a in range(KEY_CHUNK // ACC_ROWS):
                        lacc = lacc + p[a * ACC_ROWS:(a + 1) * ACC_ROWS]
                    p_ref[(t - 1) % 2, r0:r0 + KEY_CHUNK, :] = p.astype(BF16)
                if pv is not None:
                    oacc = oacc + jnp.dot(v_ref[0, pv[2] * 64:(pv[2] + 1) * 64, r0:r0 + KEY_CHUNK],
                                          p_ref[(t - 2) % 2, r0:r0 + KEY_CHUNK, :], preferred_element_type=F32)
            if sc is not None:
                m_of[t] = jnp.max(macc, axis=0, keepdims=True)
            if ex is not None:
                l_of[t - 1] = jnp.sum(lacc, axis=0, keepdims=True)
            if pv is not None:
                i, signed = pv[3], pv[4]
                o = oacc * (((-lam_ref[0]) if signed else 1.0) / l_of.pop(t - 2))
                head_acc[i] = o if i not in head_acc else head_acc[i] + o
                if last_of_head[i] == t - 2:
                    finish_head(i, head_acc.pop(i))

    qb = pl.program_id(1)

    @pl.when(qb * tq < ctx_len)
    def _():
        run_context()
        o_ref[0] = ot_ref[...].T.astype(BF16)

    @pl.when(qb * tq >= ctx_len)
    def _():
        run_latent()
        o_ref[0] = ot_ref[...].T.astype(BF16)


def _dense_attn(lam, q_t, k, v_t, post, heads, ctx_len, diff_norm, name):
    B, qrows, T = q_t.shape
    tq = ROW_TILE
    kern = functools.partial(_dense_attn_kernel, heads=heads, ctx_len=ctx_len, diff_norm=diff_norm)
    return pl.pallas_call(
        kern,
        out_shape=jax.ShapeDtypeStruct((B, T, BRANCH_W), BF16),
        grid=(B, T // tq),
        in_specs=[pl.BlockSpec(memory_space=pltpu.SMEM),
                  pl.BlockSpec((1, qrows, tq), lambda b, t: (b, 0, t)),
                  pl.BlockSpec((1, T, k.shape[2]), lambda b, t: (b, 0, 0)),
                  pl.BlockSpec((1, v_t.shape[1], T), lambda b, t: (b, 0, 0)),
                  pl.BlockSpec(post.shape, lambda b, t: (0, 0))],
        out_specs=pl.BlockSpec((1, tq, BRANCH_W), lambda b, t: (b, t, 0)),
        scratch_shapes=[pltpu.VMEM((BRANCH_W, tq), F32), pltpu.VMEM((2, T, tq), F32), pltpu.VMEM((2, T, tq), BF16)],
        compiler_params=_cparams(("arbitrary", "arbitrary")),
        name=name,
    )(lam, q_t, k, v_t, post)


def _na_plan(rows):
    kh, kw = min(NA_KH, rows), NA_KW
    q_rows = ROW_TILE // GRID_W
    win_rows = min(kh + q_rows, rows)
    nblk = rows // q_rows
    row_start = np.clip(np.arange(rows) - kh // 2, 0, rows - kh)
    col_start = np.clip(np.arange(GRID_W) - kw // 2, 0, GRID_W - kw)
    u0 = np.clip(np.arange(nblk) * q_rows - kh // 2, 0, rows - win_rows)
    wk, tq = win_rows * GRID_W, ROW_TILE
    kk, qq = np.arange(wk), np.arange(tq)
    kc, qc = kk % GRID_W, qq % GRID_W
    col_sel = (np.arange(2 * NA_KW - 1)[:, None, None]
               == (np.arange(GRID_W)[None, :, None] - np.arange(GRID_W)[None, None, :] + (NA_KW - 1))).astype(np.float32)
    types, type_of, row_sel_l, valid_l = {}, [], [], []
    for j in range(nblk):
        kr = u0[j] + kk // GRID_W
        qr = j * q_rows + qq // GRID_W
        rs = row_start[qr]
        valid = ((kr[:, None] >= rs[None, :]) & (kr[:, None] < rs[None, :] + kh)
                 & (kc[:, None] >= col_start[qc][None, :]) & (kc[:, None] < col_start[qc][None, :] + kw))
        assert valid.sum(axis=0).min() == kh * kw and valid.sum(axis=0).max() == kh * kw
        rel = (u0[j] + np.arange(win_rows))[:, None] - (j * q_rows + np.arange(q_rows))[None, :] + (NA_KH - 1)
        row_sel = (rel[:, :, None] == np.arange(2 * NA_KH - 1)[None, None, :]).astype(np.float32)
        key = (valid.tobytes(), row_sel.tobytes())
        if key not in types:
            types[key] = len(row_sel_l)
            row_sel_l.append(row_sel), valid_l.append(valid)
        type_of.append(types[key])
    assert all(int(u) * GRID_W % LANES == 0 for u in u0)
    return (np.asarray(u0, np.int32), np.asarray(type_of, np.int32), np.stack(row_sel_l), col_sel,
            np.stack(valid_l), wk)


def _na_kernel(u0_ref, ty_ref, q_ref, k_ref, v_ref, bias_ref, o_ref, ot_ref, *, ctx_len):
    del ty_ref
    wk = bias_ref.shape[2]
    j = pl.program_id(1)

    def finish():
        o_ref[0] = ot_ref[...].T.astype(BF16)

    @pl.when(j == 0)
    def _():
        for i in range(NA_HEADS):
            o, l = _softmax_pv(k_ref[0, 0:ctx_len, i * LANES:(i + 1) * LANES], q_ref[0, i * LANES:(i + 1) * LANES, :],
                               v_ref[0, i * 64:(i + 1) * 64, 0:ctx_len], jnp.exp)
            ot_ref[i * 64:(i + 1) * 64, :] = o * (1.0 / l)
        finish()

    @pl.when(j > 0)
    def _():
        ws = pl.multiple_of(ctx_len + u0_ref[j - 1] * GRID_W, LANES)

        def scores(i):
            q_t = q_ref[0, i * LANES:(i + 1) * LANES, :]
            s_c = jnp.dot(k_ref[0, 0:ctx_len, i * LANES:(i + 1) * LANES], q_t, preferred_element_type=F32)
            s_w = jnp.dot(k_ref[0, pl.ds(ws, wk), i * LANES:(i + 1) * LANES], q_t,
                          preferred_element_type=F32) + bias_ref[0, i]
            return s_c, s_w

        def fold(fn, acc, a):
            for r in range(0, a.shape[0], ACC_ROWS):
                acc = fn(acc, a[r:r + ACC_ROWS])
            return acc

        def softmax_pv(i, s_c, s_w):
            tq = s_c.shape[1]
            macc = fold(jnp.maximum, fold(jnp.maximum, jnp.full((ACC_ROWS, tq), -jnp.inf, F32), s_c), s_w)
            m = jnp.max(macc, axis=0, keepdims=True)
            p_c, p_w = jnp.exp(s_c - m), jnp.exp(s_w - m)
            lacc = fold(jnp.add, fold(jnp.add, jnp.zeros((ACC_ROWS, tq), F32), p_c), p_w)
            l = jnp.sum(lacc, axis=0, keepdims=True)
            o = (jnp.dot(v_ref[0, i * 64:(i + 1) * 64, 0:ctx_len], p_c.astype(BF16), preferred_element_type=F32)
                 + jnp.dot(v_ref[0, i * 64:(i + 1) * 64, pl.ds(ws, wk)], p_w.astype(BF16),
                           preferred_element_type=F32))
            ot_ref[i * 64:(i + 1) * 64, :] = o * (1.0 / l)

        pending = scores(0)
        for i in range(NA_HEADS):
            upcoming = scores(i + 1) if i + 1 < NA_HEADS else None
            softmax_pv(i, *pending)
            pending = upcoming
        finish()


def _na_attn(q_t, k, v_t, rpb, ctx_len):
    B, _, T = q_t.shape
    tq = ROW_TILE
    rows = (T - ctx_len) // GRID_W
    u0, type_of, row_sel, col_sel, valid, wk = _na_plan(rows)
    hp = lax.Precision.HIGHEST
    by_col = jnp.einsum("hab,bcq->hacq", rpb.astype(F32), col_sel, precision=hp)
    bias = jnp.einsum("tkra,hacq->thkcrq", row_sel, by_col, precision=hp)
    bias = jnp.where(valid[:, None], bias.reshape(row_sel.shape[0], NA_HEADS, wk, tq), MASK_VALUE)
    nb = T // tq
    assert ctx_len == tq

    grid_spec = pltpu.PrefetchScalarGridSpec(
        num_scalar_prefetch=2,
        grid=(B, nb),
        in_specs=[pl.BlockSpec((1, NA_HEADS * LANES, tq), lambda b, j, u, ty: (b, 0, j)),
                  pl.BlockSpec((1, T, NA_HEADS * LANES), lambda b, j, u, ty: (b, 0, 0)),
                  pl.BlockSpec((1, BRANCH_W, T), lambda b, j, u, ty: (b, 0, 0)),
                  pl.BlockSpec((1, NA_HEADS, wk, tq), lambda b, j, u, ty: (ty[jnp.maximum(j - 1, 0)], 0, 0, 0))],
        out_specs=pl.BlockSpec((1, tq, BRANCH_W), lambda b, j, u, ty: (b, j, 0)),
        scratch_shapes=[pltpu.VMEM((BRANCH_W, tq), F32)])
    return pl.pallas_call(
        functools.partial(_na_kernel, ctx_len=ctx_len),
        out_shape=jax.ShapeDtypeStruct((B, T, BRANCH_W), BF16),
        grid_spec=grid_spec,
        compiler_params=_cparams(("arbitrary", "arbitrary")),
        name="na_attn",
    )(jnp.asarray(u0), jnp.asarray(type_of), q_t, k, v_t, bias)


def _merge_kernel(x_ref, sc1_ref, sh1_ref, g1_ref, sc2_ref, sh2_ref, ona_ref, odf_ref, ogq_ref, oml_ref,
                  wg_ref, wb_ref, wo_ref, lng_ref, lnb_ref, rw_ref, rb_ref, tri_ref,
                  x1_ref, h2_ref, idx_ref, wts_ref, rank_ref, cnt_ref, carry_ref, hprev_ref):
    tm = x_ref.shape[1]
    step = pl.program_id(0)

    @pl.when(step == 0)
    def _():
        carry_ref[...] = jnp.zeros_like(carry_ref)
        hprev_ref[...] = jnp.zeros_like(hprev_ref)

    iota = lax.broadcasted_iota(I32, (N_EXPERTS, tm), 0)
    route = {}

    def route_logits():
        route["cur"] = lax.dot_general(rw_ref[...], hprev_ref[...], (((1,), (1,)), ((), ())),
                                       preferred_element_type=F32, precision=lax.Precision.HIGHEST) + rb_ref[...]
        route["vals"], route["idxs"] = [], []

    def route_pick():
        cur = route["cur"]
        m = jnp.max(cur, axis=0, keepdims=True)
        ik = jnp.min(jnp.where(cur == m, iota, N_EXPERTS), axis=0, keepdims=True)
        route["vals"].append(m)
        route["idxs"].append(ik)
        route["cur"] = jnp.where(iota == ik, -jnp.inf, cur)

    def route_finish():
        vals, idxs = route["vals"], route["idxs"]
        exps = [jnp.exp(v - vals[0]) for v in vals]
        denom = exps[0] + exps[1] + exps[2] + exps[3]
        wts_ref[0] = jnp.concatenate([e / denom for e in exps], axis=0)
        idx_ref[0] = jnp.concatenate(idxs, axis=0)
        live = jnp.where(step >= 1, 1.0, 0.0)
        onehot = jnp.zeros((N_EXPERTS, tm), F32)
        for ik in idxs:
            onehot = onehot + jnp.where(iota == ik, live, 0.0)
        before = jnp.dot(onehot.astype(BF16), tri_ref[...], preferred_element_type=F32) + carry_ref[...]
        rank_ref[0] = jnp.concatenate(
            [jnp.sum(jnp.where(iota == ik, before, 0.0), axis=0, keepdims=True) for ik in idxs], axis=0).astype(I32)
        carry = carry_ref[...] + jnp.sum(onehot, axis=1, keepdims=True)
        carry_ref[...] = carry
        cnt_ref[...] = carry.astype(I32)

    route_pieces = [[route_logits], [route_pick, route_pick], [route_pick, route_pick], [route_finish]]

    x = x_ref[0]
    h1 = (x * (1.0 + sc1_ref[...]) + sh1_ref[...]).astype(BF16)
    y = None
    for i, o_ref in enumerate((ona_ref, odf_ref, ogq_ref, oml_ref)):
        gate = jax.nn.sigmoid(jnp.dot(h1, wg_ref[:, i * D_MODEL:(i + 1) * D_MODEL], preferred_element_type=F32))
        term = gate * jnp.dot(o_ref[0], wb_ref[i * BRANCH_W:(i + 1) * BRANCH_W, :], preferred_element_type=F32)
        y = term if y is None else y + term
        for piece in route_pieces[i]:
            piece()
    z = jnp.dot(y.astype(BF16), wo_ref[...], preferred_element_type=F32)
    r = DEEPNORM_ALPHA * x + g1_ref[...] * z
    mu = jnp.mean(r, axis=-1, keepdims=True)
    var = jnp.mean(jnp.square(r - mu), axis=-1, keepdims=True)
    x1 = (r - mu) * lax.rsqrt(var + EPS) * lng_ref[...] + lnb_ref[...]
    x1_ref[0] = x1
    h2 = x1 * (1.0 + sc2_ref[...]) + sh2_ref[...]
    for j in range(D_MODEL // LANES):
        h2_ref[:, j, :] = h2[:, j * LANES:(j + 1) * LANES]
    hprev_ref[...] = h2


def _merge(xa, mod, o_na, o_df, o_gq, o_ml, wg, wb, wo, ln_g, ln_b, rw_t, rb, ctx_len, t_off):
    B, T, D = xa.shape
    tm = ROW_TILE
    nt = T // tm - t_off
    ctx_tiles = ctx_len // tm
    tq_rows = nt * tm
    n_tiles = B * nt

    def tile(i, lag):
        return jnp.clip(i - lag, 0, n_tiles - 1)

    def bt(i, lag):
        ti = tile(i, lag)
        return ti // nt, ti % nt

    def kind(t):
        return jnp.where(t + t_off >= ctx_tiles, 1, 0)

    def mod_spec(j, lag):
        return pl.BlockSpec((None, None, None, 1, D), lambda i: (bt(i, lag)[0], kind(bt(i, lag)[1]), j, 0, 0))

    def const(a):
        nd = a.ndim
        return pl.BlockSpec(a.shape, lambda i: (0,) * nd)

    def tok(cols):
        return pl.BlockSpec((1, tm, cols), lambda i: (bt(i, 0)[0], bt(i, 0)[1] + t_off, 0))

    tri = jnp.asarray(np.triu(np.ones((tm, tm), np.float32), 1), BF16)
    rb_b = jnp.broadcast_to(rb[:, None], (N_EXPERTS, tm)).astype(F32)
    route_shape = jax.ShapeDtypeStruct((n_tiles, TOP_K, tm), I32)
    route_spec = pl.BlockSpec((1, TOP_K, tm), lambda i: (tile(i, 1), 0, 0))
    return pl.pallas_call(
        _merge_kernel,
        out_shape=[jax.ShapeDtypeStruct((B, tq_rows, D), F32),
                   jax.ShapeDtypeStruct((B * tq_rows, D // LANES, LANES), F32),
                   route_shape, jax.ShapeDtypeStruct((n_tiles, TOP_K, tm), F32), route_shape,
                   jax.ShapeDtypeStruct((N_EXPERTS, tm), I32)],
        grid=(n_tiles + 1,),
        in_specs=[tok(D), mod_spec(1, 0), mod_spec(0, 0), mod_spec(2, 0), mod_spec(4, 0), mod_spec(3, 0),
                  tok(BRANCH_W), tok(BRANCH_W), tok(BRANCH_W), tok(BRANCH_W),
                  const(wg), const(wb), const(wo), const(ln_g), const(ln_b), const(rw_t), const(rb_b), const(tri)],
        out_specs=[pl.BlockSpec((1, tm, D), lambda i: (bt(i, 0)[0], bt(i, 0)[1], 0)),
                   pl.BlockSpec((tm, D // LANES, LANES), lambda i: (tile(i, 0), 0, 0)),
                   route_spec, route_spec, route_spec,
                   pl.BlockSpec((N_EXPERTS, tm), lambda i: (0, 0))],
        scratch_shapes=[pltpu.VMEM((N_EXPERTS, tm), F32), pltpu.VMEM((tm, D), F32)],
        compiler_params=_cparams(("arbitrary",)),
        name="merge_router",
    )(xa, mod, mod, mod, mod, mod, o_na, o_df, o_gq, o_ml, wg, wb, wo, ln_g, ln_b, rw_t, rb_b, tri)


def _row_copy(src, src_row, dst, dst_row, sem):
    return pltpu.make_async_copy(src.at[pl.ds(src_row, 1)], dst.at[pl.ds(dst_row, 1)], sem)


def _dispatch_kernel(dest_ref, h_ref, xs_in_ref, xs_ref, sem):
    del xs_in_ref
    tm = dest_ref.shape[2]

    def issue(t, c):
        for k in range(TOP_K):
            _row_copy(h_ref, t, xs_ref, dest_ref[0, k, t], sem).start(priority=k % 2)
        return c

    lax.fori_loop(0, tm, issue, 0, unroll=2)
    for k in range(TOP_K):
        pltpu.make_async_copy(h_ref, xs_ref.at[pl.ds(0, tm)], sem).wait()


def _dispatch(dest, h2, xs0):
    n_tiles, _, tm = dest.shape
    return pl.pallas_call(
        _dispatch_kernel,
        out_shape=jax.ShapeDtypeStruct(xs0.shape, F32),
        grid=(n_tiles,),
        in_specs=[pl.BlockSpec((1, TOP_K, tm), lambda i: (i, 0, 0), memory_space=pltpu.SMEM),
                  pl.BlockSpec((tm,) + h2.shape[1:], lambda i: (i, 0, 0)),
                  pl.BlockSpec(memory_space=pl.ANY)],
        out_specs=pl.BlockSpec(memory_space=pl.ANY),
        scratch_shapes=[pltpu.SemaphoreType.DMA(())],
        input_output_aliases={2: 0},
        compiler_params=_cparams(("arbitrary",)),
        name="moe_dispatch",
    )(dest, h2, xs0)


def _tile_relayout_copies(tiled_hbm, flat_vmem, tile, sem):
    tm = flat_vmem.shape[0]
    r0 = pl.multiple_of(tile * tm, tm)
    return [(tiled_hbm.at[pl.ds(r0, tm), j, :], flat_vmem.at[:, pl.ds(j * LANES, LANES)], sem)
            for j in range(D_MODEL // LANES)]


def _expert_kernel(te_ref, nu_ref, xs_ref, w1_ref, b1_ref, w2_ref, b2_ref, y_ref, w1s_ref, w2s_ref, x_buf, y_buf,
                   sem_in, sem_out, *, n_steps):
    i = pl.program_id(0)
    n_used = nu_ref[0]
    slot = lax.rem(i, 2)

    def fetch(tile, s, wait):
        for hbm, vmem, sem in _tile_relayout_copies(xs_ref, x_buf.at[s], tile, sem_in.at[s]):
            cp = pltpu.make_async_copy(hbm, vmem, sem)
            cp.wait() if wait else cp.start()

    def write_back(tile, s, wait):
        for hbm, vmem, sem in _tile_relayout_copies(y_ref, y_buf.at[s], tile, sem_out.at[s]):
            cp = pltpu.make_async_copy(vmem, hbm, sem)
            cp.wait() if wait else cp.start(priority=1)

    @pl.when(i == 0)
    def _():
        fetch(0, 0, False)

    @pl.when(i + 1 < n_used)
    def _():
        fetch(i + 1, 1 - slot, False)

    @pl.when((i == 0) | (te_ref[i] != te_ref[jnp.maximum(i - 1, 0)]))
    def _():
        w1s_ref[...] = w1_ref[...].astype(BF16)
        w2s_ref[...] = w2_ref[...].astype(BF16)

    @pl.when(i >= 2)
    def _():
        write_back(i - 2, slot, True)

    @pl.when(i < n_used)
    def _():
        fetch(i, slot, True)
        gu = jnp.dot(x_buf[slot].astype(BF16), w1s_ref[...], preferred_element_type=F32) + b1_ref[...]
        gate = jnp.minimum(gu[:, :D_FF], SWIGLU_LIMIT)
        up = jnp.clip(gu[:, D_FF:], -SWIGLU_LIMIT, SWIGLU_LIMIT)
        act = (up + 1.0) * gate * jax.nn.sigmoid(SWIGLU_ALPHA * gate)
        y_buf[slot] = jnp.dot(act.astype(BF16), w2s_ref[...], preferred_element_type=F32) + b2_ref[...]

    @pl.when(i >= n_used)
    def _():
        y_buf[slot] = jnp.zeros(y_buf.shape[1:], F32)

    write_back(i, slot, False)

    @pl.when(i == n_steps - 1)
    def _():
        write_back(i, slot, True)

        @pl.when(i >= 1)
        def _():
            write_back(i - 1, 1 - slot, True)


def _experts(tile_expert, n_used, xs, w1, b1, w2, b2, layer):
    n_slots = xs.shape[0]
    tm = ROW_TILE
    nt = n_slots // tm
    E = w1.shape[1]
    grid_spec = pltpu.PrefetchScalarGridSpec(
        num_scalar_prefetch=2, grid=(nt,),
        in_specs=[pl.BlockSpec(memory_space=pl.ANY),
                  pl.BlockSpec((None, None, D_MODEL, 2 * D_FF), lambda i, te, nu: (layer, te[i], 0, 0)),
                  pl.BlockSpec((None, None, 1, 2 * D_FF), lambda i, te, nu: (layer, te[i], 0, 0)),
                  pl.BlockSpec((None, None, D_FF, D_MODEL), lambda i, te, nu: (layer, te[i], 0, 0)),
                  pl.BlockSpec((None, None, 1, D_MODEL), lambda i, te, nu: (layer, te[i], 0, 0))],
        out_specs=pl.BlockSpec(memory_space=pl.ANY),
        scratch_shapes=[pltpu.VMEM((D_MODEL, 2 * D_FF), BF16), pltpu.VMEM((D_FF, D_MODEL), BF16),
                        pltpu.VMEM((2, tm, D_MODEL), F32), pltpu.VMEM((2, tm, D_MODEL), F32),
                        pltpu.SemaphoreType.DMA((2,)), pltpu.SemaphoreType.DMA((2,))])
    return pl.pallas_call(
        functools.partial(_expert_kernel, n_steps=nt),
        out_shape=jax.ShapeDtypeStruct(xs.shape, F32),
        grid_spec=grid_spec,
        compiler_params=_cparams(("arbitrary",)),
        name="moe_experts",
    )(tile_expert, n_used, xs, w1, b1.reshape(b1.shape[0], E, 1, -1), w2, b2.reshape(b2.shape[0], E, 1, -1))


def _combine_kernel(dest_ref, destn_ref, w_ref, x1_ref, g2_ref, lng_ref, lnb_ref, y_ref, o_ref, buf_ref, y3_ref,
                    sem, *, n_steps):
    tm = dest_ref.shape[2]
    i = pl.program_id(0)
    n = n_steps
    slot = lax.rem(i, 2)

    def issue(dr, s, t):
        for k in range(TOP_K):
            pltpu.make_async_copy(y_ref.at[pl.ds(dr[0, k, t], 1)], buf_ref.at[s, k, pl.ds(t, 1)],
                                  sem.at[s]).start(priority=k % 2)

    def weigh(t):
        acc = buf_ref[slot, 0, t] * w_ref[0, 0, t]
        for k in range(1, TOP_K):
            acc = acc + buf_ref[slot, k, t] * w_ref[0, k, t]
        y3_ref[t] = acc

    def loop(body):
        lax.fori_loop(0, tm, lambda t, c: (body(t), c)[1], 0, unroll=2)

    @pl.when(i == 0)
    def _():
        loop(lambda t: issue(dest_ref, 0, t))

    for k in range(TOP_K):
        pltpu.make_async_copy(y_ref.at[pl.ds(0, tm)], buf_ref.at[slot, k], sem.at[slot]).wait()

    @pl.when(i + 1 < n)
    def _():
        loop(lambda t: (issue(destn_ref, 1 - slot, t), weigh(t)))

    @pl.when(i + 1 >= n)
    def _():
        loop(weigh)
    y2 = jnp.concatenate([y3_ref[:, j, :] for j in range(D_MODEL // LANES)], axis=1)
    r = DEEPNORM_ALPHA * x1_ref[0] + g2_ref[...] * y2
    mu = jnp.mean(r, axis=-1, keepdims=True)
    var = jnp.mean(jnp.square(r - mu), axis=-1, keepdims=True)
    o_ref[0] = (r - mu) * lax.rsqrt(var + EPS) * lng_ref[...] + lnb_ref[...]


def _combine(dest, wts, x1, mod, ln_g, ln_b, y, ctx_len, t_off):
    B, tq_rows, D = x1.shape
    n_tiles, _, tm = dest.shape
    nt = n_tiles // B
    ctx_tiles = ctx_len // tm

    def kind(i):
        return jnp.where(i % nt + t_off >= ctx_tiles, 1, 0)

    smem = lambda f: pl.BlockSpec((1, TOP_K, tm), f, memory_space=pltpu.SMEM)
    cur = lambda i: (i, 0, 0)
    nxt = lambda i: (jnp.minimum(i + 1, n_tiles - 1), 0, 0)
    return pl.pallas_call(
        functools.partial(_combine_kernel, n_steps=n_tiles),
        out_shape=jax.ShapeDtypeStruct((B, tq_rows, D), F32),
        grid=(n_tiles,),
        in_specs=[smem(cur), smem(nxt), smem(cur),
                  pl.BlockSpec((1, tm, D), lambda i: (i // nt, i % nt, 0)),
                  pl.BlockSpec((None, None, None, 1, D), lambda i: (i // nt, kind(i), 5, 0, 0)),
                  pl.BlockSpec(ln_g.shape, lambda i: (0, 0)),
                  pl.BlockSpec(ln_b.shape, lambda i: (0, 0)),
                  pl.BlockSpec(memory_space=pl.ANY)],
        out_specs=pl.BlockSpec((1, tm, D), lambda i: (i // nt, i % nt, 0)),
        scratch_shapes=[pltpu.VMEM((2, TOP_K, tm, D // LANES, LANES), F32),
                        pltpu.VMEM((tm, D // LANES, LANES), F32), pltpu.SemaphoreType.DMA((2,))],
        compiler_params=_cparams(("arbitrary",)),
        name="moe_combine",
    )(dest, dest, wts, x1, mod, ln_g, ln_b, y)


_DF_HEADS_SPEC = tuple((((2 * i, i, False), (2 * i + 1, i, True)), i) for i in range(DF_HEADS))
_GQ_HEADS_SPEC = tuple((((i, i // (GQ_HEADS // GQ_KV_HEADS), False),), i // (GQ_HEADS // GQ_KV_HEADS))
                       for i in range(GQ_HEADS))
_ML_HEADS_SPEC = tuple((((i, i, False),), i) for i in range(ML_HEADS))


def _layer_weights(l, w_in, gq_qnorm, gq_knorm, ml_qa_norm, ml_wq_b, ml_kva_norm, ml_wkv_b):
    w_t = w_in[l][:, :O_GATES].T.astype(BF16)
    assert w_t.shape[0] == PROJ_ROWS

    tm = ROW_TILE
    gq_g = jnp.stack([gq_qnorm[l], _swap_cols(gq_qnorm[l], 64, signed=False),
                      gq_knorm[l], _swap_cols(gq_knorm[l], 64, signed=False)])
    gq_g = jnp.broadcast_to(gq_g[:, :, None], (4, GQ_DIM, tm)).astype(F32)
    qa_g = jnp.broadcast_to(ml_qa_norm[l][:, None], (ML_Q_RANK, tm)).astype(F32)
    kva_g = jnp.broadcast_to(ml_kva_norm[l][:, None], (ML_KV_RANK, tm)).astype(F32)
    wq_t = ml_wq_b[l].T.astype(BF16)
    wkv_t = ml_wkv_b[l].T.astype(BF16)
    return w_t, gq_g, qa_g, kva_g, wq_t, wkv_t


def kernel(x, c, ctx, c_ctx, w_ada, b_ada, w_in, na_rpb, df_lam, df_subln, gq_qnorm, gq_knorm, ml_qa_norm, ml_wq_b,
           ml_kva_norm, ml_wkv_b, w_branch, w_out, ln1_g, ln1_b, ln2_g, ln2_b, router_w, router_b, exp_w1, exp_b1,
           exp_w2, exp_b2):
    B, S, D = x.shape
    C = ctx.shape[1]
    T = C + S
    tm = ROW_TILE
    assert D == D_MODEL and C % tm == 0 and S % tm == 0 and C == tm

    xa = jnp.concatenate([ctx, x], axis=1)
    cc = jnp.concatenate([c, c_ctx[None], jnp.zeros((16 - B - 1, D), F32)], axis=0)
    mod_all = _ada(cc, w_ada, b_ada)
    cs32 = _rope_tables(32, C, S)
    cs64 = _rope_tables(64, C, S)

    xs = None
    for l in range(DEPTH):
        last = l == DEPTH - 1
        m = mod_all[l]
        mod = jnp.stack([jnp.broadcast_to(m[B][None], (B, 6 * D)), m[:B]], axis=1).reshape(B, 2, 6, 1, D)
        lam_init = 0.8 - 0.6 * math.exp(-0.3 * l)
        lp = df_lam[l].astype(F32)
        lam = (jnp.exp(jnp.sum(lp[0] * lp[1])) - jnp.exp(jnp.sum(lp[2] * lp[3])) + lam_init).reshape(1)
        w_t, gq_g, qa_g, kva_g, wq_t, wkv_t = _layer_weights(l, w_in, gq_qnorm, gq_knorm, ml_qa_norm, ml_wq_b,
                                                             ml_kva_norm, ml_wkv_b)
        (na_q, na_k, na_v, df_q, df_k, df_v, gq_q, gq_k, gq_v, ml_q, ml_k, ml_v) = _project(
            xa, mod, w_t, cs32, cs64, gq_g, qa_g, kva_g, wq_t, wkv_t, C)

        o_na = _na_attn(na_q, na_k, na_v, na_rpb[l], C)
        subln = jnp.broadcast_to((df_subln[l] * (1.0 - lam_init))[:, None], (DF_V, tm)).astype(F32)
        ones = jnp.ones((DF_V, tm), F32)
        o_df = _dense_attn(lam, df_q, df_k, df_v, subln, _DF_HEADS_SPEC, C, True, "df_attn")
        o_gq = _dense_attn(lam, gq_q, gq_k, gq_v, ones, _GQ_HEADS_SPEC, C, False, "gq_attn")
        o_ml = _dense_attn(lam, ml_q, ml_k, ml_v, ones, _ML_HEADS_SPEC, C, False, "ml_attn")

        t_off = C // tm if last else 0
        wg = w_in[l][:, O_GATES:].astype(BF16)
        wb = w_branch[l].reshape(N_BRANCH * BRANCH_W, D).astype(BF16)
        wo = w_out[l].astype(BF16)
        x1, h2, idx, wts, rank, cnt = _merge(xa, mod, o_na, o_df, o_gq, o_ml, wg, wb, wo,
                                             ln1_g[l][None], ln1_b[l][None], router_w[l].T, router_b[l], C, t_off)

        counts = cnt[:, 0]
        padded = ((counts + tm - 1) // tm) * tm
        ends = jnp.cumsum(padded)
        starts = (ends - padded).astype(I32)
        n_exp_tiles = (B * T * TOP_K) // tm + N_EXPERTS
        n_used = (ends[-1] // tm).astype(I32).reshape(1)
        tile_first = jnp.arange(n_exp_tiles, dtype=I32) * tm
        tile_expert = jnp.minimum(jnp.sum((ends[None, :] <= tile_first[:, None]).astype(I32), axis=1), N_EXPERTS - 1)
        tile_expert = jnp.where(tile_first < ends[-1], tile_expert, tile_expert[jnp.maximum(n_used[0] - 1, 0)])

        dest = jnp.sum(jnp.where(idx[..., None] == jnp.arange(N_EXPERTS, dtype=I32), starts, 0), axis=-1) + rank
        if xs is None:
            xs = jnp.zeros((n_exp_tiles * tm,) + h2.shape[1:], F32)
        xs = _dispatch(dest, h2, xs)
        ys = _experts(tile_expert, n_used, xs, exp_w1, exp_b1, exp_w2, exp_b2, l)
        xa = _combine(dest, wts, x1, mod, ln2_g[l][None], ln2_b[l][None], ys, C, t_off)
    return xa
```

```python
import functools
import math

import numpy as np
import jax
import jax.numpy as jnp
from jax import lax
from jax.experimental import pallas as pl
from jax.experimental.pallas import tpu as pltpu

F32, BF16, I32 = jnp.float32, jnp.bfloat16, jnp.int32

D_MODEL = 1024
DEPTH = 2
GRID_W = 64
ROPE_BASE = 10000.0
EPS = 1e-6
NA_HEADS, NA_DIM, NA_KH, NA_KW = 4, 64, 8, 16
DF_HEADS, DF_QK = 4, 32
DF_V = 2 * DF_QK
GQ_HEADS, GQ_KV_HEADS, GQ_DIM = 4, 2, 64
ML_HEADS, ML_NOPE, ML_ROPE, ML_V, ML_Q_RANK, ML_KV_RANK = 4, 64, 32, 64, 256, 128
N_BRANCH, BRANCH_W = 4, 256
N_EXPERTS, TOP_K = 32, 4
D_FF = D_MODEL
SWIGLU_LIMIT, SWIGLU_ALPHA = 7.0, 1.702
DEEPNORM_ALPHA = (2 * DEPTH) ** 0.25

IN_SIZES = (256, 256, 256, 256, 256, 256, 256, 128, 128, ML_Q_RANK, ML_KV_RANK, ML_ROPE, N_BRANCH * D_MODEL)
IN_OFFSETS = tuple(int(o) for o in np.cumsum((0,) + IN_SIZES)[:-1])
(O_NAQ, O_NAK, O_NAV, O_DFQ, O_DFK, O_DFV, O_GQQ, O_GQK, O_GQV, O_MLQA, O_MLKVA, O_MLKR, O_GATES) = IN_OFFSETS

LANES = 128
ROW_TILE = 256
MASK_VALUE = -1e30
VMEM_LIMIT = 56 * 1024 * 1024
LOG2E = math.log2(math.e)
KEY_CHUNK = 256
ACC_ROWS = 16
SCORE_SPLIT = 2

_PROJ_GROUPS = (("na_q", 256), ("na_k", 256), ("na_v", 256),
                ("df_q", 256), ("df_k", 256), ("df_v", 256),
                ("gq_q", 256), ("gq_k", 128), ("gq_v", 128),
                ("ml_qa", 256), ("ml_kva", 128), ("ml_kr", 32))
_PROJ_OFF = {}
_o = 0
for _n, _r in _PROJ_GROUPS:
    _PROJ_OFF[_n] = (_o, _r)
    _o += _r
PROJ_ROWS = _o


def _cparams(sem):
    return pltpu.CompilerParams(dimension_semantics=sem, vmem_limit_bytes=VMEM_LIMIT)


def _swap_cols(w, n, signed=True):
    lead, width = w.shape[:-1], w.shape[-1]
    w5 = w.reshape(lead + (width // n, 2, 2, n // 4))
    lo, hi = w5[..., 0, :], w5[..., 1, :]
    out = jnp.stack([-hi if signed else hi, lo], axis=-2)
    return out.reshape(lead + (width,))


def _rope_tables(n, ctx_len, seq):
    h = n // 2
    t = np.arange(seq)
    rows, cols = (t // GRID_W).astype(np.float32), (t % GRID_W).astype(np.float32)
    inv_freq = (np.float32(ROPE_BASE) ** (-np.arange(0, h, 2, dtype=np.float32) / np.float32(h))).astype(np.float32)
    ang_r = rows[:, None] * inv_freq[None, :]
    ang_c = cols[:, None] * inv_freq[None, :]
    ang = np.concatenate([ang_r, ang_r, ang_c, ang_c], axis=1)
    cos = np.concatenate([np.ones((ctx_len, n), np.float32), np.cos(ang)], axis=0).T
    sin = np.concatenate([np.zeros((ctx_len, n), np.float32), np.sin(ang)], axis=0).T
    return jnp.asarray(np.stack([cos, sin]).astype(np.float32))


def _ada_kernel(c_ref, w_ref, b_ref, o_ref):
    cc = c_ref[...]
    h = (cc * jax.nn.sigmoid(cc)).astype(BF16)
    o_ref[0] = jnp.dot(h, w_ref[0].astype(BF16), preferred_element_type=F32) + b_ref[0]


def _ada(cc, w_ada, b_ada):
    L, D, N = w_ada.shape
    R = cc.shape[0]
    tn = 512
    return pl.pallas_call(
        _ada_kernel,
        out_shape=jax.ShapeDtypeStruct((L, R, N), F32),
        grid=(L, N // tn),
        in_specs=[pl.BlockSpec((R, D), lambda l, j: (0, 0)),
                  pl.BlockSpec((1, D, tn), lambda l, j: (l, 0, j)),
                  pl.BlockSpec((1, 1, tn), lambda l, j: (l, 0, j))],
        out_specs=pl.BlockSpec((1, R, tn), lambda l, j: (l, 0, j)),
        compiler_params=_cparams(("arbitrary", "arbitrary")),
        name="ada_mod",
    )(cc, w_ada, b_ada.reshape(L, 1, N))


def _proj_kernel(x_ref, sc_ref, sh_ref, w_ref, cs32_ref, cs64_ref, gqg_ref, qag_ref, kvag_ref, wq_ref, wkv_ref,
                 na_q, na_k, na_v, df_q, df_k, df_v, gq_q, gq_k, gq_v, ml_q, ml_k, ml_v):
    tm = x_ref.shape[1]
    h = (x_ref[0] * (1.0 + sc_ref[...]) + sh_ref[...]).astype(BF16)
    p = lax.dot_general(w_ref[...], h, (((1,), (1,)), ((), ())), preferred_element_type=F32)

    def grp(name):
        o, r = _PROJ_OFF[name]
        return p[o:o + r]

    def zeros(n):
        return jnp.zeros((n, tm), F32)

    def swap(a, n):
        r, q = a.shape[0], n // 4
        a4 = a.reshape(r // (2 * q), 2, q, tm)
        return jnp.concatenate([-a4[:, 1:2], a4[:, 0:1]], axis=1).reshape(r, tm)

    def rope(a, cs_ref):
        n = cs_ref.shape[1]
        r = a.shape[0]
        a3, s3 = a.reshape(r // n, n, tm), swap(a, n).reshape(r // n, n, tm)
        return (a3 * cs_ref[0][None] + s3 * cs_ref[1][None]).reshape(r, tm)

    def head_slots(a, width):
        pieces = []
        for i in range(a.shape[0] // width):
            pieces += [a[i * width:(i + 1) * width], zeros(LANES - width)]
        return jnp.concatenate(pieces, axis=0)

    na_q[0] = head_slots(grp("na_q") * (NA_DIM ** -0.5), NA_DIM).astype(BF16)
    na_k[0] = head_slots(grp("na_k"), NA_DIM).T.astype(BF16)
    na_v[0] = grp("na_v").astype(BF16)

    q = rope(grp("df_q"), cs32_ref) * (DF_QK ** -0.5 * LOG2E)
    pieces = []
    for i in range(DF_HEADS):
        q1, q2 = q[i * 64:i * 64 + 32], q[i * 64 + 32:i * 64 + 64]
        pieces += [q1, zeros(LANES - 32), zeros(32), q2, zeros(LANES - 64)]
    df_q[0] = jnp.concatenate(pieces, axis=0).astype(BF16)
    df_k[0] = head_slots(rope(grp("df_k"), cs32_ref), 2 * DF_QK).T.astype(BF16)
    df_v[0] = grp("df_v").astype(BF16)

    cos64, sin64 = cs64_ref[0], cs64_ref[1]

    def norm_rope(a, g, g_sw, scale):
        hh = a.shape[0] // GQ_DIM
        a3, s3 = a.reshape(hh, GQ_DIM, tm), swap(a, GQ_DIM).reshape(hh, GQ_DIM, tm)
        r = lax.rsqrt(jnp.mean(a3 * a3, axis=1, keepdims=True) + EPS) * scale
        return ((a3 * (g * cos64)[None] + s3 * (g_sw * sin64)[None]) * r).reshape(hh * GQ_DIM, tm)

    gq_q[0] = head_slots(norm_rope(grp("gq_q"), gqg_ref[0], gqg_ref[1], GQ_DIM ** -0.5 * LOG2E),
                         GQ_DIM).astype(BF16)
    gq_k[0] = head_slots(norm_rope(grp("gq_k"), gqg_ref[2], gqg_ref[3], 1.0), GQ_DIM).T.astype(BF16)
    gq_v[0] = grp("gq_v").astype(BF16)

    qa = grp("ml_qa")
    qan = (qa * lax.rsqrt(jnp.mean(qa * qa, axis=0, keepdims=True) + EPS) * qag_ref[...]).astype(BF16)
    qq = jnp.dot(wq_ref[...], qan, preferred_element_type=F32)
    kva = grp("ml_kva")
    kvan = (kva * lax.rsqrt(jnp.mean(kva * kva, axis=0, keepdims=True) + EPS) * kvag_ref[...]).astype(BF16)
    kv = jnp.dot(wkv_ref[...], kvan, preferred_element_type=F32)
    k_rope = rope(grp("ml_kr"), cs32_ref)
    ml_scale = (ML_NOPE + ML_ROPE) ** -0.5 * LOG2E
    qd = ML_NOPE + ML_ROPE
    qp, kp, vp = [], [], []
    for i in range(ML_HEADS):
        b = i * LANES
        q_rope = rope(qq[i * qd + ML_NOPE:(i + 1) * qd], cs32_ref)
        qp += [qq[i * qd:i * qd + ML_NOPE] * ml_scale, q_rope * ml_scale, zeros(32)]
        kp += [kv[b:b + 64], k_rope, zeros(32)]
        vp += [kv[b + 64:b + 128]]
    ml_q[0] = jnp.concatenate(qp, axis=0).astype(BF16)
    ml_k[0] = jnp.concatenate(kp, axis=0).T.astype(BF16)
    ml_v[0] = jnp.concatenate(vp, axis=0).astype(BF16)


def _project(xa, mod, w_t, cs32, cs64, gq_g, qa_g, kva_g, wq_t, wkv_t, ctx_len):
    B, T, D = xa.shape
    tm = ROW_TILE
    nt = T // tm
    ctx_tiles = ctx_len // tm

    def kind(t):
        return jnp.where(t >= ctx_tiles, 1, 0)

    def mod_spec(j):
        return pl.BlockSpec((None, None, None, 1, D), lambda b, t: (b, kind(t), j, 0, 0))

    def const(a):
        nd = a.ndim
        return pl.BlockSpec(a.shape, lambda b, t: (0,) * nd)

    def fm(rows):
        return jax.ShapeDtypeStruct((B, rows, T), BF16), pl.BlockSpec((1, rows, tm), lambda b, t: (b, 0, t))

    def tmj(cols):
        return jax.ShapeDtypeStruct((B, T, cols), BF16), pl.BlockSpec((1, tm, cols), lambda b, t: (b, t, 0))

    outs = [fm(512), tmj(512), fm(256),
            fm(1024), tmj(512), fm(256),
            fm(512), tmj(256), fm(128),
            fm(512), tmj(512), fm(256)]
    return pl.pallas_call(
        _proj_kernel,
        out_shape=[o[0] for o in outs],
        grid=(B, nt),
        in_specs=[pl.BlockSpec((1, tm, D), lambda b, t: (b, t, 0)),
                  mod_spec(1), mod_spec(0),
                  const(w_t),
                  pl.BlockSpec((2, 32, tm), lambda b, t: (0, 0, t)),
                  pl.BlockSpec((2, 64, tm), lambda b, t: (0, 0, t)),
                  const(gq_g), const(qa_g), const(kva_g), const(wq_t), const(wkv_t)],
        out_specs=[o[1] for o in outs],
        compiler_params=_cparams(("arbitrary", "arbitrary")),
        name="in_proj",
    )(xa, mod, mod, w_t, cs32, cs64, gq_g, qa_g, kva_g, wq_t, wkv_t)


def _softmax_pv(k, q_t, v_t, exp_fn):
    s = jnp.dot(k, q_t, preferred_element_type=F32)
    m = jnp.max(s, axis=0, keepdims=True)
    p = exp_fn(s - m)
    l = jnp.sum(p, axis=0, keepdims=True)
    o = jnp.dot(v_t, p.astype(BF16), preferred_element_type=F32)
    return o, l


def _dense_attn_kernel(lam_ref, post_ref, *refs, branches, ctx_len):
    nb = len(branches)
    qkv = [refs[3 * b:3 * b + 3] for b in range(nb)]
    o_refs = refs[3 * nb:4 * nb]
    ot_refs = refs[4 * nb:5 * nb]
    s_ref, p_ref = refs[5 * nb:]
    total, tq = qkv[0][1].shape[1], qkv[0][0].shape[2]
    items = [(b, qs, ks, v_idx, i, signed) for b, (heads, _) in enumerate(branches)
             for i, (terms, v_idx) in enumerate(heads) for qs, ks, signed in terms]
    last_of_head = {(it[0], it[4]): j for j, it in enumerate(items)}

    def finish_head(b, i, acc):
        if branches[b][1]:
            acc = acc * lax.rsqrt(jnp.mean(acc * acc, axis=0, keepdims=True) + EPS) * post_ref[...]
        ot_refs[b][i * 64:(i + 1) * 64, :] = acc

    def add_term(head_acc, j, o):
        b, i = items[j][0], items[j][4]
        head_acc[b, i] = o if (b, i) not in head_acc else head_acc[b, i] + o
        if last_of_head[b, i] == j:
            finish_head(b, i, head_acc.pop((b, i)))

    def coef(signed):
        return (-lam_ref[0]) if signed else 1.0

    def run_context():
        head_acc = {}
        for j, (b, qs, ks, v_idx, i, signed) in enumerate(items):
            q_ref, k_ref, v_ref = qkv[b]
            o, l = _softmax_pv(k_ref[0, 0:ctx_len, ks * LANES:(ks + 1) * LANES], q_ref[0, qs * LANES:(qs + 1) * LANES, :],
                               v_ref[0, v_idx * 64:(v_idx + 1) * 64, 0:ctx_len], jnp.exp2)
            add_term(head_acc, j, o * (coef(signed) / l))

    def run_latent():
        n_chunks = total // KEY_CHUNK
        half = KEY_CHUNK // SCORE_SPLIT
        n_items = len(items)
        m_of, l_of, head_acc = {}, {}, {}
        for t in range(n_items + 2):
            sc = items[t] if t < n_items else None
            ex = items[t - 1] if 0 <= t - 1 < n_items else None
            pv = items[t - 2] if 0 <= t - 2 < n_items else None
            macc = jnp.full((ACC_ROWS, tq), -jnp.inf, F32)
            lacc = jnp.zeros((ACC_ROWS, tq), F32)
            oacc = jnp.zeros((64, tq), F32)
            for c in range(n_chunks):
                r0 = c * KEY_CHUNK
                if sc is not None:
                    q_ref, k_ref, _ = qkv[sc[0]]
                    q_t = q_ref[0, sc[1] * LANES:(sc[1] + 1) * LANES, :]
                    for r in range(r0, r0 + KEY_CHUNK, half):
                        s = jnp.dot(k_ref[0, r:r + half, sc[2] * LANES:(sc[2] + 1) * LANES], q_t,
                                    preferred_element_type=F32)
                        s_ref[t % 2, r:r + half, :] = s
                        for a in range(half // ACC_ROWS):
                            macc = jnp.maximum(macc, s[a * ACC_ROWS:(a + 1) * ACC_ROWS])
                if ex is not None:
                    p = jnp.exp2(s_ref[(t - 1) % 2, r0:r0 + KEY_CHUNK, :] - m_of[t - 1])
                    for a in range(KEY_CHUNK // ACC_ROWS):
                        lacc = lacc + p[a * ACC_ROWS:(a + 1) * ACC_ROWS]
                    p_ref[(t - 1) % 2, r0:r0 + KEY_CHUNK, :] = p.astype(BF16)
                if pv is not None:
                    v_ref = qkv[pv[0]][2]
                    oacc = oacc + jnp.dot(v_ref[0, pv[3] * 64:(pv[3] + 1) * 64, r0:r0 + KEY_CHUNK],
                                          p_ref[(t - 2) % 2, r0:r0 + KEY_CHUNK, :], preferred_element_type=F32)
            if sc is not None:
                m_of[t] = jnp.max(macc, axis=0, keepdims=True)
            if ex is not None:
                l_of[t - 1] = jnp.sum(lacc, axis=0, keepdims=True)
            if pv is not None:
                add_term(head_acc, t - 2, oacc * (coef(pv[5]) / l_of.pop(t - 2)))

    def write_out():
        for o_ref, ot_ref in zip(o_refs, ot_refs):
            o_ref[0] = ot_ref[...].T.astype(BF16)

    qb = pl.program_id(1)

    @pl.when(qb * tq < ctx_len)
    def _():
        run_context()
        write_out()

    @pl.when(qb * tq >= ctx_len)
    def _():
        run_latent()
        write_out()


def _dense_attn(lam, post, branches, ctx_len):
    B, _, T = branches[0][0].shape
    tq = ROW_TILE
    nb = len(branches)
    once = pl.Buffered(1)
    in_specs = [pl.BlockSpec(memory_space=pltpu.SMEM), pl.BlockSpec(post.shape, lambda b, t: (0, 0))]
    args = [lam, post]
    for q_t, k, v_t, _, _ in branches:
        in_specs += [pl.BlockSpec((1, q_t.shape[1], tq), lambda b, t: (b, 0, t)),
                     pl.BlockSpec((1, T, k.shape[2]), lambda b, t: (b, 0, 0), pipeline_mode=once),
                     pl.BlockSpec((1, v_t.shape[1], T), lambda b, t: (b, 0, 0), pipeline_mode=once)]
        args += [q_t, k, v_t]
    kern = functools.partial(_dense_attn_kernel, branches=tuple((br[3], br[4]) for br in branches), ctx_len=ctx_len)
    return pl.pallas_call(
        kern,
        out_shape=[jax.ShapeDtypeStruct((B, T, BRANCH_W), BF16)] * nb,
        grid=(B, T // tq),
        in_specs=in_specs,
        out_specs=[pl.BlockSpec((1, tq, BRANCH_W), lambda b, t: (b, t, 0))] * nb,
        scratch_shapes=[pltpu.VMEM((BRANCH_W, tq), F32)] * nb
        + [pltpu.VMEM((2, T, tq), F32), pltpu.VMEM((2, T, tq), BF16)],
        compiler_params=_cparams(("arbitrary", "arbitrary")),
        name="dense_attn",
    )(*args)


def _na_plan(rows):
    kh, kw = min(NA_KH, rows), NA_KW
    q_rows = ROW_TILE // GRID_W
    win_rows = min(kh + q_rows, rows)
    nblk = rows // q_rows
    row_start = np.clip(np.arange(rows) - kh // 2, 0, rows - kh)
    col_start = np.clip(np.arange(GRID_W) - kw // 2, 0, GRID_W - kw)
    u0 = np.clip(np.arange(nblk) * q_rows - kh // 2, 0, rows - win_rows)
    wk, tq = win_rows * GRID_W, ROW_TILE
    kk, qq = np.arange(wk), np.arange(tq)
    kc, qc = kk % GRID_W, qq % GRID_W
    col_sel = (np.arange(2 * NA_KW - 1)[:, None, None]
               == (np.arange(GRID_W)[None, :, None] - np.arange(GRID_W)[None, None, :] + (NA_KW - 1))).astype(np.float32)
    types, type_of, row_sel_l, valid_l = {}, [], [], []
    for j in range(nblk):
        kr = u0[j] + kk // GRID_W
        qr = j * q_rows + qq // GRID_W
        rs = row_start[qr]
        valid = ((kr[:, None] >= rs[None, :]) & (kr[:, None] < rs[None, :] + kh)
                 & (kc[:, None] >= col_start[qc][None, :]) & (kc[:, None] < col_start[qc][None, :] + kw))
        assert valid.sum(axis=0).min() == kh * kw and valid.sum(axis=0).max() == kh * kw
        rel = (u0[j] + np.arange(win_rows))[:, None] - (j * q_rows + np.arange(q_rows))[None, :] + (NA_KH - 1)
        row_sel = (rel[:, :, None] == np.arange(2 * NA_KH - 1)[None, None, :]).astype(np.float32)
        key = (valid.tobytes(), row_sel.tobytes())
        if key not in types:
            types[key] = len(row_sel_l)
            row_sel_l.append(row_sel), valid_l.append(valid)
        type_of.append(types[key])
    assert all(int(u) * GRID_W % LANES == 0 for u in u0)
    return (np.asarray(u0, np.int32), np.asarray(type_of, np.int32), np.stack(row_sel_l), col_sel,
            np.stack(valid_l), wk)


def _na_kernel(u0_ref, ty_ref, q_ref, k_ref, v_ref, bias_ref, o_ref, ot_ref, *, ctx_len):
    del ty_ref
    wk = bias_ref.shape[2]
    j = pl.program_id(1)

    def finish():
        o_ref[0] = ot_ref[...].T.astype(BF16)

    @pl.when(j == 0)
    def _():
        for i in range(NA_HEADS):
            o, l = _softmax_pv(k_ref[0, 0:ctx_len, i * LANES:(i + 1) * LANES], q_ref[0, i * LANES:(i + 1) * LANES, :],
                               v_ref[0, i * 64:(i + 1) * 64, 0:ctx_len], jnp.exp)
            ot_ref[i * 64:(i + 1) * 64, :] = o * (1.0 / l)
        finish()

    @pl.when(j > 0)
    def _():
        ws = pl.multiple_of(ctx_len + u0_ref[j - 1] * GRID_W, LANES)

        def scores(i):
            q_t = q_ref[0, i * LANES:(i + 1) * LANES, :]
            s_c = jnp.dot(k_ref[0, 0:ctx_len, i * LANES:(i + 1) * LANES], q_t, preferred_element_type=F32)
            s_w = jnp.dot(k_ref[0, pl.ds(ws, wk), i * LANES:(i + 1) * LANES], q_t,
                          preferred_element_type=F32) + bias_ref[0, i]
            return s_c, s_w

        def fold(fn, acc, a):
            for r in range(0, a.shape[0], ACC_ROWS):
                acc = fn(acc, a[r:r + ACC_ROWS])
            return acc

        def softmax_pv(i, s_c, s_w):
            tq = s_c.shape[1]
            macc = fold(jnp.maximum, fold(jnp.maximum, jnp.full((ACC_ROWS, tq), -jnp.inf, F32), s_c), s_w)
            m = jnp.max(macc, axis=0, keepdims=True)
            p_c, p_w = jnp.exp(s_c - m), jnp.exp(s_w - m)
            lacc = fold(jnp.add, fold(jnp.add, jnp.zeros((ACC_ROWS, tq), F32), p_c), p_w)
            l = jnp.sum(lacc, axis=0, keepdims=True)
            o = (jnp.dot(v_ref[0, i * 64:(i + 1) * 64, 0:ctx_len], p_c.astype(BF16), preferred_element_type=F32)
                 + jnp.dot(v_ref[0, i * 64:(i + 1) * 64, pl.ds(ws, wk)], p_w.astype(BF16),
                           preferred_element_type=F32))
            ot_ref[i * 64:(i + 1) * 64, :] = o * (1.0 / l)

        pending = scores(0)
        for i in range(NA_HEADS):
            upcoming = scores(i + 1) if i + 1 < NA_HEADS else None
            softmax_pv(i, *pending)
            pending = upcoming
        finish()


def _na_attn(q_t, k, v_t, rpb, ctx_len):
    B, _, T = q_t.shape
    tq = ROW_TILE
    rows = (T - ctx_len) // GRID_W
    u0, type_of, row_sel, col_sel, valid, wk = _na_plan(rows)
    hp = lax.Precision.HIGHEST
    by_col = jnp.einsum("hab,bcq->hacq", rpb.astype(F32), col_sel, precision=hp)
    bias = jnp.einsum("tkra,hacq->thkcrq", row_sel, by_col, precision=hp)
    bias = jnp.where(valid[:, None], bias.reshape(row_sel.shape[0], NA_HEADS, wk, tq), MASK_VALUE)
    nb = T // tq
    assert ctx_len == tq

    grid_spec = pltpu.PrefetchScalarGridSpec(
        num_scalar_prefetch=2,
        grid=(B, nb),
        in_specs=[pl.BlockSpec((1, NA_HEADS * LANES, tq), lambda b, j, u, ty: (b, 0, j)),
                  pl.BlockSpec((1, T, NA_HEADS * LANES), lambda b, j, u, ty: (b, 0, 0)),
                  pl.BlockSpec((1, BRANCH_W, T), lambda b, j, u, ty: (b, 0, 0)),
                  pl.BlockSpec((1, NA_HEADS, wk, tq), lambda b, j, u, ty: (ty[jnp.maximum(j - 1, 0)], 0, 0, 0))],
        out_specs=pl.BlockSpec((1, tq, BRANCH_W), lambda b, j, u, ty: (b, j, 0)),
        scratch_shapes=[pltpu.VMEM((BRANCH_W, tq), F32)])
    return pl.pallas_call(
        functools.partial(_na_kernel, ctx_len=ctx_len),
        out_shape=jax.ShapeDtypeStruct((B, T, BRANCH_W), BF16),
        grid_spec=grid_spec,
        compiler_params=_cparams(("arbitrary", "arbitrary")),
        name="na_attn",
    )(jnp.asarray(u0), jnp.asarray(type_of), q_t, k, v_t, bias)


def _merge_kernel(x_ref, sc1_ref, sh1_ref, g1_ref, sc2_ref, sh2_ref, ona_ref, odf_ref, ogq_ref, oml_ref,
                  wg_ref, wb_ref, wo_ref, lng_ref, lnb_ref, rw_ref, rb_ref, tri_ref,
                  x1_ref, h2_ref, idx_ref, wts_ref, rank_ref, cnt_ref, carry_ref, hprev_ref):
    tm = x_ref.shape[1]
    step = pl.program_id(0)

    @pl.when(step == 0)
    def _():
        carry_ref[...] = jnp.zeros_like(carry_ref)
        hprev_ref[...] = jnp.zeros_like(hprev_ref)

    iota = lax.broadcasted_iota(I32, (N_EXPERTS, tm), 0)
    route = {}

    def route_logits():
        route["cur"] = lax.dot_general(rw_ref[...], hprev_ref[...], (((1,), (1,)), ((), ())),
                                       preferred_element_type=F32, precision=lax.Precision.HIGHEST) + rb_ref[...]
        route["vals"], route["idxs"] = [], []

    def route_pick():
        cur = route["cur"]
        m = jnp.max(cur, axis=0, keepdims=True)
        ik = jnp.min(jnp.where(cur == m, iota, N_EXPERTS), axis=0, keepdims=True)
        route["vals"].append(m)
        route["idxs"].append(ik)
        route["cur"] = jnp.where(iota == ik, -jnp.inf, cur)

    def route_finish():
        vals, idxs = route["vals"], route["idxs"]
        exps = [jnp.exp(v - vals[0]) for v in vals]
        denom = exps[0] + exps[1] + exps[2] + exps[3]
        wts_ref[0] = jnp.concatenate([e / denom for e in exps], axis=0)
        idx_ref[0] = jnp.concatenate(idxs, axis=0)
        live = jnp.where(step >= 1, 1.0, 0.0)
        onehot = jnp.zeros((N_EXPERTS, tm), F32)
        for ik in idxs:
            onehot = onehot + jnp.where(iota == ik, live, 0.0)
        before = jnp.dot(onehot.astype(BF16), tri_ref[...], preferred_element_type=F32) + carry_ref[...]
        rank_ref[0] = jnp.concatenate(
            [jnp.sum(jnp.where(iota == ik, before, 0.0), axis=0, keepdims=True) for ik in idxs], axis=0).astype(I32)
        carry = carry_ref[...] + jnp.sum(onehot, axis=1, keepdims=True)
        carry_ref[...] = carry
        cnt_ref[...] = carry.astype(I32)

    route_pieces = [[route_logits], [route_pick, route_pick], [route_pick, route_pick], [route_finish]]

    x = x_ref[0]
    h1 = (x * (1.0 + sc1_ref[...]) + sh1_ref[...]).astype(BF16)
    y = None
    for i, o_ref in enumerate((ona_ref, odf_ref, ogq_ref, oml_ref)):
        gate = jax.nn.sigmoid(jnp.dot(h1, wg_ref[:, i * D_MODEL:(i + 1) * D_MODEL], preferred_element_type=F32))
        term = gate * jnp.dot(o_ref[0], wb_ref[i * BRANCH_W:(i + 1) * BRANCH_W, :], preferred_element_type=F32)
        y = term if y is None else y + term
        for piece in route_pieces[i]:
            piece()
    z = jnp.dot(y.astype(BF16), wo_ref[...], preferred_element_type=F32)
    r = DEEPNORM_ALPHA * x + g1_ref[...] * z
    mu = jnp.mean(r, axis=-1, keepdims=True)
    var = jnp.mean(jnp.square(r - mu), axis=-1, keepdims=True)
    x1 = (r - mu) * lax.rsqrt(var + EPS) * lng_ref[...] + lnb_ref[...]
    x1_ref[0] = x1
    h2 = x1 * (1.0 + sc2_ref[...]) + sh2_ref[...]
    for j in range(D_MODEL // LANES):
        h2_ref[:, j, :] = h2[:, j * LANES:(j + 1) * LANES]
    hprev_ref[...] = h2


def _merge(xa, mod, o_na, o_df, o_gq, o_ml, wg, wb, wo, ln_g, ln_b, rw_t, rb, ctx_len, t_off):
    B, T, D = xa.shape
    tm = ROW_TILE
    nt = T // tm - t_off
    ctx_tiles = ctx_len // tm
    tq_rows = nt * tm
    n_tiles = B * nt

    def tile(i, lag):
        return jnp.clip(i - lag, 0, n_tiles - 1)

    def bt(i, lag):
        ti = tile(i, lag)
        return ti // nt, ti % nt

    def kind(t):
        return jnp.where(t + t_off >= ctx_tiles, 1, 0)

    def mod_spec(j, lag):
        return pl.BlockSpec((None, None, None, 1, D), lambda i: (bt(i, lag)[0], kind(bt(i, lag)[1]), j, 0, 0))

    def const(a):
        nd = a.ndim
        return pl.BlockSpec(a.shape, lambda i: (0,) * nd)

    def tok(cols):
        return pl.BlockSpec((1, tm, cols), lambda i: (bt(i, 0)[0], bt(i, 0)[1] + t_off, 0))

    tri = jnp.asarray(np.triu(np.ones((tm, tm), np.float32), 1), BF16)
    rb_b = jnp.broadcast_to(rb[:, None], (N_EXPERTS, tm)).astype(F32)
    route_shape = jax.ShapeDtypeStruct((n_tiles, TOP_K, tm), I32)
    route_spec = pl.BlockSpec((1, TOP_K, tm), lambda i: (tile(i, 1), 0, 0))
    return pl.pallas_call(
        _merge_kernel,
        out_shape=[jax.ShapeDtypeStruct((B, tq_rows, D), F32),
                   jax.ShapeDtypeStruct((B * tq_rows, D // LANES, LANES), F32),
                   route_shape, jax.ShapeDtypeStruct((n_tiles, TOP_K, tm), F32), route_shape,
                   jax.ShapeDtypeStruct((N_EXPERTS, tm), I32)],
        grid=(n_tiles + 1,),
        in_specs=[tok(D), mod_spec(1, 0), mod_spec(0, 0), mod_spec(2, 0), mod_spec(4, 0), mod_spec(3, 0),
                  tok(BRANCH_W), tok(BRANCH_W), tok(BRANCH_W), tok(BRANCH_W),
                  const(wg), const(wb), const(wo), const(ln_g), const(ln_b), const(rw_t), const(rb_b), const(tri)],
        out_specs=[pl.BlockSpec((1, tm, D), lambda i: (bt(i, 0)[0], bt(i, 0)[1], 0)),
                   pl.BlockSpec((tm, D // LANES, LANES), lambda i: (tile(i, 0), 0, 0)),
                   route_spec, route_spec, route_spec,
                   pl.BlockSpec((N_EXPERTS, tm), lambda i: (0, 0))],
        scratch_shapes=[pltpu.VMEM((N_EXPERTS, tm), F32), pltpu.VMEM((tm, D), F32)],
        compiler_params=_cparams(("arbitrary",)),
        name="merge_router",
    )(xa, mod, mod, mod, mod, mod, o_na, o_df, o_gq, o_ml, wg, wb, wo, ln_g, ln_b, rw_t, rb_b, tri)


def _row_copy(src, src_row, dst, dst_row, sem):
    return pltpu.make_async_copy(src.at[pl.ds(src_row, 1)], dst.at[pl.ds(dst_row, 1)], sem)


def _dispatch_kernel(dest_ref, h_ref, xs_in_ref, xs_ref, sem):
    del xs_in_ref
    tm = dest_ref.shape[2]

    def issue(t, c):
        for k in range(TOP_K):
            _row_copy(h_ref, t, xs_ref, dest_ref[0, k, t], sem).start(priority=k % 2)
        return c

    lax.fori_loop(0, tm, issue, 0, unroll=2)
    for k in range(TOP_K):
        pltpu.make_async_copy(h_ref, xs_ref.at[pl.ds(0, tm)], sem).wait()


def _dispatch(dest, h2, xs0):
    n_tiles, _, tm = dest.shape
    return pl.pallas_call(
        _dispatch_kernel,
        out_shape=jax.ShapeDtypeStruct(xs0.shape, F32),
        grid=(n_tiles,),
        in_specs=[pl.BlockSpec((1, TOP_K, tm), lambda i: (i, 0, 0), memory_space=pltpu.SMEM),
                  pl.BlockSpec((tm,) + h2.shape[1:], lambda i: (i, 0, 0)),
                  pl.BlockSpec(memory_space=pl.ANY)],
        out_specs=pl.BlockSpec(memory_space=pl.ANY),
        scratch_shapes=[pltpu.SemaphoreType.DMA(())],
        input_output_aliases={2: 0},
        compiler_params=_cparams(("arbitrary",)),
        name="moe_dispatch",
    )(dest, h2, xs0)


def _tile_relayout_copies(tiled_hbm, flat_vmem, tile, sem):
    tm = flat_vmem.shape[0]
    r0 = pl.multiple_of(tile * tm, tm)
    return [(tiled_hbm.at[pl.ds(r0, tm), j, :], flat_vmem.at[:, pl.ds(j * LANES, LANES)], sem)
            for j in range(D_MODEL // LANES)]


def _expert_kernel(te_ref, nu_ref, xs_ref, w1_ref, b1_ref, w2_ref, b2_ref, y_ref, w1s_ref, w2s_ref, x_buf, y_buf,
                   sem_in, sem_out, *, n_steps):
    i = pl.program_id(0)
    n_used = nu_ref[0]
    slot = lax.rem(i, 2)

    def fetch(tile, s, wait):
        for hbm, vmem, sem in _tile_relayout_copies(xs_ref, x_buf.at[s], tile, sem_in.at[s]):
            cp = pltpu.make_async_copy(hbm, vmem, sem)
            cp.wait() if wait else cp.start()

    def write_back(tile, s, wait):
        for hbm, vmem, sem in _tile_relayout_copies(y_ref, y_buf.at[s], tile, sem_out.at[s]):
            cp = pltpu.make_async_copy(vmem, hbm, sem)
            cp.wait() if wait else cp.start(priority=1)

    @pl.when(i == 0)
    def _():
        fetch(0, 0, False)

    @pl.when(i + 1 < n_used)
    def _():
        fetch(i + 1, 1 - slot, False)

    @pl.when((i == 0) | (te_ref[i] != te_ref[jnp.maximum(i - 1, 0)]))
    def _():
        w1s_ref[...] = w1_ref[...].astype(BF16)
        w2s_ref[...] = w2_ref[...].astype(BF16)

    @pl.when(i >= 2)
    def _():
        write_back(i - 2, slot, True)

    @pl.when(i < n_used)
    def _():
        fetch(i, slot, True)
        gu = jnp.dot(x_buf[slot].astype(BF16), w1s_ref[...], preferred_element_type=F32) + b1_ref[...]
        gate = jnp.minimum(gu[:, :D_FF], SWIGLU_LIMIT)
        up = jnp.clip(gu[:, D_FF:], -SWIGLU_LIMIT, SWIGLU_LIMIT)
        act = (up + 1.0) * gate * jax.nn.sigmoid(SWIGLU_ALPHA * gate)
        y_buf[slot] = jnp.dot(act.astype(BF16), w2s_ref[...], preferred_element_type=F32) + b2_ref[...]

    @pl.when(i >= n_used)
    def _():
        y_buf[slot] = jnp.zeros(y_buf.shape[1:], F32)

    write_back(i, slot, False)

    @pl.when(i == n_steps - 1)
    def _():
        write_back(i, slot, True)

        @pl.when(i >= 1)
        def _():
            write_back(i - 1, 1 - slot, True)


def _experts(tile_expert, n_used, xs, w1, b1, w2, b2, layer):
    n_slots = xs.shape[0]
    tm = ROW_TILE
    nt = n_slots // tm
    E = w1.shape[1]
    grid_spec = pltpu.PrefetchScalarGridSpec(
        num_scalar_prefetch=2, grid=(nt,),
        in_specs=[pl.BlockSpec(memory_space=pl.ANY),
                  pl.BlockSpec((None, None, D_MODEL, 2 * D_FF), lambda i, te, nu: (layer, te[i], 0, 0)),
                  pl.BlockSpec((None, None, 1, 2 * D_FF), lambda i, te, nu: (layer, te[i], 0, 0)),
                  pl.BlockSpec((None, None, D_FF, D_MODEL), lambda i, te, nu: (layer, te[i], 0, 0)),
                  pl.BlockSpec((None, None, 1, D_MODEL), lambda i, te, nu: (layer, te[i], 0, 0))],
        out_specs=pl.BlockSpec(memory_space=pl.ANY),
        scratch_shapes=[pltpu.VMEM((D_MODEL, 2 * D_FF), BF16), pltpu.VMEM((D_FF, D_MODEL), BF16),
                        pltpu.VMEM((2, tm, D_MODEL), F32), pltpu.VMEM((2, tm, D_MODEL), F32),
                        pltpu.SemaphoreType.DMA((2,)), pltpu.SemaphoreType.DMA((2,))])
    return pl.pallas_call(
        functools.partial(_expert_kernel, n_steps=nt),
        out_shape=jax.ShapeDtypeStruct(xs.shape, F32),
        grid_spec=grid_spec,
        compiler_params=_cparams(("arbitrary",)),
        name="moe_experts",
    )(tile_expert, n_used, xs, w1, b1.reshape(b1.shape[0], E, 1, -1), w2, b2.reshape(b2.shape[0], E, 1, -1))


def _combine_kernel(dest_ref, destn_ref, w_ref, x1_ref, g2_ref, lng_ref, lnb_ref, y_ref, o_ref, buf_ref, y3_ref,
                    sem, *, n_steps):
    tm = dest_ref.shape[2]
    i = pl.program_id(0)
    n = n_steps
    slot = lax.rem(i, 2)

    def issue(dr, s, t):
        for k in range(TOP_K):
            pltpu.make_async_copy(y_ref.at[pl.ds(dr[0, k, t], 1)], buf_ref.at[s, k, pl.ds(t, 1)],
                                  sem.at[s]).start(priority=k % 2)

    def weigh(t):
        acc = buf_ref[slot, 0, t] * w_ref[0, 0, t]
        for k in range(1, TOP_K):
            acc = acc + buf_ref[slot, k, t] * w_ref[0, k, t]
        y3_ref[t] = acc

    def loop(body):
        lax.fori_loop(0, tm, lambda t, c: (body(t), c)[1], 0, unroll=2)

    @pl.when(i == 0)
    def _():
        loop(lambda t: issue(dest_ref, 0, t))

    for k in range(TOP_K):
        pltpu.make_async_copy(y_ref.at[pl.ds(0, tm)], buf_ref.at[slot, k], sem.at[slot]).wait()

    @pl.when(i + 1 < n)
    def _():
        loop(lambda t: (issue(destn_ref, 1 - slot, t), weigh(t)))

    @pl.when(i + 1 >= n)
    def _():
        loop(weigh)
    y2 = jnp.concatenate([y3_ref[:, j, :] for j in range(D_MODEL // LANES)], axis=1)
    r = DEEPNORM_ALPHA * x1_ref[0] + g2_ref[...] * y2
    mu = jnp.mean(r, axis=-1, keepdims=True)
    var = jnp.mean(jnp.square(r - mu), axis=-1, keepdims=True)
    o_ref[0] = (r - mu) * lax.rsqrt(var + EPS) * lng_ref[...] + lnb_ref[...]


def _combine(dest, wts, x1, mod, ln_g, ln_b, y, ctx_len, t_off):
    B, tq_rows, D = x1.shape
    n_tiles, _, tm = dest.shape
    nt = n_tiles // B
    ctx_tiles = ctx_len // tm

    def kind(i):
        return jnp.where(i % nt + t_off >= ctx_tiles, 1, 0)

    smem = lambda f: pl.BlockSpec((1, TOP_K, tm), f, memory_space=pltpu.SMEM)
    cur = lambda i: (i, 0, 0)
    nxt = lambda i: (jnp.minimum(i + 1, n_tiles - 1), 0, 0)
    return pl.pallas_call(
        functools.partial(_combine_kernel, n_steps=n_tiles),
        out_shape=jax.ShapeDtypeStruct((B, tq_rows, D), F32),
        grid=(n_tiles,),
        in_specs=[smem(cur), smem(nxt), smem(cur),
                  pl.BlockSpec((1, tm, D), lambda i: (i // nt, i % nt, 0)),
                  pl.BlockSpec((None, None, None, 1, D), lambda i: (i // nt, kind(i), 5, 0, 0)),
                  pl.BlockSpec(ln_g.shape, lambda i: (0, 0)),
                  pl.BlockSpec(ln_b.shape, lambda i: (0, 0)),
                  pl.BlockSpec(memory_space=pl.ANY)],
        out_specs=pl.BlockSpec((1, tm, D), lambda i: (i // nt, i % nt, 0)),
        scratch_shapes=[pltpu.VMEM((2, TOP_K, tm, D // LANES, LANES), F32),
                        pltpu.VMEM((tm, D // LANES, LANES), F32), pltpu.SemaphoreType.DMA((2,))],
        compiler_params=_cparams(("arbitrary",)),
        name="moe_combine",
    )(dest, dest, wts, x1, mod, ln_g, ln_b, y)


_DF_HEADS_SPEC = tuple((((2 * i, i, False), (2 * i + 1, i, True)), i) for i in range(DF_HEADS))
_GQ_HEADS_SPEC = tuple((((i, i // (GQ_HEADS // GQ_KV_HEADS), False),), i // (GQ_HEADS // GQ_KV_HEADS))
                       for i in range(GQ_HEADS))
_ML_HEADS_SPEC = tuple((((i, i, False),), i) for i in range(ML_HEADS))


def _layer_weights(l, w_in, gq_qnorm, gq_knorm, ml_qa_norm, ml_wq_b, ml_kva_norm, ml_wkv_b):
    w_t = w_in[l][:, :O_GATES].T.astype(BF16)
    assert w_t.shape[0] == PROJ_ROWS

    tm = ROW_TILE
    gq_g = jnp.stack([gq_qnorm[l], _swap_cols(gq_qnorm[l], 64, signed=False),
                      gq_knorm[l], _swap_cols(gq_knorm[l], 64, signed=False)])
    gq_g = jnp.broadcast_to(gq_g[:, :, None], (4, GQ_DIM, tm)).astype(F32)
    qa_g = jnp.broadcast_to(ml_qa_norm[l][:, None], (ML_Q_RANK, tm)).astype(F32)
    kva_g = jnp.broadcast_to(ml_kva_norm[l][:, None], (ML_KV_RANK, tm)).astype(F32)
    wq_t = ml_wq_b[l].T.astype(BF16)
    wkv_t = ml_wkv_b[l].T.astype(BF16)
    return w_t, gq_g, qa_g, kva_g, wq_t, wkv_t


def kernel(x, c, ctx, c_ctx, w_ada, b_ada, w_in, na_rpb, df_lam, df_subln, gq_qnorm, gq_knorm, ml_qa_norm, ml_wq_b,
           ml_kva_norm, ml_wkv_b, w_branch, w_out, ln1_g, ln1_b, ln2_g, ln2_b, router_w, router_b, exp_w1, exp_b1,
           exp_w2, exp_b2):
    B, S, D = x.shape
    C = ctx.shape[1]
    T = C + S
    tm = ROW_TILE
    assert D == D_MODEL and C % tm == 0 and S % tm == 0 and C == tm

    xa = jnp.concatenate([ctx, x], axis=1)
    cc = jnp.concatenate([c, c_ctx[None], jnp.zeros((16 - B - 1, D), F32)], axis=0)
    mod_all = _ada(cc, w_ada, b_ada)
    cs32 = _rope_tables(32, C, S)
    cs64 = _rope_tables(64, C, S)

    xs = None
    for l in range(DEPTH):
        last = l == DEPTH - 1
        m = mod_all[l]
        mod = jnp.stack([jnp.broadcast_to(m[B][None], (B, 6 * D)), m[:B]], axis=1).reshape(B, 2, 6, 1, D)
        lam_init = 0.8 - 0.6 * math.exp(-0.3 * l)
        lp = df_lam[l].astype(F32)
        lam = (jnp.exp(jnp.sum(lp[0] * lp[1])) - jnp.exp(jnp.sum(lp[2] * lp[3])) + lam_init).reshape(1)
        w_t, gq_g, qa_g, kva_g, wq_t, wkv_t = _layer_weights(l, w_in, gq_qnorm, gq_knorm, ml_qa_norm, ml_wq_b,
                                                             ml_kva_norm, ml_wkv_b)
        (na_q, na_k, na_v, df_q, df_k, df_v, gq_q, gq_k, gq_v, ml_q, ml_k, ml_v) = _project(
            xa, mod, w_t, cs32, cs64, gq_g, qa_g, kva_g, wq_t, wkv_t, C)

        o_na = _na_attn(na_q, na_k, na_v, na_rpb[l], C)
        subln = jnp.broadcast_to((df_subln[l] * (1.0 - lam_init))[:, None], (DF_V, tm)).astype(F32)
        o_df, o_gq, o_ml = _dense_attn(lam, subln, [(df_q, df_k, df_v, _DF_HEADS_SPEC, True),
                                                    (gq_q, gq_k, gq_v, _GQ_HEADS_SPEC, False),
                                                    (ml_q, ml_k, ml_v, _ML_HEADS_SPEC, False)], C)

        t_off = C // tm if last else 0
        wg = w_in[l][:, O_GATES:].astype(BF16)
        wb = w_branch[l].reshape(N_BRANCH * BRANCH_W, D).astype(BF16)
        wo = w_out[l].astype(BF16)
        x1, h2, idx, wts, rank, cnt = _merge(xa, mod, o_na, o_df, o_gq, o_ml, wg, wb, wo,
                                             ln1_g[l][None], ln1_b[l][None], router_w[l].T, router_b[l], C, t_off)

        counts = cnt[:, 0]
        padded = ((counts + tm - 1) // tm) * tm
        ends = jnp.cumsum(padded)
        starts = (ends - padded).astype(I32)
        n_exp_tiles = (B * T * TOP_K) // tm + N_EXPERTS
        n_used = (ends[-1] // tm).astype(I32).reshape(1)
        tile_first = jnp.arange(n_exp_tiles, dtype=I32) * tm
        tile_expert = jnp.minimum(jnp.sum((ends[None, :] <= tile_first[:, None]).astype(I32), axis=1), N_EXPERTS - 1)
        tile_expert = jnp.where(tile_first < ends[-1], tile_expert, tile_expert[jnp.maximum(n_used[0] - 1, 0)])

        dest = jnp.sum(jnp.where(idx[..., None] == jnp.arange(N_EXPERTS, dtype=I32), starts, 0), axis=-1) + rank
        if xs is None:
            xs = jnp.zeros((n_exp_tiles * tm,) + h2.shape[1:], F32)
        xs = _dispatch(dest, h2, xs)
        ys = _experts(tile_expert, n_used, xs, exp_w1, exp_b1, exp_w2, exp_b2, l)
        xa = _combine(dest, wts, x1, mod, ln2_g[l][None], ln2_b[l][None], ys, C, t_off)
    return xa
```

```python
import functools
import math

import numpy as np
import jax
import jax.numpy as jnp
from jax import lax
from jax.experimental import pallas as pl
from jax.experimental.pallas import tpu as pltpu

F32, BF16, I32 = jnp.float32, jnp.bfloat16, jnp.int32

D_MODEL = 1024
DEPTH = 2
GRID_W = 64
ROPE_BASE = 10000.0
EPS = 1e-6
NA_HEADS, NA_DIM, NA_KH, NA_KW = 4, 64, 8, 16
DF_HEADS, DF_QK = 4, 32
DF_V = 2 * DF_QK
GQ_HEADS, GQ_KV_HEADS, GQ_DIM = 4, 2, 64
ML_HEADS, ML_NOPE, ML_ROPE, ML_V, ML_Q_RANK, ML_KV_RANK = 4, 64, 32, 64, 256, 128
N_BRANCH, BRANCH_W = 4, 256
N_EXPERTS, TOP_K = 32, 4
D_FF = D_MODEL
SWIGLU_LIMIT, SWIGLU_ALPHA = 7.0, 1.702
DEEPNORM_ALPHA = (2 * DEPTH) ** 0.25

IN_SIZES = (256, 256, 256, 256, 256, 256, 256, 128, 128, ML_Q_RANK, ML_KV_RANK, ML_ROPE, N_BRANCH * D_MODEL)
IN_OFFSETS = tuple(int(o) for o in np.cumsum((0,) + IN_SIZES)[:-1])
(O_NAQ, O_NAK, O_NAV, O_DFQ, O_DFK, O_DFV, O_GQQ, O_GQK, O_GQV, O_MLQA, O_MLKVA, O_MLKR, O_GATES) = IN_OFFSETS

LANES = 128
ROW_TILE = 256
MASK_VALUE = -1e30
VMEM_LIMIT = 56 * 1024 * 1024
LOG2E = math.log2(math.e)
KEY_CHUNK = 256
ACC_ROWS = 16
SCORE_SPLIT = 2

_PROJ_GROUPS = (("na_q", 256), ("na_k", 256), ("na_v", 256),
                ("df_q", 256), ("df_k", 256), ("df_v", 256),
                ("gq_q", 256), ("gq_k", 128), ("gq_v", 128),
                ("ml_qa", 256), ("ml_kva", 128), ("ml_kr", 32))
_PROJ_OFF = {}
_o = 0
for _n, _r in _PROJ_GROUPS:
    _PROJ_OFF[_n] = (_o, _r)
    _o += _r
PROJ_ROWS = _o


def _cparams(sem):
    return pltpu.CompilerParams(dimension_semantics=sem, vmem_limit_bytes=VMEM_LIMIT)


def _swap_cols(w, n, signed=True):
    lead, width = w.shape[:-1], w.shape[-1]
    w5 = w.reshape(lead + (width // n, 2, 2, n // 4))
    lo, hi = w5[..., 0, :], w5[..., 1, :]
    out = jnp.stack([-hi if signed else hi, lo], axis=-2)
    return out.reshape(lead + (width,))


def _rope_tables(n, ctx_len, seq):
    h = n // 2
    t = np.arange(seq)
    rows, cols = (t // GRID_W).astype(np.float32), (t % GRID_W).astype(np.float32)
    inv_freq = (np.float32(ROPE_BASE) ** (-np.arange(0, h, 2, dtype=np.float32) / np.float32(h))).astype(np.float32)
    ang_r = rows[:, None] * inv_freq[None, :]
    ang_c = cols[:, None] * inv_freq[None, :]
    ang = np.concatenate([ang_r, ang_r, ang_c, ang_c], axis=1)
    cos = np.concatenate([np.ones((ctx_len, n), np.float32), np.cos(ang)], axis=0).T
    sin = np.concatenate([np.zeros((ctx_len, n), np.float32), np.sin(ang)], axis=0).T
    return jnp.asarray(np.stack([cos, sin]).astype(np.float32))


def _ada_kernel(c_ref, w_ref, b_ref, o_ref):
    cc = c_ref[...]
    h = (cc * jax.nn.sigmoid(cc)).astype(BF16)
    o_ref[0] = jnp.dot(h, w_ref[0].astype(BF16), preferred_element_type=F32) + b_ref[0]


def _ada(cc, w_ada, b_ada):
    L, D, N = w_ada.shape
    R = cc.shape[0]
    tn = 512
    return pl.pallas_call(
        _ada_kernel,
        out_shape=jax.ShapeDtypeStruct((L, R, N), F32),
        grid=(L, N // tn),
        in_specs=[pl.BlockSpec((R, D), lambda l, j: (0, 0)),
                  pl.BlockSpec((1, D, tn), lambda l, j: (l, 0, j)),
                  pl.BlockSpec((1, 1, tn), lambda l, j: (l, 0, j))],
        out_specs=pl.BlockSpec((1, R, tn), lambda l, j: (l, 0, j)),
        compiler_params=_cparams(("arbitrary", "arbitrary")),
        name="ada_mod",
    )(cc, w_ada, b_ada.reshape(L, 1, N))


def _token_tile(c_ref, x_ref, tile, ctx_tiles):
    return jnp.where(tile < ctx_tiles, c_ref[0], x_ref[0])


def _proj_kernel(c_ref, x_ref, sc_ref, sh_ref, w_ref, cs32_ref, cs64_ref, gqg_ref, qag_ref, kvag_ref, wq_ref, wkv_ref,
                 na_q, na_k, na_v, df_q, df_k, df_v, gq_q, gq_k, gq_v, ml_q, ml_k, ml_v, *, ctx_tiles):
    tm = x_ref.shape[1]
    x_in = _token_tile(c_ref, x_ref, pl.program_id(1), ctx_tiles)
    h = (x_in * (1.0 + sc_ref[...]) + sh_ref[...]).astype(BF16)
    p = lax.dot_general(w_ref[...], h, (((1,), (1,)), ((), ())), preferred_element_type=F32)

    def grp(name):
        o, r = _PROJ_OFF[name]
        return p[o:o + r]

    def zeros(n):
        return jnp.zeros((n, tm), F32)

    def swap(a, n):
        r, q = a.shape[0], n // 4
        a4 = a.reshape(r // (2 * q), 2, q, tm)
        return jnp.concatenate([-a4[:, 1:2], a4[:, 0:1]], axis=1).reshape(r, tm)

    def rope(a, cs_ref):
        n = cs_ref.shape[1]
        r = a.shape[0]
        a3, s3 = a.reshape(r // n, n, tm), swap(a, n).reshape(r // n, n, tm)
        return (a3 * cs_ref[0][None] + s3 * cs_ref[1][None]).reshape(r, tm)

    def head_slots(a, width):
        pieces = []
        for i in range(a.shape[0] // width):
            pieces += [a[i * width:(i + 1) * width], zeros(LANES - width)]
        return jnp.concatenate(pieces, axis=0)

    na_q[0] = head_slots(grp("na_q") * (NA_DIM ** -0.5), NA_DIM).astype(BF16)
    na_k[0] = head_slots(grp("na_k"), NA_DIM).T.astype(BF16)
    na_v[0] = grp("na_v").astype(BF16)

    q = rope(grp("df_q"), cs32_ref) * (DF_QK ** -0.5 * LOG2E)
    pieces = []
    for i in range(DF_HEADS):
        q1, q2 = q[i * 64:i * 64 + 32], q[i * 64 + 32:i * 64 + 64]
        pieces += [q1, zeros(LANES - 32), zeros(32), q2, zeros(LANES - 64)]
    df_q[0] = jnp.concatenate(pieces, axis=0).astype(BF16)
    df_k[0] = head_slots(rope(grp("df_k"), cs32_ref), 2 * DF_QK).T.astype(BF16)
    df_v[0] = grp("df_v").astype(BF16)

    cos64, sin64 = cs64_ref[0], cs64_ref[1]

    def norm_rope(a, g, g_sw, scale):
        hh = a.shape[0] // GQ_DIM
        a3, s3 = a.reshape(hh, GQ_DIM, tm), swap(a, GQ_DIM).reshape(hh, GQ_DIM, tm)
        r = lax.rsqrt(jnp.mean(a3 * a3, axis=1, keepdims=True) + EPS) * scale
        return ((a3 * (g * cos64)[None] + s3 * (g_sw * sin64)[None]) * r).reshape(hh * GQ_DIM, tm)

    gq_q[0] = head_slots(norm_rope(grp("gq_q"), gqg_ref[0], gqg_ref[1], GQ_DIM ** -0.5 * LOG2E),
                         GQ_DIM).astype(BF16)
    gq_k[0] = head_slots(norm_rope(grp("gq_k"), gqg_ref[2], gqg_ref[3], 1.0), GQ_DIM).T.astype(BF16)
    gq_v[0] = grp("gq_v").astype(BF16)

    qa = grp("ml_qa")
    qan = (qa * lax.rsqrt(jnp.mean(qa * qa, axis=0, keepdims=True) + EPS) * qag_ref[...]).astype(BF16)
    qq = jnp.dot(wq_ref[...], qan, preferred_element_type=F32)
    kva = grp("ml_kva")
    kvan = (kva * lax.rsqrt(jnp.mean(kva * kva, axis=0, keepdims=True) + EPS) * kvag_ref[...]).astype(BF16)
    kv = jnp.dot(wkv_ref[...], kvan, preferred_element_type=F32)
    k_rope = rope(grp("ml_kr"), cs32_ref)
    ml_scale = (ML_NOPE + ML_ROPE) ** -0.5 * LOG2E
    qd = ML_NOPE + ML_ROPE
    qp, kp, vp = [], [], []
    for i in range(ML_HEADS):
        b = i * LANES
        q_rope = rope(qq[i * qd + ML_NOPE:(i + 1) * qd], cs32_ref)
        qp += [qq[i * qd:i * qd + ML_NOPE] * ml_scale, q_rope * ml_scale, zeros(32)]
        kp += [kv[b:b + 64], k_rope, zeros(32)]
        vp += [kv[b + 64:b + 128]]
    ml_q[0] = jnp.concatenate(qp, axis=0).astype(BF16)
    ml_k[0] = jnp.concatenate(kp, axis=0).T.astype(BF16)
    ml_v[0] = jnp.concatenate(vp, axis=0).astype(BF16)


def _token_specs(ctx_tiles, x_tile_off, tile_of):
    def ctx_map(*g):
        return tile_of(*g)[0], 0, 0

    def lat_map(*g):
        b, t = tile_of(*g)
        return b, jnp.maximum(t - ctx_tiles, 0) + x_tile_off, 0

    return (pl.BlockSpec((1, ROW_TILE, D_MODEL), ctx_map), pl.BlockSpec((1, ROW_TILE, D_MODEL), lat_map))


def _project(ctx_src, x_src, x_tile_off, n_tok, mod, w_t, cs32, cs64, gq_g, qa_g, kva_g, wq_t, wkv_t, ctx_len):
    B, T, D = x_src.shape[0], n_tok, x_src.shape[2]
    tm = ROW_TILE
    nt = T // tm
    ctx_tiles = ctx_len // tm

    def kind(t):
        return jnp.where(t >= ctx_tiles, 1, 0)

    def mod_spec(j):
        return pl.BlockSpec((None, None, None, 1, D), lambda b, t: (b, kind(t), j, 0, 0))

    def const(a):
        nd = a.ndim
        return pl.BlockSpec(a.shape, lambda b, t: (0,) * nd)

    def fm(rows):
        return jax.ShapeDtypeStruct((B, rows, T), BF16), pl.BlockSpec((1, rows, tm), lambda b, t: (b, 0, t))

    def tmj(cols):
        return jax.ShapeDtypeStruct((B, T, cols), BF16), pl.BlockSpec((1, tm, cols), lambda b, t: (b, t, 0))

    outs = [fm(512), tmj(512), fm(256),
            fm(1024), tmj(512), fm(256),
            fm(512), tmj(256), fm(128),
            fm(512), tmj(512), fm(256)]
    return pl.pallas_call(
        functools.partial(_proj_kernel, ctx_tiles=ctx_tiles),
        out_shape=[o[0] for o in outs],
        grid=(B, nt),
        in_specs=[*_token_specs(ctx_tiles, x_tile_off, lambda b, t: (b, t)),
                  mod_spec(1), mod_spec(0),
                  const(w_t),
                  pl.BlockSpec((2, 32, tm), lambda b, t: (0, 0, t)),
                  pl.BlockSpec((2, 64, tm), lambda b, t: (0, 0, t)),
                  const(gq_g), const(qa_g), const(kva_g), const(wq_t), const(wkv_t)],
        out_specs=[o[1] for o in outs],
        compiler_params=_cparams(("arbitrary", "arbitrary")),
        name="in_proj",
    )(ctx_src, x_src, mod, mod, w_t, cs32, cs64, gq_g, qa_g, kva_g, wq_t, wkv_t)


def _softmax_pv(k, q_t, v_t, exp_fn):
    s = jnp.dot(k, q_t, preferred_element_type=F32)
    m = jnp.max(s, axis=0, keepdims=True)
    p = exp_fn(s - m)
    l = jnp.sum(p, axis=0, keepdims=True)
    o = jnp.dot(v_t, p.astype(BF16), preferred_element_type=F32)
    return o, l


def _dense_attn_kernel(lam_ref, post_ref, *refs, branches, ctx_len):
    nb = len(branches)
    qkv = [refs[3 * b:3 * b + 3] for b in range(nb)]
    o_refs = refs[3 * nb:4 * nb]
    ot_refs = refs[4 * nb:5 * nb]
    s_ref, p_ref = refs[5 * nb:]
    total, tq = qkv[0][1].shape[1], qkv[0][0].shape[2]
    items = [(b, qs, ks, v_idx, i, signed) for b, (heads, _) in enumerate(branches)
             for i, (terms, v_idx) in enumerate(heads) for qs, ks, signed in terms]
    last_of_head = {(it[0], it[4]): j for j, it in enumerate(items)}

    def finish_head(b, i, acc):
        if branches[b][1]:
            acc = acc * lax.rsqrt(jnp.mean(acc * acc, axis=0, keepdims=True) + EPS) * post_ref[...]
        ot_refs[b][i * 64:(i + 1) * 64, :] = acc

    def add_term(head_acc, j, o):
        b, i = items[j][0], items[j][4]
        head_acc[b, i] = o if (b, i) not in head_acc else head_acc[b, i] + o
        if last_of_head[b, i] == j:
            finish_head(b, i, head_acc.pop((b, i)))

    def coef(signed):
        return (-lam_ref[0]) if signed else 1.0

    def run_context():
        head_acc = {}
        for j, (b, qs, ks, v_idx, i, signed) in enumerate(items):
            q_ref, k_ref, v_ref = qkv[b]
            o, l = _softmax_pv(k_ref[0, 0:ctx_len, ks * LANES:(ks + 1) * LANES], q_ref[0, qs * LANES:(qs + 1) * LANES, :],
                               v_ref[0, v_idx * 64:(v_idx + 1) * 64, 0:ctx_len], jnp.exp2)
            add_term(head_acc, j, o * (coef(signed) / l))

    def run_latent():
        n_chunks = total // KEY_CHUNK
        half = KEY_CHUNK // SCORE_SPLIT
        n_items = len(items)
        m_of, l_of, head_acc = {}, {}, {}
        for t in range(n_items + 2):
            sc = items[t] if t < n_items else None
            ex = items[t - 1] if 0 <= t - 1 < n_items else None
            pv = items[t - 2] if 0 <= t - 2 < n_items else None
            macc = jnp.full((ACC_ROWS, tq), -jnp.inf, F32)
            lacc = jnp.zeros((ACC_ROWS, tq), F32)
            oacc = jnp.zeros((64, tq), F32)
            for c in range(n_chunks):
                r0 = c * KEY_CHUNK
                if sc is not None:
                    q_ref, k_ref, _ = qkv[sc[0]]
                    q_t = q_ref[0, sc[1] * LANES:(sc[1] + 1) * LANES, :]
                    for r in range(r0, r0 + KEY_CHUNK, half):
                        s = jnp.dot(k_ref[0, r:r + half, sc[2] * LANES:(sc[2] + 1) * LANES], q_t,
                                    preferred_element_type=F32)
                        s_ref[t % 2, r:r + half, :] = s
                        for a in range(half // ACC_ROWS):
                            macc = jnp.maximum(macc, s[a * ACC_ROWS:(a + 1) * ACC_ROWS])
                if ex is not None:
                    p = jnp.exp2(s_ref[(t - 1) % 2, r0:r0 + KEY_CHUNK, :] - m_of[t - 1])
                    for a in range(KEY_CHUNK // ACC_ROWS):
                        lacc = lacc + p[a * ACC_ROWS:(a + 1) * ACC_ROWS]
                    p_ref[(t - 1) % 2, r0:r0 + KEY_CHUNK, :] = p.astype(BF16)
                if pv is not None:
                    v_ref = qkv[pv[0]][2]
                    oacc = oacc + jnp.dot(v_ref[0, pv[3] * 64:(pv[3] + 1) * 64, r0:r0 + KEY_CHUNK],
                                          p_ref[(t - 2) % 2, r0:r0 + KEY_CHUNK, :], preferred_element_type=F32)
            if sc is not None:
                m_of[t] = jnp.max(macc, axis=0, keepdims=True)
            if ex is not None:
                l_of[t - 1] = jnp.sum(lacc, axis=0, keepdims=True)
            if pv is not None:
                add_term(head_acc, t - 2, oacc * (coef(pv[5]) / l_of.pop(t - 2)))

    def write_out():
        for o_ref, ot_ref in zip(o_refs, ot_refs):
            o_ref[0] = ot_ref[...].T.astype(BF16)

    qb = pl.program_id(1)

    @pl.when(qb * tq < ctx_len)
    def _():
        run_context()
        write_out()

    @pl.when(qb * tq >= ctx_len)
    def _():
        run_latent()
        write_out()


def _dense_attn(lam, post, branches, ctx_len):
    B, _, T = branches[0][0].shape
    tq = ROW_TILE
    nb = len(branches)
    once = pl.Buffered(1)
    in_specs = [pl.BlockSpec(memory_space=pltpu.SMEM), pl.BlockSpec(post.shape, lambda b, t: (0, 0))]
    args = [lam, post]
    for q_t, k, v_t, _, _ in branches:
        in_specs += [pl.BlockSpec((1, q_t.shape[1], tq), lambda b, t: (b, 0, t)),
                     pl.BlockSpec((1, T, k.shape[2]), lambda b, t: (b, 0, 0), pipeline_mode=once),
                     pl.BlockSpec((1, v_t.shape[1], T), lambda b, t: (b, 0, 0), pipeline_mode=once)]
        args += [q_t, k, v_t]
    kern = functools.partial(_dense_attn_kernel, branches=tuple((br[3], br[4]) for br in branches), ctx_len=ctx_len)
    return pl.pallas_call(
        kern,
        out_shape=[jax.ShapeDtypeStruct((B, T, BRANCH_W), BF16)] * nb,
        grid=(B, T // tq),
        in_specs=in_specs,
        out_specs=[pl.BlockSpec((1, tq, BRANCH_W), lambda b, t: (b, t, 0))] * nb,
        scratch_shapes=[pltpu.VMEM((BRANCH_W, tq), F32)] * nb
        + [pltpu.VMEM((2, T, tq), F32), pltpu.VMEM((2, T, tq), BF16)],
        compiler_params=_cparams(("arbitrary", "arbitrary")),
        name="dense_attn",
    )(*args)


def _na_plan(rows):
    kh, kw = min(NA_KH, rows), NA_KW
    q_rows = ROW_TILE // GRID_W
    win_rows = min(kh + q_rows, rows)
    nblk = rows // q_rows
    row_start = np.clip(np.arange(rows) - kh // 2, 0, rows - kh)
    col_start = np.clip(np.arange(GRID_W) - kw // 2, 0, GRID_W - kw)
    u0 = np.clip(np.arange(nblk) * q_rows - kh // 2, 0, rows - win_rows)
    wk, tq = win_rows * GRID_W, ROW_TILE
    kk, qq = np.arange(wk), np.arange(tq)
    kc, qc = kk % GRID_W, qq % GRID_W
    col_sel = (np.arange(2 * NA_KW - 1)[:, None, None]
               == (np.arange(GRID_W)[None, :, None] - np.arange(GRID_W)[None, None, :] + (NA_KW - 1))).astype(np.float32)
    types, type_of, row_sel_l, valid_l = {}, [], [], []
    for j in range(nblk):
        kr = u0[j] + kk // GRID_W
        qr = j * q_rows + qq // GRID_W
        rs = row_start[qr]
        valid = ((kr[:, None] >= rs[None, :]) & (kr[:, None] < rs[None, :] + kh)
                 & (kc[:, None] >= col_start[qc][None, :]) & (kc[:, None] < col_start[qc][None, :] + kw))
        assert valid.sum(axis=0).min() == kh * kw and valid.sum(axis=0).max() == kh * kw
        rel = (u0[j] + np.arange(win_rows))[:, None] - (j * q_rows + np.arange(q_rows))[None, :] + (NA_KH - 1)
        row_sel = (rel[:, :, None] == np.arange(2 * NA_KH - 1)[None, None, :]).astype(np.float32)
        key = (valid.tobytes(), row_sel.tobytes())
        if key not in types:
            types[key] = len(row_sel_l)
            row_sel_l.append(row_sel), valid_l.append(valid)
        type_of.append(types[key])
    assert all(int(u) * GRID_W % LANES == 0 for u in u0)
    return (np.asarray(u0, np.int32), np.asarray(type_of, np.int32), np.stack(row_sel_l), col_sel,
            np.stack(valid_l), wk)


def _na_kernel(u0_ref, ty_ref, q_ref, k_ref, v_ref, bias_ref, o_ref, ot_ref, *, ctx_len):
    del ty_ref
    wk = bias_ref.shape[2]
    j = pl.program_id(1)

    def finish():
        o_ref[0] = ot_ref[...].T.astype(BF16)

    @pl.when(j == 0)
    def _():
        for i in range(NA_HEADS):
            o, l = _softmax_pv(k_ref[0, 0:ctx_len, i * LANES:(i + 1) * LANES], q_ref[0, i * LANES:(i + 1) * LANES, :],
                               v_ref[0, i * 64:(i + 1) * 64, 0:ctx_len], jnp.exp)
            ot_ref[i * 64:(i + 1) * 64, :] = o * (1.0 / l)
        finish()

    @pl.when(j > 0)
    def _():
        ws = pl.multiple_of(ctx_len + u0_ref[j - 1] * GRID_W, LANES)

        def scores(i):
            q_t = q_ref[0, i * LANES:(i + 1) * LANES, :]
            s_c = jnp.dot(k_ref[0, 0:ctx_len, i * LANES:(i + 1) * LANES], q_t, preferred_element_type=F32)
            s_w = jnp.dot(k_ref[0, pl.ds(ws, wk), i * LANES:(i + 1) * LANES], q_t,
                          preferred_element_type=F32) + bias_ref[0, i]
            return s_c, s_w

        def fold(fn, acc, a):
            for r in range(0, a.shape[0], ACC_ROWS):
                acc = fn(acc, a[r:r + ACC_ROWS])
            return acc

        def softmax_pv(i, s_c, s_w):
            tq = s_c.shape[1]
            macc = fold(jnp.maximum, fold(jnp.maximum, jnp.full((ACC_ROWS, tq), -jnp.inf, F32), s_c), s_w)
            m = jnp.max(macc, axis=0, keepdims=True)
            p_c, p_w = jnp.exp(s_c - m), jnp.exp(s_w - m)
            lacc = fold(jnp.add, fold(jnp.add, jnp.zeros((ACC_ROWS, tq), F32), p_c), p_w)
            l = jnp.sum(lacc, axis=0, keepdims=True)
            o = (jnp.dot(v_ref[0, i * 64:(i + 1) * 64, 0:ctx_len], p_c.astype(BF16), preferred_element_type=F32)
                 + jnp.dot(v_ref[0, i * 64:(i + 1) * 64, pl.ds(ws, wk)], p_w.astype(BF16),
                           preferred_element_type=F32))
            ot_ref[i * 64:(i + 1) * 64, :] = o * (1.0 / l)

        pending = scores(0)
        for i in range(NA_HEADS):
            upcoming = scores(i + 1) if i + 1 < NA_HEADS else None
            softmax_pv(i, *pending)
            pending = upcoming
        finish()


def _na_attn(q_t, k, v_t, rpb, ctx_len):
    B, _, T = q_t.shape
    tq = ROW_TILE
    rows = (T - ctx_len) // GRID_W
    u0, type_of, row_sel, col_sel, valid, wk = _na_plan(rows)
    hp = lax.Precision.HIGHEST
    by_col = jnp.einsum("hab,bcq->hacq", rpb.astype(F32), col_sel, precision=hp)
    bias = jnp.einsum("tkra,hacq->thkcrq", row_sel, by_col, precision=hp)
    bias = jnp.where(valid[:, None], bias.reshape(row_sel.shape[0], NA_HEADS, wk, tq), MASK_VALUE)
    nb = T // tq
    assert ctx_len == tq

    grid_spec = pltpu.PrefetchScalarGridSpec(
        num_scalar_prefetch=2,
        grid=(B, nb),
        in_specs=[pl.BlockSpec((1, NA_HEADS * LANES, tq), lambda b, j, u, ty: (b, 0, j)),
                  pl.BlockSpec((1, T, NA_HEADS * LANES), lambda b, j, u, ty: (b, 0, 0)),
                  pl.BlockSpec((1, BRANCH_W, T), lambda b, j, u, ty: (b, 0, 0)),
                  pl.BlockSpec((1, NA_HEADS, wk, tq), lambda b, j, u, ty: (ty[jnp.maximum(j - 1, 0)], 0, 0, 0))],
        out_specs=pl.BlockSpec((1, tq, BRANCH_W), lambda b, j, u, ty: (b, j, 0)),
        scratch_shapes=[pltpu.VMEM((BRANCH_W, tq), F32)])
    return pl.pallas_call(
        functools.partial(_na_kernel, ctx_len=ctx_len),
        out_shape=jax.ShapeDtypeStruct((B, T, BRANCH_W), BF16),
        grid_spec=grid_spec,
        compiler_params=_cparams(("arbitrary", "arbitrary")),
        name="na_attn",
    )(jnp.asarray(u0), jnp.asarray(type_of), q_t, k, v_t, bias)


def _merge_kernel(c_ref, x_ref, sc1_ref, sh1_ref, g1_ref, sc2_ref, sh2_ref, ona_ref, odf_ref, ogq_ref, oml_ref,
                  wg_ref, wb_ref, wo_ref, lng_ref, lnb_ref, rw_ref, rb_ref, tri_ref,
                  x1_ref, h2_ref, idx_ref, wts_ref, rank_ref, cnt_ref, carry_ref, hprev_ref, *, abs_tile, ctx_tiles):
    tm = x_ref.shape[1]
    step = pl.program_id(0)

    @pl.when(step == 0)
    def _():
        carry_ref[...] = jnp.zeros_like(carry_ref)
        hprev_ref[...] = jnp.zeros_like(hprev_ref)

    iota = lax.broadcasted_iota(I32, (N_EXPERTS, tm), 0)
    route = {}

    def route_logits():
        route["cur"] = lax.dot_general(rw_ref[...], hprev_ref[...], (((1,), (1,)), ((), ())),
                                       preferred_element_type=F32, precision=lax.Precision.HIGHEST) + rb_ref[...]
        route["vals"], route["idxs"] = [], []

    def route_pick():
        cur = route["cur"]
        m = jnp.max(cur, axis=0, keepdims=True)
        ik = jnp.min(jnp.where(cur == m, iota, N_EXPERTS), axis=0, keepdims=True)
        route["vals"].append(m)
        route["idxs"].append(ik)
        route["cur"] = jnp.where(iota == ik, -jnp.inf, cur)

    def route_finish():
        vals, idxs = route["vals"], route["idxs"]
        exps = [jnp.exp(v - vals[0]) for v in vals]
        denom = exps[0] + exps[1] + exps[2] + exps[3]
        wts_ref[0] = jnp.concatenate([e / denom for e in exps], axis=0)
        idx_ref[0] = jnp.concatenate(idxs, axis=0)
        live = jnp.where(step >= 1, 1.0, 0.0)
        onehot = jnp.zeros((N_EXPERTS, tm), F32)
        for ik in idxs:
            onehot = onehot + jnp.where(iota == ik, live, 0.0)
        before = jnp.dot(onehot.astype(BF16), tri_ref[...], preferred_element_type=F32) + carry_ref[...]
        rank_ref[0] = jnp.concatenate(
            [jnp.sum(jnp.where(iota == ik, before, 0.0), axis=0, keepdims=True) for ik in idxs], axis=0).astype(I32)
        carry = carry_ref[...] + jnp.sum(onehot, axis=1, keepdims=True)
        carry_ref[...] = carry
        cnt_ref[...] = carry.astype(I32)

    route_pieces = [[route_logits], [route_pick, route_pick], [route_pick, route_pick], [route_finish]]

    x = _token_tile(c_ref, x_ref, abs_tile(step)[1], ctx_tiles)
    h1 = (x * (1.0 + sc1_ref[...]) + sh1_ref[...]).astype(BF16)
    y = None
    for i, o_ref in enumerate((ona_ref, odf_ref, ogq_ref, oml_ref)):
        gate = jax.nn.sigmoid(jnp.dot(h1, wg_ref[:, i * D_MODEL:(i + 1) * D_MODEL], preferred_element_type=F32))
        term = gate * jnp.dot(o_ref[0], wb_ref[i * BRANCH_W:(i + 1) * BRANCH_W, :], preferred_element_type=F32)
        y = term if y is None else y + term
        for piece in route_pieces[i]:
            piece()
    z = jnp.dot(y.astype(BF16), wo_ref[...], preferred_element_type=F32)
    r = DEEPNORM_ALPHA * x + g1_ref[...] * z
    mu = jnp.mean(r, axis=-1, keepdims=True)
    var = jnp.mean(jnp.square(r - mu), axis=-1, keepdims=True)
    x1 = (r - mu) * lax.rsqrt(var + EPS) * lng_ref[...] + lnb_ref[...]
    x1_ref[0] = x1
    h2 = x1 * (1.0 + sc2_ref[...]) + sh2_ref[...]
    for j in range(D_MODEL // LANES):
        h2_ref[:, j, :] = h2[:, j * LANES:(j + 1) * LANES]
    hprev_ref[...] = h2


def _merge(ctx_src, x_src, x_tile_off, mod, o_na, o_df, o_gq, o_ml, wg, wb, wo, ln_g, ln_b, rw_t, rb, ctx_len, t_off):
    (B, T, _), D = o_na.shape, x_src.shape[2]
    tm = ROW_TILE
    nt = T // tm - t_off
    ctx_tiles = ctx_len // tm
    tq_rows = nt * tm
    n_tiles = B * nt

    def tile(i, lag):
        return jnp.clip(i - lag, 0, n_tiles - 1)

    def bt(i, lag):
        ti = tile(i, lag)
        return ti // nt, ti % nt

    def kind(t):
        return jnp.where(t + t_off >= ctx_tiles, 1, 0)

    def mod_spec(j, lag):
        return pl.BlockSpec((None, None, None, 1, D), lambda i: (bt(i, lag)[0], kind(bt(i, lag)[1]), j, 0, 0))

    def const(a):
        nd = a.ndim
        return pl.BlockSpec(a.shape, lambda i: (0,) * nd)

    def tok(cols):
        return pl.BlockSpec((1, tm, cols), lambda i: (bt(i, 0)[0], bt(i, 0)[1] + t_off, 0))

    tri = jnp.asarray(np.triu(np.ones((tm, tm), np.float32), 1), BF16)
    rb_b = jnp.broadcast_to(rb[:, None], (N_EXPERTS, tm)).astype(F32)
    route_shape = jax.ShapeDtypeStruct((n_tiles, TOP_K, tm), I32)
    route_spec = pl.BlockSpec((1, TOP_K, tm), lambda i: (tile(i, 1), 0, 0))
    def abs_tile(i):
        return bt(i, 0)[0], bt(i, 0)[1] + t_off

    return pl.pallas_call(
        functools.partial(_merge_kernel, abs_tile=abs_tile, ctx_tiles=ctx_tiles),
        out_shape=[jax.ShapeDtypeStruct((B, tq_rows, D), F32),
                   jax.ShapeDtypeStruct((B * tq_rows, D // LANES, LANES), F32),
                   route_shape, jax.ShapeDtypeStruct((n_tiles, TOP_K, tm), F32), route_shape,
                   jax.ShapeDtypeStruct((N_EXPERTS, tm), I32)],
        grid=(n_tiles + 1,),
        in_specs=[*_token_specs(ctx_tiles, x_tile_off, abs_tile),
                  mod_spec(1, 0), mod_spec(0, 0), mod_spec(2, 0), mod_spec(4, 0), mod_spec(3, 0),
                  tok(BRANCH_W), tok(BRANCH_W), tok(BRANCH_W), tok(BRANCH_W),
                  const(wg), const(wb), const(wo), const(ln_g), const(ln_b), const(rw_t), const(rb_b), const(tri)],
        out_specs=[pl.BlockSpec((1, tm, D), lambda i: (bt(i, 0)[0], bt(i, 0)[1], 0)),
                   pl.BlockSpec((tm, D // LANES, LANES), lambda i: (tile(i, 0), 0, 0)),
                   route_spec, route_spec, route_spec,
                   pl.BlockSpec((N_EXPERTS, tm), lambda i: (0, 0))],
        scratch_shapes=[pltpu.VMEM((N_EXPERTS, tm), F32), pltpu.VMEM((tm, D), F32)],
        compiler_params=_cparams(("arbitrary",)),
        name="merge_router",
    )(ctx_src, x_src, mod, mod, mod, mod, mod, o_na, o_df, o_gq, o_ml, wg, wb, wo, ln_g, ln_b, rw_t, rb_b, tri)


def _row_copy(src, src_row, dst, dst_row, sem):
    return pltpu.make_async_copy(src.at[pl.ds(src_row, 1)], dst.at[pl.ds(dst_row, 1)], sem)


def _dispatch_kernel(dest_ref, h_ref, xs_in_ref, xs_ref, sem):
    del xs_in_ref
    tm = dest_ref.shape[2]

    def issue(t, c):
        for k in range(TOP_K):
            _row_copy(h_ref, t, xs_ref, dest_ref[0, k, t], sem).start(priority=k % 2)
        return c

    lax.fori_loop(0, tm, issue, 0, unroll=2)
    for k in range(TOP_K):
        pltpu.make_async_copy(h_ref, xs_ref.at[pl.ds(0, tm)], sem).wait()


def _dispatch(dest, h2, xs0):
    n_tiles, _, tm = dest.shape
    return pl.pallas_call(
        _dispatch_kernel,
        out_shape=jax.ShapeDtypeStruct(xs0.shape, F32),
        grid=(n_tiles,),
        in_specs=[pl.BlockSpec((1, TOP_K, tm), lambda i: (i, 0, 0), memory_space=pltpu.SMEM),
                  pl.BlockSpec((tm,) + h2.shape[1:], lambda i: (i, 0, 0)),
                  pl.BlockSpec(memory_space=pl.ANY)],
        out_specs=pl.BlockSpec(memory_space=pl.ANY),
        scratch_shapes=[pltpu.SemaphoreType.DMA(())],
        input_output_aliases={2: 0},
        compiler_params=_cparams(("arbitrary",)),
        name="moe_dispatch",
    )(dest, h2, xs0)


def _tile_relayout_copies(tiled_hbm, flat_vmem, tile, sem):
    tm = flat_vmem.shape[0]
    r0 = pl.multiple_of(tile * tm, tm)
    return [(tiled_hbm.at[pl.ds(r0, tm), j, :], flat_vmem.at[:, pl.ds(j * LANES, LANES)], sem)
            for j in range(D_MODEL // LANES)]


def _expert_kernel(te_ref, nu_ref, xs_ref, w1_ref, b1_ref, w2_ref, b2_ref, y_ref, w1s_ref, w2s_ref, x_buf, y_buf,
                   sem_in, sem_out, *, n_steps):
    i = pl.program_id(0)
    n_used = nu_ref[0]
    slot = lax.rem(i, 2)

    def fetch(tile, s, wait):
        for hbm, vmem, sem in _tile_relayout_copies(xs_ref, x_buf.at[s], tile, sem_in.at[s]):
            cp = pltpu.make_async_copy(hbm, vmem, sem)
            cp.wait() if wait else cp.start()

    def write_back(tile, s, wait):
        for hbm, vmem, sem in _tile_relayout_copies(y_ref, y_buf.at[s], tile, sem_out.at[s]):
            cp = pltpu.make_async_copy(vmem, hbm, sem)
            cp.wait() if wait else cp.start(priority=1)

    @pl.when(i == 0)
    def _():
        fetch(0, 0, False)

    @pl.when(i + 1 < n_used)
    def _():
        fetch(i + 1, 1 - slot, False)

    @pl.when((i == 0) | (te_ref[i] != te_ref[jnp.maximum(i - 1, 0)]))
    def _():
        w1s_ref[...] = w1_ref[...].astype(BF16)
        w2s_ref[...] = w2_ref[...].astype(BF16)

    @pl.when(i >= 2)
    def _():
        write_back(i - 2, slot, True)

    @pl.when(i < n_used)
    def _():
        fetch(i, slot, True)
        gu = jnp.dot(x_buf[slot].astype(BF16), w1s_ref[...], preferred_element_type=F32) + b1_ref[...]
        gate = jnp.minimum(gu[:, :D_FF], SWIGLU_LIMIT)
        up = jnp.clip(gu[:, D_FF:], -SWIGLU_LIMIT, SWIGLU_LIMIT)
        act = (up + 1.0) * gate * jax.nn.sigmoid(SWIGLU_ALPHA * gate)
        y_buf[slot] = jnp.dot(act.astype(BF16), w2s_ref[...], preferred_element_type=F32) + b2_ref[...]

    @pl.when(i >= n_used)
    def _():
        y_buf[slot] = jnp.zeros(y_buf.shape[1:], F32)

    write_back(i, slot, False)

    @pl.when(i == n_steps - 1)
    def _():
        write_back(i, slot, True)

        @pl.when(i >= 1)
        def _():
            write_back(i - 1, 1 - slot, True)


def _experts(tile_expert, n_used, xs, w1, b1, w2, b2, layer):
    n_slots = xs.shape[0]
    tm = ROW_TILE
    nt = n_slots // tm
    E = w1.shape[1]
    grid_spec = pltpu.PrefetchScalarGridSpec(
        num_scalar_prefetch=2, grid=(nt,),
        in_specs=[pl.BlockSpec(memory_space=pl.ANY),
                  pl.BlockSpec((None, None, D_MODEL, 2 * D_FF), lambda i, te, nu: (layer, te[i], 0, 0)),
                  pl.BlockSpec((None, None, 1, 2 * D_FF), lambda i, te, nu: (layer, te[i], 0, 0)),
                  pl.BlockSpec((None, None, D_FF, D_MODEL), lambda i, te, nu: (layer, te[i], 0, 0)),
                  pl.BlockSpec((None, None, 1, D_MODEL), lambda i, te, nu: (layer, te[i], 0, 0))],
        out_specs=pl.BlockSpec(memory_space=pl.ANY),
        scratch_shapes=[pltpu.VMEM((D_MODEL, 2 * D_FF), BF16), pltpu.VMEM((D_FF, D_MODEL), BF16),
                        pltpu.VMEM((2, tm, D_MODEL), F32), pltpu.VMEM((2, tm, D_MODEL), F32),
                        pltpu.SemaphoreType.DMA((2,)), pltpu.SemaphoreType.DMA((2,))])
    return pl.pallas_call(
        functools.partial(_expert_kernel, n_steps=nt),
        out_shape=jax.ShapeDtypeStruct(xs.shape, F32),
        grid_spec=grid_spec,
        compiler_params=_cparams(("arbitrary",)),
        name="moe_experts",
    )(tile_expert, n_used, xs, w1, b1.reshape(b1.shape[0], E, 1, -1), w2, b2.reshape(b2.shape[0], E, 1, -1))


def _combine_kernel(dest_ref, destn_ref, w_ref, x1_ref, g2_ref, lng_ref, lnb_ref, y_ref, o_ref, buf_ref, y3_ref,
                    sem, *, n_steps):
    tm = dest_ref.shape[2]
    i = pl.program_id(0)
    n = n_steps
    slot = lax.rem(i, 2)

    def issue(dr, s, t):
        for k in range(TOP_K):
            pltpu.make_async_copy(y_ref.at[pl.ds(dr[0, k, t], 1)], buf_ref.at[s, k, pl.ds(t, 1)],
                                  sem.at[s]).start(priority=k % 2)

    def weigh(t):
        acc = buf_ref[slot, 0, t] * w_ref[0, 0, t]
        for k in range(1, TOP_K):
            acc = acc + buf_ref[slot, k, t] * w_ref[0, k, t]
        y3_ref[t] = acc

    def loop(body):
        lax.fori_loop(0, tm, lambda t, c: (body(t), c)[1], 0, unroll=8)

    @pl.when(i == 0)
    def _():
        loop(lambda t: issue(dest_ref, 0, t))

    for k in range(TOP_K):
        pltpu.make_async_copy(y_ref.at[pl.ds(0, tm)], buf_ref.at[slot, k], sem.at[slot]).wait()

    @pl.when(i + 1 < n)
    def _():
        loop(lambda t: (issue(destn_ref, 1 - slot, t), weigh(t)))

    @pl.when(i + 1 >= n)
    def _():
        loop(weigh)
    y2 = jnp.concatenate([y3_ref[:, j, :] for j in range(D_MODEL // LANES)], axis=1)
    r = DEEPNORM_ALPHA * x1_ref[0] + g2_ref[...] * y2
    mu = jnp.mean(r, axis=-1, keepdims=True)
    var = jnp.mean(jnp.square(r - mu), axis=-1, keepdims=True)
    o_ref[0] = (r - mu) * lax.rsqrt(var + EPS) * lng_ref[...] + lnb_ref[...]


def _combine(dest, wts, x1, mod, ln_g, ln_b, y, ctx_len, t_off):
    B, tq_rows, D = x1.shape
    n_tiles, _, tm = dest.shape
    nt = n_tiles // B
    ctx_tiles = ctx_len // tm

    def kind(i):
        return jnp.where(i % nt + t_off >= ctx_tiles, 1, 0)

    smem = lambda f: pl.BlockSpec((1, TOP_K, tm), f, memory_space=pltpu.SMEM)
    cur = lambda i: (i, 0, 0)
    nxt = lambda i: (jnp.minimum(i + 1, n_tiles - 1), 0, 0)
    return pl.pallas_call(
        functools.partial(_combine_kernel, n_steps=n_tiles),
        out_shape=jax.ShapeDtypeStruct((B, tq_rows, D), F32),
        grid=(n_tiles,),
        in_specs=[smem(cur), smem(nxt), smem(cur),
                  pl.BlockSpec((1, tm, D), lambda i: (i // nt, i % nt, 0)),
                  pl.BlockSpec((None, None, None, 1, D), lambda i: (i // nt, kind(i), 5, 0, 0)),
                  pl.BlockSpec(ln_g.shape, lambda i: (0, 0)),
                  pl.BlockSpec(ln_b.shape, lambda i: (0, 0)),
                  pl.BlockSpec(memory_space=pl.ANY)],
        out_specs=pl.BlockSpec((1, tm, D), lambda i: (i // nt, i % nt, 0)),
        scratch_shapes=[pltpu.VMEM((2, TOP_K, tm, D // LANES, LANES), F32),
                        pltpu.VMEM((tm, D // LANES, LANES), F32), pltpu.SemaphoreType.DMA((2,))],
        compiler_params=_cparams(("arbitrary",)),
        name="moe_combine",
    )(dest, dest, wts, x1, mod, ln_g, ln_b, y)


_DF_HEADS_SPEC = tuple((((2 * i, i, False), (2 * i + 1, i, True)), i) for i in range(DF_HEADS))
_GQ_HEADS_SPEC = tuple((((i, i // (GQ_HEADS // GQ_KV_HEADS), False),), i // (GQ_HEADS // GQ_KV_HEADS))
                       for i in range(GQ_HEADS))
_ML_HEADS_SPEC = tuple((((i, i, False),), i) for i in range(ML_HEADS))


def _layer_weights(l, w_in, gq_qnorm, gq_knorm, ml_qa_norm, ml_wq_b, ml_kva_norm, ml_wkv_b):
    w_t = w_in[l][:, :O_GATES].T.astype(BF16)
    assert w_t.shape[0] == PROJ_ROWS

    tm = ROW_TILE
    gq_g = jnp.stack([gq_qnorm[l], _swap_cols(gq_qnorm[l], 64, signed=False),
                      gq_knorm[l], _swap_cols(gq_knorm[l], 64, signed=False)])
    gq_g = jnp.broadcast_to(gq_g[:, :, None], (4, GQ_DIM, tm)).astype(F32)
    qa_g = jnp.broadcast_to(ml_qa_norm[l][:, None], (ML_Q_RANK, tm)).astype(F32)
    kva_g = jnp.broadcast_to(ml_kva_norm[l][:, None], (ML_KV_RANK, tm)).astype(F32)
    wq_t = ml_wq_b[l].T.astype(BF16)
    wkv_t = ml_wkv_b[l].T.astype(BF16)
    return w_t, gq_g, qa_g, kva_g, wq_t, wkv_t


def kernel(x, c, ctx, c_ctx, w_ada, b_ada, w_in, na_rpb, df_lam, df_subln, gq_qnorm, gq_knorm, ml_qa_norm, ml_wq_b,
           ml_kva_norm, ml_wkv_b, w_branch, w_out, ln1_g, ln1_b, ln2_g, ln2_b, router_w, router_b, exp_w1, exp_b1,
           exp_w2, exp_b2):
    B, S, D = x.shape
    C = ctx.shape[1]
    T = C + S
    tm = ROW_TILE
    assert D == D_MODEL and C % tm == 0 and S % tm == 0 and C == tm

    ctx_src, x_src, x_tile_off = ctx, x, 0
    cc = jnp.concatenate([c, c_ctx[None], jnp.zeros((16 - B - 1, D), F32)], axis=0)
    mod_all = _ada(cc, w_ada, b_ada)
    cs32 = _rope_tables(32, C, S)
    cs64 = _rope_tables(64, C, S)

    xs = None
    for l in range(DEPTH):
        last = l == DEPTH - 1
        m = mod_all[l]
        mod = jnp.stack([jnp.broadcast_to(m[B][None], (B, 6 * D)), m[:B]], axis=1).reshape(B, 2, 6, 1, D)
        lam_init = 0.8 - 0.6 * math.exp(-0.3 * l)
        lp = df_lam[l].astype(F32)
        lam = (jnp.exp(jnp.sum(lp[0] * lp[1])) - jnp.exp(jnp.sum(lp[2] * lp[3])) + lam_init).reshape(1)
        w_t, gq_g, qa_g, kva_g, wq_t, wkv_t = _layer_weights(l, w_in, gq_qnorm, gq_knorm, ml_qa_norm, ml_wq_b,
                                                             ml_kva_norm, ml_wkv_b)
        (na_q, na_k, na_v, df_q, df_k, df_v, gq_q, gq_k, gq_v, ml_q, ml_k, ml_v) = _project(
            ctx_src, x_src, x_tile_off, T, mod, w_t, cs32, cs64, gq_g, qa_g, kva_g, wq_t, wkv_t, C)

        o_na = _na_attn(na_q, na_k, na_v, na_rpb[l], C)
        subln = jnp.broadcast_to((df_subln[l] * (1.0 - lam_init))[:, None], (DF_V, tm)).astype(F32)
        o_df, o_gq, o_ml = _dense_attn(lam, subln, [(df_q, df_k, df_v, _DF_HEADS_SPEC, True),
                                                    (gq_q, gq_k, gq_v, _GQ_HEADS_SPEC, False),
                                                    (ml_q, ml_k, ml_v, _ML_HEADS_SPEC, False)], C)

        t_off = C // tm if last else 0
        wg = w_in[l][:, O_GATES:].astype(BF16)
        wb = w_branch[l].reshape(N_BRANCH * BRANCH_W, D).astype(BF16)
        wo = w_out[l].astype(BF16)
        x1, h2, idx, wts, rank, cnt = _merge(ctx_src, x_src, x_tile_off, mod, o_na, o_df, o_gq, o_ml, wg, wb, wo,
                                             ln1_g[l][None], ln1_b[l][None], router_w[l].T, router_b[l], C, t_off)

        counts = cnt[:, 0]
        padded = ((counts + tm - 1) // tm) * tm
        ends = jnp.cumsum(padded)
        starts = (ends - padded).astype(I32)
        n_exp_tiles = (B * T * TOP_K) // tm + N_EXPERTS
        n_used = (ends[-1] // tm).astype(I32).reshape(1)
        tile_first = jnp.arange(n_exp_tiles, dtype=I32) * tm
        tile_expert = jnp.minimum(jnp.sum((ends[None, :] <= tile_first[:, None]).astype(I32), axis=1), N_EXPERTS - 1)
        tile_expert = jnp.where(tile_first < ends[-1], tile_expert, tile_expert[jnp.maximum(n_used[0] - 1, 0)])

        dest = jnp.sum(jnp.where(idx[..., None] == jnp.arange(N_EXPERTS, dtype=I32), starts, 0), axis=-1) + rank
        if xs is None:
            xs = jnp.zeros((n_exp_tiles * tm,) + h2.shape[1:], F32)
        xs = _dispatch(dest, h2, xs)
        ys = _experts(tile_expert, n_used, xs, exp_w1, exp_b1, exp_w2, exp_b2, l)
        out = _combine(dest, wts, x1, mod, ln2_g[l][None], ln2_b[l][None], ys, C, t_off)
        ctx_src, x_src, x_tile_off = out, out, C // tm
    return out
```

```python
import functools
import math

import numpy as np
import jax
import jax.numpy as jnp
from jax import lax
from jax.experimental import pallas as pl
from jax.experimental.pallas import tpu as pltpu

F32, BF16, I32 = jnp.float32, jnp.bfloat16, jnp.int32

D_MODEL = 1024
DEPTH = 2
GRID_W = 64
ROPE_BASE = 10000.0
EPS = 1e-6
NA_HEADS, NA_DIM, NA_KH, NA_KW = 4, 64, 8, 16
DF_HEADS, DF_QK = 4, 32
DF_V = 2 * DF_QK
GQ_HEADS, GQ_KV_HEADS, GQ_DIM = 4, 2, 64
ML_HEADS, ML_NOPE, ML_ROPE, ML_V, ML_Q_RANK, ML_KV_RANK = 4, 64, 32, 64, 256, 128
N_BRANCH, BRANCH_W = 4, 256
N_EXPERTS, TOP_K = 32, 4
D_FF = D_MODEL
SWIGLU_LIMIT, SWIGLU_ALPHA = 7.0, 1.702
DEEPNORM_ALPHA = (2 * DEPTH) ** 0.25

IN_SIZES = (256, 256, 256, 256, 256, 256, 256, 128, 128, ML_Q_RANK, ML_KV_RANK, ML_ROPE, N_BRANCH * D_MODEL)
IN_OFFSETS = tuple(int(o) for o in np.cumsum((0,) + IN_SIZES)[:-1])
(O_NAQ, O_NAK, O_NAV, O_DFQ, O_DFK, O_DFV, O_GQQ, O_GQK, O_GQV, O_MLQA, O_MLKVA, O_MLKR, O_GATES) = IN_OFFSETS

LANES = 128
ROW_TILE = 256
MASK_VALUE = -1e30
VMEM_LIMIT = 56 * 1024 * 1024
LOG2E = math.log2(math.e)
KEY_CHUNK = 256
ACC_ROWS = 16
SCORE_SPLIT = 2

_PROJ_GROUPS = (("na_q", 256), ("na_k", 256), ("na_v", 256),
                ("df_q", 256), ("df_k", 256), ("df_v", 256),
                ("gq_q", 256), ("gq_k", 128), ("gq_v", 128),
                ("ml_qa", 256), ("ml_kva", 128), ("ml_kr", 32))
_PROJ_OFF = {}
_o = 0
for _n, _r in _PROJ_GROUPS:
    _PROJ_OFF[_n] = (_o, _r)
    _o += _r
PROJ_ROWS = _o


def _cparams(sem):
    return pltpu.CompilerParams(dimension_semantics=sem, vmem_limit_bytes=VMEM_LIMIT)


def _partner_gain(g, n):
    g4 = g.reshape(2, 2, n // 4)
    return jnp.stack([g4[:, 1], g4[:, 0]], axis=1).reshape(n)


def _rope_tables(n, ctx_len, seq):
    h = n // 2
    t = np.arange(seq)
    rows, cols = (t // GRID_W).astype(np.float32), (t % GRID_W).astype(np.float32)
    inv_freq = (np.float32(ROPE_BASE) ** (-np.arange(0, h, 2, dtype=np.float32) / np.float32(h))).astype(np.float32)
    ang_r = rows[:, None] * inv_freq[None, :]
    ang_c = cols[:, None] * inv_freq[None, :]
    ang = np.concatenate([ang_r, ang_r, ang_c, ang_c], axis=1)
    cos = np.concatenate([np.ones((ctx_len, n), np.float32), np.cos(ang)], axis=0).T
    sin = np.concatenate([np.zeros((ctx_len, n), np.float32), np.sin(ang)], axis=0).T
    return jnp.asarray(np.stack([cos, sin]).astype(np.float32))


def _ada_kernel(c_ref, w_ref, b_ref, o_ref):
    cc = c_ref[...]
    h = (cc * jax.nn.sigmoid(cc)).astype(BF16)
    o_ref[0] = jnp.dot(h, w_ref[0].astype(BF16), preferred_element_type=F32) + b_ref[0]


def _ada(cc, w_ada, b_ada):
    L, D, N = w_ada.shape
    R = cc.shape[0]
    tn = 4 * LANES
    return pl.pallas_call(
        _ada_kernel,
        out_shape=jax.ShapeDtypeStruct((L, R, N), F32),
        grid=(L, N // tn),
        in_specs=[pl.BlockSpec((R, D), lambda l, j: (0, 0)),
                  pl.BlockSpec((1, D, tn), lambda l, j: (l, 0, j)),
                  pl.BlockSpec((1, 1, tn), lambda l, j: (l, 0, j))],
        out_specs=pl.BlockSpec((1, R, tn), lambda l, j: (l, 0, j)),
        compiler_params=_cparams(("arbitrary", "arbitrary")),
        name="ada_mod",
    )(cc, w_ada, b_ada.reshape(L, 1, N))


def _token_tile(c_ref, x_ref, tile, ctx_tiles):
    return jnp.where(tile < ctx_tiles, c_ref[0], x_ref[0])


def _proj_kernel(c_ref, x_ref, sc_ref, sh_ref, w_ref, cs32_ref, cs64_ref, gqg_ref, qag_ref, kvag_ref, wq_ref, wkv_ref,
                 na_q, na_k, na_v, df_q, df_k, df_v, gq_q, gq_k, gq_v, ml_q, ml_k, ml_v, *, ctx_tiles):
    tm = x_ref.shape[1]
    x_in = _token_tile(c_ref, x_ref, pl.program_id(1), ctx_tiles)
    h = (x_in * (1.0 + sc_ref[...]) + sh_ref[...]).astype(BF16)
    p = lax.dot_general(w_ref[...], h, (((1,), (1,)), ((), ())), preferred_element_type=F32)

    def grp(name):
        o, r = _PROJ_OFF[name]
        return p[o:o + r]

    def zeros(n):
        return jnp.zeros((n, tm), F32)

    def swap(a, n):
        r, q = a.shape[0], n // 4
        a4 = a.reshape(r // (2 * q), 2, q, tm)
        return jnp.concatenate([-a4[:, 1:2], a4[:, 0:1]], axis=1).reshape(r, tm)

    def rope(a, cs_ref):
        n = cs_ref.shape[1]
        r = a.shape[0]
        a3, s3 = a.reshape(r // n, n, tm), swap(a, n).reshape(r // n, n, tm)
        return (a3 * cs_ref[0][None] + s3 * cs_ref[1][None]).reshape(r, tm)

    def head_slots(a, width):
        pieces = []
        for i in range(a.shape[0] // width):
            pieces += [a[i * width:(i + 1) * width], zeros(LANES - width)]
        return jnp.concatenate(pieces, axis=0)

    na_q[0] = head_slots(grp("na_q") * (NA_DIM ** -0.5), NA_DIM).astype(BF16)
    na_k[0] = head_slots(grp("na_k"), NA_DIM).T.astype(BF16)
    na_v[0] = grp("na_v").astype(BF16)

    q = rope(grp("df_q"), cs32_ref) * (DF_QK ** -0.5 * LOG2E)
    pieces = []
    for i in range(DF_HEADS):
        q1, q2 = q[i * 64:i * 64 + 32], q[i * 64 + 32:i * 64 + 64]
        pieces += [q1, zeros(LANES - 32), zeros(32), q2, zeros(LANES - 64)]
    df_q[0] = jnp.concatenate(pieces, axis=0).astype(BF16)
    df_k[0] = head_slots(rope(grp("df_k"), cs32_ref), 2 * DF_QK).T.astype(BF16)
    df_v[0] = grp("df_v").astype(BF16)

    cos64, sin64 = cs64_ref[0], cs64_ref[1]

    def norm_rope(a, g, g_sw, scale):
        hh = a.shape[0] // GQ_DIM
        a3, s3 = a.reshape(hh, GQ_DIM, tm), swap(a, GQ_DIM).reshape(hh, GQ_DIM, tm)
        r = lax.rsqrt(jnp.mean(a3 * a3, axis=1, keepdims=True) + EPS) * scale
        return ((a3 * (g * cos64)[None] + s3 * (g_sw * sin64)[None]) * r).reshape(hh * GQ_DIM, tm)

    gq_q[0] = head_slots(norm_rope(grp("gq_q"), gqg_ref[0], gqg_ref[1], GQ_DIM ** -0.5 * LOG2E),
                         GQ_DIM).astype(BF16)
    gq_k[0] = head_slots(norm_rope(grp("gq_k"), gqg_ref[2], gqg_ref[3], 1.0), GQ_DIM).T.astype(BF16)
    gq_v[0] = grp("gq_v").astype(BF16)

    qa = grp("ml_qa")
    qan = (qa * lax.rsqrt(jnp.mean(qa * qa, axis=0, keepdims=True) + EPS) * qag_ref[...]).astype(BF16)
    qq = jnp.dot(wq_ref[...], qan, preferred_element_type=F32)
    kva = grp("ml_kva")
    kvan = (kva * lax.rsqrt(jnp.mean(kva * kva, axis=0, keepdims=True) + EPS) * kvag_ref[...]).astype(BF16)
    kv = jnp.dot(wkv_ref[...], kvan, preferred_element_type=F32)
    k_rope = rope(grp("ml_kr"), cs32_ref)
    ml_scale = (ML_NOPE + ML_ROPE) ** -0.5 * LOG2E
    qd = ML_NOPE + ML_ROPE
    qp, kp, vp = [], [], []
    for i in range(ML_HEADS):
        b = i * LANES
        q_rope = rope(qq[i * qd + ML_NOPE:(i + 1) * qd], cs32_ref)
        qp += [qq[i * qd:i * qd + ML_NOPE] * ml_scale, q_rope * ml_scale, zeros(32)]
        kp += [kv[b:b + 64], k_rope, zeros(32)]
        vp += [kv[b + 64:b + 128]]
    ml_q[0] = jnp.concatenate(qp, axis=0).astype(BF16)
    ml_k[0] = jnp.concatenate(kp, axis=0).T.astype(BF16)
    ml_v[0] = jnp.concatenate(vp, axis=0).astype(BF16)


def _token_specs(ctx_tiles, x_tile_off, tile_of):
    def ctx_map(*g):
        return tile_of(*g)[0], 0, 0

    def lat_map(*g):
        b, t = tile_of(*g)
        return b, jnp.maximum(t - ctx_tiles, 0) + x_tile_off, 0

    return (pl.BlockSpec((1, ROW_TILE, D_MODEL), ctx_map), pl.BlockSpec((1, ROW_TILE, D_MODEL), lat_map))


def _project(ctx_src, x_src, x_tile_off, n_tok, mod, w_t, cs32, cs64, gq_g, qa_g, kva_g, wq_t, wkv_t, ctx_len):
    B, T, D = x_src.shape[0], n_tok, x_src.shape[2]
    tm = ROW_TILE
    nt = T // tm
    ctx_tiles = ctx_len // tm

    def kind(t):
        return jnp.where(t >= ctx_tiles, 1, 0)

    def mod_spec(j):
        return pl.BlockSpec((None, None, None, 1, D), lambda b, t: (b, kind(t), j, 0, 0))

    def const(a):
        nd = a.ndim
        return pl.BlockSpec(a.shape, lambda b, t: (0,) * nd)

    def fm(rows):
        return jax.ShapeDtypeStruct((B, rows, T), BF16), pl.BlockSpec((1, rows, tm), lambda b, t: (b, 0, t))

    def tmj(cols):
        return jax.ShapeDtypeStruct((B, T, cols), BF16), pl.BlockSpec((1, tm, cols), lambda b, t: (b, t, 0))

    outs = [fm(512), tmj(512), fm(256),
            fm(1024), tmj(512), fm(256),
            fm(512), tmj(256), fm(128),
            fm(512), tmj(512), fm(256)]
    return pl.pallas_call(
        functools.partial(_proj_kernel, ctx_tiles=ctx_tiles),
        out_shape=[o[0] for o in outs],
        grid=(B, nt),
        in_specs=[*_token_specs(ctx_tiles, x_tile_off, lambda b, t: (b, t)),
                  mod_spec(1), mod_spec(0),
                  const(w_t),
                  pl.BlockSpec((2, 32, tm), lambda b, t: (0, 0, t)),
                  pl.BlockSpec((2, 64, tm), lambda b, t: (0, 0, t)),
                  const(gq_g), const(qa_g), const(kva_g), const(wq_t), const(wkv_t)],
        out_specs=[o[1] for o in outs],
        compiler_params=_cparams(("arbitrary", "arbitrary")),
        name="in_proj",
    )(ctx_src, x_src, mod, mod, w_t, cs32, cs64, gq_g, qa_g, kva_g, wq_t, wkv_t)


def _softmax_pv(k, q_t, v_t, exp_fn):
    s = jnp.dot(k, q_t, preferred_element_type=F32)
    m = jnp.max(s, axis=0, keepdims=True)
    p = exp_fn(s - m)
    l = jnp.sum(p, axis=0, keepdims=True)
    o = jnp.dot(v_t, p.astype(BF16), preferred_element_type=F32)
    return o, l


def _dense_attn_kernel(lam_ref, post_ref, *refs, branches, ctx_len):
    nb = len(branches)
    qkv = [refs[3 * b:3 * b + 3] for b in range(nb)]
    o_refs = refs[3 * nb:4 * nb]
    ot_refs = refs[4 * nb:5 * nb]
    s_ref, p_ref = refs[5 * nb:]
    total, tq = qkv[0][1].shape[1], qkv[0][0].shape[2]
    items = [(b, qs, ks, v_idx, i, signed) for b, (heads, _) in enumerate(branches)
             for i, (terms, v_idx) in enumerate(heads) for qs, ks, signed in terms]
    last_of_head = {(it[0], it[4]): j for j, it in enumerate(items)}

    def finish_head(b, i, acc):
        if branches[b][1]:
            acc = acc * lax.rsqrt(jnp.mean(acc * acc, axis=0, keepdims=True) + EPS) * post_ref[...]
        ot_refs[b][i * 64:(i + 1) * 64, :] = acc

    def add_term(head_acc, j, o):
        b, i = items[j][0], items[j][4]
        head_acc[b, i] = o if (b, i) not in head_acc else head_acc[b, i] + o
        if last_of_head[b, i] == j:
            finish_head(b, i, head_acc.pop((b, i)))

    def coef(signed):
        return (-lam_ref[0]) if signed else 1.0

    def run_context():
        head_acc = {}
        for j, (b, qs, ks, v_idx, i, signed) in enumerate(items):
            q_ref, k_ref, v_ref = qkv[b]
            o, l = _softmax_pv(k_ref[0, 0:ctx_len, ks * LANES:(ks + 1) * LANES], q_ref[0, qs * LANES:(qs + 1) * LANES, :],
                               v_ref[0, v_idx * 64:(v_idx + 1) * 64, 0:ctx_len], jnp.exp2)
            add_term(head_acc, j, o * (coef(signed) / l))

    def run_latent():
        n_chunks = total // KEY_CHUNK
        half = KEY_CHUNK // SCORE_SPLIT
        n_items = len(items)
        m_of, l_of, head_acc = {}, {}, {}
        for t in range(n_items + 2):
            sc = items[t] if t < n_items else None
            ex = items[t - 1] if 0 <= t - 1 < n_items else None
            pv = items[t - 2] if 0 <= t - 2 < n_items else None
            macc = jnp.full((ACC_ROWS, tq), -jnp.inf, F32)
            lacc = jnp.zeros((ACC_ROWS, tq), F32)
            oacc = jnp.zeros((64, tq), F32)
            for c in range(n_chunks):
                r0 = c * KEY_CHUNK
                if sc is not None:
                    q_ref, k_ref, _ = qkv[sc[0]]
                    q_t = q_ref[0, sc[1] * LANES:(sc[1] + 1) * LANES, :]
                    for r in range(r0, r0 + KEY_CHUNK, half):
                        s = jnp.dot(k_ref[0, r:r + half, sc[2] * LANES:(sc[2] + 1) * LANES], q_t,
                                    preferred_element_type=F32)
                        s_ref[t % 2, r:r + half, :] = s
                        for a in range(half // ACC_ROWS):
                            macc = jnp.maximum(macc, s[a * ACC_ROWS:(a + 1) * ACC_ROWS])
                if ex is not None:
                    p = jnp.exp2(s_ref[(t - 1) % 2, r0:r0 + KEY_CHUNK, :] - m_of[t - 1])
                    for a in range(KEY_CHUNK // ACC_ROWS):
                        lacc = lacc + p[a * ACC_ROWS:(a + 1) * ACC_ROWS]
                    p_ref[(t - 1) % 2, r0:r0 + KEY_CHUNK, :] = p.astype(BF16)
                if pv is not None:
                    v_ref = qkv[pv[0]][2]
                    oacc = oacc + jnp.dot(v_ref[0, pv[3] * 64:(pv[3] + 1) * 64, r0:r0 + KEY_CHUNK],
                                          p_ref[(t - 2) % 2, r0:r0 + KEY_CHUNK, :], preferred_element_type=F32)
            if sc is not None:
                m_of[t] = jnp.max(macc, axis=0, keepdims=True)
            if ex is not None:
                l_of[t - 1] = jnp.sum(lacc, axis=0, keepdims=True)
            if pv is not None:
                add_term(head_acc, t - 2, oacc * (coef(pv[5]) / l_of.pop(t - 2)))

    def write_out():
        for o_ref, ot_ref in zip(o_refs, ot_refs):
            o_ref[0] = ot_ref[...].T.astype(BF16)

    qb = pl.program_id(1)

    @pl.when(qb * tq < ctx_len)
    def _():
        run_context()
        write_out()

    @pl.when(qb * tq >= ctx_len)
    def _():
        run_latent()
        write_out()


def _dense_attn(lam, post, branches, ctx_len):
    B, _, T = branches[0][0].shape
    tq = ROW_TILE
    nb = len(branches)
    once = pl.Buffered(1)
    in_specs = [pl.BlockSpec(memory_space=pltpu.SMEM), pl.BlockSpec(post.shape, lambda b, t: (0, 0))]
    args = [lam, post]
    for q_t, k, v_t, _, _ in branches:
        in_specs += [pl.BlockSpec((1, q_t.shape[1], tq), lambda b, t: (b, 0, t)),
                     pl.BlockSpec((1, T, k.shape[2]), lambda b, t: (b, 0, 0), pipeline_mode=once),
                     pl.BlockSpec((1, v_t.shape[1], T), lambda b, t: (b, 0, 0), pipeline_mode=once)]
        args += [q_t, k, v_t]
    kern = functools.partial(_dense_attn_kernel, branches=tuple((br[3], br[4]) for br in branches), ctx_len=ctx_len)
    return pl.pallas_call(
        kern,
        out_shape=[jax.ShapeDtypeStruct((B, T, BRANCH_W), BF16)] * nb,
        grid=(B, T // tq),
        in_specs=in_specs,
        out_specs=[pl.BlockSpec((1, tq, BRANCH_W), lambda b, t: (b, t, 0))] * nb,
        scratch_shapes=[pltpu.VMEM((BRANCH_W, tq), F32)] * nb
        + [pltpu.VMEM((2, T, tq), F32), pltpu.VMEM((2, T, tq), BF16)],
        compiler_params=_cparams(("arbitrary", "arbitrary")),
        name="dense_attn",
    )(*args)


def _na_plan(rows):
    kh, kw = min(NA_KH, rows), NA_KW
    q_rows = ROW_TILE // GRID_W
    win_rows = min(kh + q_rows, rows)
    nblk = rows // q_rows
    row_start = np.clip(np.arange(rows) - kh // 2, 0, rows - kh)
    col_start = np.clip(np.arange(GRID_W) - kw // 2, 0, GRID_W - kw)
    u0 = np.clip(np.arange(nblk) * q_rows - kh // 2, 0, rows - win_rows)
    wk, tq = win_rows * GRID_W, ROW_TILE
    kk, qq = np.arange(wk), np.arange(tq)
    kc, qc = kk % GRID_W, qq % GRID_W
    col_sel = (np.arange(2 * NA_KW - 1)[:, None, None]
               == (np.arange(GRID_W)[None, :, None] - np.arange(GRID_W)[None, None, :] + (NA_KW - 1))).astype(np.float32)
    types, type_of, row_sel_l, valid_l = {}, [], [], []
    for j in range(nblk):
        kr = u0[j] + kk // GRID_W
        qr = j * q_rows + qq // GRID_W
        rs = row_start[qr]
        valid = ((kr[:, None] >= rs[None, :]) & (kr[:, None] < rs[None, :] + kh)
                 & (kc[:, None] >= col_start[qc][None, :]) & (kc[:, None] < col_start[qc][None, :] + kw))
        assert valid.sum(axis=0).min() == kh * kw and valid.sum(axis=0).max() == kh * kw
        rel = (u0[j] + np.arange(win_rows))[:, None] - (j * q_rows + np.arange(q_rows))[None, :] + (NA_KH - 1)
        row_sel = (rel[:, :, None] == np.arange(2 * NA_KH - 1)[None, None, :]).astype(np.float32)
        key = (valid.tobytes(), row_sel.tobytes())
        if key not in types:
            types[key] = len(row_sel_l)
            row_sel_l.append(row_sel), valid_l.append(valid)
        type_of.append(types[key])
    assert all(int(u) * GRID_W % LANES == 0 for u in u0)
    return (np.asarray(u0, np.int32), np.asarray(type_of, np.int32), np.stack(row_sel_l), col_sel,
            np.stack(valid_l), wk)


def _na_kernel(u0_ref, ty_ref, q_ref, k_ref, v_ref, bias_ref, o_ref, ot_ref, *, ctx_len):
    del ty_ref
    wk = bias_ref.shape[2]
    j = pl.program_id(1)

    def finish():
        o_ref[0] = ot_ref[...].T.astype(BF16)

    @pl.when(j == 0)
    def _():
        for i in range(NA_HEADS):
            o, l = _softmax_pv(k_ref[0, 0:ctx_len, i * LANES:(i + 1) * LANES], q_ref[0, i * LANES:(i + 1) * LANES, :],
                               v_ref[0, i * 64:(i + 1) * 64, 0:ctx_len], jnp.exp)
            ot_ref[i * 64:(i + 1) * 64, :] = o * (1.0 / l)
        finish()

    @pl.when(j > 0)
    def _():
        ws = pl.multiple_of(ctx_len + u0_ref[j - 1] * GRID_W, LANES)

        def scores(i):
            q_t = q_ref[0, i * LANES:(i + 1) * LANES, :]
            s_c = jnp.dot(k_ref[0, 0:ctx_len, i * LANES:(i + 1) * LANES], q_t, preferred_element_type=F32)
            s_w = jnp.dot(k_ref[0, pl.ds(ws, wk), i * LANES:(i + 1) * LANES], q_t,
                          preferred_element_type=F32) + bias_ref[0, i]
            return s_c, s_w

        def fold(fn, acc, a):
            for r in range(0, a.shape[0], ACC_ROWS):
                acc = fn(acc, a[r:r + ACC_ROWS])
            return acc

        def softmax_pv(i, s_c, s_w):
            tq = s_c.shape[1]
            macc = fold(jnp.maximum, fold(jnp.maximum, jnp.full((ACC_ROWS, tq), -jnp.inf, F32), s_c), s_w)
            m = jnp.max(macc, axis=0, keepdims=True)
            p_c, p_w = jnp.exp(s_c - m), jnp.exp(s_w - m)
            lacc = fold(jnp.add, fold(jnp.add, jnp.zeros((ACC_ROWS, tq), F32), p_c), p_w)
            l = jnp.sum(lacc, axis=0, keepdims=True)
            o = (jnp.dot(v_ref[0, i * 64:(i + 1) * 64, 0:ctx_len], p_c.astype(BF16), preferred_element_type=F32)
                 + jnp.dot(v_ref[0, i * 64:(i + 1) * 64, pl.ds(ws, wk)], p_w.astype(BF16),
                           preferred_element_type=F32))
            ot_ref[i * 64:(i + 1) * 64, :] = o * (1.0 / l)

        pending = scores(0)
        for i in range(NA_HEADS):
            upcoming = scores(i + 1) if i + 1 < NA_HEADS else None
            softmax_pv(i, *pending)
            pending = upcoming
        finish()


def _na_attn(q_t, k, v_t, rpb, ctx_len):
    B, _, T = q_t.shape
    tq = ROW_TILE
    rows = (T - ctx_len) // GRID_W
    u0, type_of, row_sel, col_sel, valid, wk = _na_plan(rows)
    hp = lax.Precision.HIGHEST
    by_col = jnp.einsum("hab,bcq->hacq", rpb.astype(F32), col_sel, precision=hp)
    bias = jnp.einsum("tkra,hacq->thkcrq", row_sel, by_col, precision=hp)
    bias = jnp.where(valid[:, None], bias.reshape(row_sel.shape[0], NA_HEADS, wk, tq), MASK_VALUE)
    nb = T // tq
    assert ctx_len == tq

    grid_spec = pltpu.PrefetchScalarGridSpec(
        num_scalar_prefetch=2,
        grid=(B, nb),
        in_specs=[pl.BlockSpec((1, NA_HEADS * LANES, tq), lambda b, j, u, ty: (b, 0, j)),
                  pl.BlockSpec((1, T, NA_HEADS * LANES), lambda b, j, u, ty: (b, 0, 0)),
                  pl.BlockSpec((1, BRANCH_W, T), lambda b, j, u, ty: (b, 0, 0)),
                  pl.BlockSpec((1, NA_HEADS, wk, tq), lambda b, j, u, ty: (ty[jnp.maximum(j - 1, 0)], 0, 0, 0))],
        out_specs=pl.BlockSpec((1, tq, BRANCH_W), lambda b, j, u, ty: (b, j, 0)),
        scratch_shapes=[pltpu.VMEM((BRANCH_W, tq), F32)])
    return pl.pallas_call(
        functools.partial(_na_kernel, ctx_len=ctx_len),
        out_shape=jax.ShapeDtypeStruct((B, T, BRANCH_W), BF16),
        grid_spec=grid_spec,
        compiler_params=_cparams(("arbitrary", "arbitrary")),
        name="na_attn",
    )(jnp.asarray(u0), jnp.asarray(type_of), q_t, k, v_t, bias)


def _merge_kernel(c_ref, x_ref, sc1_ref, sh1_ref, g1_ref, sc2_ref, sh2_ref, ona_ref, odf_ref, ogq_ref, oml_ref,
                  wg_ref, wb_ref, wo_ref, lng_ref, lnb_ref, rw_ref, rb_ref, tri_ref,
                  x1_ref, h2_ref, idx_ref, wts_ref, rank_ref, cnt_ref, carry_ref, hprev_ref, *, abs_tile, ctx_tiles):
    tm = x_ref.shape[1]
    step = pl.program_id(0)

    @pl.when(step == 0)
    def _():
        carry_ref[...] = jnp.zeros_like(carry_ref)
        hprev_ref[...] = jnp.zeros_like(hprev_ref)

    iota = lax.broadcasted_iota(I32, (N_EXPERTS, tm), 0)
    route = {}

    def route_logits():
        route["cur"] = lax.dot_general(rw_ref[...], hprev_ref[...], (((1,), (1,)), ((), ())),
                                       preferred_element_type=F32, precision=lax.Precision.HIGHEST) + rb_ref[...]
        route["vals"], route["idxs"] = [], []

    def route_pick():
        cur = route["cur"]
        m = jnp.max(cur, axis=0, keepdims=True)
        ik = jnp.min(jnp.where(cur == m, iota, N_EXPERTS), axis=0, keepdims=True)
        route["vals"].append(m)
        route["idxs"].append(ik)
        route["cur"] = jnp.where(iota == ik, -jnp.inf, cur)

    def route_finish():
        vals, idxs = route["vals"], route["idxs"]
        exps = [jnp.exp(v - vals[0]) for v in vals]
        denom = exps[0] + exps[1] + exps[2] + exps[3]
        wts_ref[0] = jnp.concatenate([e / denom for e in exps], axis=0)
        idx_ref[0] = jnp.concatenate(idxs, axis=0)
        live = jnp.where(step >= 1, 1.0, 0.0)
        onehot = jnp.zeros((N_EXPERTS, tm), F32)
        for ik in idxs:
            onehot = onehot + jnp.where(iota == ik, live, 0.0)
        before = jnp.dot(onehot.astype(BF16), tri_ref[...], preferred_element_type=F32) + carry_ref[...]
        rank_ref[0] = jnp.concatenate(
            [jnp.sum(jnp.where(iota == ik, before, 0.0), axis=0, keepdims=True) for ik in idxs], axis=0).astype(I32)
        carry = carry_ref[...] + jnp.sum(onehot, axis=1, keepdims=True)
        carry_ref[...] = carry
        cnt_ref[...] = carry.astype(I32)

    route_pieces = [[route_logits], [route_pick, route_pick], [route_pick, route_pick], [route_finish]]

    x = _token_tile(c_ref, x_ref, abs_tile(step)[1], ctx_tiles)
    h1 = (x * (1.0 + sc1_ref[...]) + sh1_ref[...]).astype(BF16)
    y = None
    for i, o_ref in enumerate((ona_ref, odf_ref, ogq_ref, oml_ref)):
        gate = jax.nn.sigmoid(jnp.dot(h1, wg_ref[:, i * D_MODEL:(i + 1) * D_MODEL], preferred_element_type=F32))
        term = gate * jnp.dot(o_ref[0], wb_ref[i * BRANCH_W:(i + 1) * BRANCH_W, :], preferred_element_type=F32)
        y = term if y is None else y + term
        for piece in route_pieces[i]:
            piece()
    z = jnp.dot(y.astype(BF16), wo_ref[...], preferred_element_type=F32)
    r = DEEPNORM_ALPHA * x + g1_ref[...] * z
    mu = jnp.mean(r, axis=-1, keepdims=True)
    var = jnp.mean(jnp.square(r - mu), axis=-1, keepdims=True)
    x1 = (r - mu) * lax.rsqrt(var + EPS) * lng_ref[...] + lnb_ref[...]
    x1_ref[0] = x1
    h2 = x1 * (1.0 + sc2_ref[...]) + sh2_ref[...]
    for j in range(D_MODEL // LANES):
        h2_ref[:, j, :] = h2[:, j * LANES:(j + 1) * LANES]
    hprev_ref[...] = h2


def _merge(ctx_src, x_src, x_tile_off, mod, o_na, o_df, o_gq, o_ml, wg, wb, wo, ln_g, ln_b, rw_t, rb, ctx_len, t_off):
    (B, T, _), D = o_na.shape, x_src.shape[2]
    tm = ROW_TILE
    nt = T // tm - t_off
    ctx_tiles = ctx_len // tm
    tq_rows = nt * tm
    n_tiles = B * nt

    def tile(i, lag):
        return jnp.clip(i - lag, 0, n_tiles - 1)

    def bt(i, lag):
        ti = tile(i, lag)
        return ti // nt, ti % nt

    def kind(t):
        return jnp.where(t + t_off >= ctx_tiles, 1, 0)

    def mod_spec(j, lag):
        return pl.BlockSpec((None, None, None, 1, D), lambda i: (bt(i, lag)[0], kind(bt(i, lag)[1]), j, 0, 0))

    def const(a):
        nd = a.ndim
        return pl.BlockSpec(a.shape, lambda i: (0,) * nd)

    def tok(cols):
        return pl.BlockSpec((1, tm, cols), lambda i: (bt(i, 0)[0], bt(i, 0)[1] + t_off, 0))

    tri = jnp.asarray(np.triu(np.ones((tm, tm), np.float32), 1), BF16)
    rb_b = jnp.broadcast_to(rb[:, None], (N_EXPERTS, tm)).astype(F32)
    route_shape = jax.ShapeDtypeStruct((n_tiles, TOP_K, tm), I32)
    route_spec = pl.BlockSpec((1, TOP_K, tm), lambda i: (tile(i, 1), 0, 0))
    def abs_tile(i):
        return bt(i, 0)[0], bt(i, 0)[1] + t_off

    return pl.pallas_call(
        functools.partial(_merge_kernel, abs_tile=abs_tile, ctx_tiles=ctx_tiles),
        out_shape=[jax.ShapeDtypeStruct((B, tq_rows, D), F32),
                   jax.ShapeDtypeStruct((B * tq_rows, D // LANES, LANES), F32),
                   route_shape, jax.ShapeDtypeStruct((n_tiles, TOP_K, tm), F32), route_shape,
                   jax.ShapeDtypeStruct((N_EXPERTS, tm), I32)],
        grid=(n_tiles + 1,),
        in_specs=[*_token_specs(ctx_tiles, x_tile_off, abs_tile),
                  mod_spec(1, 0), mod_spec(0, 0), mod_spec(2, 0), mod_spec(4, 0), mod_spec(3, 0),
                  tok(BRANCH_W), tok(BRANCH_W), tok(BRANCH_W), tok(BRANCH_W),
                  const(wg), const(wb), const(wo), const(ln_g), const(ln_b), const(rw_t), const(rb_b), const(tri)],
        out_specs=[pl.BlockSpec((1, tm, D), lambda i: (bt(i, 0)[0], bt(i, 0)[1], 0)),
                   pl.BlockSpec((tm, D // LANES, LANES), lambda i: (tile(i, 0), 0, 0)),
                   route_spec, route_spec, route_spec,
                   pl.BlockSpec((N_EXPERTS, tm), lambda i: (0, 0))],
        scratch_shapes=[pltpu.VMEM((N_EXPERTS, tm), F32), pltpu.VMEM((tm, D), F32)],
        compiler_params=_cparams(("arbitrary",)),
        name="merge_router",
    )(ctx_src, x_src, mod, mod, mod, mod, mod, o_na, o_df, o_gq, o_ml, wg, wb, wo, ln_g, ln_b, rw_t, rb_b, tri)


def _row_copy(src, src_row, dst, dst_row, sem):
    return pltpu.make_async_copy(src.at[pl.ds(src_row, 1)], dst.at[pl.ds(dst_row, 1)], sem)


def _dispatch_kernel(dest_ref, h_ref, xs_in_ref, xs_ref, sem):
    del xs_in_ref
    tm = dest_ref.shape[2]

    def issue(t, c):
        for k in range(TOP_K):
            _row_copy(h_ref, t, xs_ref, dest_ref[0, k, t], sem).start(priority=k % 2)
        return c

    lax.fori_loop(0, tm, issue, 0, unroll=2)
    for k in range(TOP_K):
        pltpu.make_async_copy(h_ref, xs_ref.at[pl.ds(0, tm)], sem).wait()


def _dispatch(dest, h2, xs0):
    n_tiles, _, tm = dest.shape
    return pl.pallas_call(
        _dispatch_kernel,
        out_shape=jax.ShapeDtypeStruct(xs0.shape, F32),
        grid=(n_tiles,),
        in_specs=[pl.BlockSpec((1, TOP_K, tm), lambda i: (i, 0, 0), memory_space=pltpu.SMEM),
                  pl.BlockSpec((tm,) + h2.shape[1:], lambda i: (i, 0, 0)),
                  pl.BlockSpec(memory_space=pl.ANY)],
        out_specs=pl.BlockSpec(memory_space=pl.ANY),
        scratch_shapes=[pltpu.SemaphoreType.DMA(())],
        input_output_aliases={2: 0},
        compiler_params=_cparams(("arbitrary",)),
        name="moe_dispatch",
    )(dest, h2, xs0)


def _tile_relayout_copies(tiled_hbm, flat_vmem, tile, sem):
    tm = flat_vmem.shape[0]
    r0 = pl.multiple_of(tile * tm, tm)
    return [(tiled_hbm.at[pl.ds(r0, tm), j, :], flat_vmem.at[:, pl.ds(j * LANES, LANES)], sem)
            for j in range(D_MODEL // LANES)]


def _expert_kernel(te_ref, nu_ref, xs_ref, w1_ref, b1_ref, w2_ref, b2_ref, y_ref, w1s_ref, w2s_ref, x_buf, y_buf,
                   sem_in, sem_out, *, n_steps):
    i = pl.program_id(0)
    n_used = nu_ref[0]
    slot = lax.rem(i, 2)

    def fetch(tile, s, wait):
        for hbm, vmem, sem in _tile_relayout_copies(xs_ref, x_buf.at[s], tile, sem_in.at[s]):
            cp = pltpu.make_async_copy(hbm, vmem, sem)
            cp.wait() if wait else cp.start()

    def write_back(tile, s, wait):
        for hbm, vmem, sem in _tile_relayout_copies(y_ref, y_buf.at[s], tile, sem_out.at[s]):
            cp = pltpu.make_async_copy(vmem, hbm, sem)
            cp.wait() if wait else cp.start(priority=1)

    @pl.when(i == 0)
    def _():
        fetch(0, 0, False)

    @pl.when(i + 1 < n_used)
    def _():
        fetch(i + 1, 1 - slot, False)

    @pl.when((i == 0) | (te_ref[i] != te_ref[jnp.maximum(i - 1, 0)]))
    def _():
        w1s_ref[...] = w1_ref[...].astype(BF16)
        w2s_ref[...] = w2_ref[...].astype(BF16)

    @pl.when(i >= 2)
    def _():
        write_back(i - 2, slot, True)

    @pl.when(i < n_used)
    def _():
        fetch(i, slot, True)
        gu = jnp.dot(x_buf[slot].astype(BF16), w1s_ref[...], preferred_element_type=F32) + b1_ref[...]
        gate = jnp.minimum(gu[:, :D_FF], SWIGLU_LIMIT)
        up = jnp.clip(gu[:, D_FF:], -SWIGLU_LIMIT, SWIGLU_LIMIT)
        act = (up + 1.0) * gate * jax.nn.sigmoid(SWIGLU_ALPHA * gate)
        y_buf[slot] = jnp.dot(act.astype(BF16), w2s_ref[...], preferred_element_type=F32) + b2_ref[...]

    @pl.when(i >= n_used)
    def _():
        y_buf[slot] = jnp.zeros(y_buf.shape[1:], F32)

    write_back(i, slot, False)

    @pl.when(i == n_steps - 1)
    def _():
        write_back(i, slot, True)

        @pl.when(i >= 1)
        def _():
            write_back(i - 1, 1 - slot, True)


def _experts(tile_expert, n_used, xs, w1, b1, w2, b2, layer):
    n_slots = xs.shape[0]
    tm = ROW_TILE
    nt = n_slots // tm
    E = w1.shape[1]
    grid_spec = pltpu.PrefetchScalarGridSpec(
        num_scalar_prefetch=2, grid=(nt,),
        in_specs=[pl.BlockSpec(memory_space=pl.ANY),
                  pl.BlockSpec((None, None, D_MODEL, 2 * D_FF), lambda i, te, nu: (layer, te[i], 0, 0)),
                  pl.BlockSpec((None, None, 1, 2 * D_FF), lambda i, te, nu: (layer, te[i], 0, 0)),
                  pl.BlockSpec((None, None, D_FF, D_MODEL), lambda i, te, nu: (layer, te[i], 0, 0)),
                  pl.BlockSpec((None, None, 1, D_MODEL), lambda i, te, nu: (layer, te[i], 0, 0))],
        out_specs=pl.BlockSpec(memory_space=pl.ANY),
        scratch_shapes=[pltpu.VMEM((D_MODEL, 2 * D_FF), BF16), pltpu.VMEM((D_FF, D_MODEL), BF16),
                        pltpu.VMEM((2, tm, D_MODEL), F32), pltpu.VMEM((2, tm, D_MODEL), F32),
                        pltpu.SemaphoreType.DMA((2,)), pltpu.SemaphoreType.DMA((2,))])
    return pl.pallas_call(
        functools.partial(_expert_kernel, n_steps=nt),
        out_shape=jax.ShapeDtypeStruct(xs.shape, F32),
        grid_spec=grid_spec,
        compiler_params=_cparams(("arbitrary",)),
        name="moe_experts",
    )(tile_expert, n_used, xs, w1, b1.reshape(b1.shape[0], E, 1, -1), w2, b2.reshape(b2.shape[0], E, 1, -1))


def _combine_kernel(dest_ref, destn_ref, w_ref, x1_ref, g2_ref, lng_ref, lnb_ref, y_ref, o_ref, buf_ref, y3_ref,
                    sem, *, n_steps):
    tm = dest_ref.shape[2]
    i = pl.program_id(0)
    n = n_steps
    slot = lax.rem(i, 2)

    def issue(dr, s, t):
        for k in range(TOP_K):
            pltpu.make_async_copy(y_ref.at[pl.ds(dr[0, k, t], 1)], buf_ref.at[s, k, pl.ds(t, 1)],
                                  sem.at[s]).start(priority=k % 2)

    def weigh(t):
        acc = buf_ref[slot, 0, t] * w_ref[0, 0, t]
        for k in range(1, TOP_K):
            acc = acc + buf_ref[slot, k, t] * w_ref[0, k, t]
        y3_ref[t] = acc

    def loop(body):
        lax.fori_loop(0, tm, lambda t, c: (body(t), c)[1], 0, unroll=8)

    @pl.when(i == 0)
    def _():
        loop(lambda t: issue(dest_ref, 0, t))

    for k in range(TOP_K):
        pltpu.make_async_copy(y_ref.at[pl.ds(0, tm)], buf_ref.at[slot, k], sem.at[slot]).wait()

    @pl.when(i + 1 < n)
    def _():
        loop(lambda t: (issue(destn_ref, 1 - slot, t), weigh(t)))

    @pl.when(i + 1 >= n)
    def _():
        loop(weigh)
    y2 = jnp.concatenate([y3_ref[:, j, :] for j in range(D_MODEL // LANES)], axis=1)
    r = DEEPNORM_ALPHA * x1_ref[0] + g2_ref[...] * y2
    mu = jnp.mean(r, axis=-1, keepdims=True)
    var = jnp.mean(jnp.square(r - mu), axis=-1, keepdims=True)
    o_ref[0] = (r - mu) * lax.rsqrt(var + EPS) * lng_ref[...] + lnb_ref[...]


def _combine(dest, wts, x1, mod, ln_g, ln_b, y, ctx_len, t_off):
    B, tq_rows, D = x1.shape
    n_tiles, _, tm = dest.shape
    nt = n_tiles // B
    ctx_tiles = ctx_len // tm

    def kind(i):
        return jnp.where(i % nt + t_off >= ctx_tiles, 1, 0)

    smem = lambda f: pl.BlockSpec((1, TOP_K, tm), f, memory_space=pltpu.SMEM)
    cur = lambda i: (i, 0, 0)
    nxt = lambda i: (jnp.minimum(i + 1, n_tiles - 1), 0, 0)
    return pl.pallas_call(
        functools.partial(_combine_kernel, n_steps=n_tiles),
        out_shape=jax.ShapeDtypeStruct((B, tq_rows, D), F32),
        grid=(n_tiles,),
        in_specs=[smem(cur), smem(nxt), smem(cur),
                  pl.BlockSpec((1, tm, D), lambda i: (i // nt, i % nt, 0)),
                  pl.BlockSpec((None, None, None, 1, D), lambda i: (i // nt, kind(i), 5, 0, 0)),
                  pl.BlockSpec(ln_g.shape, lambda i: (0, 0)),
                  pl.BlockSpec(ln_b.shape, lambda i: (0, 0)),
                  pl.BlockSpec(memory_space=pl.ANY)],
        out_specs=pl.BlockSpec((1, tm, D), lambda i: (i // nt, i % nt, 0)),
        scratch_shapes=[pltpu.VMEM((2, TOP_K, tm, D // LANES, LANES), F32),
                        pltpu.VMEM((tm, D // LANES, LANES), F32), pltpu.SemaphoreType.DMA((2,))],
        compiler_params=_cparams(("arbitrary",)),
        name="moe_combine",
    )(dest, dest, wts, x1, mod, ln_g, ln_b, y)


_DF_HEADS_SPEC = tuple((((2 * i, i, False), (2 * i + 1, i, True)), i) for i in range(DF_HEADS))
_GQ_HEADS_SPEC = tuple((((i, i // (GQ_HEADS // GQ_KV_HEADS), False),), i // (GQ_HEADS // GQ_KV_HEADS))
                       for i in range(GQ_HEADS))
_ML_HEADS_SPEC = tuple((((i, i, False),), i) for i in range(ML_HEADS))


def _layer_weights(l, w_in, gq_qnorm, gq_knorm, ml_qa_norm, ml_wq_b, ml_kva_norm, ml_wkv_b):
    w_t = w_in[l][:, :O_GATES].T.astype(BF16)
    assert w_t.shape[0] == PROJ_ROWS

    tm = ROW_TILE
    gq_g = jnp.stack([gq_qnorm[l], _partner_gain(gq_qnorm[l], GQ_DIM),
                      gq_knorm[l], _partner_gain(gq_knorm[l], GQ_DIM)])
    gq_g = jnp.broadcast_to(gq_g[:, :, None], (4, GQ_DIM, tm)).astype(F32)
    qa_g = jnp.broadcast_to(ml_qa_norm[l][:, None], (ML_Q_RANK, tm)).astype(F32)
    kva_g = jnp.broadcast_to(ml_kva_norm[l][:, None], (ML_KV_RANK, tm)).astype(F32)
    wq_t = ml_wq_b[l].T.astype(BF16)
    wkv_t = ml_wkv_b[l].T.astype(BF16)
    return w_t, gq_g, qa_g, kva_g, wq_t, wkv_t


def kernel(x, c, ctx, c_ctx, w_ada, b_ada, w_in, na_rpb, df_lam, df_subln, gq_qnorm, gq_knorm, ml_qa_norm, ml_wq_b,
           ml_kva_norm, ml_wkv_b, w_branch, w_out, ln1_g, ln1_b, ln2_g, ln2_b, router_w, router_b, exp_w1, exp_b1,
           exp_w2, exp_b2):
    B, S, D = x.shape
    C = ctx.shape[1]
    T = C + S
    tm = ROW_TILE
    assert D == D_MODEL and C % tm == 0 and S % tm == 0 and C == tm

    ctx_src, x_src, x_tile_off = ctx, x, 0
    cc = jnp.concatenate([c, c_ctx[None], jnp.zeros((16 - B - 1, D), F32)], axis=0)
    mod_all = _ada(cc, w_ada, b_ada)
    cs32 = _rope_tables(32, C, S)
    cs64 = _rope_tables(64, C, S)

    xs = None
    for l in range(DEPTH):
        last = l == DEPTH - 1
        m = mod_all[l]
        mod = jnp.stack([jnp.broadcast_to(m[B][None], (B, 6 * D)), m[:B]], axis=1).reshape(B, 2, 6, 1, D)
        lam_init = 0.8 - 0.6 * math.exp(-0.3 * l)
        lp = df_lam[l].astype(F32)
        lam = (jnp.exp(jnp.sum(lp[0] * lp[1])) - jnp.exp(jnp.sum(lp[2] * lp[3])) + lam_init).reshape(1)
        w_t, gq_g, qa_g, kva_g, wq_t, wkv_t = _layer_weights(l, w_in, gq_qnorm, gq_knorm, ml_qa_norm, ml_wq_b,
                                                             ml_kva_norm, ml_wkv_b)
        (na_q, na_k, na_v, df_q, df_k, df_v, gq_q, gq_k, gq_v, ml_q, ml_k, ml_v) = _project(
            ctx_src, x_src, x_tile_off, T, mod, w_t, cs32, cs64, gq_g, qa_g, kva_g, wq_t, wkv_t, C)

        o_na = _na_attn(na_q, na_k, na_v, na_rpb[l], C)
        subln = jnp.broadcast_to((df_subln[l] * (1.0 - lam_init))[:, None], (DF_V, tm)).astype(F32)
        o_df, o_gq, o_ml = _dense_attn(lam, subln, [(df_q, df_k, df_v, _DF_HEADS_SPEC, True),
                                                    (gq_q, gq_k, gq_v, _GQ_HEADS_SPEC, False),
                                                    (ml_q, ml_k, ml_v, _ML_HEADS_SPEC, False)], C)

        t_off = C // tm if last else 0
        wg = w_in[l][:, O_GATES:].astype(BF16)
        wb = w_branch[l].reshape(N_BRANCH * BRANCH_W, D).astype(BF16)
        wo = w_out[l].astype(BF16)
        x1, h2, idx, wts, rank, cnt = _merge(ctx_src, x_src, x_tile_off, mod, o_na, o_df, o_gq, o_ml, wg, wb, wo,
                                             ln1_g[l][None], ln1_b[l][None], router_w[l].T, router_b[l], C, t_off)

        counts = cnt[:, 0]
        padded = ((counts + tm - 1) // tm) * tm
        ends = jnp.cumsum(padded)
        starts = (ends - padded).astype(I32)
        n_exp_tiles = (B * T * TOP_K) // tm + N_EXPERTS
        n_used = (ends[-1] // tm).astype(I32).reshape(1)
        tile_first = jnp.arange(n_exp_tiles, dtype=I32) * tm
        tile_expert = jnp.minimum(jnp.sum((ends[None, :] <= tile_first[:, None]).astype(I32), axis=1), N_EXPERTS - 1)
        tile_expert = jnp.where(tile_first < ends[-1], tile_expert, tile_expert[jnp.maximum(n_used[0] - 1, 0)])

        dest = jnp.sum(jnp.where(idx[..., None] == jnp.arange(N_EXPERTS, dtype=I32), starts, 0), axis=-1) + rank
        if xs is None:
            xs = jnp.zeros((n_exp_tiles * tm,) + h2.shape[1:], F32)
        xs = _dispatch(dest, h2, xs)
        ys = _experts(tile_expert, n_used, xs, exp_w1, exp_b1, exp_w2, exp_b2, l)
        out = _combine(dest, wts, x1, mod, ln2_g[l][None], ln2_b[l][None], ys, C, t_off)
        ctx_src, x_src, x_tile_off = out, out, C // tm
    return out
```

```python
import functools
import math

import numpy as np
import jax
import jax.numpy as jnp
from jax import lax
from jax.experimental import pallas as pl
from jax.experimental.pallas import tpu as pltpu

F32, BF16, I32 = jnp.float32, jnp.bfloat16, jnp.int32

D_MODEL = 1024
DEPTH = 2
GRID_W = 64
ROPE_BASE = 10000.0
EPS = 1e-6
NA_HEADS, NA_DIM, NA_KH, NA_KW = 4, 64, 8, 16
DF_HEADS, DF_QK = 4, 32
DF_V = 2 * DF_QK
GQ_HEADS, GQ_KV_HEADS, GQ_DIM = 4, 2, 64
ML_HEADS, ML_NOPE, ML_ROPE, ML_V, ML_Q_RANK, ML_KV_RANK = 4, 64, 32, 64, 256, 128
N_BRANCH, BRANCH_W = 4, 256
N_EXPERTS, TOP_K = 32, 4
D_FF = D_MODEL
SWIGLU_LIMIT, SWIGLU_ALPHA = 7.0, 1.702
DEEPNORM_ALPHA = (2 * DEPTH) ** 0.25

IN_SIZES = (256, 256, 256, 256, 256, 256, 256, 128, 128, ML_Q_RANK, ML_KV_RANK, ML_ROPE, N_BRANCH * D_MODEL)
IN_OFFSETS = tuple(int(o) for o in np.cumsum((0,) + IN_SIZES)[:-1])
(O_NAQ, O_NAK, O_NAV, O_DFQ, O_DFK, O_DFV, O_GQQ, O_GQK, O_GQV, O_MLQA, O_MLKVA, O_MLKR, O_GATES) = IN_OFFSETS

LANES = 128
ROW_TILE = 256
MASK_VALUE = -1e30
VMEM_LIMIT = 56 * 1024 * 1024
LOG2E = math.log2(math.e)
KEY_CHUNK = 256
ACC_ROWS = 16
SCORE_SPLIT = 2

_PROJ_GROUPS = (("na_q", 256), ("na_k", 256), ("na_v", 256),
                ("df_q", 256), ("df_k", 256), ("df_v", 256),
                ("gq_q", 256), ("gq_k", 128), ("gq_v", 128),
                ("ml_qa", 256), ("ml_kva", 128), ("ml_kr", 32))
_PROJ_OFF = {}
_o = 0
for _n, _r in _PROJ_GROUPS:
    _PROJ_OFF[_n] = (_o, _r)
    _o += _r
PROJ_ROWS = _o


def _cparams(sem):
    return pltpu.CompilerParams(dimension_semantics=sem, vmem_limit_bytes=VMEM_LIMIT)


def _partner_gain(g, n):
    g4 = g.reshape(2, 2, n // 4)
    return jnp.stack([g4[:, 1], g4[:, 0]], axis=1).reshape(n)


def _rope_tables(n, ctx_len, seq):
    h = n // 2
    t = np.arange(seq)
    rows, cols = (t // GRID_W).astype(np.float32), (t % GRID_W).astype(np.float32)
    inv_freq = (np.float32(ROPE_BASE) ** (-np.arange(0, h, 2, dtype=np.float32) / np.float32(h))).astype(np.float32)
    ang_r = rows[:, None] * inv_freq[None, :]
    ang_c = cols[:, None] * inv_freq[None, :]
    ang = np.concatenate([ang_r, ang_r, ang_c, ang_c], axis=1)
    cos = np.concatenate([np.ones((ctx_len, n), np.float32), np.cos(ang)], axis=0).T
    sin = np.concatenate([np.zeros((ctx_len, n), np.float32), np.sin(ang)], axis=0).T
    return jnp.asarray(np.stack([cos, sin]).astype(np.float32))


def _ada_kernel(c_ref, w_ref, b_ref, o_ref):
    cc = c_ref[...]
    h = (cc * jax.nn.sigmoid(cc)).astype(BF16)
    o_ref[0] = jnp.dot(h, w_ref[0].astype(BF16), preferred_element_type=F32) + b_ref[0]


def _ada(cc, w_ada, b_ada):
    L, D, N = w_ada.shape
    R = cc.shape[0]
    tn = 4 * LANES
    return pl.pallas_call(
        _ada_kernel,
        out_shape=jax.ShapeDtypeStruct((L, R, N), F32),
        grid=(L, N // tn),
        in_specs=[pl.BlockSpec((R, D), lambda l, j: (0, 0)),
                  pl.BlockSpec((1, D, tn), lambda l, j: (l, 0, j)),
                  pl.BlockSpec((1, 1, tn), lambda l, j: (l, 0, j))],
        out_specs=pl.BlockSpec((1, R, tn), lambda l, j: (l, 0, j)),
        compiler_params=_cparams(("arbitrary", "arbitrary")),
        name="ada_mod",
    )(cc, w_ada, b_ada.reshape(L, 1, N))


def _token_tile(c_ref, x_ref, tile, ctx_tiles):
    return jnp.where(tile < ctx_tiles, c_ref[0], x_ref[0])


def _proj_kernel(c_ref, x_ref, sc_ref, sh_ref, w_ref, cs32_ref, cs64_ref, gqg_ref, qag_ref, kvag_ref, wq_ref, wkv_ref,
                 na_q, na_k, na_v, df_q, df_k, df_v, gq_q, gq_k, gq_v, ml_q, ml_k, ml_v, *, ctx_tiles):
    tm = x_ref.shape[1]
    x_in = _token_tile(c_ref, x_ref, pl.program_id(1), ctx_tiles)
    h = (x_in * (1.0 + sc_ref[...]) + sh_ref[...]).astype(BF16)
    p = lax.dot_general(w_ref[...], h, (((1,), (1,)), ((), ())), preferred_element_type=F32)

    def grp(name):
        o, r = _PROJ_OFF[name]
        return p[o:o + r]

    def zeros(n):
        return jnp.zeros((n, tm), F32)

    def swap(a, n):
        r, q = a.shape[0], n // 4
        a4 = a.reshape(r // (2 * q), 2, q, tm)
        return jnp.concatenate([-a4[:, 1:2], a4[:, 0:1]], axis=1).reshape(r, tm)

    def rope(a, cs_ref):
        n = cs_ref.shape[1]
        r = a.shape[0]
        a3, s3 = a.reshape(r // n, n, tm), swap(a, n).reshape(r // n, n, tm)
        return (a3 * cs_ref[0][None] + s3 * cs_ref[1][None]).reshape(r, tm)

    def head_slots(a, width):
        pieces = []
        for i in range(a.shape[0] // width):
            pieces += [a[i * width:(i + 1) * width], zeros(LANES - width)]
        return jnp.concatenate(pieces, axis=0)

    na_q[0] = head_slots(grp("na_q") * (NA_DIM ** -0.5), NA_DIM).astype(BF16)
    na_k[0] = head_slots(grp("na_k"), NA_DIM).T.astype(BF16)
    na_v[0] = grp("na_v").astype(BF16)

    q = rope(grp("df_q"), cs32_ref) * (DF_QK ** -0.5 * LOG2E)
    pieces = []
    for i in range(DF_HEADS):
        q1, q2 = q[i * 64:i * 64 + 32], q[i * 64 + 32:i * 64 + 64]
        pieces += [q1, zeros(LANES - 32), zeros(32), q2, zeros(LANES - 64)]
    df_q[0] = jnp.concatenate(pieces, axis=0).astype(BF16)
    df_k[0] = head_slots(rope(grp("df_k"), cs32_ref), 2 * DF_QK).T.astype(BF16)
    df_v[0] = grp("df_v").astype(BF16)

    cos64, sin64 = cs64_ref[0], cs64_ref[1]

    def norm_rope(a, g, g_sw, scale):
        hh = a.shape[0] // GQ_DIM
        a3, s3 = a.reshape(hh, GQ_DIM, tm), swap(a, GQ_DIM).reshape(hh, GQ_DIM, tm)
        r = lax.rsqrt(jnp.mean(a3 * a3, axis=1, keepdims=True) + EPS) * scale
        return ((a3 * (g * cos64)[None] + s3 * (g_sw * sin64)[None]) * r).reshape(hh * GQ_DIM, tm)

    gq_q[0] = head_slots(norm_rope(grp("gq_q"), gqg_ref[0], gqg_ref[1], GQ_DIM ** -0.5 * LOG2E),
                         GQ_DIM).astype(BF16)
    gq_k[0] = head_slots(norm_rope(grp("gq_k"), gqg_ref[2], gqg_ref[3], 1.0), GQ_DIM).T.astype(BF16)
    gq_v[0] = grp("gq_v").astype(BF16)

    qa = grp("ml_qa")
    qan = (qa * lax.rsqrt(jnp.mean(qa * qa, axis=0, keepdims=True) + EPS) * qag_ref[...]).astype(BF16)
    qq = jnp.dot(wq_ref[...], qan, preferred_element_type=F32)
    kva = grp("ml_kva")
    kvan = (kva * lax.rsqrt(jnp.mean(kva * kva, axis=0, keepdims=True) + EPS) * kvag_ref[...]).astype(BF16)
    kv = jnp.dot(wkv_ref[...], kvan, preferred_element_type=F32)
    k_rope = rope(grp("ml_kr"), cs32_ref)
    ml_scale = (ML_NOPE + ML_ROPE) ** -0.5 * LOG2E
    qd = ML_NOPE + ML_ROPE
    qp, kp, vp = [], [], []
    for i in range(ML_HEADS):
        b = i * LANES
        q_rope = rope(qq[i * qd + ML_NOPE:(i + 1) * qd], cs32_ref)
        qp += [qq[i * qd:i * qd + ML_NOPE] * ml_scale, q_rope * ml_scale, zeros(32)]
        kp += [kv[b:b + 64], k_rope, zeros(32)]
        vp += [kv[b + 64:b + 128]]
    ml_q[0] = jnp.concatenate(qp, axis=0).astype(BF16)
    ml_k[0] = jnp.concatenate(kp, axis=0).T.astype(BF16)
    ml_v[0] = jnp.concatenate(vp, axis=0).astype(BF16)


def _token_specs(ctx_tiles, x_tile_off, tile_of):
    def ctx_map(*g):
        return tile_of(*g)[0], 0, 0

    def lat_map(*g):
        b, t = tile_of(*g)
        return b, jnp.maximum(t - ctx_tiles, 0) + x_tile_off, 0

    return (pl.BlockSpec((1, ROW_TILE, D_MODEL), ctx_map), pl.BlockSpec((1, ROW_TILE, D_MODEL), lat_map))


def _project(ctx_src, x_src, x_tile_off, n_tok, mod, w_t, cs32, cs64, gq_g, qa_g, kva_g, wq_t, wkv_t, ctx_len):
    B, T, D = x_src.shape[0], n_tok, x_src.shape[2]
    tm = ROW_TILE
    nt = T // tm
    ctx_tiles = ctx_len // tm

    def kind(t):
        return jnp.where(t >= ctx_tiles, 1, 0)

    def mod_spec(j):
        return pl.BlockSpec((None, None, None, 1, D), lambda b, t: (b, kind(t), j, 0, 0))

    def const(a):
        nd = a.ndim
        return pl.BlockSpec(a.shape, lambda b, t: (0,) * nd)

    def fm(rows):
        return jax.ShapeDtypeStruct((B, rows, T), BF16), pl.BlockSpec((1, rows, tm), lambda b, t: (b, 0, t))

    def tmj(cols):
        return jax.ShapeDtypeStruct((B, T, cols), BF16), pl.BlockSpec((1, tm, cols), lambda b, t: (b, t, 0))

    outs = [fm(512), tmj(512), fm(256),
            fm(1024), tmj(512), fm(256),
            fm(512), tmj(256), fm(128),
            fm(512), tmj(512), fm(256)]
    return pl.pallas_call(
        functools.partial(_proj_kernel, ctx_tiles=ctx_tiles),
        out_shape=[o[0] for o in outs],
        grid=(B, nt),
        in_specs=[*_token_specs(ctx_tiles, x_tile_off, lambda b, t: (b, t)),
                  mod_spec(1), mod_spec(0),
                  const(w_t),
                  pl.BlockSpec((2, 32, tm), lambda b, t: (0, 0, t)),
                  pl.BlockSpec((2, 64, tm), lambda b, t: (0, 0, t)),
                  const(gq_g), const(qa_g), const(kva_g), const(wq_t), const(wkv_t)],
        out_specs=[o[1] for o in outs],
        compiler_params=_cparams(("arbitrary", "arbitrary")),
        name="in_proj",
    )(ctx_src, x_src, mod, mod, w_t, cs32, cs64, gq_g, qa_g, kva_g, wq_t, wkv_t)


def _softmax_pv(k, q_t, v_t, exp_fn):
    s = jnp.dot(k, q_t, preferred_element_type=F32)
    m = jnp.max(s, axis=0, keepdims=True)
    p = exp_fn(s - m)
    l = jnp.sum(p, axis=0, keepdims=True)
    o = jnp.dot(v_t, p.astype(BF16), preferred_element_type=F32)
    return o, l


def _dense_attn_kernel(lam_ref, post_ref, *refs, branches, ctx_len):
    nb = len(branches)
    qkv = [refs[3 * b:3 * b + 3] for b in range(nb)]
    o_refs = refs[3 * nb:4 * nb]
    ot_refs = refs[4 * nb:5 * nb]
    s_ref, p_ref = refs[5 * nb:]
    total, tq = qkv[0][1].shape[1], qkv[0][0].shape[2]
    items = [(b, qs, ks, v_idx, i, signed) for b, (heads, _) in enumerate(branches)
             for i, (terms, v_idx) in enumerate(heads) for qs, ks, signed in terms]
    last_of_head = {(it[0], it[4]): j for j, it in enumerate(items)}

    def finish_head(b, i, acc):
        if branches[b][1]:
            acc = acc * lax.rsqrt(jnp.mean(acc * acc, axis=0, keepdims=True) + EPS) * post_ref[...]
        ot_refs[b][i * 64:(i + 1) * 64, :] = acc

    def add_term(head_acc, j, o):
        b, i = items[j][0], items[j][4]
        head_acc[b, i] = o if (b, i) not in head_acc else head_acc[b, i] + o
        if last_of_head[b, i] == j:
            finish_head(b, i, head_acc.pop((b, i)))

    def coef(signed):
        return (-lam_ref[0]) if signed else 1.0

    def run_context():
        head_acc = {}
        for j, (b, qs, ks, v_idx, i, signed) in enumerate(items):
            q_ref, k_ref, v_ref = qkv[b]
            o, l = _softmax_pv(k_ref[0, 0:ctx_len, ks * LANES:(ks + 1) * LANES], q_ref[0, qs * LANES:(qs + 1) * LANES, :],
                               v_ref[0, v_idx * 64:(v_idx + 1) * 64, 0:ctx_len], jnp.exp2)
            add_term(head_acc, j, o * (coef(signed) / l))

    def run_latent():
        n_chunks = total // KEY_CHUNK
        half = KEY_CHUNK // SCORE_SPLIT
        n_items = len(items)
        m_of, l_of, head_acc = {}, {}, {}
        for t in range(n_items + 2):
            sc = items[t] if t < n_items else None
            ex = items[t - 1] if 0 <= t - 1 < n_items else None
            pv = items[t - 2] if 0 <= t - 2 < n_items else None
            macc = jnp.full((ACC_ROWS, tq), -jnp.inf, F32)
            lacc = jnp.zeros((ACC_ROWS, tq), F32)
            oacc = jnp.zeros((64, tq), F32)
            for c in range(n_chunks):
                r0 = c * KEY_CHUNK
                if sc is not None:
                    q_ref, k_ref, _ = qkv[sc[0]]
                    q_t = q_ref[0, sc[1] * LANES:(sc[1] + 1) * LANES, :]
                    for r in range(r0, r0 + KEY_CHUNK, half):
                        s = jnp.dot(k_ref[0, r:r + half, sc[2] * LANES:(sc[2] + 1) * LANES], q_t,
                                    preferred_element_type=F32)
                        s_ref[t % 2, r:r + half, :] = s
                        for a in range(half // ACC_ROWS):
                            macc = jnp.maximum(macc, s[a * ACC_ROWS:(a + 1) * ACC_ROWS])
                if ex is not None:
                    p = jnp.exp2(s_ref[(t - 1) % 2, r0:r0 + KEY_CHUNK, :] - m_of[t - 1])
                    for a in range(KEY_CHUNK // ACC_ROWS):
                        lacc = lacc + p[a * ACC_ROWS:(a + 1) * ACC_ROWS]
                    p_ref[(t - 1) % 2, r0:r0 + KEY_CHUNK, :] = p.astype(BF16)
                if pv is not None:
                    v_ref = qkv[pv[0]][2]
                    oacc = oacc + jnp.dot(v_ref[0, pv[3] * 64:(pv[3] + 1) * 64, r0:r0 + KEY_CHUNK],
                                          p_ref[(t - 2) % 2, r0:r0 + KEY_CHUNK, :], preferred_element_type=F32)
            if sc is not None:
                m_of[t] = jnp.max(macc, axis=0, keepdims=True)
            if ex is not None:
                l_of[t - 1] = jnp.sum(lacc, axis=0, keepdims=True)
            if pv is not None:
                add_term(head_acc, t - 2, oacc * (coef(pv[5]) / l_of.pop(t - 2)))

    def write_out():
        for o_ref, ot_ref in zip(o_refs, ot_refs):
            o_ref[0] = ot_ref[...].T.astype(BF16)

    qb = pl.program_id(1)

    @pl.when(qb * tq < ctx_len)
    def _():
        run_context()
        write_out()

    @pl.when(qb * tq >= ctx_len)
    def _():
        run_latent()
        write_out()


def _dense_attn(lam, post, branches, ctx_len):
    B, _, T = branches[0][0].shape
    tq = ROW_TILE
    nb = len(branches)
    once = pl.Buffered(1)
    in_specs = [pl.BlockSpec(memory_space=pltpu.SMEM), pl.BlockSpec(post.shape, lambda b, t: (0, 0))]
    args = [lam, post]
    for q_t, k, v_t, _, _ in branches:
        in_specs += [pl.BlockSpec((1, q_t.shape[1], tq), lambda b, t: (b, 0, t)),
                     pl.BlockSpec((1, T, k.shape[2]), lambda b, t: (b, 0, 0), pipeline_mode=once),
                     pl.BlockSpec((1, v_t.shape[1], T), lambda b, t: (b, 0, 0), pipeline_mode=once)]
        args += [q_t, k, v_t]
    kern = functools.partial(_dense_attn_kernel, branches=tuple((br[3], br[4]) for br in branches), ctx_len=ctx_len)
    return pl.pallas_call(
        kern,
        out_shape=[jax.ShapeDtypeStruct((B, T, BRANCH_W), BF16)] * nb,
        grid=(B, T // tq),
        in_specs=in_specs,
        out_specs=[pl.BlockSpec((1, tq, BRANCH_W), lambda b, t: (b, t, 0))] * nb,
        scratch_shapes=[pltpu.VMEM((BRANCH_W, tq), F32)] * nb
        + [pltpu.VMEM((2, T, tq), F32), pltpu.VMEM((2, T, tq), BF16)],
        compiler_params=_cparams(("arbitrary", "arbitrary")),
        name="dense_attn",
    )(*args)


def _na_plan(rows):
    kh, kw = min(NA_KH, rows), NA_KW
    q_rows = ROW_TILE // GRID_W
    win_rows = min(kh + q_rows, rows)
    nblk = rows // q_rows
    row_start = np.clip(np.arange(rows) - kh // 2, 0, rows - kh)
    col_start = np.clip(np.arange(GRID_W) - kw // 2, 0, GRID_W - kw)
    u0 = np.clip(np.arange(nblk) * q_rows - kh // 2, 0, rows - win_rows)
    wk, tq = win_rows * GRID_W, ROW_TILE
    kk, qq = np.arange(wk), np.arange(tq)
    kc, qc = kk % GRID_W, qq % GRID_W
    col_sel = (np.arange(2 * NA_KW - 1)[:, None, None]
               == (np.arange(GRID_W)[None, :, None] - np.arange(GRID_W)[None, None, :] + (NA_KW - 1))).astype(np.float32)
    types, type_of, row_sel_l, valid_l = {}, [], [], []
    for j in range(nblk):
        kr = u0[j] + kk // GRID_W
        qr = j * q_rows + qq // GRID_W
        rs = row_start[qr]
        valid = ((kr[:, None] >= rs[None, :]) & (kr[:, None] < rs[None, :] + kh)
                 & (kc[:, None] >= col_start[qc][None, :]) & (kc[:, None] < col_start[qc][None, :] + kw))
        assert valid.sum(axis=0).min() == kh * kw and valid.sum(axis=0).max() == kh * kw
        rel = (u0[j] + np.arange(win_rows))[:, None] - (j * q_rows + np.arange(q_rows))[None, :] + (NA_KH - 1)
        row_sel = (rel[:, :, None] == np.arange(2 * NA_KH - 1)[None, None, :]).astype(np.float32)
        key = (valid.tobytes(), row_sel.tobytes())
        if key not in types:
            types[key] = len(row_sel_l)
            row_sel_l.append(row_sel), valid_l.append(valid)
        type_of.append(types[key])
    assert all(int(u) * GRID_W % LANES == 0 for u in u0)
    return (np.asarray(u0, np.int32), np.asarray(type_of, np.int32), np.stack(row_sel_l), col_sel,
            np.stack(valid_l), wk)


def _na_kernel(u0_ref, ty_ref, q_ref, k_ref, v_ref, bias_ref, o_ref, ot_ref, *, ctx_len):
    del ty_ref
    wk = bias_ref.shape[2]
    j = pl.program_id(1)

    def finish():
        o_ref[0] = ot_ref[...].T.astype(BF16)

    @pl.when(j == 0)
    def _():
        for i in range(NA_HEADS):
            o, l = _softmax_pv(k_ref[0, 0:ctx_len, i * LANES:(i + 1) * LANES], q_ref[0, i * LANES:(i + 1) * LANES, :],
                               v_ref[0, i * 64:(i + 1) * 64, 0:ctx_len], jnp.exp)
            ot_ref[i * 64:(i + 1) * 64, :] = o * (1.0 / l)
        finish()

    @pl.when(j > 0)
    def _():
        ws = pl.multiple_of(ctx_len + u0_ref[j - 1] * GRID_W, LANES)

        def scores(i):
            q_t = q_ref[0, i * LANES:(i + 1) * LANES, :]
            s_c = jnp.dot(k_ref[0, 0:ctx_len, i * LANES:(i + 1) * LANES], q_t, preferred_element_type=F32)
            s_w = jnp.dot(k_ref[0, pl.ds(ws, wk), i * LANES:(i + 1) * LANES], q_t,
                          preferred_element_type=F32) + bias_ref[0, i]
            return s_c, s_w

        def fold(fn, acc, a):
            for r in range(0, a.shape[0], ACC_ROWS):
                acc = fn(acc, a[r:r + ACC_ROWS])
            return acc

        def softmax_pv(i, s_c, s_w):
            tq = s_c.shape[1]
            macc = fold(jnp.maximum, fold(jnp.maximum, jnp.full((ACC_ROWS, tq), -jnp.inf, F32), s_c), s_w)
            m = jnp.max(macc, axis=0, keepdims=True)
            p_c, p_w = jnp.exp(s_c - m), jnp.exp(s_w - m)
            lacc = fold(jnp.add, fold(jnp.add, jnp.zeros((ACC_ROWS, tq), F32), p_c), p_w)
            l = jnp.sum(lacc, axis=0, keepdims=True)
            o = (jnp.dot(v_ref[0, i * 64:(i + 1) * 64, 0:ctx_len], p_c.astype(BF16), preferred_element_type=F32)
                 + jnp.dot(v_ref[0, i * 64:(i + 1) * 64, pl.ds(ws, wk)], p_w.astype(BF16),
                           preferred_element_type=F32))
            ot_ref[i * 64:(i + 1) * 64, :] = o * (1.0 / l)

        pending = scores(0)
        for i in range(NA_HEADS):
            upcoming = scores(i + 1) if i + 1 < NA_HEADS else None
            softmax_pv(i, *pending)
            pending = upcoming
        finish()


def _na_attn(q_t, k, v_t, rpb, ctx_len):
    B, _, T = q_t.shape
    tq = ROW_TILE
    rows = (T - ctx_len) // GRID_W
    u0, type_of, row_sel, col_sel, valid, wk = _na_plan(rows)
    hp = lax.Precision.HIGHEST
    by_col = jnp.einsum("hab,bcq->hacq", rpb.astype(F32), col_sel, precision=hp)
    bias = jnp.einsum("tkra,hacq->thkcrq", row_sel, by_col, precision=hp)
    bias = jnp.where(valid[:, None], bias.reshape(row_sel.shape[0], NA_HEADS, wk, tq), MASK_VALUE)
    nb = T // tq
    assert ctx_len == tq

    grid_spec = pltpu.PrefetchScalarGridSpec(
        num_scalar_prefetch=2,
        grid=(B, nb),
        in_specs=[pl.BlockSpec((1, NA_HEADS * LANES, tq), lambda b, j, u, ty: (b, 0, j)),
                  pl.BlockSpec((1, T, NA_HEADS * LANES), lambda b, j, u, ty: (b, 0, 0)),
                  pl.BlockSpec((1, BRANCH_W, T), lambda b, j, u, ty: (b, 0, 0)),
                  pl.BlockSpec((1, NA_HEADS, wk, tq), lambda b, j, u, ty: (ty[jnp.maximum(j - 1, 0)], 0, 0, 0))],
        out_specs=pl.BlockSpec((1, tq, BRANCH_W), lambda b, j, u, ty: (b, j, 0)),
        scratch_shapes=[pltpu.VMEM((BRANCH_W, tq), F32)])
    return pl.pallas_call(
        functools.partial(_na_kernel, ctx_len=ctx_len),
        out_shape=jax.ShapeDtypeStruct((B, T, BRANCH_W), BF16),
        grid_spec=grid_spec,
        compiler_params=_cparams(("arbitrary", "arbitrary")),
        name="na_attn",
    )(jnp.asarray(u0), jnp.asarray(type_of), q_t, k, v_t, bias)


def _merge_kernel(c_ref, x_ref, sc1_ref, sh1_ref, g1_ref, sc2_ref, sh2_ref, ona_ref, odf_ref, ogq_ref, oml_ref,
                  wg_ref, wb_ref, wo_ref, lng_ref, lnb_ref, rw_ref, rb_ref, tri_ref,
                  x1_ref, h2_ref, idx_ref, wts_ref, rank_ref, cnt_ref, carry_ref, hprev_ref, sem, *, abs_tile,
                  ctx_tiles, n_tiles):
    tm = x_ref.shape[1]
    step = pl.program_id(0)
    cur = lax.rem(step, 2)
    prev = 1 - cur

    def h2_copies(tile, slot, wait):
        for hbm, vmem, s in _tile_relayout_copies(h2_ref, hprev_ref.at[slot], tile, sem.at[slot]):
            cp = pltpu.make_async_copy(vmem, hbm, s)
            cp.wait() if wait else cp.start()

    @pl.when(step == 0)
    def _():
        carry_ref[...] = jnp.zeros_like(carry_ref)
        hprev_ref[...] = jnp.zeros_like(hprev_ref)

    @pl.when(step >= 2)
    def _():
        h2_copies(step - 2, cur, True)

    iota = lax.broadcasted_iota(I32, (N_EXPERTS, tm), 0)
    route = {}

    def route_logits():
        route["cur"] = lax.dot_general(rw_ref[...], hprev_ref[prev], (((1,), (1,)), ((), ())),
                                       preferred_element_type=F32, precision=lax.Precision.HIGHEST) + rb_ref[...]
        route["vals"], route["idxs"] = [], []

    def route_pick():
        cur = route["cur"]
        m = jnp.max(cur, axis=0, keepdims=True)
        ik = jnp.min(jnp.where(cur == m, iota, N_EXPERTS), axis=0, keepdims=True)
        route["vals"].append(m)
        route["idxs"].append(ik)
        route["cur"] = jnp.where(iota == ik, -jnp.inf, cur)

    def route_finish():
        vals, idxs = route["vals"], route["idxs"]
        exps = [jnp.exp(v - vals[0]) for v in vals]
        denom = exps[0] + exps[1] + exps[2] + exps[3]
        wts_ref[0] = jnp.concatenate([e / denom for e in exps], axis=0)
        idx_ref[0] = jnp.concatenate(idxs, axis=0)
        live = jnp.where(step >= 1, 1.0, 0.0)
        onehot = jnp.zeros((N_EXPERTS, tm), F32)
        for ik in idxs:
            onehot = onehot + jnp.where(iota == ik, live, 0.0)
        before = jnp.dot(onehot.astype(BF16), tri_ref[...], preferred_element_type=F32) + carry_ref[...]
        rank_ref[0] = jnp.concatenate(
            [jnp.sum(jnp.where(iota == ik, before, 0.0), axis=0, keepdims=True) for ik in idxs], axis=0).astype(I32)
        carry = carry_ref[...] + jnp.sum(onehot, axis=1, keepdims=True)
        carry_ref[...] = carry
        cnt_ref[...] = carry.astype(I32)

    route_pieces = [[route_logits], [route_pick, route_pick], [route_pick, route_pick], [route_finish]]

    x = _token_tile(c_ref, x_ref, abs_tile(step)[1], ctx_tiles)
    h1 = (x * (1.0 + sc1_ref[...]) + sh1_ref[...]).astype(BF16)
    y = None
    for i, o_ref in enumerate((ona_ref, odf_ref, ogq_ref, oml_ref)):
        gate = jax.nn.sigmoid(jnp.dot(h1, wg_ref[:, i * D_MODEL:(i + 1) * D_MODEL], preferred_element_type=F32))
        term = gate * jnp.dot(o_ref[0], wb_ref[i * BRANCH_W:(i + 1) * BRANCH_W, :], preferred_element_type=F32)
        y = term if y is None else y + term
        for piece in route_pieces[i]:
            piece()
    z = jnp.dot(y.astype(BF16), wo_ref[...], preferred_element_type=F32)
    r = DEEPNORM_ALPHA * x + g1_ref[...] * z
    mu = jnp.mean(r, axis=-1, keepdims=True)
    var = jnp.mean(jnp.square(r - mu), axis=-1, keepdims=True)
    x1 = (r - mu) * lax.rsqrt(var + EPS) * lng_ref[...] + lnb_ref[...]
    x1_ref[0] = x1
    hprev_ref[cur] = x1 * (1.0 + sc2_ref[...]) + sh2_ref[...]

    @pl.when(step < n_tiles)
    def _():
        h2_copies(step, cur, False)

    @pl.when(step == n_tiles)
    def _():
        h2_copies(n_tiles - 1, prev, True)


def _merge(ctx_src, x_src, x_tile_off, mod, o_na, o_df, o_gq, o_ml, wg, wb, wo, ln_g, ln_b, rw_t, rb, ctx_len, t_off):
    (B, T, _), D = o_na.shape, x_src.shape[2]
    tm = ROW_TILE
    nt = T // tm - t_off
    ctx_tiles = ctx_len // tm
    tq_rows = nt * tm
    n_tiles = B * nt

    def tile(i, lag):
        return jnp.clip(i - lag, 0, n_tiles - 1)

    def bt(i, lag):
        ti = tile(i, lag)
        return ti // nt, ti % nt

    def kind(t):
        return jnp.where(t + t_off >= ctx_tiles, 1, 0)

    def mod_spec(j, lag):
        return pl.BlockSpec((None, None, None, 1, D), lambda i: (bt(i, lag)[0], kind(bt(i, lag)[1]), j, 0, 0))

    def const(a):
        nd = a.ndim
        return pl.BlockSpec(a.shape, lambda i: (0,) * nd)

    def tok(cols):
        return pl.BlockSpec((1, tm, cols), lambda i: (bt(i, 0)[0], bt(i, 0)[1] + t_off, 0))

    tri = jnp.asarray(np.triu(np.ones((tm, tm), np.float32), 1), BF16)
    rb_b = jnp.broadcast_to(rb[:, None], (N_EXPERTS, tm)).astype(F32)
    route_shape = jax.ShapeDtypeStruct((n_tiles, TOP_K, tm), I32)
    route_spec = pl.BlockSpec((1, TOP_K, tm), lambda i: (tile(i, 1), 0, 0))
    def abs_tile(i):
        return bt(i, 0)[0], bt(i, 0)[1] + t_off

    return pl.pallas_call(
        functools.partial(_merge_kernel, abs_tile=abs_tile, ctx_tiles=ctx_tiles, n_tiles=n_tiles),
        out_shape=[jax.ShapeDtypeStruct((B, tq_rows, D), F32),
                   jax.ShapeDtypeStruct((B * tq_rows, D // LANES, LANES), F32),
                   route_shape, jax.ShapeDtypeStruct((n_tiles, TOP_K, tm), F32), route_shape,
                   jax.ShapeDtypeStruct((N_EXPERTS, tm), I32)],
        grid=(n_tiles + 1,),
        in_specs=[*_token_specs(ctx_tiles, x_tile_off, abs_tile),
                  mod_spec(1, 0), mod_spec(0, 0), mod_spec(2, 0), mod_spec(4, 0), mod_spec(3, 0),
                  tok(BRANCH_W), tok(BRANCH_W), tok(BRANCH_W), tok(BRANCH_W),
                  const(wg), const(wb), const(wo), const(ln_g), const(ln_b), const(rw_t), const(rb_b), const(tri)],
        out_specs=[pl.BlockSpec((1, tm, D), lambda i: (bt(i, 0)[0], bt(i, 0)[1], 0)),
                   pl.BlockSpec(memory_space=pl.ANY),
                   route_spec, route_spec, route_spec,
                   pl.BlockSpec((N_EXPERTS, tm), lambda i: (0, 0))],
        scratch_shapes=[pltpu.VMEM((N_EXPERTS, tm), F32), pltpu.VMEM((2, tm, D), F32),
                        pltpu.SemaphoreType.DMA((2,))],
        compiler_params=_cparams(("arbitrary",)),
        name="merge_router",
    )(ctx_src, x_src, mod, mod, mod, mod, mod, o_na, o_df, o_gq, o_ml, wg, wb, wo, ln_g, ln_b, rw_t, rb_b, tri)


def _row_copy(src, src_row, dst, dst_row, sem):
    return pltpu.make_async_copy(src.at[pl.ds(src_row, 1)], dst.at[pl.ds(dst_row, 1)], sem)


def _dispatch_kernel(dest_ref, h_ref, xs_in_ref, xs_ref, sem):
    del xs_in_ref
    tm = dest_ref.shape[2]

    def issue(t, c):
        for k in range(TOP_K):
            _row_copy(h_ref, t, xs_ref, dest_ref[0, k, t], sem).start(priority=k % 2)
        return c

    lax.fori_loop(0, tm, issue, 0, unroll=2)
    for k in range(TOP_K):
        pltpu.make_async_copy(h_ref, xs_ref.at[pl.ds(0, tm)], sem).wait()


def _dispatch(dest, h2, xs0):
    n_tiles, _, tm = dest.shape
    return pl.pallas_call(
        _dispatch_kernel,
        out_shape=jax.ShapeDtypeStruct(xs0.shape, F32),
        grid=(n_tiles,),
        in_specs=[pl.BlockSpec((1, TOP_K, tm), lambda i: (i, 0, 0), memory_space=pltpu.SMEM),
                  pl.BlockSpec((tm,) + h2.shape[1:], lambda i: (i, 0, 0)),
                  pl.BlockSpec(memory_space=pl.ANY)],
        out_specs=pl.BlockSpec(memory_space=pl.ANY),
        scratch_shapes=[pltpu.SemaphoreType.DMA(())],
        input_output_aliases={2: 0},
        compiler_params=_cparams(("arbitrary",)),
        name="moe_dispatch",
    )(dest, h2, xs0)


def _tile_relayout_copies(tiled_hbm, flat_vmem, tile, sem):
    tm = flat_vmem.shape[0]
    r0 = pl.multiple_of(tile * tm, tm)
    return [(tiled_hbm.at[pl.ds(r0, tm), j, :], flat_vmem.at[:, pl.ds(j * LANES, LANES)], sem)
            for j in range(D_MODEL // LANES)]


def _expert_kernel(te_ref, nu_ref, xs_ref, w1_ref, b1_ref, w2_ref, b2_ref, y_ref, w1s_ref, w2s_ref, x_buf, y_buf,
                   sem_in, sem_out, *, n_steps):
    i = pl.program_id(0)
    n_used = nu_ref[0]
    slot = lax.rem(i, 2)

    def fetch(tile, s, wait):
        for hbm, vmem, sem in _tile_relayout_copies(xs_ref, x_buf.at[s], tile, sem_in.at[s]):
            cp = pltpu.make_async_copy(hbm, vmem, sem)
            cp.wait() if wait else cp.start()

    def write_back(tile, s, wait):
        for hbm, vmem, sem in _tile_relayout_copies(y_ref, y_buf.at[s], tile, sem_out.at[s]):
            cp = pltpu.make_async_copy(vmem, hbm, sem)
            cp.wait() if wait else cp.start(priority=1)

    @pl.when(i == 0)
    def _():
        fetch(0, 0, False)

    @pl.when(i + 1 < n_used)
    def _():
        fetch(i + 1, 1 - slot, False)

    @pl.when((i == 0) | (te_ref[i] != te_ref[jnp.maximum(i - 1, 0)]))
    def _():
        w1s_ref[...] = w1_ref[...].astype(BF16)
        w2s_ref[...] = w2_ref[...].astype(BF16)

    @pl.when(i >= 2)
    def _():
        write_back(i - 2, slot, True)

    @pl.when(i < n_used)
    def _():
        fetch(i, slot, True)
        gu = jnp.dot(x_buf[slot].astype(BF16), w1s_ref[...], preferred_element_type=F32) + b1_ref[...]
        gate = jnp.minimum(gu[:, :D_FF], SWIGLU_LIMIT)
        up = jnp.clip(gu[:, D_FF:], -SWIGLU_LIMIT, SWIGLU_LIMIT)
        act = (up + 1.0) * gate * jax.nn.sigmoid(SWIGLU_ALPHA * gate)
        y_buf[slot] = jnp.dot(act.astype(BF16), w2s_ref[...], preferred_element_type=F32) + b2_ref[...]

    @pl.when(i >= n_used)
    def _():
        y_buf[slot] = jnp.zeros(y_buf.shape[1:], F32)

    write_back(i, slot, False)

    @pl.when(i == n_steps - 1)
    def _():
        write_back(i, slot, True)

        @pl.when(i >= 1)
        def _():
            write_back(i - 1, 1 - slot, True)


def _experts(tile_expert, n_used, xs, w1, b1, w2, b2, layer):
    n_slots = xs.shape[0]
    tm = ROW_TILE
    nt = n_slots // tm
    E = w1.shape[1]
    grid_spec = pltpu.PrefetchScalarGridSpec(
        num_scalar_prefetch=2, grid=(nt,),
        in_specs=[pl.BlockSpec(memory_space=pl.ANY),
                  pl.BlockSpec((None, None, D_MODEL, 2 * D_FF), lambda i, te, nu: (layer, te[i], 0, 0)),
                  pl.BlockSpec((None, None, 1, 2 * D_FF), lambda i, te, nu: (layer, te[i], 0, 0)),
                  pl.BlockSpec((None, None, D_FF, D_MODEL), lambda i, te, nu: (layer, te[i], 0, 0)),
                  pl.BlockSpec((None, None, 1, D_MODEL), lambda i, te, nu: (layer, te[i], 0, 0))],
        out_specs=pl.BlockSpec(memory_space=pl.ANY),
        scratch_shapes=[pltpu.VMEM((D_MODEL, 2 * D_FF), BF16), pltpu.VMEM((D_FF, D_MODEL), BF16),
                        pltpu.VMEM((2, tm, D_MODEL), F32), pltpu.VMEM((2, tm, D_MODEL), F32),
                        pltpu.SemaphoreType.DMA((2,)), pltpu.SemaphoreType.DMA((2,))])
    return pl.pallas_call(
        functools.partial(_expert_kernel, n_steps=nt),
        out_shape=jax.ShapeDtypeStruct(xs.shape, F32),
        grid_spec=grid_spec,
        compiler_params=_cparams(("arbitrary",)),
        name="moe_experts",
    )(tile_expert, n_used, xs, w1, b1.reshape(b1.shape[0], E, 1, -1), w2, b2.reshape(b2.shape[0], E, 1, -1))


def _combine_kernel(dest_ref, destn_ref, w_ref, x1_ref, g2_ref, lng_ref, lnb_ref, y_ref, o_ref, buf_ref, y3_ref,
                    sem, *, n_steps):
    tm = dest_ref.shape[2]
    i = pl.program_id(0)
    n = n_steps
    slot = lax.rem(i, 2)

    def issue(dr, s, t):
        for k in range(TOP_K):
            pltpu.make_async_copy(y_ref.at[pl.ds(dr[0, k, t], 1)], buf_ref.at[s, k, pl.ds(t, 1)],
                                  sem.at[s]).start(priority=k % 2)

    def weigh(t):
        acc = buf_ref[slot, 0, t] * w_ref[0, 0, t]
        for k in range(1, TOP_K):
            acc = acc + buf_ref[slot, k, t] * w_ref[0, k, t]
        y3_ref[t] = acc

    def loop(body):
        lax.fori_loop(0, tm, lambda t, c: (body(t), c)[1], 0, unroll=8)

    @pl.when(i == 0)
    def _():
        loop(lambda t: issue(dest_ref, 0, t))

    for k in range(TOP_K):
        pltpu.make_async_copy(y_ref.at[pl.ds(0, tm)], buf_ref.at[slot, k], sem.at[slot]).wait()

    @pl.when(i + 1 < n)
    def _():
        loop(lambda t: (issue(destn_ref, 1 - slot, t), weigh(t)))

    @pl.when(i + 1 >= n)
    def _():
        loop(weigh)
    y2 = jnp.concatenate([y3_ref[:, j, :] for j in range(D_MODEL // LANES)], axis=1)
    r = DEEPNORM_ALPHA * x1_ref[0] + g2_ref[...] * y2
    mu = jnp.mean(r, axis=-1, keepdims=True)
    var = jnp.mean(jnp.square(r - mu), axis=-1, keepdims=True)
    o_ref[0] = (r - mu) * lax.rsqrt(var + EPS) * lng_ref[...] + lnb_ref[...]


def _combine(dest, wts, x1, mod, ln_g, ln_b, y, ctx_len, t_off):
    B, tq_rows, D = x1.shape
    n_tiles, _, tm = dest.shape
    nt = n_tiles // B
    ctx_tiles = ctx_len // tm

    def kind(i):
        return jnp.where(i % nt + t_off >= ctx_tiles, 1, 0)

    smem = lambda f: pl.BlockSpec((1, TOP_K, tm), f, memory_space=pltpu.SMEM)
    cur = lambda i: (i, 0, 0)
    nxt = lambda i: (jnp.minimum(i + 1, n_tiles - 1), 0, 0)
    return pl.pallas_call(
        functools.partial(_combine_kernel, n_steps=n_tiles),
        out_shape=jax.ShapeDtypeStruct((B, tq_rows, D), F32),
        grid=(n_tiles,),
        in_specs=[smem(cur), smem(nxt), smem(cur),
                  pl.BlockSpec((1, tm, D), lambda i: (i // nt, i % nt, 0)),
                  pl.BlockSpec((None, None, None, 1, D), lambda i: (i // nt, kind(i), 5, 0, 0)),
                  pl.BlockSpec(ln_g.shape, lambda i: (0, 0)),
                  pl.BlockSpec(ln_b.shape, lambda i: (0, 0)),
                  pl.BlockSpec(memory_space=pl.ANY)],
        out_specs=pl.BlockSpec((1, tm, D), lambda i: (i // nt, i % nt, 0)),
        scratch_shapes=[pltpu.VMEM((2, TOP_K, tm, D // LANES, LANES), F32),
                        pltpu.VMEM((tm, D // LANES, LANES), F32), pltpu.SemaphoreType.DMA((2,))],
        compiler_params=_cparams(("arbitrary",)),
        name="moe_combine",
    )(dest, dest, wts, x1, mod, ln_g, ln_b, y)


_DF_HEADS_SPEC = tuple((((2 * i, i, False), (2 * i + 1, i, True)), i) for i in range(DF_HEADS))
_GQ_HEADS_SPEC = tuple((((i, i // (GQ_HEADS // GQ_KV_HEADS), False),), i // (GQ_HEADS // GQ_KV_HEADS))
                       for i in range(GQ_HEADS))
_ML_HEADS_SPEC = tuple((((i, i, False),), i) for i in range(ML_HEADS))


def _layer_weights(l, w_in, gq_qnorm, gq_knorm, ml_qa_norm, ml_wq_b, ml_kva_norm, ml_wkv_b):
    w_t = w_in[l][:, :O_GATES].T.astype(BF16)
    assert w_t.shape[0] == PROJ_ROWS

    tm = ROW_TILE
    gq_g = jnp.stack([gq_qnorm[l], _partner_gain(gq_qnorm[l], GQ_DIM),
                      gq_knorm[l], _partner_gain(gq_knorm[l], GQ_DIM)])
    gq_g = jnp.broadcast_to(gq_g[:, :, None], (4, GQ_DIM, tm)).astype(F32)
    qa_g = jnp.broadcast_to(ml_qa_norm[l][:, None], (ML_Q_RANK, tm)).astype(F32)
    kva_g = jnp.broadcast_to(ml_kva_norm[l][:, None], (ML_KV_RANK, tm)).astype(F32)
    wq_t = ml_wq_b[l].T.astype(BF16)
    wkv_t = ml_wkv_b[l].T.astype(BF16)
    return w_t, gq_g, qa_g, kva_g, wq_t, wkv_t


def kernel(x, c, ctx, c_ctx, w_ada, b_ada, w_in, na_rpb, df_lam, df_subln, gq_qnorm, gq_knorm, ml_qa_norm, ml_wq_b,
           ml_kva_norm, ml_wkv_b, w_branch, w_out, ln1_g, ln1_b, ln2_g, ln2_b, router_w, router_b, exp_w1, exp_b1,
           exp_w2, exp_b2):
    B, S, D = x.shape
    C = ctx.shape[1]
    T = C + S
    tm = ROW_TILE
    assert D == D_MODEL and C % tm == 0 and S % tm == 0 and C == tm

    ctx_src, x_src, x_tile_off = ctx, x, 0
    cc = jnp.concatenate([c, c_ctx[None], jnp.zeros((16 - B - 1, D), F32)], axis=0)
    mod_all = _ada(cc, w_ada, b_ada)
    cs32 = _rope_tables(32, C, S)
    cs64 = _rope_tables(64, C, S)

    xs = None
    for l in range(DEPTH):
        last = l == DEPTH - 1
        m = mod_all[l]
        mod = jnp.stack([jnp.broadcast_to(m[B][None], (B, 6 * D)), m[:B]], axis=1).reshape(B, 2, 6, 1, D)
        lam_init = 0.8 - 0.6 * math.exp(-0.3 * l)
        lp = df_lam[l].astype(F32)
        lam = (jnp.exp(jnp.sum(lp[0] * lp[1])) - jnp.exp(jnp.sum(lp[2] * lp[3])) + lam_init).reshape(1)
        w_t, gq_g, qa_g, kva_g, wq_t, wkv_t = _layer_weights(l, w_in, gq_qnorm, gq_knorm, ml_qa_norm, ml_wq_b,
                                                             ml_kva_norm, ml_wkv_b)
        (na_q, na_k, na_v, df_q, df_k, df_v, gq_q, gq_k, gq_v, ml_q, ml_k, ml_v) = _project(
            ctx_src, x_src, x_tile_off, T, mod, w_t, cs32, cs64, gq_g, qa_g, kva_g, wq_t, wkv_t, C)

        o_na = _na_attn(na_q, na_k, na_v, na_rpb[l], C)
        subln = jnp.broadcast_to((df_subln[l] * (1.0 - lam_init))[:, None], (DF_V, tm)).astype(F32)
        o_df, o_gq, o_ml = _dense_attn(lam, subln, [(df_q, df_k, df_v, _DF_HEADS_SPEC, True),
                                                    (gq_q, gq_k, gq_v, _GQ_HEADS_SPEC, False),
                                                    (ml_q, ml_k, ml_v, _ML_HEADS_SPEC, False)], C)

        t_off = C // tm if last else 0
        wg = w_in[l][:, O_GATES:].astype(BF16)
        wb = w_branch[l].reshape(N_BRANCH * BRANCH_W, D).astype(BF16)
        wo = w_out[l].astype(BF16)
        x1, h2, idx, wts, rank, cnt = _merge(ctx_src, x_src, x_tile_off, mod, o_na, o_df, o_gq, o_ml, wg, wb, wo,
                                             ln1_g[l][None], ln1_b[l][None], router_w[l].T, router_b[l], C, t_off)

        counts = cnt[:, 0]
        padded = ((counts + tm - 1) // tm) * tm
        ends = jnp.cumsum(padded)
        starts = (ends - padded).astype(I32)
        n_exp_tiles = (B * T * TOP_K) // tm + N_EXPERTS
        n_used = (ends[-1] // tm).astype(I32).reshape(1)
        tile_first = jnp.arange(n_exp_tiles, dtype=I32) * tm
        tile_expert = jnp.minimum(jnp.sum((ends[None, :] <= tile_first[:, None]).astype(I32), axis=1), N_EXPERTS - 1)
        tile_expert = jnp.where(tile_first < ends[-1], tile_expert, tile_expert[jnp.maximum(n_used[0] - 1, 0)])

        dest = jnp.sum(jnp.where(idx[..., None] == jnp.arange(N_EXPERTS, dtype=I32), starts, 0), axis=-1) + rank
        if xs is None:
            xs = jnp.zeros((n_exp_tiles * tm,) + h2.shape[1:], F32)
        xs = _dispatch(dest, h2, xs)
        ys = _experts(tile_expert, n_used, xs, exp_w1, exp_b1, exp_w2, exp_b2, l)
        out = _combine(dest, wts, x1, mod, ln2_g[l][None], ln2_b[l][None], ys, C, t_off)
        ctx_src, x_src, x_tile_off = out, out, C // tm
    return out
```

```python
import functools
import math

import numpy as np
import jax
import jax.numpy as jnp
from jax import lax
from jax.experimental import pallas as pl
from jax.experimental.pallas import tpu as pltpu

F32, BF16, I32 = jnp.float32, jnp.bfloat16, jnp.int32

D_MODEL = 1024
DEPTH = 2
GRID_W = 64
ROPE_BASE = 10000.0
EPS = 1e-6
NA_HEADS, NA_DIM, NA_KH, NA_KW = 4, 64, 8, 16
DF_HEADS, DF_QK = 4, 32
DF_V = 2 * DF_QK
GQ_HEADS, GQ_KV_HEADS, GQ_DIM = 4, 2, 64
ML_HEADS, ML_NOPE, ML_ROPE, ML_V, ML_Q_RANK, ML_KV_RANK = 4, 64, 32, 64, 256, 128
N_BRANCH, BRANCH_W = 4, 256
N_EXPERTS, TOP_K = 32, 4
D_FF = D_MODEL
SWIGLU_LIMIT, SWIGLU_ALPHA = 7.0, 1.702
DEEPNORM_ALPHA = (2 * DEPTH) ** 0.25

IN_SIZES = (256, 256, 256, 256, 256, 256, 256, 128, 128, ML_Q_RANK, ML_KV_RANK, ML_ROPE, N_BRANCH * D_MODEL)
IN_OFFSETS = tuple(int(o) for o in np.cumsum((0,) + IN_SIZES)[:-1])
(O_NAQ, O_NAK, O_NAV, O_DFQ, O_DFK, O_DFV, O_GQQ, O_GQK, O_GQV, O_MLQA, O_MLKVA, O_MLKR, O_GATES) = IN_OFFSETS

LANES = 128
ROW_TILE = 256
MASK_VALUE = -1e30
VMEM_LIMIT = 56 * 1024 * 1024
LOG2E = math.log2(math.e)
KEY_CHUNK = 256
ACC_ROWS = 16
SCORE_SPLIT = 2

_PROJ_GROUPS = (("na_q", 256), ("na_k", 256), ("na_v", 256),
                ("df_q", 256), ("df_k", 256), ("df_v", 256),
                ("gq_q", 256), ("gq_k", 128), ("gq_v", 128),
                ("ml_qa", 256), ("ml_kva", 128), ("ml_kr", 32))
_PROJ_OFF = {}
_o = 0
for _n, _r in _PROJ_GROUPS:
    _PROJ_OFF[_n] = (_o, _r)
    _o += _r
PROJ_ROWS = _o


def _cparams(sem):
    return pltpu.CompilerParams(dimension_semantics=sem, vmem_limit_bytes=VMEM_LIMIT)


def _partner_gain(g, n):
    g4 = g.reshape(2, 2, n // 4)
    return jnp.stack([g4[:, 1], g4[:, 0]], axis=1).reshape(n)


def _rope_tables(n, ctx_len, seq):
    h = n // 2
    t = np.arange(seq)
    rows, cols = (t // GRID_W).astype(np.float32), (t % GRID_W).astype(np.float32)
    inv_freq = (np.float32(ROPE_BASE) ** (-np.arange(0, h, 2, dtype=np.float32) / np.float32(h))).astype(np.float32)
    ang_r = rows[:, None] * inv_freq[None, :]
    ang_c = cols[:, None] * inv_freq[None, :]
    ang = np.concatenate([ang_r, ang_r, ang_c, ang_c], axis=1)
    cos = np.concatenate([np.ones((ctx_len, n), np.float32), np.cos(ang)], axis=0).T
    sin = np.concatenate([np.zeros((ctx_len, n), np.float32), np.sin(ang)], axis=0).T
    return jnp.asarray(np.stack([cos, sin]).astype(np.float32))


def _ada_kernel(c_ref, w_ref, b_ref, o_ref):
    cc = c_ref[...]
    h = (cc * jax.nn.sigmoid(cc)).astype(BF16)
    o_ref[0] = jnp.dot(h, w_ref[0].astype(BF16), preferred_element_type=F32) + b_ref[0]


def _ada(cc, w_ada, b_ada):
    L, D, N = w_ada.shape
    R = cc.shape[0]
    tn = 4 * LANES
    return pl.pallas_call(
        _ada_kernel,
        out_shape=jax.ShapeDtypeStruct((L, R, N), F32),
        grid=(L, N // tn),
        in_specs=[pl.BlockSpec((R, D), lambda l, j: (0, 0)),
                  pl.BlockSpec((1, D, tn), lambda l, j: (l, 0, j)),
                  pl.BlockSpec((1, 1, tn), lambda l, j: (l, 0, j))],
        out_specs=pl.BlockSpec((1, R, tn), lambda l, j: (l, 0, j)),
        compiler_params=_cparams(("arbitrary", "arbitrary")),
        name="ada_mod",
    )(cc, w_ada, b_ada.reshape(L, 1, N))


def _token_tile(c_ref, x_ref, tile, ctx_tiles):
    return jnp.where(tile < ctx_tiles, c_ref[0], x_ref[0])


def _proj_kernel(c_ref, x_ref, sc_ref, sh_ref, w_ref, cs32_ref, cs64_ref, gqg_ref, qag_ref, kvag_ref, wq_ref, wkv_ref,
                 na_q, na_k, na_v, df_q, df_k, df_v, gq_q, gq_k, gq_v, ml_q, ml_k, ml_v, *, ctx_tiles):
    tm = x_ref.shape[1]
    x_in = _token_tile(c_ref, x_ref, pl.program_id(1), ctx_tiles)
    h = (x_in * (1.0 + sc_ref[...]) + sh_ref[...]).astype(BF16)
    p_br = {}
    bounds = {"na": (0, 768), "df": (768, 1536), "gq": (1536, 2048), "ml": (2048, PROJ_ROWS)}

    def project(br):
        lo, hi = bounds[br]
        p_br[br] = lax.dot_general(w_ref[lo:hi, :], h, (((1,), (1,)), ((), ())), preferred_element_type=F32)

    def grp(name):
        o, r = _PROJ_OFF[name]
        br = name[:2]
        o -= bounds[br][0]
        return p_br[br][o:o + r]

    project("na")
    project("df")

    def zeros(n):
        return jnp.zeros((n, tm), F32)

    def swap(a, n):
        r, q = a.shape[0], n // 4
        a4 = a.reshape(r // (2 * q), 2, q, tm)
        return jnp.concatenate([-a4[:, 1:2], a4[:, 0:1]], axis=1).reshape(r, tm)

    def rope(a, cs_ref):
        n = cs_ref.shape[1]
        r = a.shape[0]
        a3, s3 = a.reshape(r // n, n, tm), swap(a, n).reshape(r // n, n, tm)
        return (a3 * cs_ref[0][None] + s3 * cs_ref[1][None]).reshape(r, tm)

    def head_slots(a, width):
        pieces = []
        for i in range(a.shape[0] // width):
            pieces += [a[i * width:(i + 1) * width], zeros(LANES - width)]
        return jnp.concatenate(pieces, axis=0)

    na_q[0] = head_slots(grp("na_q") * (NA_DIM ** -0.5), NA_DIM).astype(BF16)
    na_k[0] = head_slots(grp("na_k"), NA_DIM).T.astype(BF16)
    na_v[0] = grp("na_v").astype(BF16)

    project("gq")
    q = rope(grp("df_q"), cs32_ref) * (DF_QK ** -0.5 * LOG2E)
    pieces = []
    for i in range(DF_HEADS):
        q1, q2 = q[i * 64:i * 64 + 32], q[i * 64 + 32:i * 64 + 64]
        pieces += [q1, zeros(LANES - 32), zeros(32), q2, zeros(LANES - 64)]
    df_q[0] = jnp.concatenate(pieces, axis=0).astype(BF16)
    df_k[0] = head_slots(rope(grp("df_k"), cs32_ref), 2 * DF_QK).T.astype(BF16)
    df_v[0] = grp("df_v").astype(BF16)

    project("ml")
    cos64, sin64 = cs64_ref[0], cs64_ref[1]

    def norm_rope(a, g, g_sw, scale):
        hh = a.shape[0] // GQ_DIM
        a3, s3 = a.reshape(hh, GQ_DIM, tm), swap(a, GQ_DIM).reshape(hh, GQ_DIM, tm)
        r = lax.rsqrt(jnp.mean(a3 * a3, axis=1, keepdims=True) + EPS) * scale
        return ((a3 * (g * cos64)[None] + s3 * (g_sw * sin64)[None]) * r).reshape(hh * GQ_DIM, tm)

    gq_q[0] = head_slots(norm_rope(grp("gq_q"), gqg_ref[0], gqg_ref[1], GQ_DIM ** -0.5 * LOG2E),
                         GQ_DIM).astype(BF16)
    gq_k[0] = head_slots(norm_rope(grp("gq_k"), gqg_ref[2], gqg_ref[3], 1.0), GQ_DIM).T.astype(BF16)
    gq_v[0] = grp("gq_v").astype(BF16)

    qa = grp("ml_qa")
    qan = (qa * lax.rsqrt(jnp.mean(qa * qa, axis=0, keepdims=True) + EPS) * qag_ref[...]).astype(BF16)
    qq = jnp.dot(wq_ref[...], qan, preferred_element_type=F32)
    kva = grp("ml_kva")
    kvan = (kva * lax.rsqrt(jnp.mean(kva * kva, axis=0, keepdims=True) + EPS) * kvag_ref[...]).astype(BF16)
    kv = jnp.dot(wkv_ref[...], kvan, preferred_element_type=F32)
    k_rope = rope(grp("ml_kr"), cs32_ref)
    ml_scale = (ML_NOPE + ML_ROPE) ** -0.5 * LOG2E
    qd = ML_NOPE + ML_ROPE
    qp, kp, vp = [], [], []
    for i in range(ML_HEADS):
        b = i * LANES
        q_rope = rope(qq[i * qd + ML_NOPE:(i + 1) * qd], cs32_ref)
        qp += [qq[i * qd:i * qd + ML_NOPE] * ml_scale, q_rope * ml_scale, zeros(32)]
        kp += [kv[b:b + 64], k_rope, zeros(32)]
        vp += [kv[b + 64:b + 128]]
    ml_q[0] = jnp.concatenate(qp, axis=0).astype(BF16)
    ml_k[0] = jnp.concatenate(kp, axis=0).T.astype(BF16)
    ml_v[0] = jnp.concatenate(vp, axis=0).astype(BF16)


def _token_specs(ctx_tiles, x_tile_off, tile_of):
    def ctx_map(*g):
        return tile_of(*g)[0], 0, 0

    def lat_map(*g):
        b, t = tile_of(*g)
        return b, jnp.maximum(t - ctx_tiles, 0) + x_tile_off, 0

    return (pl.BlockSpec((1, ROW_TILE, D_MODEL), ctx_map), pl.BlockSpec((1, ROW_TILE, D_MODEL), lat_map))


def _project(ctx_src, x_src, x_tile_off, n_tok, mod, w_t, cs32, cs64, gq_g, qa_g, kva_g, wq_t, wkv_t, ctx_len):
    B, T, D = x_src.shape[0], n_tok, x_src.shape[2]
    tm = ROW_TILE
    nt = T // tm
    ctx_tiles = ctx_len // tm

    def kind(t):
        return jnp.where(t >= ctx_tiles, 1, 0)

    def mod_spec(j):
        return pl.BlockSpec((None, None, None, 1, D), lambda b, t: (b, kind(t), j, 0, 0))

    def const(a):
        nd = a.ndim
        return pl.BlockSpec(a.shape, lambda b, t: (0,) * nd)

    def fm(rows):
        return jax.ShapeDtypeStruct((B, rows, T), BF16), pl.BlockSpec((1, rows, tm), lambda b, t: (b, 0, t))

    def tmj(cols):
        return jax.ShapeDtypeStruct((B, T, cols), BF16), pl.BlockSpec((1, tm, cols), lambda b, t: (b, t, 0))

    outs = [fm(512), tmj(512), fm(256),
            fm(1024), tmj(512), fm(256),
            fm(512), tmj(256), fm(128),
            fm(512), tmj(512), fm(256)]
    return pl.pallas_call(
        functools.partial(_proj_kernel, ctx_tiles=ctx_tiles),
        out_shape=[o[0] for o in outs],
        grid=(B, nt),
        in_specs=[*_token_specs(ctx_tiles, x_tile_off, lambda b, t: (b, t)),
                  mod_spec(1), mod_spec(0),
                  const(w_t),
                  pl.BlockSpec((2, 32, tm), lambda b, t: (0, 0, t)),
                  pl.BlockSpec((2, 64, tm), lambda b, t: (0, 0, t)),
                  const(gq_g), const(qa_g), const(kva_g), const(wq_t), const(wkv_t)],
        out_specs=[o[1] for o in outs],
        compiler_params=_cparams(("arbitrary", "arbitrary")),
        name="in_proj",
    )(ctx_src, x_src, mod, mod, w_t, cs32, cs64, gq_g, qa_g, kva_g, wq_t, wkv_t)


def _softmax_pv(k, q_t, v_t, exp_fn):
    s = jnp.dot(k, q_t, preferred_element_type=F32)
    m = jnp.max(s, axis=0, keepdims=True)
    p = exp_fn(s - m)
    l = jnp.sum(p, axis=0, keepdims=True)
    o = jnp.dot(v_t, p.astype(BF16), preferred_element_type=F32)
    return o, l


def _dense_attn_kernel(lam_ref, post_ref, *refs, branches, ctx_len):
    nb = len(branches)
    qkv = [refs[3 * b:3 * b + 3] for b in range(nb)]
    o_refs = refs[3 * nb:4 * nb]
    ot_refs = refs[4 * nb:5 * nb]
    s_ref, p_ref = refs[5 * nb:]
    total, tq = qkv[0][1].shape[1], qkv[0][0].shape[2]
    items = [(b, qs, ks, v_idx, i, signed) for b, (heads, _) in enumerate(branches)
             for i, (terms, v_idx) in enumerate(heads) for qs, ks, signed in terms]
    last_of_head = {(it[0], it[4]): j for j, it in enumerate(items)}

    def finish_head(b, i, acc):
        if branches[b][1]:
            acc = acc * lax.rsqrt(jnp.mean(acc * acc, axis=0, keepdims=True) + EPS) * post_ref[...]
        ot_refs[b][i * 64:(i + 1) * 64, :] = acc

    def add_term(head_acc, j, o):
        b, i = items[j][0], items[j][4]
        head_acc[b, i] = o if (b, i) not in head_acc else head_acc[b, i] + o
        if last_of_head[b, i] == j:
            finish_head(b, i, head_acc.pop((b, i)))

    def coef(signed):
        return (-lam_ref[0]) if signed else 1.0

    def run_context():
        head_acc = {}
        for j, (b, qs, ks, v_idx, i, signed) in enumerate(items):
            q_ref, k_ref, v_ref = qkv[b]
            o, l = _softmax_pv(k_ref[0, 0:ctx_len, ks * LANES:(ks + 1) * LANES], q_ref[0, qs * LANES:(qs + 1) * LANES, :],
                               v_ref[0, v_idx * 64:(v_idx + 1) * 64, 0:ctx_len], jnp.exp2)
            add_term(head_acc, j, o * (coef(signed) / l))

    def run_latent():
        n_chunks = total // KEY_CHUNK
        half = KEY_CHUNK // SCORE_SPLIT
        n_items = len(items)
        m_of, l_of, head_acc = {}, {}, {}
        for t in range(n_items + 2):
            sc = items[t] if t < n_items else None
            ex = items[t - 1] if 0 <= t - 1 < n_items else None
            pv = items[t - 2] if 0 <= t - 2 < n_items else None
            macc = jnp.full((ACC_ROWS, tq), -jnp.inf, F32)
            lacc = jnp.zeros((ACC_ROWS, tq), F32)
            oacc = jnp.zeros((64, tq), F32)
            for c in range(n_chunks):
                r0 = c * KEY_CHUNK
                if sc is not None:
                    q_ref, k_ref, _ = qkv[sc[0]]
                    q_t = q_ref[0, sc[1] * LANES:(sc[1] + 1) * LANES, :]
                    for r in range(r0, r0 + KEY_CHUNK, half):
                        s = jnp.dot(k_ref[0, r:r + half, sc[2] * LANES:(sc[2] + 1) * LANES], q_t,
                                    preferred_element_type=F32)
                        s_ref[t % 2, r:r + half, :] = s
                        for a in range(half // ACC_ROWS):
                            macc = jnp.maximum(macc, s[a * ACC_ROWS:(a + 1) * ACC_ROWS])
                if ex is not None:
                    p = jnp.exp2(s_ref[(t - 1) % 2, r0:r0 + KEY_CHUNK, :] - m_of[t - 1])
                    for a in range(KEY_CHUNK // ACC_ROWS):
                        lacc = lacc + p[a * ACC_ROWS:(a + 1) * ACC_ROWS]
                    p_ref[(t - 1) % 2, r0:r0 + KEY_CHUNK, :] = p.astype(BF16)
                if pv is not None:
                    v_ref = qkv[pv[0]][2]
                    oacc = oacc + jnp.dot(v_ref[0, pv[3] * 64:(pv[3] + 1) * 64, r0:r0 + KEY_CHUNK],
                                          p_ref[(t - 2) % 2, r0:r0 + KEY_CHUNK, :], preferred_element_type=F32)
            if sc is not None:
                m_of[t] = jnp.max(macc, axis=0, keepdims=True)
            if ex is not None:
                l_of[t - 1] = jnp.sum(lacc, axis=0, keepdims=True)
            if pv is not None:
                add_term(head_acc, t - 2, oacc * (coef(pv[5]) / l_of.pop(t - 2)))

    def write_out():
        for o_ref, ot_ref in zip(o_refs, ot_refs):
            o_ref[0] = ot_ref[...].T.astype(BF16)

    qb = pl.program_id(1)

    @pl.when(qb * tq < ctx_len)
    def _():
        run_context()
        write_out()

    @pl.when(qb * tq >= ctx_len)
    def _():
        run_latent()
        write_out()


def _dense_attn(lam, post, branches, ctx_len):
    B, _, T = branches[0][0].shape
    tq = ROW_TILE
    nb = len(branches)
    once = pl.Buffered(1)
    in_specs = [pl.BlockSpec(memory_space=pltpu.SMEM), pl.BlockSpec(post.shape, lambda b, t: (0, 0))]
    args = [lam, post]
    for q_t, k, v_t, _, _ in branches:
        in_specs += [pl.BlockSpec((1, q_t.shape[1], tq), lambda b, t: (b, 0, t)),
                     pl.BlockSpec((1, T, k.shape[2]), lambda b, t: (b, 0, 0), pipeline_mode=once),
                     pl.BlockSpec((1, v_t.shape[1], T), lambda b, t: (b, 0, 0), pipeline_mode=once)]
        args += [q_t, k, v_t]
    kern = functools.partial(_dense_attn_kernel, branches=tuple((br[3], br[4]) for br in branches), ctx_len=ctx_len)
    return pl.pallas_call(
        kern,
        out_shape=[jax.ShapeDtypeStruct((B, T, BRANCH_W), BF16)] * nb,
        grid=(B, T // tq),
        in_specs=in_specs,
        out_specs=[pl.BlockSpec((1, tq, BRANCH_W), lambda b, t: (b, t, 0))] * nb,
        scratch_shapes=[pltpu.VMEM((BRANCH_W, tq), F32)] * nb
        + [pltpu.VMEM((2, T, tq), F32), pltpu.VMEM((2, T, tq), BF16)],
        compiler_params=_cparams(("arbitrary", "arbitrary")),
        name="dense_attn",
    )(*args)


def _na_plan(rows):
    kh, kw = min(NA_KH, rows), NA_KW
    q_rows = ROW_TILE // GRID_W
    win_rows = min(kh + q_rows, rows)
    nblk = rows // q_rows
    row_start = np.clip(np.arange(rows) - kh // 2, 0, rows - kh)
    col_start = np.clip(np.arange(GRID_W) - kw // 2, 0, GRID_W - kw)
    u0 = np.clip(np.arange(nblk) * q_rows - kh // 2, 0, rows - win_rows)
    wk, tq = win_rows * GRID_W, ROW_TILE
    kk, qq = np.arange(wk), np.arange(tq)
    kc, qc = kk % GRID_W, qq % GRID_W
    col_sel = (np.arange(2 * NA_KW - 1)[:, None, None]
               == (np.arange(GRID_W)[None, :, None] - np.arange(GRID_W)[None, None, :] + (NA_KW - 1))).astype(np.float32)
    types, type_of, row_sel_l, valid_l = {}, [], [], []
    for j in range(nblk):
        kr = u0[j] + kk // GRID_W
        qr = j * q_rows + qq // GRID_W
        rs = row_start[qr]
        valid = ((kr[:, None] >= rs[None, :]) & (kr[:, None] < rs[None, :] + kh)
                 & (kc[:, None] >= col_start[qc][None, :]) & (kc[:, None] < col_start[qc][None, :] + kw))
        assert valid.sum(axis=0).min() == kh * kw and valid.sum(axis=0).max() == kh * kw
        rel = (u0[j] + np.arange(win_rows))[:, None] - (j * q_rows + np.arange(q_rows))[None, :] + (NA_KH - 1)
        row_sel = (rel[:, :, None] == np.arange(2 * NA_KH - 1)[None, None, :]).astype(np.float32)
        key = (valid.tobytes(), row_sel.tobytes())
        if key not in types:
            types[key] = len(row_sel_l)
            row_sel_l.append(row_sel), valid_l.append(valid)
        type_of.append(types[key])
    assert all(int(u) * GRID_W % LANES == 0 for u in u0)
    return (np.asarray(u0, np.int32), np.asarray(type_of, np.int32), np.stack(row_sel_l), col_sel,
            np.stack(valid_l), wk)


def _na_kernel(u0_ref, ty_ref, q_ref, k_ref, v_ref, bias_ref, o_ref, ot_ref, *, ctx_len):
    del ty_ref
    wk = bias_ref.shape[2]
    j = pl.program_id(1)

    def finish():
        o_ref[0] = ot_ref[...].T.astype(BF16)

    @pl.when(j == 0)
    def _():
        for i in range(NA_HEADS):
            o, l = _softmax_pv(k_ref[0, 0:ctx_len, i * LANES:(i + 1) * LANES], q_ref[0, i * LANES:(i + 1) * LANES, :],
                               v_ref[0, i * 64:(i + 1) * 64, 0:ctx_len], jnp.exp)
            ot_ref[i * 64:(i + 1) * 64, :] = o * (1.0 / l)
        finish()

    @pl.when(j > 0)
    def _():
        ws = pl.multiple_of(ctx_len + u0_ref[j - 1] * GRID_W, LANES)

        def scores(i):
            q_t = q_ref[0, i * LANES:(i + 1) * LANES, :]
            s_c = jnp.dot(k_ref[0, 0:ctx_len, i * LANES:(i + 1) * LANES], q_t, preferred_element_type=F32)
            s_w = jnp.dot(k_ref[0, pl.ds(ws, wk), i * LANES:(i + 1) * LANES], q_t,
                          preferred_element_type=F32) + bias_ref[0, i]
            return s_c, s_w

        def fold(fn, acc, a):
            for r in range(0, a.shape[0], ACC_ROWS):
                acc = fn(acc, a[r:r + ACC_ROWS])
            return acc

        def softmax_pv(i, s_c, s_w):
            tq = s_c.shape[1]
            macc = fold(jnp.maximum, fold(jnp.maximum, jnp.full((ACC_ROWS, tq), -jnp.inf, F32), s_c), s_w)
            m = jnp.max(macc, axis=0, keepdims=True)
            p_c, p_w = jnp.exp(s_c - m), jnp.exp(s_w - m)
            lacc = fold(jnp.add, fold(jnp.add, jnp.zeros((ACC_ROWS, tq), F32), p_c), p_w)
            l = jnp.sum(lacc, axis=0, keepdims=True)
            o = (jnp.dot(v_ref[0, i * 64:(i + 1) * 64, 0:ctx_len], p_c.astype(BF16), preferred_element_type=F32)
                 + jnp.dot(v_ref[0, i * 64:(i + 1) * 64, pl.ds(ws, wk)], p_w.astype(BF16),
                           preferred_element_type=F32))
            ot_ref[i * 64:(i + 1) * 64, :] = o * (1.0 / l)

        pending = scores(0)
        for i in range(NA_HEADS):
            upcoming = scores(i + 1) if i + 1 < NA_HEADS else None
            softmax_pv(i, *pending)
            pending = upcoming
        finish()


def _na_attn(q_t, k, v_t, rpb, ctx_len):
    B, _, T = q_t.shape
    tq = ROW_TILE
    rows = (T - ctx_len) // GRID_W
    u0, type_of, row_sel, col_sel, valid, wk = _na_plan(rows)
    hp = lax.Precision.HIGHEST
    by_col = jnp.einsum("hab,bcq->hacq", rpb.astype(F32), col_sel, precision=hp)
    bias = jnp.einsum("tkra,hacq->thkcrq", row_sel, by_col, precision=hp)
    bias = jnp.where(valid[:, None], bias.reshape(row_sel.shape[0], NA_HEADS, wk, tq), MASK_VALUE)
    nb = T // tq
    assert ctx_len == tq

    grid_spec = pltpu.PrefetchScalarGridSpec(
        num_scalar_prefetch=2,
        grid=(B, nb),
        in_specs=[pl.BlockSpec((1, NA_HEADS * LANES, tq), lambda b, j, u, ty: (b, 0, j)),
                  pl.BlockSpec((1, T, NA_HEADS * LANES), lambda b, j, u, ty: (b, 0, 0)),
                  pl.BlockSpec((1, BRANCH_W, T), lambda b, j, u, ty: (b, 0, 0)),
                  pl.BlockSpec((1, NA_HEADS, wk, tq), lambda b, j, u, ty: (ty[jnp.maximum(j - 1, 0)], 0, 0, 0))],
        out_specs=pl.BlockSpec((1, tq, BRANCH_W), lambda b, j, u, ty: (b, j, 0)),
        scratch_shapes=[pltpu.VMEM((BRANCH_W, tq), F32)])
    return pl.pallas_call(
        functools.partial(_na_kernel, ctx_len=ctx_len),
        out_shape=jax.ShapeDtypeStruct((B, T, BRANCH_W), BF16),
        grid_spec=grid_spec,
        compiler_params=_cparams(("arbitrary", "arbitrary")),
        name="na_attn",
    )(jnp.asarray(u0), jnp.asarray(type_of), q_t, k, v_t, bias)


def _merge_kernel(c_ref, x_ref, sc1_ref, sh1_ref, g1_ref, sc2_ref, sh2_ref, ona_ref, odf_ref, ogq_ref, oml_ref,
                  wg_ref, wb_ref, wo_ref, lng_ref, lnb_ref, rw_ref, rb_ref, tri_ref,
                  x1_ref, h2_ref, idx_ref, wts_ref, rank_ref, cnt_ref, carry_ref, hprev_ref, sem, *, abs_tile,
                  ctx_tiles, n_tiles):
    tm = x_ref.shape[1]
    step = pl.program_id(0)
    cur = lax.rem(step, 2)
    prev = 1 - cur

    def h2_copies(tile, slot, wait):
        for hbm, vmem, s in _tile_relayout_copies(h2_ref, hprev_ref.at[slot], tile, sem.at[slot]):
            cp = pltpu.make_async_copy(vmem, hbm, s)
            cp.wait() if wait else cp.start()

    @pl.when(step == 0)
    def _():
        carry_ref[...] = jnp.zeros_like(carry_ref)
        hprev_ref[...] = jnp.zeros_like(hprev_ref)

    @pl.when(step >= 2)
    def _():
        h2_copies(step - 2, cur, True)

    iota = lax.broadcasted_iota(I32, (N_EXPERTS, tm), 0)
    route = {}

    def route_logits():
        route["cur"] = lax.dot_general(rw_ref[...], hprev_ref[prev], (((1,), (1,)), ((), ())),
                                       preferred_element_type=F32, precision=lax.Precision.HIGHEST) + rb_ref[...]
        route["vals"], route["idxs"] = [], []

    def route_pick():
        cur = route["cur"]
        m = jnp.max(cur, axis=0, keepdims=True)
        ik = jnp.min(jnp.where(cur == m, iota, N_EXPERTS), axis=0, keepdims=True)
        route["vals"].append(m)
        route["idxs"].append(ik)
        route["cur"] = jnp.where(iota == ik, -jnp.inf, cur)

    def route_finish():
        vals, idxs = route["vals"], route["idxs"]
        exps = [jnp.exp(v - vals[0]) for v in vals]
        denom = exps[0] + exps[1] + exps[2] + exps[3]
        wts_ref[0] = jnp.concatenate([e / denom for e in exps], axis=0)
        idx_ref[0] = jnp.concatenate(idxs, axis=0)
        live = jnp.where(step >= 1, 1.0, 0.0)
        onehot = jnp.zeros((N_EXPERTS, tm), F32)
        for ik in idxs:
            onehot = onehot + jnp.where(iota == ik, live, 0.0)
        before = jnp.dot(onehot.astype(BF16), tri_ref[...], preferred_element_type=F32) + carry_ref[...]
        rank_ref[0] = jnp.concatenate(
            [jnp.sum(jnp.where(iota == ik, before, 0.0), axis=0, keepdims=True) for ik in idxs], axis=0).astype(I32)
        carry = carry_ref[...] + jnp.sum(onehot, axis=1, keepdims=True)
        carry_ref[...] = carry
        cnt_ref[...] = carry.astype(I32)

    route_pieces = [[route_logits], [route_pick, route_pick], [route_pick, route_pick], [route_finish]]

    x = _token_tile(c_ref, x_ref, abs_tile(step)[1], ctx_tiles)
    h1 = (x * (1.0 + sc1_ref[...]) + sh1_ref[...]).astype(BF16)
    y = None
    for i, o_ref in enumerate((ona_ref, odf_ref, ogq_ref, oml_ref)):
        gate = jax.nn.sigmoid(jnp.dot(h1, wg_ref[:, i * D_MODEL:(i + 1) * D_MODEL], preferred_element_type=F32))
        term = gate * jnp.dot(o_ref[0], wb_ref[i * BRANCH_W:(i + 1) * BRANCH_W, :], preferred_element_type=F32)
        y = term if y is None else y + term
        for piece in route_pieces[i]:
            piece()
    z = jnp.dot(y.astype(BF16), wo_ref[...], preferred_element_type=F32)
    r = DEEPNORM_ALPHA * x + g1_ref[...] * z
    mu = jnp.mean(r, axis=-1, keepdims=True)
    var = jnp.mean(jnp.square(r - mu), axis=-1, keepdims=True)
    x1 = (r - mu) * lax.rsqrt(var + EPS) * lng_ref[...] + lnb_ref[...]
    x1_ref[0] = x1
    hprev_ref[cur] = x1 * (1.0 + sc2_ref[...]) + sh2_ref[...]

    @pl.when(step < n_tiles)
    def _():
        h2_copies(step, cur, False)

    @pl.when(step == n_tiles)
    def _():
        h2_copies(n_tiles - 1, prev, True)


def _merge(ctx_src, x_src, x_tile_off, mod, o_na, o_df, o_gq, o_ml, wg, wb, wo, ln_g, ln_b, rw_t, rb, ctx_len, t_off):
    (B, T, _), D = o_na.shape, x_src.shape[2]
    tm = ROW_TILE
    nt = T // tm - t_off
    ctx_tiles = ctx_len // tm
    tq_rows = nt * tm
    n_tiles = B * nt

    def tile(i, lag):
        return jnp.clip(i - lag, 0, n_tiles - 1)

    def bt(i, lag):
        ti = tile(i, lag)
        return ti // nt, ti % nt

    def kind(t):
        return jnp.where(t + t_off >= ctx_tiles, 1, 0)

    def mod_spec(j, lag):
        return pl.BlockSpec((None, None, None, 1, D), lambda i: (bt(i, lag)[0], kind(bt(i, lag)[1]), j, 0, 0))

    def const(a):
        nd = a.ndim
        return pl.BlockSpec(a.shape, lambda i: (0,) * nd)

    def tok(cols):
        return pl.BlockSpec((1, tm, cols), lambda i: (bt(i, 0)[0], bt(i, 0)[1] + t_off, 0))

    tri = jnp.asarray(np.triu(np.ones((tm, tm), np.float32), 1), BF16)
    rb_b = jnp.broadcast_to(rb[:, None], (N_EXPERTS, tm)).astype(F32)
    route_shape = jax.ShapeDtypeStruct((n_tiles, TOP_K, tm), I32)
    route_spec = pl.BlockSpec((1, TOP_K, tm), lambda i: (tile(i, 1), 0, 0))
    def abs_tile(i):
        return bt(i, 0)[0], bt(i, 0)[1] + t_off

    return pl.pallas_call(
        functools.partial(_merge_kernel, abs_tile=abs_tile, ctx_tiles=ctx_tiles, n_tiles=n_tiles),
        out_shape=[jax.ShapeDtypeStruct((B, tq_rows, D), F32),
                   jax.ShapeDtypeStruct((B * tq_rows, D // LANES, LANES), F32),
                   route_shape, jax.ShapeDtypeStruct((n_tiles, TOP_K, tm), F32), route_shape,
                   jax.ShapeDtypeStruct((N_EXPERTS, tm), I32)],
        grid=(n_tiles + 1,),
        in_specs=[*_token_specs(ctx_tiles, x_tile_off, abs_tile),
                  mod_spec(1, 0), mod_spec(0, 0), mod_spec(2, 0), mod_spec(4, 0), mod_spec(3, 0),
                  tok(BRANCH_W), tok(BRANCH_W), tok(BRANCH_W), tok(BRANCH_W),
                  const(wg), const(wb), const(wo), const(ln_g), const(ln_b), const(rw_t), const(rb_b), const(tri)],
        out_specs=[pl.BlockSpec((1, tm, D), lambda i: (bt(i, 0)[0], bt(i, 0)[1], 0)),
                   pl.BlockSpec(memory_space=pl.ANY),
                   route_spec, route_spec, route_spec,
                   pl.BlockSpec((N_EXPERTS, tm), lambda i: (0, 0))],
        scratch_shapes=[pltpu.VMEM((N_EXPERTS, tm), F32), pltpu.VMEM((2, tm, D), F32),
                        pltpu.SemaphoreType.DMA((2,))],
        compiler_params=_cparams(("arbitrary",)),
        name="merge_router",
    )(ctx_src, x_src, mod, mod, mod, mod, mod, o_na, o_df, o_gq, o_ml, wg, wb, wo, ln_g, ln_b, rw_t, rb_b, tri)


def _row_copy(src, src_row, dst, dst_row, sem):
    return pltpu.make_async_copy(src.at[pl.ds(src_row, 1)], dst.at[pl.ds(dst_row, 1)], sem)


def _dispatch_kernel(dest_ref, h_ref, xs_in_ref, xs_ref, sem):
    del xs_in_ref
    tm = dest_ref.shape[2]

    def issue(t, c):
        for k in range(TOP_K):
            _row_copy(h_ref, t, xs_ref, dest_ref[0, k, t], sem).start(priority=k % 2)
        return c

    lax.fori_loop(0, tm, issue, 0, unroll=2)
    for k in range(TOP_K):
        pltpu.make_async_copy(h_ref, xs_ref.at[pl.ds(0, tm)], sem).wait()


def _dispatch(dest, h2, xs0):
    n_tiles, _, tm = dest.shape
    return pl.pallas_call(
        _dispatch_kernel,
        out_shape=jax.ShapeDtypeStruct(xs0.shape, F32),
        grid=(n_tiles,),
        in_specs=[pl.BlockSpec((1, TOP_K, tm), lambda i: (i, 0, 0), memory_space=pltpu.SMEM),
                  pl.BlockSpec((tm,) + h2.shape[1:], lambda i: (i, 0, 0)),
                  pl.BlockSpec(memory_space=pl.ANY)],
        out_specs=pl.BlockSpec(memory_space=pl.ANY),
        scratch_shapes=[pltpu.SemaphoreType.DMA(())],
        input_output_aliases={2: 0},
        compiler_params=_cparams(("arbitrary",)),
        name="moe_dispatch",
    )(dest, h2, xs0)


def _tile_relayout_copies(tiled_hbm, flat_vmem, tile, sem):
    tm = flat_vmem.shape[0]
    r0 = pl.multiple_of(tile * tm, tm)
    return [(tiled_hbm.at[pl.ds(r0, tm), j, :], flat_vmem.at[:, pl.ds(j * LANES, LANES)], sem)
            for j in range(D_MODEL // LANES)]


def _expert_kernel(te_ref, nu_ref, xs_ref, w1_ref, b1_ref, w2_ref, b2_ref, y_ref, w1s_ref, w2s_ref, x_buf, y_buf,
                   sem_in, sem_out, *, n_steps):
    i = pl.program_id(0)
    n_used = nu_ref[0]
    slot = lax.rem(i, 2)

    def fetch(tile, s, wait):
        for hbm, vmem, sem in _tile_relayout_copies(xs_ref, x_buf.at[s], tile, sem_in.at[s]):
            cp = pltpu.make_async_copy(hbm, vmem, sem)
            cp.wait() if wait else cp.start()

    def write_back(tile, s, wait):
        for hbm, vmem, sem in _tile_relayout_copies(y_ref, y_buf.at[s], tile, sem_out.at[s]):
            cp = pltpu.make_async_copy(vmem, hbm, sem)
            cp.wait() if wait else cp.start(priority=1)

    @pl.when(i == 0)
    def _():
        fetch(0, 0, False)

    @pl.when(i + 1 < n_used)
    def _():
        fetch(i + 1, 1 - slot, False)

    @pl.when((i == 0) | (te_ref[i] != te_ref[jnp.maximum(i - 1, 0)]))
    def _():
        w1s_ref[...] = w1_ref[...].astype(BF16)
        w2s_ref[...] = w2_ref[...].astype(BF16)

    @pl.when(i >= 2)
    def _():
        write_back(i - 2, slot, True)

    @pl.when(i < n_used)
    def _():
        fetch(i, slot, True)
        gu = jnp.dot(x_buf[slot].astype(BF16), w1s_ref[...], preferred_element_type=F32) + b1_ref[...]
        gate = jnp.minimum(gu[:, :D_FF], SWIGLU_LIMIT)
        up = jnp.clip(gu[:, D_FF:], -SWIGLU_LIMIT, SWIGLU_LIMIT)
        act = (up + 1.0) * gate * jax.nn.sigmoid(SWIGLU_ALPHA * gate)
        y_buf[slot] = jnp.dot(act.astype(BF16), w2s_ref[...], preferred_element_type=F32) + b2_ref[...]

    @pl.when(i >= n_used)
    def _():
        y_buf[slot] = jnp.zeros(y_buf.shape[1:], F32)

    write_back(i, slot, False)

    @pl.when(i == n_steps - 1)
    def _():
        write_back(i, slot, True)

        @pl.when(i >= 1)
        def _():
            write_back(i - 1, 1 - slot, True)


def _experts(tile_expert, n_used, xs, w1, b1, w2, b2, layer):
    n_slots = xs.shape[0]
    tm = ROW_TILE
    nt = n_slots // tm
    E = w1.shape[1]
    grid_spec = pltpu.PrefetchScalarGridSpec(
        num_scalar_prefetch=2, grid=(nt,),
        in_specs=[pl.BlockSpec(memory_space=pl.ANY),
                  pl.BlockSpec((None, None, D_MODEL, 2 * D_FF), lambda i, te, nu: (layer, te[i], 0, 0)),
                  pl.BlockSpec((None, None, 1, 2 * D_FF), lambda i, te, nu: (layer, te[i], 0, 0)),
                  pl.BlockSpec((None, None, D_FF, D_MODEL), lambda i, te, nu: (layer, te[i], 0, 0)),
                  pl.BlockSpec((None, None, 1, D_MODEL), lambda i, te, nu: (layer, te[i], 0, 0))],
        out_specs=pl.BlockSpec(memory_space=pl.ANY),
        scratch_shapes=[pltpu.VMEM((D_MODEL, 2 * D_FF), BF16), pltpu.VMEM((D_FF, D_MODEL), BF16),
                        pltpu.VMEM((2, tm, D_MODEL), F32), pltpu.VMEM((2, tm, D_MODEL), F32),
                        pltpu.SemaphoreType.DMA((2,)), pltpu.SemaphoreType.DMA((2,))])
    return pl.pallas_call(
        functools.partial(_expert_kernel, n_steps=nt),
        out_shape=jax.ShapeDtypeStruct(xs.shape, F32),
        grid_spec=grid_spec,
        compiler_params=_cparams(("arbitrary",)),
        name="moe_experts",
    )(tile_expert, n_used, xs, w1, b1.reshape(b1.shape[0], E, 1, -1), w2, b2.reshape(b2.shape[0], E, 1, -1))


def _combine_kernel(dest_ref, destn_ref, w_ref, x1_ref, g2_ref, lng_ref, lnb_ref, y_ref, o_ref, buf_ref, y3_ref,
                    sem, *, n_steps):
    tm = dest_ref.shape[2]
    i = pl.program_id(0)
    n = n_steps
    slot = lax.rem(i, 2)

    def issue(dr, s, t):
        for k in range(TOP_K):
            pltpu.make_async_copy(y_ref.at[pl.ds(dr[0, k, t], 1)], buf_ref.at[s, k, pl.ds(t, 1)],
                                  sem.at[s]).start(priority=k % 2)

    def weigh(t):
        acc = buf_ref[slot, 0, t] * w_ref[0, 0, t]
        for k in range(1, TOP_K):
            acc = acc + buf_ref[slot, k, t] * w_ref[0, k, t]
        y3_ref[t] = acc

    def loop(body):
        lax.fori_loop(0, tm, lambda t, c: (body(t), c)[1], 0, unroll=8)

    @pl.when(i == 0)
    def _():
        loop(lambda t: issue(dest_ref, 0, t))

    for k in range(TOP_K):
        pltpu.make_async_copy(y_ref.at[pl.ds(0, tm)], buf_ref.at[slot, k], sem.at[slot]).wait()

    @pl.when(i + 1 < n)
    def _():
        loop(lambda t: (issue(destn_ref, 1 - slot, t), weigh(t)))

    @pl.when(i + 1 >= n)
    def _():
        loop(weigh)
    y2 = jnp.concatenate([y3_ref[:, j, :] for j in range(D_MODEL // LANES)], axis=1)
    r = DEEPNORM_ALPHA * x1_ref[0] + g2_ref[...] * y2
    mu = jnp.mean(r, axis=-1, keepdims=True)
    var = jnp.mean(jnp.square(r - mu), axis=-1, keepdims=True)
    o_ref[0] = (r - mu) * lax.rsqrt(var + EPS) * lng_ref[...] + lnb_ref[...]


def _combine(dest, wts, x1, mod, ln_g, ln_b, y, ctx_len, t_off):
    B, tq_rows, D = x1.shape
    n_tiles, _, tm = dest.shape
    nt = n_tiles // B
    ctx_tiles = ctx_len // tm

    def kind(i):
        return jnp.where(i % nt + t_off >= ctx_tiles, 1, 0)

    smem = lambda f: pl.BlockSpec((1, TOP_K, tm), f, memory_space=pltpu.SMEM)
    cur = lambda i: (i, 0, 0)
    nxt = lambda i: (jnp.minimum(i + 1, n_tiles - 1), 0, 0)
    return pl.pallas_call(
        functools.partial(_combine_kernel, n_steps=n_tiles),
        out_shape=jax.ShapeDtypeStruct((B, tq_rows, D), F32),
        grid=(n_tiles,),
        in_specs=[smem(cur), smem(nxt), smem(cur),
                  pl.BlockSpec((1, tm, D), lambda i: (i // nt, i % nt, 0)),
                  pl.BlockSpec((None, None, None, 1, D), lambda i: (i // nt, kind(i), 5, 0, 0)),
                  pl.BlockSpec(ln_g.shape, lambda i: (0, 0)),
                  pl.BlockSpec(ln_b.shape, lambda i: (0, 0)),
                  pl.BlockSpec(memory_space=pl.ANY)],
        out_specs=pl.BlockSpec((1, tm, D), lambda i: (i // nt, i % nt, 0)),
        scratch_shapes=[pltpu.VMEM((2, TOP_K, tm, D // LANES, LANES), F32),
                        pltpu.VMEM((tm, D // LANES, LANES), F32), pltpu.SemaphoreType.DMA((2,))],
        compiler_params=_cparams(("arbitrary",)),
        name="moe_combine",
    )(dest, dest, wts, x1, mod, ln_g, ln_b, y)


_DF_HEADS_SPEC = tuple((((2 * i, i, False), (2 * i + 1, i, True)), i) for i in range(DF_HEADS))
_GQ_HEADS_SPEC = tuple((((i, i // (GQ_HEADS // GQ_KV_HEADS), False),), i // (GQ_HEADS // GQ_KV_HEADS))
                       for i in range(GQ_HEADS))
_ML_HEADS_SPEC = tuple((((i, i, False),), i) for i in range(ML_HEADS))


def _layer_weights(l, w_in, gq_qnorm, gq_knorm, ml_qa_norm, ml_wq_b, ml_kva_norm, ml_wkv_b):
    w_t = w_in[l][:, :O_GATES].T.astype(BF16)
    assert w_t.shape[0] == PROJ_ROWS

    tm = ROW_TILE
    gq_g = jnp.stack([gq_qnorm[l], _partner_gain(gq_qnorm[l], GQ_DIM),
                      gq_knorm[l], _partner_gain(gq_knorm[l], GQ_DIM)])
    gq_g = jnp.broadcast_to(gq_g[:, :, None], (4, GQ_DIM, tm)).astype(F32)
    qa_g = jnp.broadcast_to(ml_qa_norm[l][:, None], (ML_Q_RANK, tm)).astype(F32)
    kva_g = jnp.broadcast_to(ml_kva_norm[l][:, None], (ML_KV_RANK, tm)).astype(F32)
    wq_t = ml_wq_b[l].T.astype(BF16)
    wkv_t = ml_wkv_b[l].T.astype(BF16)
    return w_t, gq_g, qa_g, kva_g, wq_t, wkv_t


def kernel(x, c, ctx, c_ctx, w_ada, b_ada, w_in, na_rpb, df_lam, df_subln, gq_qnorm, gq_knorm, ml_qa_norm, ml_wq_b,
           ml_kva_norm, ml_wkv_b, w_branch, w_out, ln1_g, ln1_b, ln2_g, ln2_b, router_w, router_b, exp_w1, exp_b1,
           exp_w2, exp_b2):
    B, S, D = x.shape
    C = ctx.shape[1]
    T = C + S
    tm = ROW_TILE
    assert D == D_MODEL and C % tm == 0 and S % tm == 0 and C == tm

    ctx_src, x_src, x_tile_off = ctx, x, 0
    cc = jnp.concatenate([c, c_ctx[None], jnp.zeros((16 - B - 1, D), F32)], axis=0)
    mod_all = _ada(cc, w_ada, b_ada)
    cs32 = _rope_tables(32, C, S)
    cs64 = _rope_tables(64, C, S)

    xs = None
    for l in range(DEPTH):
        last = l == DEPTH - 1
        m = mod_all[l]
        mod = jnp.stack([jnp.broadcast_to(m[B][None], (B, 6 * D)), m[:B]], axis=1).reshape(B, 2, 6, 1, D)
        lam_init = 0.8 - 0.6 * math.exp(-0.3 * l)
        lp = df_lam[l].astype(F32)
        lam = (jnp.exp(jnp.sum(lp[0] * lp[1])) - jnp.exp(jnp.sum(lp[2] * lp[3])) + lam_init).reshape(1)
        w_t, gq_g, qa_g, kva_g, wq_t, wkv_t = _layer_weights(l, w_in, gq_qnorm, gq_knorm, ml_qa_norm, ml_wq_b,
                                                             ml_kva_norm, ml_wkv_b)
        (na_q, na_k, na_v, df_q, df_k, df_v, gq_q, gq_k, gq_v, ml_q, ml_k, ml_v) = _project(
            ctx_src, x_src, x_tile_off, T, mod, w_t, cs32, cs64, gq_g, qa_g, kva_g, wq_t, wkv_t, C)

        o_na = _na_attn(na_q, na_k, na_v, na_rpb[l], C)
        subln = jnp.broadcast_to((df_subln[l] * (1.0 - lam_init))[:, None], (DF_V, tm)).astype(F32)
        o_df, o_gq, o_ml = _dense_attn(lam, subln, [(df_q, df_k, df_v, _DF_HEADS_SPEC, True),
                                                    (gq_q, gq_k, gq_v, _GQ_HEADS_SPEC, False),
                                                    (ml_q, ml_k, ml_v, _ML_HEADS_SPEC, False)], C)

        t_off = C // tm if last else 0
        wg = w_in[l][:, O_GATES:].astype(BF16)
        wb = w_branch[l].reshape(N_BRANCH * BRANCH_W, D).astype(BF16)
        wo = w_out[l].astype(BF16)
        x1, h2, idx, wts, rank, cnt = _merge(ctx_src, x_src, x_tile_off, mod, o_na, o_df, o_gq, o_ml, wg, wb, wo,
                                             ln1_g[l][None], ln1_b[l][None], router_w[l].T, router_b[l], C, t_off)

        counts = cnt[:, 0]
        padded = ((counts + tm - 1) // tm) * tm
        ends = jnp.cumsum(padded)
        starts = (ends - padded).astype(I32)
        n_exp_tiles = (B * T * TOP_K) // tm + N_EXPERTS
        n_used = (ends[-1] // tm).astype(I32).reshape(1)
        tile_first = jnp.arange(n_exp_tiles, dtype=I32) * tm
        tile_expert = jnp.minimum(jnp.sum((ends[None, :] <= tile_first[:, None]).astype(I32), axis=1), N_EXPERTS - 1)
        tile_expert = jnp.where(tile_first < ends[-1], tile_expert, tile_expert[jnp.maximum(n_used[0] - 1, 0)])

        dest = jnp.sum(jnp.where(idx[..., None] == jnp.arange(N_EXPERTS, dtype=I32), starts, 0), axis=-1) + rank
        if xs is None:
            xs = jnp.zeros((n_exp_tiles * tm,) + h2.shape[1:], F32)
        xs = _dispatch(dest, h2, xs)
        ys = _experts(tile_expert, n_used, xs, exp_w1, exp_b1, exp_w2, exp_b2, l)
        out = _combine(dest, wts, x1, mod, ln2_g[l][None], ln2_b[l][None], ys, C, t_off)
        ctx_src, x_src, x_tile_off = out, out, C // tm
    return out
```
